```python
import math
import jax, jax.numpy as jnp
from jax import lax
import numpy as np

D_MODEL = 1024
BATCH = 2
SEQ = 8192
DEPTH = 2

N_A_LAYERS = DEPTH // 2
N_B_LAYERS = DEPTH - N_A_LAYERS
RET_HEADS = 4
RET_QK_DIM = D_MODEL // RET_HEADS
RET_V_DIM = 2 * RET_QK_DIM
RET_CHUNK = 128
RET_ROT_BASE = 10000.0
ATT_HEADS = 8
ATT_HEAD_DIM = D_MODEL // ATT_HEADS
ROPE_DIM = ATT_HEAD_DIM // 4
ROPE_THETA = 500000.0
MOBA_BLOCK = 256
MOBA_TOPK = 3
MOBA_Q_CHUNK = 32
N_GROUPS = 4
EXPERTS_PER_GROUP = 4
D_EXPERT = D_MODEL // 2
EXPERT_TOPK = 2
NORM_EPS = 1e-6

kernel_name = 'yoco_retention_moba_hmoe'


def rms_norm(x, g):
    x32 = x.astype(jnp.float32)
    y = x32 * lax.rsqrt(jnp.mean(x32 * x32, axis=-1, keepdims=True) + NORM_EPS)
    return (y * g.astype(jnp.float32)).astype(x.dtype)


def rotary(x, pos, rot_dim, theta):
    half = rot_dim // 2
    inv = 1.0 / (theta ** (jnp.arange(half, dtype=jnp.float32) / half))
    ang = pos.astype(jnp.float32)[:, None] * inv[None, :]
    cos = jnp.cos(ang)[:, None, :]
    sin = jnp.sin(ang)[:, None, :]
    xr = x[..., :rot_dim].astype(jnp.float32)
    x1, x2 = xr[..., :half], xr[..., half:]
    rot = jnp.concatenate([x1 * cos - x2 * sin, x2 * cos + x1 * sin], axis=-1).astype(x.dtype)
    return jnp.concatenate([rot, x[..., rot_dim:]], axis=-1)


def retention(x, w_in, w_out):
    B, S, _ = x.shape
    H, dk, dv, C = RET_HEADS, RET_QK_DIM, RET_V_DIM, RET_CHUNK
    f32 = jnp.float32
    proj = x @ w_in
    q, k, v, g = jnp.split(proj, [H * dk, 2 * H * dk, 2 * H * dk + H * dv], axis=-1)
    pos = jnp.arange(S)
    q = rotary(q.reshape(B, S, H, dk), pos, dk, RET_ROT_BASE)
    k = rotary(k.reshape(B, S, H, dk), pos, dk, RET_ROT_BASE) * (dk ** -0.5)
    v = v.reshape(B, S, H, dv)
    n_c = S // C
    qc = q.astype(f32).reshape(B, n_c, C, H, dk).transpose(1, 0, 3, 2, 4)
    kc = k.astype(f32).reshape(B, n_c, C, H, dk).transpose(1, 0, 3, 2, 4)
    vc = v.astype(f32).reshape(B, n_c, C, H, dv).transpose(1, 0, 3, 2, 4)
    log_gamma = jnp.log1p(-jnp.power(2.0, -5.0 - jnp.arange(H, dtype=f32)))
    idx = jnp.arange(C, dtype=f32)
    diff = idx[:, None] - idx[None, :]
    decay_mask = jnp.where(diff >= 0, jnp.exp(log_gamma[:, None, None] * jnp.maximum(diff, 0.0)), 0.0)
    q_decay = jnp.exp(log_gamma[:, None] * (idx + 1.0))
    k_decay = jnp.exp(log_gamma[:, None] * (C - 1.0 - idx))
    chunk_decay = jnp.exp(log_gamma * C)
    scores = jnp.einsum('nbhcd,nbhmd->nbhcm', qc, kc) * decay_mask[None, None]
    o_intra = jnp.einsum('nbhcm,nbhme->nbhce', scores, vc)

    def step(state, inp):
        qi, ki, vi = inp
        o = jnp.einsum('bhcd,bhde->bhce', qi * q_decay[None, :, :, None], state)
        state = state * chunk_decay[None, :, None, None] + jnp.einsum(
            'bhcd,bhce->bhde', ki * k_decay[None, :, :, None], vi)
        return state, o

    state0 = jnp.zeros((B, H, dk, dv), f32)
    _, o_cross = lax.scan(step, state0, (qc, kc, vc))
    o = (o_intra + o_cross).transpose(1, 0, 3, 2, 4).reshape(B, S, H, dv)
    o = o * lax.rsqrt(jnp.mean(o * o, axis=-1, keepdims=True) + NORM_EPS)
    y = jax.nn.silu(g.astype(f32)) * o.reshape(B, S, H * dv)
    return y.astype(x.dtype) @ w_out


def shared_kv(h, kv_norm, w_kv):
    B, S, _ = h.shape
    H, dh, BLK = ATT_HEADS, ATT_HEAD_DIM, MOBA_BLOCK
    hn = rms_norm(h, kv_norm)
    k, v = jnp.split(hn @ w_kv, 2, axis=-1)
    k = rotary(k.reshape(B, S, H, dh), jnp.arange(S), ROPE_DIM, ROPE_THETA)
    v = v.reshape(B, S, H, dh)
    n_blk = -(-S // BLK)
    pad = n_blk * BLK - S
    k = jnp.pad(k, ((0, 0), (0, pad), (0, 0), (0, 0))).transpose(0, 2, 1, 3).reshape(B, H, n_blk, BLK, dh)
    v = jnp.pad(v, ((0, 0), (0, pad), (0, 0), (0, 0))).transpose(0, 2, 1, 3).reshape(B, H, n_blk, BLK, dh)
    k_mean = jnp.mean(k.astype(jnp.float32), axis=3).astype(k.dtype)
    return k, v, k_mean


def moba_attention(x, w_q, w_o, kb, vb, k_mean):
    B, S, _ = x.shape
    H, dh, BLK, QC = ATT_HEADS, ATT_HEAD_DIM, MOBA_BLOCK, MOBA_Q_CHUNK
    n_blk = kb.shape[2]
    s_pad = n_blk * BLK
    topk = min(MOBA_TOPK, n_blk)
    q = (x @ w_q).reshape(B, S, H, dh)
    q = rotary(q, jnp.arange(S), ROPE_DIM, ROPE_THETA) * (dh ** -0.5)
    q = jnp.pad(q, ((0, 0), (0, s_pad - S), (0, 0), (0, 0))).transpose(0, 2, 1, 3)
    n_qc = s_pad // QC
    qs = q.reshape(B, H, n_qc, QC, dh).transpose(2, 0, 1, 3, 4)
    b_idx = jnp.arange(B)[:, None, None, None]
    h_idx = jnp.arange(H)[None, :, None, None]
    blk_ids = jnp.arange(n_blk)
    neg_inf = -jnp.inf

    def attend(args):
        qi, ci = args
        q0 = ci * QC
        blk = q0 // BLK
        gate = jnp.einsum('bhqd,bhnd->bhqn', qi, k_mean).astype(jnp.float32)
        gate = jnp.where(blk_ids < blk, gate, neg_inf)
        _, sel = lax.top_k(gate, topk)
        slot_ok = jnp.arange(topk) < blk
        ks = kb[b_idx, h_idx, sel]
        vs = vb[b_idx, h_idx, sel]
        s_sel = jnp.einsum('bhqd,bhqnkd->bhqnk', qi, ks).astype(jnp.float32)
        s_sel = jnp.where(slot_ok[:, None], s_sel, neg_inf).reshape(B, H, QC, topk * BLK)
        k_own = lax.dynamic_index_in_dim(kb, blk, axis=2, keepdims=False)
        v_own = lax.dynamic_index_in_dim(vb, blk, axis=2, keepdims=False)
        s_own = jnp.einsum('bhqd,bhkd->bhqk', qi, k_own).astype(jnp.float32)
        q_pos = q0 + jnp.arange(QC)
        k_pos = blk * BLK + jnp.arange(BLK)
        s_own = jnp.where(k_pos[None, :] <= q_pos[:, None], s_own, neg_inf)
        p = jax.nn.softmax(jnp.concatenate([s_sel, s_own], axis=-1), axis=-1).astype(vb.dtype)
        p_sel = p[..., :topk * BLK].reshape(B, H, QC, topk, BLK)
        p_own = p[..., topk * BLK:]
        return (jnp.einsum('bhqnk,bhqnkd->bhqd', p_sel, vs)
                + jnp.einsum('bhqk,bhkd->bhqd', p_own, v_own))

    o = lax.map(attend, (qs, jnp.arange(n_qc)))
    o = o.transpose(1, 0, 3, 2, 4).reshape(B, s_pad, H * dh)[:, :S]
    return o @ w_o


def hier_moe(x, w_rg, b_rg, w_re, b_re, w_gate, w_up, w_down):
    B, S, D = x.shape
    G, E = N_GROUPS, EXPERTS_PER_GROUP
    t = x.reshape(-1, D)
    g_prob = jax.nn.softmax((t @ w_rg).astype(jnp.float32) + b_rg, axis=-1)
    g_val, g_idx = lax.top_k(g_prob, 1)
    e_logits_all = jnp.einsum('td,gde->tge', t, w_re).astype(jnp.float32) + b_re
    e_logits = jnp.take_along_axis(e_logits_all, g_idx[:, :, None], axis=1)[:, 0]
    e_val, e_idx = lax.top_k(e_logits, EXPERT_TOPK)
    e_w = jax.nn.softmax(e_val, axis=-1)
    within = jnp.sum(jax.nn.one_hot(e_idx, E) * e_w[..., None], axis=1)
    group_w = jax.nn.one_hot(g_idx[:, 0], G) * g_val
    combine = (group_w[:, :, None] * within[:, None, :]).astype(x.dtype)
    y = jnp.zeros_like(t)
    for g in range(G):
        hg = jax.nn.silu(jnp.einsum('td,edf->tef', t, w_gate[g])) * jnp.einsum('td,edf->tef', t, w_up[g])
        y = y + jnp.einsum('tef,efd->td', hg * combine[:, g, :, None], w_down[g])
    return y.reshape(B, S, D)


def setup_inputs(seed: int = 0) -> dict:
    key = jax.random.key(seed)
    ks = jax.random.split(key, 18)
    f32 = jnp.float32

    def normal(k, shape, scale):
        return jax.random.normal(k, shape, f32) * scale

    def gain(k, shape):
        return 1.0 + 0.05 * jax.random.normal(k, shape, f32)

    out_scale = (2.0 * DEPTH) ** -0.5
    ret_in_cols = 2 * RET_HEADS * RET_QK_DIM + 2 * RET_HEADS * RET_V_DIM
    att_w = ATT_HEADS * ATT_HEAD_DIM
    return {
        'x': normal(ks[0], (BATCH, SEQ, D_MODEL), 1.0),
        'ret_norm': gain(ks[1], (N_A_LAYERS, D_MODEL)),
        'ret_w_in': normal(ks[2], (N_A_LAYERS, D_MODEL, ret_in_cols), D_MODEL ** -0.5),
        'ret_w_out': normal(ks[3], (N_A_LAYERS, RET_HEADS * RET_V_DIM, D_MODEL), (RET_HEADS * RET_V_DIM) ** -0.5 * out_scale),
        'kv_norm': gain(ks[4], (D_MODEL,)),
        'w_kv': normal(ks[5], (D_MODEL, 2 * att_w), D_MODEL ** -0.5),
        'attn_norm': gain(ks[6], (N_B_LAYERS, D_MODEL)),
        'w_q': normal(ks[7], (N_B_LAYERS, D_MODEL, att_w), D_MODEL ** -0.5),
        'w_o': normal(ks[8], (N_B_LAYERS, att_w, D_MODEL), att_w ** -0.5 * out_scale),
        'ffn_norm': gain(ks[9], (DEPTH, D_MODEL)),
        'router_group_w': normal(ks[10], (DEPTH, D_MODEL, N_GROUPS), D_MODEL ** -0.5),
        'router_group_b': normal(ks[11], (DEPTH, N_GROUPS), 0.01),
        'router_expert_w': normal(ks[12], (DEPTH, N_GROUPS, D_MODEL, EXPERTS_PER_GROUP), D_MODEL ** -0.5),
        'router_expert_b': normal(ks[13], (DEPTH, N_GROUPS, EXPERTS_PER_GROUP), 0.01),
        'expert_w_gate': normal(ks[14], (DEPTH, N_GROUPS, EXPERTS_PER_GROUP, D_MODEL, D_EXPERT), D_MODEL ** -0.5),
        'expert_w_up': normal(ks[15], (DEPTH, N_GROUPS, EXPERTS_PER_GROUP, D_MODEL, D_EXPERT), D_MODEL ** -0.5),
        'expert_w_down': normal(ks[16], (DEPTH, N_GROUPS, EXPERTS_PER_GROUP, D_EXPERT, D_MODEL), D_EXPERT ** -0.5 * out_scale),
        'final_norm': gain(ks[17], (D_MODEL,)),
    }


def reference(x, ret_norm, ret_w_in, ret_w_out, kv_norm, w_kv, attn_norm, w_q, w_o,
              ffn_norm, router_group_w, router_group_b, router_expert_w, router_expert_b,
              expert_w_gate, expert_w_up, expert_w_down, final_norm):
    h = x
    kb = vb = k_mean = None
    for layer in range(DEPTH):
        if layer < N_A_LAYERS:
            h = h + retention(rms_norm(h, ret_norm[layer]), ret_w_in[layer], ret_w_out[layer])
        else:
            if layer == N_A_LAYERS:
                kb, vb, k_mean = shared_kv(h, kv_norm, w_kv)
            j = layer - N_A_LAYERS
            h = h + moba_attention(rms_norm(h, attn_norm[j]), w_q[j], w_o[j], kb, vb, k_mean)
        h = h + hier_moe(rms_norm(h, ffn_norm[layer]), router_group_w[layer], router_group_b[layer],
                         router_expert_w[layer], router_expert_b[layer], expert_w_gate[layer],
                         expert_w_up[layer], expert_w_down[layer])
    return rms_norm(h, final_norm)
```

```python
import functools
import math

import jax
import jax.numpy as jnp
from jax import lax
from jax.experimental import pallas as pl
from jax.experimental.pallas import tpu as pltpu

F32 = jnp.float32
BF16 = jnp.bfloat16

NORM_EPS = 1e-6
RET_HEADS = 4
RET_ROT_BASE = 10000.0
ATT_HEADS = 8
ROPE_FRACTION = 4
ROPE_THETA = 500000.0
MOBA_BLOCK = 256
MOBA_TOPK = 3
N_GROUPS = 4
EXPERTS_PER_GROUP = 4
PAIR_LO = (0, 0, 0, 1, 1, 2)
PAIR_HI = (1, 2, 3, 2, 3, 3)
N_PAIRS = len(PAIR_LO)
N_CLASSES = N_GROUPS * N_PAIRS

LANES = 128
RET_KERNEL_CHUNK = 256
MOE_TILE = 128
V7X_VMEM_BYTES = 64 * 1024 * 1024


def _vmem_limit(estimate_bytes):
    return int(min(V7X_VMEM_BYTES - 8 * 1024 * 1024, max(32 * 1024 * 1024, estimate_bytes * 5 // 4)))


def _rms_scale(x):
    return lax.rsqrt(jnp.mean(x * x, axis=-1, keepdims=True) + NORM_EPS)


def _silu(a):
    return a * jax.nn.sigmoid(a)


def _ret_in_kernel(x_ref, g_ref, w_ref, cos_ref, sin_ref, q_ref, k_ref, v_ref, gate_ref, *, heads, dk, dv):
    x = x_ref[...]
    xn = (x * _rms_scale(x) * g_ref[...]).astype(BF16)
    cos = cos_ref[...]
    sin = sin_ref[...]
    half = dk // 2
    k_scale = dk ** -0.5

    def rot_store(dst_ref, col0, h, scale):
        p = jnp.dot(xn, w_ref[:, col0 + h * dk:col0 + (h + 1) * dk], preferred_element_type=F32)
        x1 = p[:, :half]
        x2 = p[:, half:]
        dst_ref[:, h * dk:h * dk + half] = ((x1 * cos - x2 * sin) * scale).astype(BF16)
        dst_ref[:, h * dk + half:(h + 1) * dk] = ((x2 * cos + x1 * sin) * scale).astype(BF16)

    for h in range(heads):
        rot_store(q_ref, 0, h, 1.0)
        rot_store(k_ref, heads * dk, h, k_scale)
    v0 = 2 * heads * dk
    g0 = v0 + heads * dv
    for h in range(heads):
        v_ref[:, h * dv:(h + 1) * dv] = jnp.dot(
            xn, w_ref[:, v0 + h * dv:v0 + (h + 1) * dv], preferred_element_type=F32).astype(BF16)
        gate_ref[:, h * dv:(h + 1) * dv] = jnp.dot(
            xn, w_ref[:, g0 + h * dv:g0 + (h + 1) * dv], preferred_element_type=F32).astype(BF16)


def _ret_in_proj(h, norm_g, w_bf16, cos, sin, seq, tm=256):
    T, D = h.shape
    heads = RET_HEADS
    dk = D // heads
    dv = 2 * dk
    n_cols = w_bf16.shape[1]
    n_pos_tiles = seq // tm
    est = 2 * D * n_cols * 2 + 2 * tm * D * 4 + 2 * tm * n_cols * 2 + 4 * tm * LANES * 4
    return pl.pallas_call(
        functools.partial(_ret_in_kernel, heads=heads, dk=dk, dv=dv),
        grid=(T // tm,),
        in_specs=[
            pl.BlockSpec((tm, D), lambda i: (i, 0)),
            pl.BlockSpec((1, D), lambda i: (0, 0)),
            pl.BlockSpec((D, n_cols), lambda i: (0, 0)),
            pl.BlockSpec((tm, dk // 2), lambda i: (i % n_pos_tiles, 0)),
            pl.BlockSpec((tm, dk // 2), lambda i: (i % n_pos_tiles, 0)),
        ],
        out_specs=[
            pl.BlockSpec((tm, heads * dk), lambda i: (i, 0)),
            pl.BlockSpec((tm, heads * dk), lambda i: (i, 0)),
            pl.BlockSpec((tm, heads * dv), lambda i: (i, 0)),
            pl.BlockSpec((tm, heads * dv), lambda i: (i, 0)),
        ],
        out_shape=[
            jax.ShapeDtypeStruct((T, heads * dk), BF16),
            jax.ShapeDtypeStruct((T, heads * dk), BF16),
            jax.ShapeDtypeStruct((T, heads * dv), BF16),
            jax.ShapeDtypeStruct((T, heads * dv), BF16),
        ],
        compiler_params=pltpu.CompilerParams(
            dimension_semantics=("arbitrary",), vmem_limit_bytes=_vmem_limit(est)),
        name="ret_in_proj",
    )(h, norm_g, w_bf16, cos, sin)


def _ret_core_kernel(q_ref, k_ref, v_ref, g_ref, dm_ref, qd_ref, kd_ref, cd_ref, y_ref, state_ref):
    @pl.when(pl.program_id(2) == 0)
    def _():
        state_ref[...] = jnp.zeros_like(state_ref)

    q = q_ref[...]
    k = k_ref[...]
    v = v_ref[...]
    s = lax.dot_general(q, k, (((1,), (1,)), ((), ())), preferred_element_type=F32) * dm_ref[...]
    o = jnp.dot(s.astype(BF16), v, preferred_element_type=F32)
    state = state_ref[...]
    qs = (q.astype(F32) * qd_ref[...]).astype(BF16)
    o = o + jnp.dot(qs, state.astype(BF16), preferred_element_type=F32)
    ks = (k.astype(F32) * kd_ref[...]).astype(BF16)
    state_ref[...] = state * cd_ref[...] + lax.dot_general(
        ks, v, (((0,), (0,)), ((), ())), preferred_element_type=F32)
    o = o * _rms_scale(o)
    y_ref[...] = (_silu(g_ref[...].astype(F32)) * o).astype(BF16)


def _ret_decay_tables(heads, chunk, dv):
    log_gamma = jnp.log1p(-jnp.power(2.0, -5.0 - jnp.arange(heads, dtype=F32)))
    idx = jnp.arange(chunk, dtype=F32)
    diff = idx[:, None] - idx[None, :]
    dmask = jnp.where(diff >= 0, jnp.exp(log_gamma[:, None, None] * jnp.maximum(diff, 0.0)), 0.0)
    qd = jnp.exp(log_gamma[:, None] * (idx + 1.0))[:, :, None]
    kd = jnp.exp(log_gamma[:, None] * (chunk - 1.0 - idx))[:, :, None]
    cd = jnp.broadcast_to(jnp.exp(log_gamma * chunk)[:, None, None], (heads, 1, dv))
    return dmask, qd, kd, cd


def _ret_core(q, k, v, gate, batch, seq):
    T = q.shape[0]
    heads = RET_HEADS
    dk = q.shape[1] // heads
    dv = v.shape[1] // heads
    cc = RET_KERNEL_CHUNK
    nc = seq // cc
    dmask, qd, kd, cd = _ret_decay_tables(heads, cc, dv)
    row = lambda b, h, c: (b * nc + c, h)
    tab = lambda b, h, c: (h, 0, 0)
    return pl.pallas_call(
        _ret_core_kernel,
        grid=(batch, heads, nc),
        in_specs=[
            pl.BlockSpec((cc, dk), row),
            pl.BlockSpec((cc, dk), row),
            pl.BlockSpec((cc, dv), row),
            pl.BlockSpec((cc, dv), row),
            pl.BlockSpec((None, cc, cc), tab),
            pl.BlockSpec((None, cc, 1), tab),
            pl.BlockSpec((None, cc, 1), tab),
            pl.BlockSpec((None, 1, dv), tab),
        ],
        out_specs=pl.BlockSpec((cc, dv), row),
        out_shape=jax.ShapeDtypeStruct((T, heads * dv), BF16),
        scratch_shapes=[pltpu.VMEM((dk, dv), F32)],
        compiler_params=pltpu.CompilerParams(dimension_semantics=("arbitrary", "arbitrary", "arbitrary")),
        name="ret_core",
    )(q, k, v, gate, dmask, qd, kd, cd)


def _proj_res_kernel(y_ref, w_ref, r_ref, o_ref):
    o_ref[...] = r_ref[...] + jnp.dot(y_ref[...], w_ref[...], preferred_element_type=F32)


def _proj_residual(y, w_bf16, resid, name, tm=512):
    T, K = y.shape
    D = w_bf16.shape[1]
    est = 2 * K * D * 2 + 2 * tm * K * 2 + 4 * tm * D * 4
    return pl.pallas_call(
        _proj_res_kernel,
        grid=(T // tm,),
        in_specs=[
            pl.BlockSpec((tm, K), lambda i: (i, 0)),
            pl.BlockSpec((K, D), lambda i: (0, 0)),
            pl.BlockSpec((tm, D), lambda i: (i, 0)),
        ],
        out_specs=pl.BlockSpec((tm, D), lambda i: (i, 0)),
        out_shape=jax.ShapeDtypeStruct((T, D), F32),
        compiler_params=pltpu.CompilerParams(
            dimension_semantics=("arbitrary",), vmem_limit_bytes=_vmem_limit(est)),
        name=name,
    )(y, w_bf16, resid)


def _router_kernel(h_ref, g_ref, w_ref, b_ref, meta_ref, *, groups, experts):
    x = h_ref[...]
    xn = x * _rms_scale(x) * g_ref[...]
    logits = jnp.dot(xn, w_ref[...], preferred_element_type=F32, precision=lax.Precision.HIGHEST) + b_ref[...]
    lane = lax.broadcasted_iota(jnp.int32, logits.shape, 1)
    neg = -jnp.inf
    big = jnp.int32(LANES)

    def argmax_first(vals):
        m = jnp.max(vals, axis=-1, keepdims=True)
        first = jnp.min(jnp.where(vals == m, lane, big), axis=-1, keepdims=True)
        return m, first

    gl = jnp.where(lane < groups, logits, neg)
    gmax, gidx = argmax_first(gl)
    gsum = jnp.sum(jnp.where(lane < groups, jnp.exp(gl - gmax), 0.0), axis=-1, keepdims=True)
    g_val = 1.0 / gsum
    e0 = groups + experts * gidx
    el = jnp.where((lane >= e0) & (lane < e0 + experts), logits, neg)
    m1, i1 = argmax_first(el)
    m2, i2 = argmax_first(jnp.where(lane == i1, neg, el))
    t = jnp.exp(m2 - m1)
    w1 = 1.0 / (1.0 + t)
    w2 = t / (1.0 + t)
    first_is_lo = i1 < i2
    lo = jnp.minimum(i1, i2) - e0
    hi = jnp.maximum(i1, i2) - e0
    w_lo = g_val * jnp.where(first_is_lo, w1, w2)
    w_hi = g_val * jnp.where(first_is_lo, w2, w1)
    pair = (lo * (7 - lo)) // 2 + (hi - lo - 1)
    cls = (gidx * N_PAIRS + pair).astype(F32)
    meta_ref[...] = jnp.where(lane == 0, cls, jnp.where(lane == 1, w_lo, jnp.where(lane == 2, w_hi, 0.0)))


def _router(h, norm_g, w_rg, b_rg, w_re, b_re, tm=512):
    T, D = h.shape
    G, E = N_GROUPS, EXPERTS_PER_GROUP
    n_used = G + G * E
    w = jnp.concatenate([w_rg, jnp.transpose(w_re, (1, 0, 2)).reshape(D, G * E)], axis=1)
    w = jnp.pad(w, ((0, 0), (0, LANES - n_used)))
    b = jnp.pad(jnp.concatenate([b_rg, b_re.reshape(G * E)]), (0, LANES - n_used))[None, :]
    return pl.pallas_call(
        functools.partial(_router_kernel, groups=G, experts=E),
        grid=(T // tm,),
        in_specs=[
            pl.BlockSpec((tm, D), lambda i: (i, 0)),
            pl.BlockSpec((1, D), lambda i: (0, 0)),
            pl.BlockSpec((D, LANES), lambda i: (0, 0)),
            pl.BlockSpec((1, LANES), lambda i: (0, 0)),
        ],
        out_specs=pl.BlockSpec((tm, LANES), lambda i: (i, 0)),
        out_shape=jax.ShapeDtypeStruct((T, LANES), F32),
        compiler_params=pltpu.CompilerParams(dimension_semantics=("arbitrary",)),
        name="router",
    )(h, norm_g, w, b)


def _route_schedule(meta, tm, n_tiles):
    T = meta.shape[0]
    cls = meta[:, 0].astype(jnp.int32)
    counts = jnp.zeros((N_CLASSES,), jnp.int32).at[cls].add(1)
    tiles_per = (counts + tm - 1) // tm
    tile_end = jnp.cumsum(tiles_per)
    tile_begin = tile_end - tiles_per
    cstart = jnp.cumsum(counts) - counts
    order = jnp.argsort(cls, stable=True).astype(jnp.int32)
    scls = cls[order]
    ppos = jnp.arange(T, dtype=jnp.int32) - cstart[scls] + tile_begin[scls] * tm
    row_token = jnp.zeros((n_tiles * tm,), jnp.int32).at[ppos].set(order)
    t = jnp.arange(n_tiles, dtype=jnp.int32)
    live = t < tile_end[-1]
    tcls = jnp.searchsorted(tile_end, jnp.where(live, t, tile_end[-1] - 1), side="right").astype(jnp.int32)
    nvalid = jnp.where(live, jnp.clip(counts[tcls] - (t - tile_begin[tcls]) * tm, 0, tm), 0).astype(jnp.int32)
    pair = tcls % N_PAIRS
    tile_g = tcls // N_PAIRS
    tile_lo = jnp.asarray(PAIR_LO, jnp.int32)[pair]
    tile_hi = jnp.asarray(PAIR_HI, jnp.int32)[pair]
    w_sorted = meta[row_token, 1:3]
    return tile_g, tile_lo, tile_hi, nvalid, row_token, w_sorted


def _moe_kernel(tg_ref, tlo_ref, thi_ref, tnv_ref, rt_ref,
                h_hbm, ws_ref, gn_ref, fn_ref, wg_lo, wu_lo, wd_lo, wg_hi, wu_hi, wd_hi,
                out_hbm, xbuf, obuf, gsem, ssem, *, tm, final_norm):
    del tg_ref, tlo_ref, thi_ref
    i = pl.program_id(0)
    n = pl.num_programs(0)
    slot = i % 2

    def gather_copy(t, r, sl):
        tok = rt_ref[t * tm + r]
        return pltpu.make_async_copy(h_hbm.at[pl.ds(tok, 1)], xbuf.at[sl, pl.ds(r, 1)], gsem.at[sl])

    def scatter_copy(t, r, sl):
        tok = rt_ref[t * tm + r]
        return pltpu.make_async_copy(obuf.at[sl, pl.ds(r, 1)], out_hbm.at[pl.ds(tok, 1)], ssem.at[sl])

    def for_rows(count, fn):
        def body(r, c):
            fn(r)
            return c
        lax.fori_loop(0, count, body, 0)

    def gather_rows(t):
        return jnp.where(tnv_ref[t] > 0, tm, 0)

    def start_gather(t, sl):
        for_rows(gather_rows(t), lambda r: gather_copy(t, r, sl).start())

    def wait_gather(t, sl):
        for_rows(gather_rows(t), lambda r: gather_copy(t, r, sl).wait())

    def start_scatter(t, sl):
        for_rows(tnv_ref[t], lambda r: scatter_copy(t, r, sl).start())

    def wait_scatter(t, sl):
        for_rows(tnv_ref[t], lambda r: scatter_copy(t, r, sl).wait())

    @pl.when(i == 0)
    def _():
        start_gather(0, 0)

    @pl.when(i + 1 < n)
    def _():
        start_gather(i + 1, 1 - slot)

    wait_gather(i, slot)

    @pl.when(i >= 2)
    def _():
        wait_scatter(i - 2, slot)

    @pl.when(tnv_ref[i] > 0)
    def _():
        x = xbuf[slot]
        xn = (x * _rms_scale(x) * gn_ref[...]).astype(BF16)

        def expert(wg, wu, wd, w):
            a = jnp.dot(xn, wg[...], preferred_element_type=F32)
            b = jnp.dot(xn, wu[...], preferred_element_type=F32)
            return jnp.dot((_silu(a) * b * w).astype(BF16), wd[...], preferred_element_type=F32)

        y = expert(wg_lo, wu_lo, wd_lo, ws_ref[:, 0:1]) + expert(wg_hi, wu_hi, wd_hi, ws_ref[:, 1:2])
        o = x + y
        if final_norm:
            o = o * _rms_scale(o) * fn_ref[...]
        obuf[slot] = o

    start_scatter(i, slot)

    @pl.when(i == n - 1)
    def _():
        @pl.when(i >= 1)
        def _():
            wait_scatter(i - 1, 1 - slot)
        wait_scatter(i, slot)


def _moe_layer(h, ffn_norm_g, final_norm_g, meta, wg, wu, wd, final_norm):
    T, D = h.shape
    F = wg.shape[-1]
    tm = MOE_TILE
    n_tiles = T // tm + N_CLASSES
    tile_g, tile_lo, tile_hi, nvalid, row_token, w_sorted = _route_schedule(meta, tm, n_tiles)

    def w_spec(shape, which):
        def index(i, tg, tlo, thi, tnv, rt):
            return (tg[i], (tlo if which == 0 else thi)[i], 0, 0)
        return pl.BlockSpec((None, None) + shape, index)

    est = 2 * 6 * D * F * 2 + 4 * tm * D * 4 + 8 * tm * F * 4
    return pl.pallas_call(
        functools.partial(_moe_kernel, tm=tm, final_norm=final_norm),
        grid_spec=pltpu.PrefetchScalarGridSpec(
            num_scalar_prefetch=5,
            grid=(n_tiles,),
            in_specs=[
                pl.BlockSpec(memory_space=pl.ANY),
                pl.BlockSpec((tm, 2), lambda i, *_: (i, 0)),
                pl.BlockSpec((1, D), lambda i, *_: (0, 0)),
                pl.BlockSpec((1, D), lambda i, *_: (0, 0)),
                w_spec((D, F), 0), w_spec((D, F), 0), w_spec((F, D), 0),
                w_spec((D, F), 1), w_spec((D, F), 1), w_spec((F, D), 1),
            ],
            out_specs=pl.BlockSpec(memory_space=pl.ANY),
            scratch_shapes=[
                pltpu.VMEM((2, tm, D), F32),
                pltpu.VMEM((2, tm, D), F32),
                pltpu.SemaphoreType.DMA((2,)),
                pltpu.SemaphoreType.DMA((2,)),
            ],
        ),
        out_shape=jax.ShapeDtypeStruct((T, D), F32),
        compiler_params=pltpu.CompilerParams(
            dimension_semantics=("arbitrary",), vmem_limit_bytes=_vmem_limit(est)),
        name="moe_experts",
    )(tile_g, tile_lo, tile_hi, nvalid, row_token,
      h, w_sorted, ffn_norm_g, final_norm_g, wg, wu, wd, wg, wu, wd)


def _partial_rotary(xh, cosf, sinp, sinm, half):
    lanes = xh.shape[-1]
    return xh * cosf + pltpu.roll(xh, half, 1) * sinp + pltpu.roll(xh, lanes - half, 1) * sinm


def _kvq_kernel(h_ref, gkv_ref, gq_ref, wkv_ref, wq_ref, cosf_ref, sinp_ref, sinm_ref,
                k_ref, v_ref, q_ref, km_ref, *, heads, dh, half):
    x = h_ref[...]
    xr = x * _rms_scale(x)
    xkv = (xr * gkv_ref[...]).astype(BF16)
    xq = (xr * gq_ref[...]).astype(BF16)
    cosf = cosf_ref[...]
    sinp = sinp_ref[...]
    sinm = sinm_ref[...]
    width = heads * dh
    q_scale = dh ** -0.5
    for hh in range(heads):
        cols = slice(hh * dh, (hh + 1) * dh)
        kh = jnp.dot(xkv, wkv_ref[:, hh * dh:(hh + 1) * dh], preferred_element_type=F32)
        kh = _partial_rotary(kh, cosf, sinp, sinm, half)
        k_ref[:, cols] = kh.astype(BF16)
        km_ref[:, cols] = jnp.mean(kh, axis=0, keepdims=True)
        v_ref[:, cols] = jnp.dot(
            xkv, wkv_ref[:, width + hh * dh:width + (hh + 1) * dh], preferred_element_type=F32).astype(BF16)
        qh = jnp.dot(xq, wq_ref[:, cols], preferred_element_type=F32)
        q_ref[:, cols] = (_partial_rotary(qh, cosf, sinp, sinm, half) * q_scale).astype(BF16)


def _rope_tables(seq, dh):
    rot = dh // ROPE_FRACTION
    half = rot // 2
    inv = 1.0 / (ROPE_THETA ** (jnp.arange(half, dtype=F32) / half))
    ang = jnp.arange(seq, dtype=F32)[:, None] * inv[None, :]
    cos = jnp.cos(ang)
    sin = jnp.sin(ang)
    zeros = jnp.zeros((seq, dh - rot), F32)
    z_half = jnp.zeros((seq, half), F32)
    cosf = jnp.concatenate([cos, cos, jnp.ones((seq, dh - rot), F32)], axis=1)
    sinp = jnp.concatenate([z_half, sin, zeros], axis=1)
    sinm = jnp.concatenate([-sin, z_half, zeros], axis=1)
    return cosf, sinp, sinm, half


def _kvq_proj(h, kv_norm_g, q_norm_g, wkv_bf16, wq_bf16, seq):
    T, D = h.shape
    heads = ATT_HEADS
    dh = D // heads
    tm = MOBA_BLOCK
    n_pos_tiles = seq // tm
    cosf, sinp, sinm, half = _rope_tables(seq, dh)
    width = heads * dh
    est = 2 * (D * 2 * width + D * width) * 2 + 2 * tm * D * 4 + 6 * tm * width * 2 + 8 * tm * dh * 4
    pos = lambda i: (i % n_pos_tiles, 0)
    row = lambda i: (i, 0)
    const = lambda i: (0, 0)
    return pl.pallas_call(
        functools.partial(_kvq_kernel, heads=heads, dh=dh, half=half),
        grid=(T // tm,),
        in_specs=[
            pl.BlockSpec((tm, D), row),
            pl.BlockSpec((1, D), const),
            pl.BlockSpec((1, D), const),
            pl.BlockSpec((D, 2 * width), const),
            pl.BlockSpec((D, width), const),
            pl.BlockSpec((tm, dh), pos),
            pl.BlockSpec((tm, dh), pos),
            pl.BlockSpec((tm, dh), pos),
        ],
        out_specs=[
            pl.BlockSpec((tm, width), row),
            pl.BlockSpec((tm, width), row),
            pl.BlockSpec((tm, width), row),
            pl.BlockSpec((None, 1, width), lambda i: (i, 0, 0)),
        ],
        out_shape=[
            jax.ShapeDtypeStruct((T, width), BF16),
            jax.ShapeDtypeStruct((T, width), BF16),
            jax.ShapeDtypeStruct((T, width), BF16),
            jax.ShapeDtypeStruct((T // tm, 1, width), F32),
        ],
        compiler_params=pltpu.CompilerParams(
            dimension_semantics=("arbitrary",), vmem_limit_bytes=_vmem_limit(est)),
        name="kvq_proj",
    )(h, kv_norm_g, q_norm_g, wkv_bf16, wq_bf16, cosf, sinp, sinm)


def _moba_kernel(q_ref, k_ref, v_ref, km_ref, o_ref, *, blk, topk):
    qi = pl.program_id(2)
    q = q_ref[...]
    n_blk = km_ref.shape[0]
    gate = lax.dot_general(q.astype(F32), km_ref[...], (((1,), (1,)), ((), ())),
                           preferred_element_type=F32, precision=lax.Precision.HIGHEST)
    lane = lax.broadcasted_iota(jnp.int32, gate.shape, 1)
    neg = -jnp.inf
    g = jnp.where(lane < qi, gate, neg)
    selm = jnp.zeros(gate.shape, F32)
    for _ in range(topk):
        m = jnp.max(g, axis=-1, keepdims=True)
        first = jnp.min(jnp.where(g == m, lane, n_blk), axis=-1, keepdims=True)
        pick = (lane == first) & (m > neg)
        selm = jnp.where(pick, 1.0, selm)
        g = jnp.where(lane == first, neg, g)

    def scores(j):
        start = pl.multiple_of(j * blk, blk)
        kj = k_ref[pl.ds(start, blk), :]
        vj = v_ref[pl.ds(start, blk), :]
        return lax.dot_general(q, kj, (((1,), (1,)), ((), ())), preferred_element_type=F32), vj

    s, vd = scores(qi)
    r_id = lax.broadcasted_iota(jnp.int32, s.shape, 0)
    c_id = lax.broadcasted_iota(jnp.int32, s.shape, 1)
    s = jnp.where(c_id <= r_id, s, neg)
    m0 = jnp.max(s, axis=-1, keepdims=True)
    p = jnp.exp(s - m0)
    l0 = jnp.sum(p, axis=-1, keepdims=True)
    acc0 = jnp.dot(p.astype(BF16), vd, preferred_element_type=F32)

    def body(j, carry):
        m, l, acc = carry
        s, vj = scores(j)
        chosen = jnp.sum(jnp.where(lane == j, selm, 0.0), axis=-1, keepdims=True)
        s = jnp.where(chosen > 0.0, s, neg)
        m_new = jnp.maximum(m, jnp.max(s, axis=-1, keepdims=True))
        alpha = jnp.exp(m - m_new)
        p = jnp.exp(s - m_new)
        l = alpha * l + jnp.sum(p, axis=-1, keepdims=True)
        acc = alpha * acc + jnp.dot(p.astype(BF16), vj, preferred_element_type=F32)
        return m_new, l, acc

    _, l, acc = lax.fori_loop(0, qi, body, (m0, l0, acc0))
    o_ref[...] = (acc / l).astype(BF16)


def _moba_attention(q, k, v, k_mean, batch, seq):
    T, width = q.shape
    heads = ATT_HEADS
    dh = width // heads
    blk = MOBA_BLOCK
    n_blk = seq // blk
    topk = min(MOBA_TOPK, n_blk)
    km = k_mean.reshape(batch, n_blk, width)
    qrow = lambda b, h, i: (b * n_blk + i, h)
    full = lambda b, h, i: (b, h)
    est = 4 * seq * dh * 2 + 8 * blk * dh * 4 + 16 * blk * blk * 4
    return pl.pallas_call(
        functools.partial(_moba_kernel, blk=blk, topk=topk),
        grid=(batch, heads, n_blk),
        in_specs=[
            pl.BlockSpec((blk, dh), qrow),
            pl.BlockSpec((seq, dh), full),
            pl.BlockSpec((seq, dh), full),
            pl.BlockSpec((None, n_blk, dh), lambda b, h, i: (b, 0, h)),
        ],
        out_specs=pl.BlockSpec((blk, dh), qrow),
        out_shape=jax.ShapeDtypeStruct((T, width), BF16),
        compiler_params=pltpu.CompilerParams(
            dimension_semantics=("arbitrary", "arbitrary", "arbitrary"), vmem_limit_bytes=_vmem_limit(est)),
        name="moba_attention",
    )(q, k, v, km)


def _ret_rope_tables(seq, dk):
    half = dk // 2
    inv = 1.0 / (RET_ROT_BASE ** (jnp.arange(half, dtype=F32) / half))
    ang = jnp.arange(seq, dtype=F32)[:, None] * inv[None, :]
    return jnp.cos(ang), jnp.sin(ang)


def kernel(x, ret_norm, ret_w_in, ret_w_out, kv_norm, w_kv, attn_norm, w_q, w_o, ffn_norm, router_group_w, router_group_b, router_expert_w, router_expert_b, expert_w_gate, expert_w_up, expert_w_down, final_norm):
    B, S, D = x.shape
    T = B * S
    assert S % MOBA_BLOCK == 0 and S % RET_KERNEL_CHUNK == 0 and T % MOE_TILE == 0
    assert ret_norm.shape[0] == 1 and attn_norm.shape[0] == 1 and ffn_norm.shape[0] == 2
    h = x.reshape(T, D)
    final_g = final_norm[None, :]

    def moe(h, layer, last):
        meta = _router(h, ffn_norm[layer][None, :], router_group_w[layer], router_group_b[layer],
                       router_expert_w[layer], router_expert_b[layer])
        return _moe_layer(h, ffn_norm[layer][None, :], final_g, meta,
                          expert_w_gate[layer].astype(BF16), expert_w_up[layer].astype(BF16),
                          expert_w_down[layer].astype(BF16), final_norm=last)

    cos, sin = _ret_rope_tables(S, D // RET_HEADS)
    q, k, v, gate = _ret_in_proj(h, ret_norm[0][None, :], ret_w_in[0].astype(BF16), cos, sin, S)
    y = _ret_core(q, k, v, gate, B, S)
    h = _proj_residual(y, ret_w_out[0].astype(BF16), h, "ret_out_proj")
    h = moe(h, 0, False)

    k2, v2, q2, k_mean = _kvq_proj(h, kv_norm[None, :], attn_norm[0][None, :],
                                   w_kv.astype(BF16), w_q[0].astype(BF16), S)
    o = _moba_attention(q2, k2, v2, k_mean, B, S)
    h = _proj_residual(o, w_o[0].astype(BF16), h, "attn_out_proj")
    h = moe(h, 1, True)
    return h.reshape(B, S, D)
```

```python
import functools

import jax
import jax.numpy as jnp
from jax import lax
from jax.experimental import pallas as pl
from jax.experimental.pallas import tpu as pltpu

F32 = jnp.float32
BF16 = jnp.bfloat16

NORM_EPS = 1e-6
RET_HEADS = 4
RET_ROT_BASE = 10000.0
ATT_HEADS = 8
ROPE_FRACTION = 4
ROPE_THETA = 500000.0
MOBA_BLOCK = 256
MOBA_TOPK = 3
N_GROUPS = 4
EXPERTS_PER_GROUP = 4
PAIR_LO = (0, 0, 0, 1, 1, 2)
PAIR_HI = (1, 2, 3, 2, 3, 3)
N_PAIRS = len(PAIR_LO)
N_CLASSES = N_GROUPS * N_PAIRS

LANES = 128
SUBLANES = 8
RET_KERNEL_CHUNK = 256
MOE_TILE = 128
ROUTE_ROWS = 32
DMA_ISSUE_UNROLL = 8
MOBA_HEAD_GROUP = 4
MOBA_KEY_CHUNK = 64
LOG2_E = 1.4426950408889634
V7X_VMEM_BYTES = 64 * 1024 * 1024


def _vmem_limit(estimate_bytes):
    return int(min(V7X_VMEM_BYTES - 8 * 1024 * 1024, max(32 * 1024 * 1024, estimate_bytes * 5 // 4)))


def _rms_scale(x):
    return lax.rsqrt(jnp.mean(x * x, axis=-1, keepdims=True) + NORM_EPS)


def _silu(a):
    return a * jax.nn.sigmoid(a)


def _ret_in_kernel(x_ref, g_ref, w_ref, cos_ref, sin_ref, q_ref, k_ref, v_ref, gate_ref, *, heads, dk, dv):
    x = x_ref[...]
    xn = (x * _rms_scale(x) * g_ref[...]).astype(BF16)
    cos = cos_ref[...]
    sin = sin_ref[...]
    half = dk // 2
    k_scale = dk ** -0.5

    def rot_store(dst_ref, col0, h, scale):
        p = jnp.dot(xn, w_ref[:, col0 + h * dk:col0 + (h + 1) * dk], preferred_element_type=F32)
        x1 = p[:, :half]
        x2 = p[:, half:]
        dst_ref[:, h * dk:h * dk + half] = ((x1 * cos - x2 * sin) * scale).astype(BF16)
        dst_ref[:, h * dk + half:(h + 1) * dk] = ((x2 * cos + x1 * sin) * scale).astype(BF16)

    for h in range(heads):
        rot_store(q_ref, 0, h, 1.0)
        rot_store(k_ref, heads * dk, h, k_scale)
    v0 = 2 * heads * dk
    g0 = v0 + heads * dv
    for h in range(heads):
        v_ref[:, h * dv:(h + 1) * dv] = jnp.dot(
            xn, w_ref[:, v0 + h * dv:v0 + (h + 1) * dv], preferred_element_type=F32).astype(BF16)
        gate_ref[:, h * dv:(h + 1) * dv] = jnp.dot(
            xn, w_ref[:, g0 + h * dv:g0 + (h + 1) * dv], preferred_element_type=F32).astype(BF16)


def _ret_in_proj(h, norm_g, w_bf16, cos, sin, seq, tm=256):
    T, D = h.shape
    heads = RET_HEADS
    dk = D // heads
    dv = 2 * dk
    n_cols = w_bf16.shape[1]
    n_pos_tiles = seq // tm
    est = 2 * D * n_cols * 2 + 2 * tm * D * 4 + 2 * tm * n_cols * 2 + 4 * tm * LANES * 4
    return pl.pallas_call(
        functools.partial(_ret_in_kernel, heads=heads, dk=dk, dv=dv),
        grid=(T // tm,),
        in_specs=[
            pl.BlockSpec((tm, D), lambda i: (i, 0)),
            pl.BlockSpec((1, D), lambda i: (0, 0)),
            pl.BlockSpec((D, n_cols), lambda i: (0, 0)),
            pl.BlockSpec((tm, dk // 2), lambda i: (i % n_pos_tiles, 0)),
            pl.BlockSpec((tm, dk // 2), lambda i: (i % n_pos_tiles, 0)),
        ],
        out_specs=[
            pl.BlockSpec((tm, heads * dk), lambda i: (i, 0)),
            pl.BlockSpec((tm, heads * dk), lambda i: (i, 0)),
            pl.BlockSpec((tm, heads * dv), lambda i: (i, 0)),
            pl.BlockSpec((tm, heads * dv), lambda i: (i, 0)),
        ],
        out_shape=[
            jax.ShapeDtypeStruct((T, heads * dk), BF16),
            jax.ShapeDtypeStruct((T, heads * dk), BF16),
            jax.ShapeDtypeStruct((T, heads * dv), BF16),
            jax.ShapeDtypeStruct((T, heads * dv), BF16),
        ],
        compiler_params=pltpu.CompilerParams(
            dimension_semantics=("arbitrary",), vmem_limit_bytes=_vmem_limit(est)),
        name="ret_in_proj",
    )(h, norm_g, w_bf16, cos, sin)


def _ret_core_kernel(q_ref, k_ref, v_ref, g_ref, dm_ref, qd_ref, kd_ref, cd_ref, y_ref, state_ref):
    @pl.when(pl.program_id(2) == 0)
    def _():
        state_ref[...] = jnp.zeros_like(state_ref)

    q = q_ref[...]
    k = k_ref[...]
    v = v_ref[...]
    s = lax.dot_general(q, k, (((1,), (1,)), ((), ())), preferred_element_type=F32) * dm_ref[...]
    o = jnp.dot(s.astype(BF16), v, preferred_element_type=F32)
    state = state_ref[...]
    qs = (q.astype(F32) * qd_ref[...]).astype(BF16)
    o = o + jnp.dot(qs, state.astype(BF16), preferred_element_type=F32)
    ks = (k.astype(F32) * kd_ref[...]).astype(BF16)
    state_ref[...] = state * cd_ref[...] + lax.dot_general(
        ks, v, (((0,), (0,)), ((), ())), preferred_element_type=F32)
    o = o * _rms_scale(o)
    y_ref[...] = (_silu(g_ref[...].astype(F32)) * o).astype(BF16)


def _ret_decay_tables(heads, chunk, dv):
    log_gamma = jnp.log1p(-jnp.power(2.0, -5.0 - jnp.arange(heads, dtype=F32)))
    idx = jnp.arange(chunk, dtype=F32)
    diff = idx[:, None] - idx[None, :]
    dmask = jnp.where(diff >= 0, jnp.exp(log_gamma[:, None, None] * jnp.maximum(diff, 0.0)), 0.0)
    qd = jnp.exp(log_gamma[:, None] * (idx + 1.0))[:, :, None]
    kd = jnp.exp(log_gamma[:, None] * (chunk - 1.0 - idx))[:, :, None]
    cd = jnp.broadcast_to(jnp.exp(log_gamma * chunk)[:, None, None], (heads, 1, dv))
    return dmask, qd, kd, cd


def _ret_core(q, k, v, gate, batch, seq):
    T = q.shape[0]
    heads = RET_HEADS
    dk = q.shape[1] // heads
    dv = v.shape[1] // heads
    cc = RET_KERNEL_CHUNK
    nc = seq // cc
    dmask, qd, kd, cd = _ret_decay_tables(heads, cc, dv)
    row = lambda b, h, c: (b * nc + c, h)
    tab = lambda b, h, c: (h, 0, 0)
    return pl.pallas_call(
        _ret_core_kernel,
        grid=(batch, heads, nc),
        in_specs=[
            pl.BlockSpec((cc, dk), row),
            pl.BlockSpec((cc, dk), row),
            pl.BlockSpec((cc, dv), row),
            pl.BlockSpec((cc, dv), row),
            pl.BlockSpec((None, cc, cc), tab),
            pl.BlockSpec((None, cc, 1), tab),
            pl.BlockSpec((None, cc, 1), tab),
            pl.BlockSpec((None, 1, dv), tab),
        ],
        out_specs=pl.BlockSpec((cc, dv), row),
        out_shape=jax.ShapeDtypeStruct((T, heads * dv), BF16),
        scratch_shapes=[pltpu.VMEM((dk, dv), F32)],
        compiler_params=pltpu.CompilerParams(dimension_semantics=("arbitrary", "arbitrary", "arbitrary")),
        name="ret_core",
    )(q, k, v, gate, dmask, qd, kd, cd)


def _proj_router_kernel(y_ref, w_ref, r_ref, g_ref, wr_ref, br_ref, haug_ref, route_ref, counts_ref, run_ref,
                        *, groups, experts, d_model):
    @pl.when(pl.program_id(0) == 0)
    def _():
        run_ref[...] = jnp.zeros_like(run_ref)

    h = r_ref[...] + jnp.dot(y_ref[...], w_ref[...], preferred_element_type=F32)
    xn = h * _rms_scale(h) * g_ref[...]
    logits = jnp.dot(xn, wr_ref[...], preferred_element_type=F32, precision=lax.Precision.HIGHEST) + br_ref[...]
    lt = logits.T[:ROUTE_ROWS]
    tm = lt.shape[1]
    row = lax.broadcasted_iota(jnp.int32, lt.shape, 0)
    neg = -jnp.inf

    def argmax_first(vals):
        m = jnp.max(vals, axis=0, keepdims=True)
        first = jnp.min(jnp.where(vals == m, row, ROUTE_ROWS), axis=0, keepdims=True)
        return m, first

    gl = jnp.where(row < groups, lt, neg)
    gmax, gidx = argmax_first(gl)
    gsum = jnp.sum(jnp.where(row < groups, jnp.exp(gl - gmax), 0.0), axis=0, keepdims=True)
    g_val = 1.0 / gsum
    e0 = groups + experts * gidx
    el = jnp.where((row >= e0) & (row < e0 + experts), lt, neg)
    m1, i1 = argmax_first(el)
    m2, i2 = argmax_first(jnp.where(row == i1, neg, el))
    t = jnp.exp(m2 - m1)
    w1 = 1.0 / (1.0 + t)
    w2 = t / (1.0 + t)
    first_is_lo = i1 < i2
    lo = jnp.minimum(i1, i2) - e0
    hi = jnp.maximum(i1, i2) - e0
    w_lo = g_val * jnp.where(first_is_lo, w1, w2)
    w_hi = g_val * jnp.where(first_is_lo, w2, w1)
    pair = (lo * (7 - lo)) // 2 + (hi - lo - 1)
    cls = gidx * N_PAIRS + pair

    onehot = row == cls
    oh = jnp.where(onehot, 1.0, 0.0)
    before = lax.broadcasted_iota(jnp.int32, (tm, tm), 0) < lax.broadcasted_iota(jnp.int32, (tm, tm), 1)
    prefix = jnp.dot(oh.astype(BF16), jnp.where(before, 1.0, 0.0).astype(BF16), preferred_element_type=F32)
    rank = jnp.sum(jnp.where(onehot, prefix + run_ref[:, 0:1], 0.0), axis=0, keepdims=True)
    run_new = run_ref[...] + jnp.sum(oh, axis=1, keepdims=True)
    run_ref[...] = run_new
    counts_ref[...] = run_new

    r8 = lax.broadcasted_iota(jnp.int32, route_ref.shape, 0)
    route_ref[...] = jnp.where(r8 == 0, cls, jnp.where(r8 == 1, rank.astype(jnp.int32), 0))
    rl = lax.broadcasted_iota(jnp.int32, (LANES, tm), 0)
    meta_t = jnp.where(rl == 0, w_lo, jnp.where(rl == 1, w_hi, 0.0))
    haug_ref[:, :d_model] = h
    haug_ref[:, d_model:] = meta_t.T


def _proj_router(y, w_bf16, resid, norm_g, w_rg, b_rg, w_re, b_re, name, tm=512):
    T, K = y.shape
    D = w_bf16.shape[1]
    G, E = N_GROUPS, EXPERTS_PER_GROUP
    n_used = G + G * E
    wr = jnp.concatenate([w_rg, jnp.transpose(w_re, (1, 0, 2)).reshape(D, G * E)], axis=1)
    wr = jnp.pad(wr, ((0, 0), (0, LANES - n_used)))
    br = jnp.pad(jnp.concatenate([b_rg, b_re.reshape(G * E)]), (0, LANES - n_used))[None, :]
    est = 2 * K * D * 2 + 2 * tm * K * 2 + 6 * tm * D * 4 + 2 * D * LANES * 4 + 4 * tm * tm * 4
    const = lambda i: (0, 0)
    return pl.pallas_call(
        functools.partial(_proj_router_kernel, groups=G, experts=E, d_model=D),
        grid=(T // tm,),
        in_specs=[
            pl.BlockSpec((tm, K), lambda i: (i, 0)),
            pl.BlockSpec((K, D), const),
            pl.BlockSpec((tm, D), lambda i: (i, 0)),
            pl.BlockSpec((1, D), const),
            pl.BlockSpec((D, LANES), const),
            pl.BlockSpec((1, LANES), const),
        ],
        out_specs=[
            pl.BlockSpec((tm, D + LANES), lambda i: (i, 0)),
            pl.BlockSpec((8, tm), lambda i: (0, i)),
            pl.BlockSpec((ROUTE_ROWS, LANES), const),
        ],
        out_shape=[
            jax.ShapeDtypeStruct((T, D + LANES), F32),
            jax.ShapeDtypeStruct((8, T), jnp.int32),
            jax.ShapeDtypeStruct((ROUTE_ROWS, LANES), F32),
        ],
        scratch_shapes=[pltpu.VMEM((ROUTE_ROWS, LANES), F32)],
        compiler_params=pltpu.CompilerParams(
            dimension_semantics=("arbitrary",), vmem_limit_bytes=_vmem_limit(est)),
        name=name,
    )(y, w_bf16, resid, norm_g, wr, br)


def _tile_schedule(counts_blk, tm, n_tiles):
    counts = counts_blk[:N_CLASSES, 0].astype(jnp.int32)
    tiles_per = (counts + tm - 1) // tm
    tile_end = jnp.cumsum(tiles_per)
    tile_begin = tile_end - tiles_per
    cstart = jnp.cumsum(counts) - counts
    t = jnp.arange(n_tiles, dtype=jnp.int32)
    live = t < tile_end[-1]
    tq = jnp.where(live, t, tile_end[-1] - 1)
    tcls = jnp.sum((tile_end[None, :] <= tq[:, None]).astype(jnp.int32), axis=1)
    k = tq - tile_begin[tcls]
    nvalid = jnp.where(live, jnp.clip(counts[tcls] - k * tm, 0, tm), 0).astype(jnp.int32)
    base = (cstart[tcls] + k * tm).astype(jnp.int32)
    pair = tcls % N_PAIRS
    tile_g = (tcls // N_PAIRS).astype(jnp.int32)
    tile_lo = jnp.asarray(PAIR_LO, jnp.int32)[pair]
    tile_hi = jnp.asarray(PAIR_HI, jnp.int32)[pair]
    cstart_pad = jnp.pad(cstart, (0, ROUTE_ROWS - N_CLASSES)).astype(jnp.int32)
    return tile_g, tile_lo, tile_hi, nvalid, base, cstart_pad


def _moe_kernel(tg_ref, tlo_ref, thi_ref, tnv_ref, tbase_ref, cs_ref, cls_ref, rank_ref,
                h_hbm, gn_ref, fn_ref, wg_lo, wu_lo, wd_lo, wg_hi, wu_hi, wd_hi,
                out_hbm, rt_ref, xbuf, obuf, gsem, ssem, *, tm, d_model, final_norm):
    del tg_ref, tlo_ref, thi_ref
    i = pl.program_id(0)
    n = pl.num_programs(0)
    slot = i % 2
    n_tok = cls_ref.shape[0]

    def for_rows(count, fn):
        def group(gi, c):
            for u in range(DMA_ISSUE_UNROLL):
                fn(gi * DMA_ISSUE_UNROLL + u)
            return c

        def single(r, c):
            fn(r)
            return c
        n_groups = count // DMA_ISSUE_UNROLL
        lax.fori_loop(0, n_groups, group, 0)
        lax.fori_loop(n_groups * DMA_ISSUE_UNROLL, count, single, 0)

    def token(t, r):
        return rt_ref[tbase_ref[t] + r]

    def gather_copy(t, r, sl):
        return pltpu.make_async_copy(h_hbm.at[pl.ds(token(t, r), 1)], xbuf.at[sl, pl.ds(r, 1)], gsem.at[sl])

    def scatter_copy(t, r, sl):
        return pltpu.make_async_copy(obuf.at[sl, pl.ds(r, 1)], out_hbm.at[pl.ds(token(t, r), 1)], ssem.at[sl])

    def gather_block(sl, n_rows):
        rows = pl.ds(0, n_rows)
        return pltpu.make_async_copy(h_hbm.at[rows], xbuf.at[sl, rows], gsem.at[sl])

    def scatter_block(sl, n_rows):
        rows = pl.ds(0, n_rows)
        return pltpu.make_async_copy(obuf.at[sl, rows], out_hbm.at[rows], ssem.at[sl])

    def wait_rows(t, sl, block_copy, row_copy):
        count = tnv_ref[t]
        aligned = pl.multiple_of((count // SUBLANES) * SUBLANES, SUBLANES)

        @pl.when(aligned > 0)
        def _():
            block_copy(sl, aligned).wait()

        def single(r, c):
            row_copy(t, r, sl).wait()
            return c
        lax.fori_loop(aligned, count, single, 0)

    def start_gather(t, sl):
        for_rows(tnv_ref[t], lambda r: gather_copy(t, r, sl).start())

    def start_scatter(t, sl):
        for_rows(tnv_ref[t], lambda r: scatter_copy(t, r, sl).start())

    def wait_gather(t, sl):
        wait_rows(t, sl, gather_block, gather_copy)

    def wait_scatter(t, sl):
        wait_rows(t, sl, scatter_block, scatter_copy)

    @pl.when(i == 0)
    def _():
        xbuf[...] = jnp.zeros_like(xbuf)

        def place(tok):
            rt_ref[cs_ref[cls_ref[tok]] + rank_ref[tok]] = tok
        for_rows(n_tok, place)
        start_gather(0, 0)

    @pl.when(i + 1 < n)
    def _():
        start_gather(i + 1, 1 - slot)

    wait_gather(i, slot)

    @pl.when(i >= 2)
    def _():
        wait_scatter(i - 2, slot)

    @pl.when(tnv_ref[i] > 0)
    def _():
        xa = xbuf[slot]
        x = xa[:, :d_model]
        xn = (x * _rms_scale(x) * gn_ref[...]).astype(BF16)

        def expert(wg, wu, wd, w):
            a = jnp.dot(xn, wg[...], preferred_element_type=F32)
            b = jnp.dot(xn, wu[...], preferred_element_type=F32)
            return jnp.dot((_silu(a) * b * w).astype(BF16), wd[...], preferred_element_type=F32)

        y = (expert(wg_lo, wu_lo, wd_lo, xa[:, d_model:d_model + 1])
             + expert(wg_hi, wu_hi, wd_hi, xa[:, d_model + 1:d_model + 2]))
        o = x + y
        if final_norm:
            o = o * _rms_scale(o) * fn_ref[...]
        obuf[slot] = o

    start_scatter(i, slot)

    @pl.when(i == n - 1)
    def _():
        @pl.when(i >= 1)
        def _():
            wait_scatter(i - 1, 1 - slot)
        wait_scatter(i, slot)


def _moe_layer(h_aug, route, counts_blk, ffn_norm_g, final_norm_g, wg, wu, wd, final_norm):
    T, DA = h_aug.shape
    D = DA - LANES
    F = wg.shape[-1]
    tm = MOE_TILE
    n_tiles = T // tm + N_CLASSES
    tile_g, tile_lo, tile_hi, nvalid, base, cstart = _tile_schedule(counts_blk, tm, n_tiles)

    def w_spec(shape, which):
        def index(i, tg, tlo, thi, *_):
            return (tg[i], (tlo if which == 0 else thi)[i], 0, 0)
        return pl.BlockSpec((None, None) + shape, index)

    est = 2 * 6 * D * F * 2 + 2 * tm * DA * 4 + 2 * tm * D * 4 + 8 * tm * F * 4
    return pl.pallas_call(
        functools.partial(_moe_kernel, tm=tm, d_model=D, final_norm=final_norm),
        grid_spec=pltpu.PrefetchScalarGridSpec(
            num_scalar_prefetch=8,
            grid=(n_tiles,),
            in_specs=[
                pl.BlockSpec(memory_space=pl.ANY),
                pl.BlockSpec((1, D), lambda i, *_: (0, 0)),
                pl.BlockSpec((1, D), lambda i, *_: (0, 0)),
                w_spec((D, F), 0), w_spec((D, F), 0), w_spec((F, D), 0),
                w_spec((D, F), 1), w_spec((D, F), 1), w_spec((F, D), 1),
            ],
            out_specs=pl.BlockSpec(memory_space=pl.ANY),
            scratch_shapes=[
                pltpu.SMEM((T,), jnp.int32),
                pltpu.VMEM((2, tm, DA), F32),
                pltpu.VMEM((2, tm, D), F32),
                pltpu.SemaphoreType.DMA((2,)),
                pltpu.SemaphoreType.DMA((2,)),
            ],
        ),
        out_shape=jax.ShapeDtypeStruct((T, D), F32),
        compiler_params=pltpu.CompilerParams(
            dimension_semantics=("arbitrary",), vmem_limit_bytes=_vmem_limit(est)),
        name="moe_experts",
    )(tile_g, tile_lo, tile_hi, nvalid, base, cstart, route[0], route[1],
      h_aug, ffn_norm_g, final_norm_g, wg, wu, wd, wg, wu, wd)


def _partial_rotary(xh, cosf, sinp, sinm, half):
    lanes = xh.shape[-1]
    return xh * cosf + pltpu.roll(xh, half, 1) * sinp + pltpu.roll(xh, lanes - half, 1) * sinm


def _kvq_kernel(h_ref, gkv_ref, gq_ref, wkv_ref, wq_ref, cosf_ref, sinp_ref, sinm_ref,
                k_ref, vt_ref, qt_ref, km_ref, *, heads, dh, half):
    x = h_ref[...]
    xr = x * _rms_scale(x)
    xkv = (xr * gkv_ref[...]).astype(BF16)
    xq = (xr * gq_ref[...]).astype(BF16)
    cosf = cosf_ref[...]
    sinp = sinp_ref[...]
    sinm = sinm_ref[...]
    width = heads * dh
    q_scale = dh ** -0.5 * LOG2_E
    for hh in range(heads):
        cols = slice(hh * dh, (hh + 1) * dh)
        kh = jnp.dot(xkv, wkv_ref[:, hh * dh:(hh + 1) * dh], preferred_element_type=F32)
        kh = _partial_rotary(kh, cosf, sinp, sinm, half)
        k_ref[hh] = kh.astype(BF16)
        km_ref[:, cols] = jnp.mean(kh, axis=0, keepdims=True)
        vh = jnp.dot(xkv, wkv_ref[:, width + hh * dh:width + (hh + 1) * dh], preferred_element_type=F32)
        vt_ref[hh] = vh.T.astype(BF16)
        qh = jnp.dot(xq, wq_ref[:, cols], preferred_element_type=F32)
        qt_ref[hh] = (_partial_rotary(qh, cosf, sinp, sinm, half) * q_scale).T.astype(BF16)


def _rope_tables(seq, dh):
    rot = dh // ROPE_FRACTION
    half = rot // 2
    inv = 1.0 / (ROPE_THETA ** (jnp.arange(half, dtype=F32) / half))
    ang = jnp.arange(seq, dtype=F32)[:, None] * inv[None, :]
    cos = jnp.cos(ang)
    sin = jnp.sin(ang)
    zeros = jnp.zeros((seq, dh - rot), F32)
    z_half = jnp.zeros((seq, half), F32)
    cosf = jnp.concatenate([cos, cos, jnp.ones((seq, dh - rot), F32)], axis=1)
    sinp = jnp.concatenate([z_half, sin, zeros], axis=1)
    sinm = jnp.concatenate([-sin, z_half, zeros], axis=1)
    return cosf, sinp, sinm, half


def _kvq_proj(h, kv_norm_g, q_norm_g, wkv_bf16, wq_bf16, seq):
    T, D = h.shape
    heads = ATT_HEADS
    dh = D // heads
    tm = MOBA_BLOCK
    n_pos_tiles = seq // tm
    cosf, sinp, sinm, half = _rope_tables(seq, dh)
    width = heads * dh
    est = 2 * (D * 2 * width + D * width) * 2 + 2 * tm * D * 4 + 6 * tm * width * 2 + 8 * tm * dh * 4
    pos = lambda i: (i % n_pos_tiles, 0)
    const = lambda i: (0, 0)
    per_block = lambda i: (0, i, 0, 0)
    return pl.pallas_call(
        functools.partial(_kvq_kernel, heads=heads, dh=dh, half=half),
        grid=(T // tm,),
        in_specs=[
            pl.BlockSpec((tm, D), lambda i: (i, 0)),
            pl.BlockSpec((1, D), const),
            pl.BlockSpec((1, D), const),
            pl.BlockSpec((D, 2 * width), const),
            pl.BlockSpec((D, width), const),
            pl.BlockSpec((tm, dh), pos),
            pl.BlockSpec((tm, dh), pos),
            pl.BlockSpec((tm, dh), pos),
        ],
        out_specs=[
            pl.BlockSpec((heads, None, tm, dh), per_block),
            pl.BlockSpec((heads, None, dh, tm), per_block),
            pl.BlockSpec((heads, None, dh, tm), per_block),
            pl.BlockSpec((None, 1, width), lambda i: (i, 0, 0)),
        ],
        out_shape=[
            jax.ShapeDtypeStruct((heads, T // tm, tm, dh), BF16),
            jax.ShapeDtypeStruct((heads, T // tm, dh, tm), BF16),
            jax.ShapeDtypeStruct((heads, T // tm, dh, tm), BF16),
            jax.ShapeDtypeStruct((T // tm, 1, width), F32),
        ],
        compiler_params=pltpu.CompilerParams(
            dimension_semantics=("arbitrary",), vmem_limit_bytes=_vmem_limit(est)),
        name="kvq_proj",
    )(h, kv_norm_g, q_norm_g, wkv_bf16, wq_bf16, cosf, sinp, sinm)


def _moba_kernel(qt_ref, k_ref, vt_ref, km_ref, o_ref, sel_ref, acc_ref, s_ref, p_ref, *, topk, group, dh):
    qi = pl.program_id(2)
    n_blk = km_ref.shape[0]
    neg = -jnp.inf

    def scores(g, j):
        return jnp.dot(k_ref[g, j], qt_ref[g], preferred_element_type=F32)

    m0, l0 = [], []
    for g in range(group):
        gate = jnp.dot(km_ref[:, g * dh:(g + 1) * dh], qt_ref[g].astype(F32), preferred_element_type=F32,
                       precision=lax.Precision.HIGHEST)
        blk_id = lax.broadcasted_iota(jnp.int32, gate.shape, 0)
        gv = jnp.where(blk_id < qi, gate, neg)
        sel = jnp.zeros(gate.shape, F32)
        for _ in range(topk):
            m = jnp.max(gv, axis=0, keepdims=True)
            first = jnp.min(jnp.where(gv == m, blk_id, n_blk), axis=0, keepdims=True)
            sel = jnp.where((blk_id == first) & (m > neg), 1.0, sel)
            gv = jnp.where(blk_id == first, neg, gv)
        sel_ref[g] = sel

        s = scores(g, qi)
        key_id = lax.broadcasted_iota(jnp.int32, s.shape, 0)
        qry_id = lax.broadcasted_iota(jnp.int32, s.shape, 1)
        s = jnp.where(key_id <= qry_id, s, neg)
        m = jnp.max(s, axis=0, keepdims=True)
        p = jnp.exp2(s - m)
        m0.append(m)
        l0.append(jnp.sum(p, axis=0, keepdims=True))
        acc_ref[g] = jnp.dot(vt_ref[g, qi], p.astype(BF16), preferred_element_type=F32)

    blk = s_ref.shape[1]
    chunks = [slice(c, c + MOBA_KEY_CHUNK) for c in range(0, blk, MOBA_KEY_CHUNK)]

    def body(j, carry):
        ms, ls = carry
        for g in range(group):
            s_ref[g] = jnp.where(sel_ref[g, pl.ds(j, 1), :] > 0.0, scores(g, j), neg)
        new_m, new_l, alphas = [], [], []
        for g in range(group):
            part = s_ref[g, chunks[0], :]
            for c in chunks[1:]:
                part = jnp.maximum(part, s_ref[g, c, :])
            m_new = jnp.maximum(ms[g], jnp.max(part, axis=0, keepdims=True))
            alpha = jnp.exp2(ms[g] - m_new)
            psum = None
            for c in chunks:
                p = jnp.exp2(s_ref[g, c, :] - m_new)
                p_ref[g, c, :] = p.astype(BF16)
                psum = p if psum is None else psum + p
            new_m.append(m_new)
            new_l.append(alpha * ls[g] + jnp.sum(psum, axis=0, keepdims=True))
            alphas.append(alpha)
        for g in range(group):
            acc_ref[g] = alphas[g] * acc_ref[g] + jnp.dot(vt_ref[g, j], p_ref[g], preferred_element_type=F32)
        return tuple(new_m), tuple(new_l)

    _, ls = lax.fori_loop(0, qi, body, (tuple(m0), tuple(l0)))
    for g in range(group):
        o_ref[:, g * dh:(g + 1) * dh] = (acc_ref[g] / ls[g]).T.astype(BF16)


def _moba_attention(qt, k, vt, k_mean, batch, seq):
    heads, n_tiles, dh, blk = qt.shape
    n_blk = seq // blk
    topk = min(MOBA_TOPK, n_blk)
    width = heads * dh
    group = MOBA_HEAD_GROUP
    km = k_mean.reshape(batch, n_blk, width)
    per_seq = lambda b, h, i: (h, b, 0, 0)
    est = 2 * 2 * group * seq * dh * 2 + 2 * group * blk * dh * 4 + 16 * blk * blk * 4
    return pl.pallas_call(
        functools.partial(_moba_kernel, topk=topk, group=group, dh=dh),
        grid=(batch, heads // group, n_blk),
        in_specs=[
            pl.BlockSpec((group, None, dh, blk), lambda b, h, i: (h, b * n_blk + i, 0, 0)),
            pl.BlockSpec((group, n_blk, blk, dh), per_seq),
            pl.BlockSpec((group, n_blk, dh, blk), per_seq),
            pl.BlockSpec((None, n_blk, group * dh), lambda b, h, i: (b, 0, h)),
        ],
        out_specs=pl.BlockSpec((blk, group * dh), lambda b, h, i: (b * n_blk + i, h)),
        out_shape=jax.ShapeDtypeStruct((n_tiles * blk, width), BF16),
        scratch_shapes=[pltpu.VMEM((group, n_blk, blk), F32), pltpu.VMEM((group, dh, blk), F32),
                        pltpu.VMEM((group, blk, blk), F32), pltpu.VMEM((group, blk, blk), BF16)],
        compiler_params=pltpu.CompilerParams(
            dimension_semantics=("arbitrary", "arbitrary", "arbitrary"), vmem_limit_bytes=_vmem_limit(est)),
        name="moba_attention",
    )(qt, k, vt, km)


def _ret_rope_tables(seq, dk):
    half = dk // 2
    inv = 1.0 / (RET_ROT_BASE ** (jnp.arange(half, dtype=F32) / half))
    ang = jnp.arange(seq, dtype=F32)[:, None] * inv[None, :]
    return jnp.cos(ang), jnp.sin(ang)


def kernel(x, ret_norm, ret_w_in, ret_w_out, kv_norm, w_kv, attn_norm, w_q, w_o, ffn_norm, router_group_w, router_group_b, router_expert_w, router_expert_b, expert_w_gate, expert_w_up, expert_w_down, final_norm):
    B, S, D = x.shape
    T = B * S
    assert S % MOBA_BLOCK == 0 and S % RET_KERNEL_CHUNK == 0 and T % MOE_TILE == 0
    assert ret_norm.shape[0] == 1 and attn_norm.shape[0] == 1 and ffn_norm.shape[0] == 2
    h = x.reshape(T, D)
    final_g = final_norm[None, :]

    def expert_weights(layer):
        return (expert_w_gate[layer].astype(BF16), expert_w_up[layer].astype(BF16),
                expert_w_down[layer].astype(BF16))

    def proj_router(y, w, resid, layer, name):
        return _proj_router(y, w.astype(BF16), resid, ffn_norm[layer][None, :], router_group_w[layer],
                            router_group_b[layer], router_expert_w[layer], router_expert_b[layer], name)

    cos, sin = _ret_rope_tables(S, D // RET_HEADS)
    q, k, v, gate = _ret_in_proj(h, ret_norm[0][None, :], ret_w_in[0].astype(BF16), cos, sin, S)
    y = _ret_core(q, k, v, gate, B, S)
    h_aug, route, counts = proj_router(y, ret_w_out[0], h, 0, "ret_out_proj_router")
    h = _moe_layer(h_aug, route, counts, ffn_norm[0][None, :], final_g, *expert_weights(0), final_norm=False)

    k2, vt2, qt2, k_mean = _kvq_proj(h, kv_norm[None, :], attn_norm[0][None, :],
                                     w_kv.astype(BF16), w_q[0].astype(BF16), S)
    o = _moba_attention(qt2, k2, vt2, k_mean, B, S)
    h_aug, route, counts = proj_router(o, w_o[0], h, 1, "attn_out_proj_router")
    h = _moe_layer(h_aug, route, counts, ffn_norm[1][None, :], final_g, *expert_weights(1), final_norm=True)
    return h.reshape(B, S, D)
```

```python
import functools

import jax
import jax.numpy as jnp
from jax import lax
from jax.experimental import pallas as pl
from jax.experimental.pallas import tpu as pltpu

F32 = jnp.float32
BF16 = jnp.bfloat16

NORM_EPS = 1e-6
RET_HEADS = 4
RET_ROT_BASE = 10000.0
ATT_HEADS = 8
ROPE_FRACTION = 4
ROPE_THETA = 500000.0
MOBA_BLOCK = 256
MOBA_TOPK = 3
N_GROUPS = 4
EXPERTS_PER_GROUP = 4
PAIR_LO = (0, 0, 0, 1, 1, 2)
PAIR_HI = (1, 2, 3, 2, 3, 3)
N_PAIRS = len(PAIR_LO)
N_CLASSES = N_GROUPS * N_PAIRS

LANES = 128
SUBLANES = 8
RET_KERNEL_CHUNK = 256
MOE_TILE = 128
ROUTE_ROWS = 32
DMA_ISSUE_UNROLL = 8
MOBA_HEAD_GROUP = 4
MOBA_KEY_CHUNK = 64
VT_ONES_ROWS = 16
LOG2_E = 1.4426950408889634
V7X_VMEM_BYTES = 64 * 1024 * 1024


def _vmem_limit(estimate_bytes):
    return int(min(V7X_VMEM_BYTES - 8 * 1024 * 1024, max(32 * 1024 * 1024, estimate_bytes * 5 // 4)))


def _rms_scale(x):
    return lax.rsqrt(jnp.mean(x * x, axis=-1, keepdims=True) + NORM_EPS)


def _silu(a):
    return a * jax.nn.sigmoid(a)


def _ret_in_kernel(x_ref, g_ref, w_ref, cb_ref, sb_ref, cr_ref, sr_ref, q_ref, k_ref, v_ref, gate_ref,
                   *, heads, dk, dv):
    x = x_ref[...]
    xn = (x * _rms_scale(x) * g_ref[...]).astype(BF16)
    cos, sin = _tile_cos_sin(cb_ref, sb_ref, cr_ref, sr_ref)
    half = dk // 2
    k_scale = dk ** -0.5

    def rot_store(dst_ref, col0, h, scale):
        p = jnp.dot(xn, w_ref[:, col0 + h * dk:col0 + (h + 1) * dk], preferred_element_type=F32)
        x1 = p[:, :half]
        x2 = p[:, half:]
        dst_ref[:, h * dk:h * dk + half] = ((x1 * cos - x2 * sin) * scale).astype(BF16)
        dst_ref[:, h * dk + half:(h + 1) * dk] = ((x2 * cos + x1 * sin) * scale).astype(BF16)

    for h in range(heads):
        rot_store(q_ref, 0, h, 1.0)
        rot_store(k_ref, heads * dk, h, k_scale)
    v0 = 2 * heads * dk
    g0 = v0 + heads * dv
    for h in range(heads):
        v_ref[:, h * dv:(h + 1) * dv] = jnp.dot(
            xn, w_ref[:, v0 + h * dv:v0 + (h + 1) * dv], preferred_element_type=F32).astype(BF16)
        gate_ref[:, h * dv:(h + 1) * dv] = jnp.dot(
            xn, w_ref[:, g0 + h * dv:g0 + (h + 1) * dv], preferred_element_type=F32).astype(BF16)


def _ret_in_proj(h, norm_g, w_bf16, seq, tm=256):
    T, D = h.shape
    heads = RET_HEADS
    dk = D // heads
    dv = 2 * dk
    n_cols = w_bf16.shape[1]
    n_pos_tiles = seq // tm
    half = dk // 2
    inv = 1.0 / (RET_ROT_BASE ** (jnp.arange(half, dtype=F32) / half))
    cos_b, sin_b, cos_r, sin_r = _angle_tables(inv, seq, tm)
    pos = lambda i: (i % n_pos_tiles, 0, 0)
    est = 2 * D * n_cols * 2 + 2 * tm * D * 4 + 2 * tm * n_cols * 2 + 4 * tm * LANES * 4
    return pl.pallas_call(
        functools.partial(_ret_in_kernel, heads=heads, dk=dk, dv=dv),
        grid=(T // tm,),
        in_specs=[
            pl.BlockSpec((tm, D), lambda i: (i, 0)),
            pl.BlockSpec((1, D), lambda i: (0, 0)),
            pl.BlockSpec((D, n_cols), lambda i: (0, 0)),
            pl.BlockSpec((None, 1, half), pos),
            pl.BlockSpec((None, 1, half), pos),
            pl.BlockSpec((tm, half), lambda i: (0, 0)),
            pl.BlockSpec((tm, half), lambda i: (0, 0)),
        ],
        out_specs=[
            pl.BlockSpec((tm, heads * dk), lambda i: (i, 0)),
            pl.BlockSpec((tm, heads * dk), lambda i: (i, 0)),
            pl.BlockSpec((tm, heads * dv), lambda i: (i, 0)),
            pl.BlockSpec((tm, heads * dv), lambda i: (i, 0)),
        ],
        out_shape=[
            jax.ShapeDtypeStruct((T, heads * dk), BF16),
            jax.ShapeDtypeStruct((T, heads * dk), BF16),
            jax.ShapeDtypeStruct((T, heads * dv), BF16),
            jax.ShapeDtypeStruct((T, heads * dv), BF16),
        ],
        compiler_params=pltpu.CompilerParams(
            dimension_semantics=("arbitrary",), vmem_limit_bytes=_vmem_limit(est)),
        name="ret_in_proj",
    )(h, norm_g, w_bf16, cos_b, sin_b, cos_r, sin_r)


def _ret_core_kernel(q_ref, k_ref, v_ref, g_ref, dm_ref, qd_ref, kd_ref, cd_ref, y_ref, state_ref, *, heads):
    @pl.when(pl.program_id(1) == 0)
    def _():
        state_ref[...] = jnp.zeros_like(state_ref)

    dk = q_ref.shape[1] // heads
    dv = v_ref.shape[1] // heads
    for h in range(heads):
        q = q_ref[:, h * dk:(h + 1) * dk]
        k = k_ref[:, h * dk:(h + 1) * dk]
        v = v_ref[:, h * dv:(h + 1) * dv]
        s = lax.dot_general(q, k, (((1,), (1,)), ((), ())), preferred_element_type=F32) * dm_ref[h]
        o = jnp.dot(s.astype(BF16), v, preferred_element_type=F32)
        state = state_ref[h]
        o = o + qd_ref[h] * jnp.dot(q, state.astype(BF16), preferred_element_type=F32)
        ks = (k.astype(F32) * kd_ref[h]).astype(BF16)
        state_ref[h] = state * cd_ref[h] + lax.dot_general(
            ks, v, (((0,), (0,)), ((), ())), preferred_element_type=F32)
        o = o * _rms_scale(o)
        y_ref[:, h * dv:(h + 1) * dv] = (_silu(g_ref[:, h * dv:(h + 1) * dv].astype(F32)) * o).astype(BF16)


def _ret_decay_tables(heads, chunk, dv):
    log_gamma = jnp.log1p(-jnp.power(2.0, -5.0 - jnp.arange(heads, dtype=F32)))
    idx = jnp.arange(chunk, dtype=F32)
    diff = idx[:, None] - idx[None, :]
    dmask = jnp.where(diff >= 0, jnp.exp(log_gamma[:, None, None] * jnp.maximum(diff, 0.0)), 0.0)
    qd = jnp.exp(log_gamma[:, None] * (idx + 1.0))[:, :, None]
    kd = jnp.exp(log_gamma[:, None] * (chunk - 1.0 - idx))[:, :, None]
    cd = jnp.broadcast_to(jnp.exp(log_gamma * chunk)[:, None, None], (heads, 1, dv))
    return dmask, qd, kd, cd


def _ret_core(q, k, v, gate, batch, seq):
    T = q.shape[0]
    heads = RET_HEADS
    dk = q.shape[1] // heads
    dv = v.shape[1] // heads
    cc = RET_KERNEL_CHUNK
    nc = seq // cc
    dmask, qd, kd, cd = _ret_decay_tables(heads, cc, dv)
    row = lambda b, c: (b * nc + c, 0)
    tab = lambda b, c: (0, 0, 0)
    return pl.pallas_call(
        functools.partial(_ret_core_kernel, heads=heads),
        grid=(batch, nc),
        in_specs=[
            pl.BlockSpec((cc, heads * dk), row),
            pl.BlockSpec((cc, heads * dk), row),
            pl.BlockSpec((cc, heads * dv), row),
            pl.BlockSpec((cc, heads * dv), row),
            pl.BlockSpec((heads, cc, cc), tab),
            pl.BlockSpec((heads, cc, 1), tab),
            pl.BlockSpec((heads, cc, 1), tab),
            pl.BlockSpec((heads, 1, dv), tab),
        ],
        out_specs=pl.BlockSpec((cc, heads * dv), row),
        out_shape=jax.ShapeDtypeStruct((T, heads * dv), BF16),
        scratch_shapes=[pltpu.VMEM((heads, dk, dv), F32)],
        compiler_params=pltpu.CompilerParams(dimension_semantics=("arbitrary", "arbitrary")),
        name="ret_core",
    )(q, k, v, gate, dmask, qd, kd, cd)


def _proj_router_kernel(y_ref, w_ref, r_ref, g_ref, wr_ref, br_ref, haug_ref, route_ref, counts_ref, run_ref,
                        *, groups, experts, d_model):
    @pl.when(pl.program_id(0) == 0)
    def _():
        run_ref[...] = jnp.zeros_like(run_ref)

    h = r_ref[...] + jnp.dot(y_ref[...], w_ref[...], preferred_element_type=F32)
    xn = h * _rms_scale(h) * g_ref[...]
    xh = xn.astype(BF16)
    xl = (xn - xh.astype(F32)).astype(BF16)
    hi_both = jnp.dot(xh, wr_ref[...], preferred_element_type=F32)
    lo_hi = jnp.dot(xl, wr_ref[:, :LANES], preferred_element_type=F32)
    logits = hi_both[:, :LANES] + (hi_both[:, LANES:] + lo_hi) + br_ref[...]
    lt = logits.T[:ROUTE_ROWS]
    tm = lt.shape[1]
    row = lax.broadcasted_iota(jnp.int32, lt.shape, 0)
    neg = -jnp.inf

    def argmax_first(vals):
        m = jnp.max(vals, axis=0, keepdims=True)
        first = jnp.min(jnp.where(vals == m, row, ROUTE_ROWS), axis=0, keepdims=True)
        return m, first

    gl = jnp.where(row < groups, lt, neg)
    gmax, gidx = argmax_first(gl)
    gsum = jnp.sum(jnp.where(row < groups, jnp.exp(gl - gmax), 0.0), axis=0, keepdims=True)
    g_val = 1.0 / gsum
    e0 = groups + experts * gidx
    el = jnp.where((row >= e0) & (row < e0 + experts), lt, neg)
    m1, i1 = argmax_first(el)
    m2, i2 = argmax_first(jnp.where(row == i1, neg, el))
    t = jnp.exp(m2 - m1)
    w1 = 1.0 / (1.0 + t)
    w2 = t / (1.0 + t)
    first_is_lo = i1 < i2
    lo = jnp.minimum(i1, i2) - e0
    hi = jnp.maximum(i1, i2) - e0
    w_lo = g_val * jnp.where(first_is_lo, w1, w2)
    w_hi = g_val * jnp.where(first_is_lo, w2, w1)
    pair = (lo * (7 - lo)) // 2 + (hi - lo - 1)
    cls = gidx * N_PAIRS + pair

    onehot = row == cls
    oh = jnp.where(onehot, 1.0, 0.0)
    before = lax.broadcasted_iota(jnp.int32, (tm, tm), 0) < lax.broadcasted_iota(jnp.int32, (tm, tm), 1)
    prefix = jnp.dot(oh.astype(BF16), jnp.where(before, 1.0, 0.0).astype(BF16), preferred_element_type=F32)
    rank = jnp.sum(jnp.where(onehot, prefix + run_ref[:, 0:1], 0.0), axis=0, keepdims=True)
    run_new = run_ref[...] + jnp.sum(oh, axis=1, keepdims=True)
    run_ref[...] = run_new
    counts_ref[...] = run_new

    r8 = lax.broadcasted_iota(jnp.int32, route_ref.shape, 0)
    route_ref[...] = jnp.where(r8 == 0, cls, jnp.where(r8 == 1, rank.astype(jnp.int32), 0))
    rl = lax.broadcasted_iota(jnp.int32, (LANES, tm), 0)
    meta_t = jnp.where(rl == 0, w_lo, jnp.where(rl == 1, w_hi, 0.0))
    haug_ref[:, :d_model] = h
    haug_ref[:, d_model:] = meta_t.T


def _proj_router(y, w_bf16, resid, norm_g, w_rg, b_rg, w_re, b_re, name, tm=512):
    T, K = y.shape
    D = w_bf16.shape[1]
    G, E = N_GROUPS, EXPERTS_PER_GROUP
    n_used = G + G * E
    wr = jnp.concatenate([w_rg, jnp.transpose(w_re, (1, 0, 2)).reshape(D, G * E)], axis=1)
    wr = jnp.pad(wr, ((0, 0), (0, LANES - n_used)))
    wr_hi = wr.astype(BF16)
    wr = jnp.concatenate([wr_hi, (wr - wr_hi.astype(F32)).astype(BF16)], axis=1)
    br = jnp.pad(jnp.concatenate([b_rg, b_re.reshape(G * E)]), (0, LANES - n_used))[None, :]
    est = 2 * K * D * 2 + 2 * tm * K * 2 + 6 * tm * D * 4 + 2 * D * LANES * 4 + 4 * tm * tm * 4
    const = lambda i: (0, 0)
    return pl.pallas_call(
        functools.partial(_proj_router_kernel, groups=G, experts=E, d_model=D),
        grid=(T // tm,),
        in_specs=[
            pl.BlockSpec((tm, K), lambda i: (i, 0)),
            pl.BlockSpec((K, D), const),
            pl.BlockSpec((tm, D), lambda i: (i, 0)),
            pl.BlockSpec((1, D), const),
            pl.BlockSpec((D, 2 * LANES), const),
            pl.BlockSpec((1, LANES), const),
        ],
        out_specs=[
            pl.BlockSpec((tm, D + LANES), lambda i: (i, 0)),
            pl.BlockSpec((8, tm), lambda i: (0, i)),
            pl.BlockSpec((ROUTE_ROWS, LANES), const),
        ],
        out_shape=[
            jax.ShapeDtypeStruct((T, D + LANES), F32),
            jax.ShapeDtypeStruct((8, T), jnp.int32),
            jax.ShapeDtypeStruct((ROUTE_ROWS, LANES), F32),
        ],
        scratch_shapes=[pltpu.VMEM((ROUTE_ROWS, LANES), F32)],
        compiler_params=pltpu.CompilerParams(
            dimension_semantics=("arbitrary",), vmem_limit_bytes=_vmem_limit(est)),
        name=name,
    )(y, w_bf16, resid, norm_g, wr, br)


def _tile_schedule(counts_blk, tm, n_tiles):
    counts = counts_blk[:N_CLASSES, 0].astype(jnp.int32)
    tiles_per = (counts + tm - 1) // tm
    tile_end = jnp.cumsum(tiles_per)
    tile_begin = tile_end - tiles_per
    cstart = jnp.cumsum(counts) - counts
    t = jnp.arange(n_tiles, dtype=jnp.int32)
    live = t < tile_end[-1]
    tq = jnp.where(live, t, tile_end[-1] - 1)
    tcls = jnp.sum((tile_end[None, :] <= tq[:, None]).astype(jnp.int32), axis=1)
    k = tq - tile_begin[tcls]
    nvalid = jnp.where(live, jnp.clip(counts[tcls] - k * tm, 0, tm), 0).astype(jnp.int32)
    base = (cstart[tcls] + k * tm).astype(jnp.int32)
    pair = tcls % N_PAIRS
    tile_g = (tcls // N_PAIRS).astype(jnp.int32)
    tile_lo = jnp.asarray(PAIR_LO, jnp.int32)[pair]
    tile_hi = jnp.asarray(PAIR_HI, jnp.int32)[pair]
    cstart_pad = jnp.pad(cstart, (0, ROUTE_ROWS - N_CLASSES)).astype(jnp.int32)
    return tile_g, tile_lo, tile_hi, nvalid, base, cstart_pad


def _moe_kernel(tg_ref, tlo_ref, thi_ref, tnv_ref, tbase_ref, cs_ref, cls_ref, rank_ref,
                h_hbm, gn_ref, fn_ref, wg_lo, wu_lo, wd_lo, wg_hi, wu_hi, wd_hi,
                out_hbm, rt_ref, xbuf, obuf, w_in_bf, w_out_bf, gsem, ssem, *, tm, d_model, final_norm):
    i = pl.program_id(0)
    n = pl.num_programs(0)
    slot = i % 2
    n_tok = cls_ref.shape[0]

    def for_rows(count, fn):
        def group(gi, c):
            for u in range(DMA_ISSUE_UNROLL):
                fn(gi * DMA_ISSUE_UNROLL + u)
            return c

        def single(r, c):
            fn(r)
            return c
        n_groups = count // DMA_ISSUE_UNROLL
        lax.fori_loop(0, n_groups, group, 0)
        lax.fori_loop(n_groups * DMA_ISSUE_UNROLL, count, single, 0)

    def token(t, r):
        return rt_ref[tbase_ref[t] + r]

    def gather_copy(t, r, sl):
        return pltpu.make_async_copy(h_hbm.at[pl.ds(token(t, r), 1)], xbuf.at[sl, pl.ds(r, 1)], gsem.at[sl])

    def scatter_copy(t, r, sl):
        return pltpu.make_async_copy(obuf.at[sl, pl.ds(r, 1)], out_hbm.at[pl.ds(token(t, r), 1)], ssem.at[sl])

    def gather_block(sl, n_rows):
        rows = pl.ds(0, n_rows)
        return pltpu.make_async_copy(h_hbm.at[rows], xbuf.at[sl, rows], gsem.at[sl])

    def scatter_block(sl, n_rows):
        rows = pl.ds(0, n_rows)
        return pltpu.make_async_copy(obuf.at[sl, rows], out_hbm.at[rows], ssem.at[sl])

    def wait_rows(t, sl, block_copy, row_copy):
        count = tnv_ref[t]
        aligned = pl.multiple_of((count // SUBLANES) * SUBLANES, SUBLANES)

        @pl.when(aligned > 0)
        def _():
            block_copy(sl, aligned).wait()

        def single(r, c):
            row_copy(t, r, sl).wait()
            return c
        lax.fori_loop(aligned, count, single, 0)

    def start_gather(t, sl):
        for_rows(tnv_ref[t], lambda r: gather_copy(t, r, sl).start())

    def start_scatter(t, sl):
        for_rows(tnv_ref[t], lambda r: scatter_copy(t, r, sl).start())

    def wait_gather(t, sl):
        wait_rows(t, sl, gather_block, gather_copy)

    def wait_scatter(t, sl):
        wait_rows(t, sl, scatter_block, scatter_copy)

    @pl.when(i == 0)
    def _():
        xbuf[...] = jnp.zeros_like(xbuf)

        def place(tok):
            rt_ref[cs_ref[cls_ref[tok]] + rank_ref[tok]] = tok
        for_rows(n_tok, place)
        start_gather(0, 0)

    @pl.when(i + 1 < n)
    def _():
        start_gather(i + 1, 1 - slot)

    wait_gather(i, slot)

    @pl.when(i >= 2)
    def _():
        wait_scatter(i - 2, slot)

    prev = jnp.maximum(i - 1, 0)
    group_changed = (i == 0) | (tg_ref[i] != tg_ref[prev])
    for which, t_ref, (wg, wu, wd) in ((0, tlo_ref, (wg_lo, wu_lo, wd_lo)), (1, thi_ref, (wg_hi, wu_hi, wd_hi))):
        @pl.when(group_changed | (t_ref[i] != t_ref[prev]))
        def _():
            w_in_bf[2 * which] = wg[...].astype(BF16)
            w_in_bf[2 * which + 1] = wu[...].astype(BF16)
            w_out_bf[which] = wd[...].astype(BF16)

    @pl.when(tnv_ref[i] > 0)
    def _():
        xa = xbuf[slot]
        x = xa[:, :d_model]
        xn = (x * _rms_scale(x) * gn_ref[...]).astype(BF16)

        def expert(which, w):
            a = jnp.dot(xn, w_in_bf[2 * which], preferred_element_type=F32)
            b = jnp.dot(xn, w_in_bf[2 * which + 1], preferred_element_type=F32)
            return jnp.dot((_silu(a) * b * w).astype(BF16), w_out_bf[which], preferred_element_type=F32)

        y = expert(0, xa[:, d_model:d_model + 1]) + expert(1, xa[:, d_model + 1:d_model + 2])
        o = x + y
        if final_norm:
            o = o * _rms_scale(o) * fn_ref[...]
        obuf[slot] = o

    start_scatter(i, slot)

    @pl.when(i == n - 1)
    def _():
        @pl.when(i >= 1)
        def _():
            wait_scatter(i - 1, 1 - slot)
        wait_scatter(i, slot)


def _moe_layer(h_aug, route, counts_blk, ffn_norm_g, final_norm_g, wg, wu, wd, layer, final_norm):
    T, DA = h_aug.shape
    D = DA - LANES
    F = wg.shape[-1]
    tm = MOE_TILE
    n_tiles = T // tm + N_CLASSES
    tile_g, tile_lo, tile_hi, nvalid, base, cstart = _tile_schedule(counts_blk, tm, n_tiles)

    def w_spec(shape, which):
        def index(i, tg, tlo, thi, *_):
            return (layer, tg[i], (tlo if which == 0 else thi)[i], 0, 0)
        return pl.BlockSpec((None, None, None) + shape, index)

    est = 2 * 6 * D * F * 4 + 6 * D * F * 2 + 2 * tm * DA * 4 + 2 * tm * D * 4 + 8 * tm * F * 4
    return pl.pallas_call(
        functools.partial(_moe_kernel, tm=tm, d_model=D, final_norm=final_norm),
        grid_spec=pltpu.PrefetchScalarGridSpec(
            num_scalar_prefetch=8,
            grid=(n_tiles,),
            in_specs=[
                pl.BlockSpec(memory_space=pl.ANY),
                pl.BlockSpec((1, D), lambda i, *_: (0, 0)),
                pl.BlockSpec((1, D), lambda i, *_: (0, 0)),
                w_spec((D, F), 0), w_spec((D, F), 0), w_spec((F, D), 0),
                w_spec((D, F), 1), w_spec((D, F), 1), w_spec((F, D), 1),
            ],
            out_specs=pl.BlockSpec(memory_space=pl.ANY),
            scratch_shapes=[
                pltpu.SMEM((T,), jnp.int32),
                pltpu.VMEM((2, tm, DA), F32),
                pltpu.VMEM((2, tm, D), F32),
                pltpu.VMEM((4, D, F), BF16),
                pltpu.VMEM((2, F, D), BF16),
                pltpu.SemaphoreType.DMA((2,)),
                pltpu.SemaphoreType.DMA((2,)),
            ],
        ),
        out_shape=jax.ShapeDtypeStruct((T, D), F32),
        compiler_params=pltpu.CompilerParams(
            dimension_semantics=("arbitrary",), vmem_limit_bytes=_vmem_limit(est)),
        name="moe_experts",
    )(tile_g, tile_lo, tile_hi, nvalid, base, cstart, route[0], route[1],
      h_aug, ffn_norm_g, final_norm_g, wg, wu, wd, wg, wu, wd)


def _partial_rotary(xh, cosf, sinp, sinm, half):
    lanes = xh.shape[-1]
    return xh * cosf + pltpu.roll(xh, half, 1) * sinp + pltpu.roll(xh, lanes - half, 1) * sinm


def _tile_cos_sin(cb_ref, sb_ref, cr_ref, sr_ref):
    cb, sb, cr, sr = cb_ref[...], sb_ref[...], cr_ref[...], sr_ref[...]
    return cb * cr - sb * sr, sb * cr + cb * sr


def _kvq_kernel(h_ref, gkv_ref, gq_ref, wkv_ref, wq_ref, cb_ref, sb_ref, cr_ref, sr_ref,
                k_ref, vt_ref, qt_ref, km_ref, *, heads, dh, half):
    x = h_ref[...]
    xr = x * _rms_scale(x)
    xkv = (xr * gkv_ref[...]).astype(BF16)
    xq = (xr * gq_ref[...]).astype(BF16)
    cosf, sin_all = _tile_cos_sin(cb_ref, sb_ref, cr_ref, sr_ref)
    lane = lax.broadcasted_iota(jnp.int32, sin_all.shape, 1)
    sinp = jnp.where(lane >= half, sin_all, 0.0)
    sinm = jnp.where(lane < half, -sin_all, 0.0)
    width = heads * dh
    q_scale = dh ** -0.5 * LOG2_E
    pair_w = 2 * dh
    for c0 in range(0, width, pair_w):
        kk = jnp.dot(xkv, wkv_ref[:, c0:c0 + pair_w], preferred_element_type=F32)
        vv = jnp.dot(xkv, wkv_ref[:, width + c0:width + c0 + pair_w], preferred_element_type=F32)
        qq = jnp.dot(xq, wq_ref[:, c0:c0 + pair_w], preferred_element_type=F32)
        for u in range(2):
            hh = c0 // dh + u
            cols = slice(hh * dh, (hh + 1) * dh)
            kh = _partial_rotary(kk[:, u * dh:(u + 1) * dh], cosf, sinp, sinm, half)
            k_ref[hh] = kh.astype(BF16)
            km_ref[:, cols] = jnp.mean(kh, axis=0, keepdims=True)
            vt_ref[hh, :dh] = vv[:, u * dh:(u + 1) * dh].T.astype(BF16)
            vt_ref[hh, dh:] = jnp.ones((VT_ONES_ROWS, vv.shape[0]), BF16)
            qh = _partial_rotary(qq[:, u * dh:(u + 1) * dh], cosf, sinp, sinm, half)
            qt_ref[hh] = (qh * q_scale).T.astype(BF16)


def _angle_tables(inv_freq, seq, tm):
    base = jnp.arange(0, seq, tm, dtype=F32)[:, None] * inv_freq[None, :]
    offs = jnp.arange(tm, dtype=F32)[:, None] * inv_freq[None, :]
    return jnp.cos(base)[:, None, :], jnp.sin(base)[:, None, :], jnp.cos(offs), jnp.sin(offs)


def _rope_tables(seq, dh, tm):
    rot = dh // ROPE_FRACTION
    half = rot // 2
    inv = 1.0 / (ROPE_THETA ** (jnp.arange(half, dtype=F32) / half))
    inv_lanes = jnp.concatenate([inv, inv, jnp.zeros((dh - rot,), F32)])
    return _angle_tables(inv_lanes, seq, tm) + (half,)


def _kvq_proj(h, kv_norm_g, q_norm_g, wkv_bf16, wq_bf16, seq):
    T, D = h.shape
    heads = ATT_HEADS
    dh = D // heads
    tm = MOBA_BLOCK
    n_pos_tiles = seq // tm
    cos_b, sin_b, cos_r, sin_r, half = _rope_tables(seq, dh, tm)
    width = heads * dh
    est = 2 * (D * 2 * width + D * width) * 2 + 2 * tm * D * 4 + 6 * tm * width * 2 + 8 * tm * dh * 4
    pos = lambda i: (i % n_pos_tiles, 0, 0)
    const = lambda i: (0, 0)
    per_block = lambda i: (0, i, 0, 0)
    return pl.pallas_call(
        functools.partial(_kvq_kernel, heads=heads, dh=dh, half=half),
        grid=(T // tm,),
        in_specs=[
            pl.BlockSpec((tm, D), lambda i: (i, 0)),
            pl.BlockSpec((1, D), const),
            pl.BlockSpec((1, D), const),
            pl.BlockSpec((D, 2 * width), const),
            pl.BlockSpec((D, width), const),
            pl.BlockSpec((None, 1, dh), pos),
            pl.BlockSpec((None, 1, dh), pos),
            pl.BlockSpec((tm, dh), const),
            pl.BlockSpec((tm, dh), const),
        ],
        out_specs=[
            pl.BlockSpec((heads, None, tm, dh), per_block),
            pl.BlockSpec((heads, None, dh + VT_ONES_ROWS, tm), per_block),
            pl.BlockSpec((heads, None, dh, tm), per_block),
            pl.BlockSpec((None, 1, width), lambda i: (i, 0, 0)),
        ],
        out_shape=[
            jax.ShapeDtypeStruct((heads, T // tm, tm, dh), BF16),
            jax.ShapeDtypeStruct((heads, T // tm, dh + VT_ONES_ROWS, tm), BF16),
            jax.ShapeDtypeStruct((heads, T // tm, dh, tm), BF16),
            jax.ShapeDtypeStruct((T // tm, 1, width), F32),
        ],
        compiler_params=pltpu.CompilerParams(
            dimension_semantics=("arbitrary",), vmem_limit_bytes=_vmem_limit(est)),
        name="kvq_proj",
    )(h, kv_norm_g, q_norm_g, wkv_bf16, wq_bf16, cos_b, sin_b, cos_r, sin_r)


def _moba_kernel(qt_ref, k_ref, vt_ref, km_ref, o_ref, sel_ref, acc_ref, sa_ref, sb_ref, pa_ref, pb_ref,
                 *, topk, group, dh):
    qi = pl.program_id(2)
    n_blk = km_ref.shape[0]
    blk = sa_ref.shape[1]
    neg = -jnp.inf
    chunks = [slice(c, c + MOBA_KEY_CHUNK) for c in range(0, blk, MOBA_KEY_CHUNK)]

    def scores(g, j):
        return jnp.dot(k_ref[g, j], qt_ref[g], preferred_element_type=F32)

    def stage_scores(s_buf, j):
        jc = jnp.minimum(j, n_blk - 1)
        for g in range(group):
            s_buf[g] = scores(g, jc) + sel_ref[g, pl.ds(jc, 1), :]

    def stage_softmax(s_buf, p_buf, ms):
        new_m, alphas = [], []
        for g in range(group):
            part = s_buf[g, chunks[0], :]
            for c in chunks[1:]:
                part = jnp.maximum(part, s_buf[g, c, :])
            m_blk = jnp.max(part, axis=0, keepdims=True)
            m_new = m_blk if ms is None else jnp.maximum(ms[g], m_blk)
            for c in chunks:
                p_buf[g, c, :] = jnp.exp2(s_buf[g, c, :] - m_new).astype(BF16)
            new_m.append(m_new)
            alphas.append(None if ms is None else jnp.exp2(ms[g] - m_new))
        return tuple(new_m), alphas

    def stage_pv(p_buf, j, alphas):
        jc = jnp.minimum(j, n_blk - 1)
        for g in range(group):
            pv = jnp.dot(vt_ref[g, jc], p_buf[g], preferred_element_type=F32)
            acc_ref[g] = pv if alphas[g] is None else alphas[g] * acc_ref[g] + pv

    for g in range(group):
        gate = jnp.dot(km_ref[:, g * dh:(g + 1) * dh], qt_ref[g].astype(F32), preferred_element_type=F32,
                       precision=lax.Precision.HIGHEST)
        blk_id = lax.broadcasted_iota(jnp.int32, gate.shape, 0)
        gv = jnp.where(blk_id < qi, gate, neg)
        bias = jnp.full(gate.shape, neg, F32)
        for _ in range(topk):
            m = jnp.max(gv, axis=0, keepdims=True)
            first = jnp.min(jnp.where(gv == m, blk_id, n_blk), axis=0, keepdims=True)
            bias = jnp.where((blk_id == first) & (m > neg), 0.0, bias)
            gv = jnp.where(blk_id == first, neg, gv)
        sel_ref[g] = bias

    key_id = lax.broadcasted_iota(jnp.int32, (blk, blk), 0)
    qry_id = lax.broadcasted_iota(jnp.int32, (blk, blk), 1)
    for g in range(group):
        sb_ref[g] = jnp.where(key_id <= qry_id, scores(g, qi), neg)
    stage_scores(sa_ref, 0)
    ms, alphas = stage_softmax(sb_ref, pb_ref, None)
    stage_pv(pb_ref, qi, alphas)

    def body(t, ms):
        j = 2 * t
        ms, alphas = stage_softmax(sa_ref, pa_ref, ms)
        stage_scores(sb_ref, j + 1)
        stage_pv(pa_ref, j, alphas)
        ms, alphas = stage_softmax(sb_ref, pb_ref, ms)
        stage_scores(sa_ref, j + 2)
        stage_pv(pb_ref, j + 1, alphas)
        return ms

    lax.fori_loop(0, (qi + 1) // 2, body, ms)
    for g in range(group):
        acc = acc_ref[g]
        o_ref[:, g * dh:(g + 1) * dh] = (acc[:dh] / acc[dh:dh + 1]).T.astype(BF16)


def _moba_attention(qt, k, vt, k_mean, batch, seq):
    heads, n_tiles, dh, blk = qt.shape
    dv_rows = vt.shape[2]
    n_blk = seq // blk
    topk = min(MOBA_TOPK, n_blk)
    width = heads * dh
    group = MOBA_HEAD_GROUP
    km = k_mean.reshape(batch, n_blk, width)
    per_seq = lambda b, h, i: (h, b, 0, 0)
    est = 2 * 2 * group * seq * dh * 2 + 2 * group * blk * dh * 4 + 16 * blk * blk * 4
    return pl.pallas_call(
        functools.partial(_moba_kernel, topk=topk, group=group, dh=dh),
        grid=(batch, heads // group, n_blk),
        in_specs=[
            pl.BlockSpec((group, None, dh, blk), lambda b, h, i: (h, b * n_blk + i, 0, 0)),
            pl.BlockSpec((group, n_blk, blk, dh), per_seq),
            pl.BlockSpec((group, n_blk, dv_rows, blk), per_seq),
            pl.BlockSpec((None, n_blk, group * dh), lambda b, h, i: (b, 0, h)),
        ],
        out_specs=pl.BlockSpec((blk, group * dh), lambda b, h, i: (b * n_blk + i, h)),
        out_shape=jax.ShapeDtypeStruct((n_tiles * blk, width), BF16),
        scratch_shapes=[pltpu.VMEM((group, n_blk, blk), F32), pltpu.VMEM((group, dv_rows, blk), F32),
                        pltpu.VMEM((group, blk, blk), F32), pltpu.VMEM((group, blk, blk), F32),
                        pltpu.VMEM((group, blk, blk), BF16), pltpu.VMEM((group, blk, blk), BF16)],
        compiler_params=pltpu.CompilerParams(
            dimension_semantics=("arbitrary", "arbitrary", "arbitrary"), vmem_limit_bytes=_vmem_limit(est)),
        name="moba_attention",
    )(qt, k, vt, km)


def kernel(x, ret_norm, ret_w_in, ret_w_out, kv_norm, w_kv, attn_norm, w_q, w_o, ffn_norm, router_group_w, router_group_b, router_expert_w, router_expert_b, expert_w_gate, expert_w_up, expert_w_down, final_norm):
    B, S, D = x.shape
    T = B * S
    assert S % MOBA_BLOCK == 0 and S % RET_KERNEL_CHUNK == 0 and T % MOE_TILE == 0
    assert ret_norm.shape[0] == 1 and attn_norm.shape[0] == 1 and ffn_norm.shape[0] == 2
    h = x.reshape(T, D)
    final_g = final_norm[None, :]

    def proj_router(y, w, resid, layer, name):
        return _proj_router(y, w.astype(BF16), resid, ffn_norm[layer][None, :], router_group_w[layer],
                            router_group_b[layer], router_expert_w[layer], router_expert_b[layer], name)

    def moe(h_aug, route, counts, layer, last):
        return _moe_layer(h_aug, route, counts, ffn_norm[layer][None, :], final_g,
                          expert_w_gate, expert_w_up, expert_w_down, layer, final_norm=last)

    q, k, v, gate = _ret_in_proj(h, ret_norm[0][None, :], ret_w_in[0].astype(BF16), S)
    y = _ret_core(q, k, v, gate, B, S)
    h = moe(*proj_router(y, ret_w_out[0], h, 0, "ret_out_proj_router"), 0, False)

    k2, vt2, qt2, k_mean = _kvq_proj(h, kv_norm[None, :], attn_norm[0][None, :],
                                     w_kv.astype(BF16), w_q[0].astype(BF16), S)
    o = _moba_attention(qt2, k2, vt2, k_mean, B, S)
    h = moe(*proj_router(o, w_o[0], h, 1, "attn_out_proj_router"), 1, True)
    return h.reshape(B, S, D)
```

```python
import functools

import jax
import jax.numpy as jnp
from jax import lax
from jax.experimental import pallas as pl
from jax.experimental.pallas import tpu as pltpu

F32 = jnp.float32
BF16 = jnp.bfloat16

NORM_EPS = 1e-6
RET_HEADS = 4
RET_ROT_BASE = 10000.0
ATT_HEADS = 8
ROPE_FRACTION = 4
ROPE_THETA = 500000.0
MOBA_BLOCK = 256
MOBA_TOPK = 3
N_GROUPS = 4
EXPERTS_PER_GROUP = 4
PAIR_LO = (0, 0, 1, 1, 0, 2)
PAIR_HI = (1, 2, 2, 3, 3, 3)
N_PAIRS = len(PAIR_LO)
N_CLASSES = N_GROUPS * N_PAIRS

LANES = 128
SUBLANES = 8
RET_KERNEL_CHUNK = 256
MOE_TILE = 128
ROUTE_ROWS = 32
DMA_ISSUE_UNROLL = 8
MOBA_HEAD_GROUP = 4
MOBA_KEY_CHUNK = 64
VT_ONES_ROWS = 16
LOG2_E = 1.4426950408889634
V7X_VMEM_BYTES = 64 * 1024 * 1024


def _vmem_limit(estimate_bytes):
    return int(min(V7X_VMEM_BYTES - 8 * 1024 * 1024, max(32 * 1024 * 1024, estimate_bytes * 5 // 4)))


def _rms_scale(x):
    return lax.rsqrt(jnp.mean(x * x, axis=-1, keepdims=True) + NORM_EPS)


def _silu(a):
    return a * jax.nn.sigmoid(a)


def _ret_in_kernel(x_ref, g_ref, w_ref, cb_ref, sb_ref, cr_ref, sr_ref, q_ref, k_ref, v_ref, gate_ref,
                   *, heads, dk, dv):
    x = x_ref[...]
    xn = (x * _rms_scale(x) * g_ref[...]).astype(BF16)
    cos, sin = _tile_cos_sin(cb_ref, sb_ref, cr_ref, sr_ref)
    half = dk // 2
    k_scale = dk ** -0.5

    def rot_store(dst_ref, col0, h, scale):
        p = jnp.dot(xn, w_ref[:, col0 + h * dk:col0 + (h + 1) * dk], preferred_element_type=F32)
        x1 = p[:, :half]
        x2 = p[:, half:]
        dst_ref[:, h * dk:h * dk + half] = ((x1 * cos - x2 * sin) * scale).astype(BF16)
        dst_ref[:, h * dk + half:(h + 1) * dk] = ((x2 * cos + x1 * sin) * scale).astype(BF16)

    for h in range(heads):
        rot_store(q_ref, 0, h, 1.0)
        rot_store(k_ref, heads * dk, h, k_scale)
    v0 = 2 * heads * dk
    g0 = v0 + heads * dv
    for h in range(heads):
        v_ref[:, h * dv:(h + 1) * dv] = jnp.dot(
            xn, w_ref[:, v0 + h * dv:v0 + (h + 1) * dv], preferred_element_type=F32).astype(BF16)
        gate_ref[:, h * dv:(h + 1) * dv] = jnp.dot(
            xn, w_ref[:, g0 + h * dv:g0 + (h + 1) * dv], preferred_element_type=F32).astype(BF16)


def _ret_in_proj(h, norm_g, w_bf16, seq, tm=256):
    T, D = h.shape
    heads = RET_HEADS
    dk = D // heads
    dv = 2 * dk
    n_cols = w_bf16.shape[1]
    n_pos_tiles = seq // tm
    half = dk // 2
    inv = 1.0 / (RET_ROT_BASE ** (jnp.arange(half, dtype=F32) / half))
    cos_b, sin_b, cos_r, sin_r = _angle_tables(inv, seq, tm)
    pos = lambda i: (i % n_pos_tiles, 0, 0)
    est = 2 * D * n_cols * 2 + 2 * tm * D * 4 + 2 * tm * n_cols * 2 + 4 * tm * LANES * 4
    return pl.pallas_call(
        functools.partial(_ret_in_kernel, heads=heads, dk=dk, dv=dv),
        grid=(T // tm,),
        in_specs=[
            pl.BlockSpec((tm, D), lambda i: (i, 0)),
            pl.BlockSpec((1, D), lambda i: (0, 0)),
            pl.BlockSpec((D, n_cols), lambda i: (0, 0)),
            pl.BlockSpec((None, 1, half), pos),
            pl.BlockSpec((None, 1, half), pos),
            pl.BlockSpec((tm, half), lambda i: (0, 0)),
            pl.BlockSpec((tm, half), lambda i: (0, 0)),
        ],
        out_specs=[
            pl.BlockSpec((tm, heads * dk), lambda i: (i, 0)),
            pl.BlockSpec((tm, heads * dk), lambda i: (i, 0)),
            pl.BlockSpec((tm, heads * dv), lambda i: (i, 0)),
            pl.BlockSpec((tm, heads * dv), lambda i: (i, 0)),
        ],
        out_shape=[
            jax.ShapeDtypeStruct((T, heads * dk), BF16),
            jax.ShapeDtypeStruct((T, heads * dk), BF16),
            jax.ShapeDtypeStruct((T, heads * dv), BF16),
            jax.ShapeDtypeStruct((T, heads * dv), BF16),
        ],
        compiler_params=pltpu.CompilerParams(
            dimension_semantics=("arbitrary",), vmem_limit_bytes=_vmem_limit(est)),
        name="ret_in_proj",
    )(h, norm_g, w_bf16, cos_b, sin_b, cos_r, sin_r)


def _ret_core_kernel(q_ref, k_ref, v_ref, g_ref, dm_ref, qd_ref, kd_ref, cd_ref, y_ref, state_ref, *, heads):
    @pl.when(pl.program_id(1) == 0)
    def _():
        state_ref[...] = jnp.zeros_like(state_ref)

    dk = q_ref.shape[1] // heads
    dv = v_ref.shape[1] // heads
    for h in range(heads):
        q = q_ref[:, h * dk:(h + 1) * dk]
        k = k_ref[:, h * dk:(h + 1) * dk]
        v = v_ref[:, h * dv:(h + 1) * dv]
        s = lax.dot_general(q, k, (((1,), (1,)), ((), ())), preferred_element_type=F32) * dm_ref[h]
        o = jnp.dot(s.astype(BF16), v, preferred_element_type=F32)
        state = state_ref[h]
        o = o + qd_ref[h] * jnp.dot(q, state.astype(BF16), preferred_element_type=F32)
        ks = (k.astype(F32) * kd_ref[h]).astype(BF16)
        state_ref[h] = state * cd_ref[h] + lax.dot_general(
            ks, v, (((0,), (0,)), ((), ())), preferred_element_type=F32)
        o = o * _rms_scale(o)
        y_ref[:, h * dv:(h + 1) * dv] = (_silu(g_ref[:, h * dv:(h + 1) * dv].astype(F32)) * o).astype(BF16)


def _ret_decay_tables(heads, chunk, dv):
    log_gamma = jnp.log1p(-jnp.power(2.0, -5.0 - jnp.arange(heads, dtype=F32)))
    idx = jnp.arange(chunk, dtype=F32)
    diff = idx[:, None] - idx[None, :]
    dmask = jnp.where(diff >= 0, jnp.exp(log_gamma[:, None, None] * jnp.maximum(diff, 0.0)), 0.0)
    qd = jnp.exp(log_gamma[:, None] * (idx + 1.0))[:, :, None]
    kd = jnp.exp(log_gamma[:, None] * (chunk - 1.0 - idx))[:, :, None]
    cd = jnp.broadcast_to(jnp.exp(log_gamma * chunk)[:, None, None], (heads, 1, dv))
    return dmask, qd, kd, cd


def _ret_core(q, k, v, gate, batch, seq):
    T = q.shape[0]
    heads = RET_HEADS
    dk = q.shape[1] // heads
    dv = v.shape[1] // heads
    cc = RET_KERNEL_CHUNK
    nc = seq // cc
    dmask, qd, kd, cd = _ret_decay_tables(heads, cc, dv)
    row = lambda b, c: (b * nc + c, 0)
    tab = lambda b, c: (0, 0, 0)
    return pl.pallas_call(
        functools.partial(_ret_core_kernel, heads=heads),
        grid=(batch, nc),
        in_specs=[
            pl.BlockSpec((cc, heads * dk), row),
            pl.BlockSpec((cc, heads * dk), row),
            pl.BlockSpec((cc, heads * dv), row),
            pl.BlockSpec((cc, heads * dv), row),
            pl.BlockSpec((heads, cc, cc), tab),
            pl.BlockSpec((heads, cc, 1), tab),
            pl.BlockSpec((heads, cc, 1), tab),
            pl.BlockSpec((heads, 1, dv), tab),
        ],
        out_specs=pl.BlockSpec((cc, heads * dv), row),
        out_shape=jax.ShapeDtypeStruct((T, heads * dv), BF16),
        scratch_shapes=[pltpu.VMEM((heads, dk, dv), F32)],
        compiler_params=pltpu.CompilerParams(dimension_semantics=("arbitrary", "arbitrary")),
        name="ret_core",
    )(q, k, v, gate, dmask, qd, kd, cd)


def _proj_router_kernel(y_ref, w_ref, r_ref, g_ref, wr_ref, br_ref, haug_ref, route_ref, counts_ref, run_ref,
                        *, groups, experts, d_model):
    @pl.when(pl.program_id(0) == 0)
    def _():
        run_ref[...] = jnp.zeros_like(run_ref)

    h = r_ref[...] + jnp.dot(y_ref[...], w_ref[...], preferred_element_type=F32)
    xn = h * _rms_scale(h) * g_ref[...]
    xh = xn.astype(BF16)
    xl = (xn - xh.astype(F32)).astype(BF16)
    hi_both = jnp.dot(xh, wr_ref[...], preferred_element_type=F32)
    lo_hi = jnp.dot(xl, wr_ref[:, :LANES], preferred_element_type=F32)
    logits = hi_both[:, :LANES] + (hi_both[:, LANES:] + lo_hi) + br_ref[...]
    lt = logits.T[:ROUTE_ROWS]
    tm = lt.shape[1]
    row = lax.broadcasted_iota(jnp.int32, lt.shape, 0)
    neg = -jnp.inf

    def argmax_first(vals):
        m = jnp.max(vals, axis=0, keepdims=True)
        first = jnp.min(jnp.where(vals == m, row, ROUTE_ROWS), axis=0, keepdims=True)
        return m, first

    gl = jnp.where(row < groups, lt, neg)
    gmax, gidx = argmax_first(gl)
    gsum = jnp.sum(jnp.where(row < groups, jnp.exp(gl - gmax), 0.0), axis=0, keepdims=True)
    g_val = 1.0 / gsum
    e0 = groups + experts * gidx
    el = jnp.where((row >= e0) & (row < e0 + experts), lt, neg)
    m1, i1 = argmax_first(el)
    m2, i2 = argmax_first(jnp.where(row == i1, neg, el))
    t = jnp.exp(m2 - m1)
    w1 = 1.0 / (1.0 + t)
    w2 = t / (1.0 + t)
    first_is_lo = i1 < i2
    lo = jnp.minimum(i1, i2) - e0
    hi = jnp.maximum(i1, i2) - e0
    w_lo = g_val * jnp.where(first_is_lo, w1, w2)
    w_hi = g_val * jnp.where(first_is_lo, w2, w1)
    lex = (lo * (7 - lo)) // 2 + (hi - lo - 1)
    pair = jnp.where(lex == 2, 4, jnp.where(lex == 3, 2, jnp.where(lex == 4, 3, lex)))
    cls = gidx * N_PAIRS + pair

    onehot = row == cls
    oh = jnp.where(onehot, 1.0, 0.0)
    before = lax.broadcasted_iota(jnp.int32, (tm, tm), 0) < lax.broadcasted_iota(jnp.int32, (tm, tm), 1)
    prefix = jnp.dot(oh.astype(BF16), jnp.where(before, 1.0, 0.0).astype(BF16), preferred_element_type=F32)
    rank = jnp.sum(jnp.where(onehot, prefix + run_ref[:, 0:1], 0.0), axis=0, keepdims=True)
    run_new = run_ref[...] + jnp.sum(oh, axis=1, keepdims=True)
    run_ref[...] = run_new
    counts_ref[...] = run_new

    r8 = lax.broadcasted_iota(jnp.int32, route_ref.shape, 0)
    route_ref[...] = jnp.where(r8 == 0, cls, jnp.where(r8 == 1, rank.astype(jnp.int32), 0))
    rl = lax.broadcasted_iota(jnp.int32, (LANES, tm), 0)
    meta_t = jnp.where(rl == 0, w_lo, jnp.where(rl == 1, w_hi, 0.0))
    haug_ref[:, :d_model] = h
    haug_ref[:, d_model:] = meta_t.T


def _proj_router(y, w_bf16, resid, norm_g, w_rg, b_rg, w_re, b_re, name, tm=512):
    T, K = y.shape
    D = w_bf16.shape[1]
    G, E = N_GROUPS, EXPERTS_PER_GROUP
    n_used = G + G * E
    wr = jnp.concatenate([w_rg, jnp.transpose(w_re, (1, 0, 2)).reshape(D, G * E)], axis=1)
    wr = jnp.pad(wr, ((0, 0), (0, LANES - n_used)))
    wr_hi = wr.astype(BF16)
    wr = jnp.concatenate([wr_hi, (wr - wr_hi.astype(F32)).astype(BF16)], axis=1)
    br = jnp.pad(jnp.concatenate([b_rg, b_re.reshape(G * E)]), (0, LANES - n_used))[None, :]
    est = 2 * K * D * 2 + 2 * tm * K * 2 + 6 * tm * D * 4 + 2 * D * LANES * 4 + 4 * tm * tm * 4
    const = lambda i: (0, 0)
    return pl.pallas_call(
        functools.partial(_proj_router_kernel, groups=G, experts=E, d_model=D),
        grid=(T // tm,),
        in_specs=[
            pl.BlockSpec((tm, K), lambda i: (i, 0)),
            pl.BlockSpec((K, D), const),
            pl.BlockSpec((tm, D), lambda i: (i, 0)),
            pl.BlockSpec((1, D), const),
            pl.BlockSpec((D, 2 * LANES), const),
            pl.BlockSpec((1, LANES), const),
        ],
        out_specs=[
            pl.BlockSpec((tm, D + LANES), lambda i: (i, 0)),
            pl.BlockSpec((8, tm), lambda i: (0, i)),
            pl.BlockSpec((ROUTE_ROWS, LANES), const),
        ],
        out_shape=[
            jax.ShapeDtypeStruct((T, D + LANES), F32),
            jax.ShapeDtypeStruct((8, T), jnp.int32),
            jax.ShapeDtypeStruct((ROUTE_ROWS, LANES), F32),
        ],
        scratch_shapes=[pltpu.VMEM((ROUTE_ROWS, LANES), F32)],
        compiler_params=pltpu.CompilerParams(
            dimension_semantics=("arbitrary",), vmem_limit_bytes=_vmem_limit(est)),
        name=name,
    )(y, w_bf16, resid, norm_g, wr, br)


def _tile_schedule(counts_blk, route, tm, n_tiles):
    counts = counts_blk[:N_CLASSES, 0].astype(jnp.int32)
    tiles_per = (counts + tm - 1) // tm
    tile_end = jnp.cumsum(tiles_per)
    tile_begin = tile_end - tiles_per
    cstart = jnp.cumsum(counts) - counts
    t = jnp.arange(n_tiles, dtype=jnp.int32)
    live = t < tile_end[-1]
    tq = jnp.where(live, t, tile_end[-1] - 1)
    tcls = jnp.sum((tile_end[None, :] <= tq[:, None]).astype(jnp.int32), axis=1)
    k = tq - tile_begin[tcls]
    nvalid = jnp.where(live, jnp.clip(counts[tcls] - k * tm, 0, tm), 0).astype(jnp.int32)
    base = jnp.where(live, cstart[tcls] + k * tm, 0).astype(jnp.int32)
    pair = tcls % N_PAIRS
    tile_g = (tcls // N_PAIRS).astype(jnp.int32)
    tile_lo = jnp.asarray(PAIR_LO, jnp.int32)[pair]
    tile_hi = jnp.asarray(PAIR_HI, jnp.int32)[pair]
    sorted_pos = (cstart[route[0]] + route[1]).astype(jnp.int32)
    return tile_g, tile_lo, tile_hi, nvalid, base, sorted_pos


def _moe_kernel(tg_ref, tlo_ref, thi_ref, tnv_ref, tbase_ref, pos_ref,
                h_hbm, gn_ref, fn_ref, wg_lo, wu_lo, wd_lo, wg_hi, wu_hi, wd_hi,
                out_hbm, rt_ref, x0, x1, o0, o1, w_in_bf, w_out_bf, gsem, ssem,
                *, tm, d_model, n_tok, final_norm):
    i = pl.program_id(0)
    n = pl.num_programs(0)
    xbufs = (x0, x1)
    obufs = (o0, o1)

    def issue_gather(t, sl):
        base = tbase_ref[t]
        for r in range(tm):
            tok = rt_ref[base + r]
            pltpu.make_async_copy(h_hbm.at[pl.ds(tok, 1)], xbufs[sl].at[pl.ds(r, 1)], gsem.at[sl]).start()

    def issue_scatter(sl, base, n_valid):
        spare = n_tok + sl * tm
        for r in range(tm):
            dst = jnp.where(r < n_valid, rt_ref[base + r], spare + r)
            pltpu.make_async_copy(obufs[sl].at[pl.ds(r, 1)], out_hbm.at[pl.ds(dst, 1)], ssem.at[sl]).start()

    def wait_gather(sl):
        pltpu.make_async_copy(h_hbm.at[pl.ds(0, tm)], xbufs[sl], gsem.at[sl]).wait()

    def wait_scatter(sl):
        pltpu.make_async_copy(obufs[sl], out_hbm.at[pl.ds(0, tm)], ssem.at[sl]).wait()

    def live(t):
        return tnv_ref[jnp.maximum(t, 0)] > 0

    @pl.when(i == 0)
    def _():
        o0[...] = jnp.zeros_like(o0)
        o1[...] = jnp.zeros_like(o1)

        def place(g, c):
            for u in range(DMA_ISSUE_UNROLL):
                tok = g * DMA_ISSUE_UNROLL + u
                rt_ref[pos_ref[tok]] = tok
            return c
        lax.fori_loop(0, n_tok // DMA_ISSUE_UNROLL, place, 0)
        for r in range(tm):
            rt_ref[n_tok + r] = 0
        issue_gather(0, 0)
        issue_scatter(0, 0, 0)

    prev = jnp.maximum(i - 1, 0)
    group_changed = (i == 0) | (tg_ref[i] != tg_ref[prev])
    for which, t_ref, (wg, wu, wd) in ((0, tlo_ref, (wg_lo, wu_lo, wd_lo)), (1, thi_ref, (wg_hi, wu_hi, wd_hi))):
        @pl.when(group_changed | (t_ref[i] != t_ref[prev]))
        def _():
            w_in_bf[2 * which] = wg[...].astype(BF16)
            w_in_bf[2 * which + 1] = wu[...].astype(BF16)
            w_out_bf[which] = wd[...].astype(BF16)

    def tile(cur):
        oth = 1 - cur
        @pl.when((i == 0) | live(i - 1))
        def _():
            wait_gather(cur)

        @pl.when((i <= 1) | live(i - 2))
        def _():
            wait_scatter(cur)

        prev_base = tbase_ref[prev]
        prev_valid = jnp.where(i == 0, 0, tnv_ref[prev])

        @pl.when(live(i))
        def _():
            issue_gather(i + 1, oth)
            xa = xbufs[cur][...]
            x = xa[:, :d_model]
            xn = (x * _rms_scale(x) * gn_ref[...]).astype(BF16)

            def expert(which, w):
                a = jnp.dot(xn, w_in_bf[2 * which], preferred_element_type=F32)
                b = jnp.dot(xn, w_in_bf[2 * which + 1], preferred_element_type=F32)
                return jnp.dot((_silu(a) * b * w).astype(BF16), w_out_bf[which], preferred_element_type=F32)

            y = expert(0, xa[:, d_model:d_model + 1]) + expert(1, xa[:, d_model + 1:d_model + 2])
            o = x + y
            if final_norm:
                o = o * _rms_scale(o) * fn_ref[...]
            obufs[cur][...] = o
            issue_scatter(oth, prev_base, prev_valid)

        @pl.when(jnp.logical_not(live(i)) & (i >= 1) & live(i - 1))
        def _():
            issue_scatter(oth, prev_base, prev_valid)

        @pl.when((i == n - 1) & live(i - 1))
        def _():
            wait_scatter(oth)

    for parity in range(2):
        @pl.when(i % 2 == parity)
        def _():
            tile(parity)


def _moe_layer(h_aug, route, counts_blk, ffn_norm_g, final_norm_g, wg, wu, wd, layer, final_norm):
    T, DA = h_aug.shape
    D = DA - LANES
    F = wg.shape[-1]
    tm = MOE_TILE
    n_tiles = T // tm + N_CLASSES
    assert n_tiles % 2 == 0
    tile_g, tile_lo, tile_hi, nvalid, base, sorted_pos = _tile_schedule(counts_blk, route, tm, n_tiles)

    def w_spec(shape, which):
        def index(i, tg, tlo, thi, *_):
            return (layer, tg[i], (tlo if which == 0 else thi)[i], 0, 0)
        return pl.BlockSpec((None, None, None) + shape, index)

    est = 2 * 6 * D * F * 4 + 6 * D * F * 2 + 2 * tm * DA * 4 + 2 * tm * D * 4 + 8 * tm * F * 4
    return pl.pallas_call(
        functools.partial(_moe_kernel, tm=tm, d_model=D, n_tok=T, final_norm=final_norm),
        grid_spec=pltpu.PrefetchScalarGridSpec(
            num_scalar_prefetch=6,
            grid=(n_tiles,),
            in_specs=[
                pl.BlockSpec(memory_space=pl.ANY),
                pl.BlockSpec((1, D), lambda i, *_: (0, 0)),
                pl.BlockSpec((1, D), lambda i, *_: (0, 0)),
                w_spec((D, F), 0), w_spec((D, F), 0), w_spec((F, D), 0),
                w_spec((D, F), 1), w_spec((D, F), 1), w_spec((F, D), 1),
            ],
            out_specs=pl.BlockSpec(memory_space=pl.ANY),
            scratch_shapes=[
                pltpu.SMEM((T + tm,), jnp.int32),
                pltpu.VMEM((tm, DA), F32),
                pltpu.VMEM((tm, DA), F32),
                pltpu.VMEM((tm, D), F32),
                pltpu.VMEM((tm, D), F32),
                pltpu.VMEM((4, D, F), BF16),
                pltpu.VMEM((2, F, D), BF16),
                pltpu.SemaphoreType.DMA((2,)),
                pltpu.SemaphoreType.DMA((2,)),
            ],
        ),
        out_shape=jax.ShapeDtypeStruct((T + 2 * tm, D), F32),
        compiler_params=pltpu.CompilerParams(
            dimension_semantics=("arbitrary",), vmem_limit_bytes=_vmem_limit(est)),
        name="moe_experts",
    )(tile_g, tile_lo, tile_hi, nvalid, base, sorted_pos,
      h_aug, ffn_norm_g, final_norm_g, wg, wu, wd, wg, wu, wd)


def _partial_rotary(xh, cosf, sinp, sinm, half):
    lanes = xh.shape[-1]
    return xh * cosf + pltpu.roll(xh, half, 1) * sinp + pltpu.roll(xh, lanes - half, 1) * sinm


def _tile_cos_sin(cb_ref, sb_ref, cr_ref, sr_ref):
    cb, sb, cr, sr = cb_ref[...], sb_ref[...], cr_ref[...], sr_ref[...]
    return cb * cr - sb * sr, sb * cr + cb * sr


def _kvq_kernel(h_ref, gkv_ref, gq_ref, wkv_ref, wq_ref, cb_ref, sb_ref, cr_ref, sr_ref,
                k_ref, vt_ref, qt_ref, km_ref, *, heads, dh, half):
    x = h_ref[...]
    xr = x * _rms_scale(x)
    xkv = (xr * gkv_ref[...]).astype(BF16)
    xq = (xr * gq_ref[...]).astype(BF16)
    cosf, sin_all = _tile_cos_sin(cb_ref, sb_ref, cr_ref, sr_ref)
    lane = lax.broadcasted_iota(jnp.int32, sin_all.shape, 1)
    sinp = jnp.where(lane >= half, sin_all, 0.0)
    sinm = jnp.where(lane < half, -sin_all, 0.0)
    width = heads * dh
    q_scale = dh ** -0.5 * LOG2_E
    pair_w = 2 * dh
    for c0 in range(0, width, pair_w):
        kk = jnp.dot(xkv, wkv_ref[:, c0:c0 + pair_w], preferred_element_type=F32)
        vv = jnp.dot(xkv, wkv_ref[:, width + c0:width + c0 + pair_w], preferred_element_type=F32)
        qq = jnp.dot(xq, wq_ref[:, c0:c0 + pair_w], preferred_element_type=F32)
        for u in range(2):
            hh = c0 // dh + u
            cols = slice(hh * dh, (hh + 1) * dh)
            kh = _partial_rotary(kk[:, u * dh:(u + 1) * dh], cosf, sinp, sinm, half)
            k_ref[hh] = kh.astype(BF16)
            km_ref[:, cols] = jnp.mean(kh, axis=0, keepdims=True)
            vt_ref[hh, :dh] = vv[:, u * dh:(u + 1) * dh].T.astype(BF16)
            vt_ref[hh, dh:] = jnp.ones((VT_ONES_ROWS, vv.shape[0]), BF16)
            qh = _partial_rotary(qq[:, u * dh:(u + 1) * dh], cosf, sinp, sinm, half)
            qt_ref[hh] = (qh * q_scale).T.astype(BF16)


def _angle_tables(inv_freq, seq, tm):
    base = jnp.arange(0, seq, tm, dtype=F32)[:, None] * inv_freq[None, :]
    offs = jnp.arange(tm, dtype=F32)[:, None] * inv_freq[None, :]
    return jnp.cos(base)[:, None, :], jnp.sin(base)[:, None, :], jnp.cos(offs), jnp.sin(offs)


def _rope_tables(seq, dh, tm):
    rot = dh // ROPE_FRACTION
    half = rot // 2
    inv = 1.0 / (ROPE_THETA ** (jnp.arange(half, dtype=F32) / half))
    inv_lanes = jnp.concatenate([inv, inv, jnp.zeros((dh - rot,), F32)])
    return _angle_tables(inv_lanes, seq, tm) + (half,)


def _kvq_proj(h, n_tok, kv_norm_g, q_norm_g, wkv_bf16, wq_bf16, seq):
    T, D = n_tok, h.shape[1]
    heads = ATT_HEADS
    dh = D // heads
    tm = MOBA_BLOCK
    n_pos_tiles = seq // tm
    cos_b, sin_b, cos_r, sin_r, half = _rope_tables(seq, dh, tm)
    width = heads * dh
    est = 2 * (D * 2 * width + D * width) * 2 + 2 * tm * D * 4 + 6 * tm * width * 2 + 8 * tm * dh * 4
    pos = lambda i: (i % n_pos_tiles, 0, 0)
    const = lambda i: (0, 0)
    per_block = lambda i: (0, i, 0, 0)
    return pl.pallas_call(
        functools.partial(_kvq_kernel, heads=heads, dh=dh, half=half),
        grid=(T // tm,),
        in_specs=[
            pl.BlockSpec((tm, D), lambda i: (i, 0)),
            pl.BlockSpec((1, D), const),
            pl.BlockSpec((1, D), const),
            pl.BlockSpec((D, 2 * width), const),
            pl.BlockSpec((D, width), const),
            pl.BlockSpec((None, 1, dh), pos),
            pl.BlockSpec((None, 1, dh), pos),
            pl.BlockSpec((tm, dh), const),
            pl.BlockSpec((tm, dh), const),
        ],
        out_specs=[
            pl.BlockSpec((heads, None, tm, dh), per_block),
            pl.BlockSpec((heads, None, dh + VT_ONES_ROWS, tm), per_block),
            pl.BlockSpec((heads, None, dh, tm), per_block),
            pl.BlockSpec((None, 1, width), lambda i: (i, 0, 0)),
        ],
        out_shape=[
            jax.ShapeDtypeStruct((heads, T // tm, tm, dh), BF16),
            jax.ShapeDtypeStruct((heads, T // tm, dh + VT_ONES_ROWS, tm), BF16),
            jax.ShapeDtypeStruct((heads, T // tm, dh, tm), BF16),
            jax.ShapeDtypeStruct((T // tm, 1, width), F32),
        ],
        compiler_params=pltpu.CompilerParams(
            dimension_semantics=("arbitrary",), vmem_limit_bytes=_vmem_limit(est)),
        name="kvq_proj",
    )(h, kv_norm_g, q_norm_g, wkv_bf16, wq_bf16, cos_b, sin_b, cos_r, sin_r)


def _moba_kernel(qt_ref, k_ref, vt_ref, km_ref, o_ref, sel_ref, acc_ref, sa_ref, sb_ref, pa_ref, pb_ref,
                 *, topk, group, dh):
    qi = pl.program_id(2)
    n_blk = km_ref.shape[0]
    blk = sa_ref.shape[1]
    neg = -jnp.inf
    chunks = [slice(c, c + MOBA_KEY_CHUNK) for c in range(0, blk, MOBA_KEY_CHUNK)]

    def scores(g, j):
        return jnp.dot(k_ref[g, j], qt_ref[g], preferred_element_type=F32)

    def stage_scores(s_buf, j):
        jc = jnp.minimum(j, n_blk - 1)
        for g in range(group):
            s_buf[g] = scores(g, jc) + sel_ref[g, pl.ds(jc, 1), :]

    def stage_softmax(s_buf, p_buf, ms):
        new_m, alphas = [], []
        for g in range(group):
            part = s_buf[g, chunks[0], :]
            for c in chunks[1:]:
                part = jnp.maximum(part, s_buf[g, c, :])
            m_blk = jnp.max(part, axis=0, keepdims=True)
            m_new = m_blk if ms is None else jnp.maximum(ms[g], m_blk)
            for c in chunks:
                p_buf[g, c, :] = jnp.exp2(s_buf[g, c, :] - m_new).astype(BF16)
            new_m.append(m_new)
            alphas.append(None if ms is None else jnp.exp2(ms[g] - m_new))
        return tuple(new_m), alphas

    def stage_pv(p_buf, j, alphas):
        jc = jnp.minimum(j, n_blk - 1)
        for g in range(group):
            pv = jnp.dot(vt_ref[g, jc], p_buf[g], preferred_element_type=F32)
            acc_ref[g] = pv if alphas[g] is None else alphas[g] * acc_ref[g] + pv

    for g in range(group):
        gate = jnp.dot(km_ref[:, g * dh:(g + 1) * dh], qt_ref[g].astype(F32), preferred_element_type=F32,
                       precision=lax.Precision.HIGHEST)
        blk_id = lax.broadcasted_iota(jnp.int32, gate.shape, 0)
        gv = jnp.where(blk_id < qi, gate, neg)
        bias = jnp.full(gate.shape, neg, F32)
        for _ in range(topk):
            m = jnp.max(gv, axis=0, keepdims=True)
            first = jnp.min(jnp.where(gv == m, blk_id, n_blk), axis=0, keepdims=True)
            bias = jnp.where((blk_id == first) & (m > neg), 0.0, bias)
            gv = jnp.where(blk_id == first, neg, gv)
        sel_ref[g] = bias

    key_id = lax.broadcasted_iota(jnp.int32, (blk, blk), 0)
    qry_id = lax.broadcasted_iota(jnp.int32, (blk, blk), 1)
    for g in range(group):
        sb_ref[g] = jnp.where(key_id <= qry_id, scores(g, qi), neg)
    stage_scores(sa_ref, 0)
    ms, alphas = stage_softmax(sb_ref, pb_ref, None)
    stage_pv(pb_ref, qi, alphas)

    def body(t, ms):
        j = 2 * t
        ms, alphas = stage_softmax(sa_ref, pa_ref, ms)
        stage_scores(sb_ref, j + 1)
        stage_pv(pa_ref, j, alphas)
        ms, alphas = stage_softmax(sb_ref, pb_ref, ms)
        stage_scores(sa_ref, j + 2)
        stage_pv(pb_ref, j + 1, alphas)
        return ms

    lax.fori_loop(0, (qi + 1) // 2, body, ms)
    for g in range(group):
        acc = acc_ref[g]
        o_ref[:, g * dh:(g + 1) * dh] = (acc[:dh] / acc[dh:dh + 1]).T.astype(BF16)


def _moba_attention(qt, k, vt, k_mean, batch, seq):
    heads, n_tiles, dh, blk = qt.shape
    dv_rows = vt.shape[2]
    n_blk = seq // blk
    topk = min(MOBA_TOPK, n_blk)
    width = heads * dh
    group = MOBA_HEAD_GROUP
    km = k_mean.reshape(batch, n_blk, width)
    per_seq = lambda b, h, i: (h, b, 0, 0)
    est = 2 * 2 * group * seq * dh * 2 + 2 * group * blk * dh * 4 + 16 * blk * blk * 4
    return pl.pallas_call(
        functools.partial(_moba_kernel, topk=topk, group=group, dh=dh),
        grid=(batch, heads // group, n_blk),
        in_specs=[
            pl.BlockSpec((group, None, dh, blk), lambda b, h, i: (h, b * n_blk + i, 0, 0)),
            pl.BlockSpec((group, n_blk, blk, dh), per_seq),
            pl.BlockSpec((group, n_blk, dv_rows, blk), per_seq),
            pl.BlockSpec((None, n_blk, group * dh), lambda b, h, i: (b, 0, h)),
        ],
        out_specs=pl.BlockSpec((blk, group * dh), lambda b, h, i: (b * n_blk + i, h)),
        out_shape=jax.ShapeDtypeStruct((n_tiles * blk, width), BF16),
        scratch_shapes=[pltpu.VMEM((group, n_blk, blk), F32), pltpu.VMEM((group, dv_rows, blk), F32),
                        pltpu.VMEM((group, blk, blk), F32), pltpu.VMEM((group, blk, blk), F32),
                        pltpu.VMEM((group, blk, blk), BF16), pltpu.VMEM((group, blk, blk), BF16)],
        compiler_params=pltpu.CompilerParams(
            dimension_semantics=("arbitrary", "arbitrary", "arbitrary"), vmem_limit_bytes=_vmem_limit(est)),
        name="moba_attention",
    )(qt, k, vt, km)


def kernel(x, ret_norm, ret_w_in, ret_w_out, kv_norm, w_kv, attn_norm, w_q, w_o, ffn_norm, router_group_w, router_group_b, router_expert_w, router_expert_b, expert_w_gate, expert_w_up, expert_w_down, final_norm):
    B, S, D = x.shape
    T = B * S
    assert S % MOBA_BLOCK == 0 and S % RET_KERNEL_CHUNK == 0 and T % MOE_TILE == 0
    assert ret_norm.shape[0] == 1 and attn_norm.shape[0] == 1 and ffn_norm.shape[0] == 2
    h = x.reshape(T, D)
    final_g = final_norm[None, :]

    def proj_router(y, w, resid, layer, name):
        return _proj_router(y, w.astype(BF16), resid, ffn_norm[layer][None, :], router_group_w[layer],
                            router_group_b[layer], router_expert_w[layer], router_expert_b[layer], name)

    def moe(h_aug, route, counts, layer, last):
        return _moe_layer(h_aug, route, counts, ffn_norm[layer][None, :], final_g,
                          expert_w_gate, expert_w_up, expert_w_down, layer, final_norm=last)

    q, k, v, gate = _ret_in_proj(h, ret_norm[0][None, :], ret_w_in[0].astype(BF16), S)
    y = _ret_core(q, k, v, gate, B, S)
    h = moe(*proj_router(y, ret_w_out[0], h, 0, "ret_out_proj_router"), 0, False)

    k2, vt2, qt2, k_mean = _kvq_proj(h, T, kv_norm[None, :], attn_norm[0][None, :],
                                     w_kv.astype(BF16), w_q[0].astype(BF16), S)
    o = _moba_attention(qt2, k2, vt2, k_mean, B, S)
    h = moe(*proj_router(o, w_o[0], h, 1, "attn_out_proj_router"), 1, True)
    return h[:T].reshape(B, S, D)
```

```python
import functools

import jax
import jax.numpy as jnp
from jax import lax
from jax.experimental import pallas as pl
from jax.experimental.pallas import tpu as pltpu

F32 = jnp.float32
BF16 = jnp.bfloat16

NORM_EPS = 1e-6
RET_HEADS = 4
RET_ROT_BASE = 10000.0
ATT_HEADS = 8
ROPE_FRACTION = 4
ROPE_THETA = 500000.0
MOBA_BLOCK = 256
MOBA_TOPK = 3
N_GROUPS = 4
EXPERTS_PER_GROUP = 4
PAIR_LO = (0, 0, 1, 1, 0, 2)
PAIR_HI = (1, 2, 2, 3, 3, 3)
N_PAIRS = len(PAIR_LO)
N_CLASSES = N_GROUPS * N_PAIRS

LANES = 128
SUBLANES = 8
RET_KERNEL_CHUNK = 256
MOE_TILE = 128
ROUTE_ROWS = 32
MOBA_HEAD_GROUP = 4
MOBA_KEY_CHUNK = 64
VT_ONES_ROWS = 16
LOG2_E = 1.4426950408889634
V7X_VMEM_BYTES = 64 * 1024 * 1024


def _vmem_limit(estimate_bytes):
    return int(min(V7X_VMEM_BYTES - 8 * 1024 * 1024, max(32 * 1024 * 1024, estimate_bytes * 5 // 4)))


def _rms_scale(x):
    return lax.rsqrt(jnp.mean(x * x, axis=-1, keepdims=True) + NORM_EPS)


def _silu(a):
    return a * jax.nn.sigmoid(a)


def _ret_in_kernel(x_ref, g_ref, w_ref, cb_ref, sb_ref, cr_ref, sr_ref, q_ref, k_ref, v_ref, gate_ref,
                   *, heads, dk, dv):
    x = x_ref[...]
    xn = (x * _rms_scale(x) * g_ref[...]).astype(BF16)
    cos, sin = _tile_cos_sin(cb_ref, sb_ref, cr_ref, sr_ref)
    half = dk // 2
    k_scale = dk ** -0.5

    def rot_store(dst_ref, col0, h, scale):
        p = jnp.dot(xn, w_ref[:, col0 + h * dk:col0 + (h + 1) * dk], preferred_element_type=F32)
        x1 = p[:, :half]
        x2 = p[:, half:]
        dst_ref[:, h * dk:h * dk + half] = ((x1 * cos - x2 * sin) * scale).astype(BF16)
        dst_ref[:, h * dk + half:(h + 1) * dk] = ((x2 * cos + x1 * sin) * scale).astype(BF16)

    for h in range(heads):
        rot_store(q_ref, 0, h, 1.0)
        rot_store(k_ref, heads * dk, h, k_scale)
    v0 = 2 * heads * dk
    g0 = v0 + heads * dv
    for h in range(heads):
        v_ref[:, h * dv:(h + 1) * dv] = jnp.dot(
            xn, w_ref[:, v0 + h * dv:v0 + (h + 1) * dv], preferred_element_type=F32).astype(BF16)
        gate_ref[:, h * dv:(h + 1) * dv] = jnp.dot(
            xn, w_ref[:, g0 + h * dv:g0 + (h + 1) * dv], preferred_element_type=F32).astype(BF16)


def _ret_in_proj(h, norm_g, w_bf16, seq, tm=256):
    T, D = h.shape
    heads = RET_HEADS
    dk = D // heads
    dv = 2 * dk
    n_cols = w_bf16.shape[1]
    n_pos_tiles = seq // tm
    half = dk // 2
    inv = 1.0 / (RET_ROT_BASE ** (jnp.arange(half, dtype=F32) / half))
    cos_b, sin_b, cos_r, sin_r = _angle_tables(inv, seq, tm)
    pos = lambda i: (i % n_pos_tiles, 0, 0)
    est = 2 * D * n_cols * 2 + 2 * tm * D * 4 + 2 * tm * n_cols * 2 + 4 * tm * LANES * 4
    return pl.pallas_call(
        functools.partial(_ret_in_kernel, heads=heads, dk=dk, dv=dv),
        grid=(T // tm,),
        in_specs=[
            pl.BlockSpec((tm, D), lambda i: (i, 0)),
            pl.BlockSpec((1, D), lambda i: (0, 0)),
            pl.BlockSpec((D, n_cols), lambda i: (0, 0)),
            pl.BlockSpec((None, 1, half), pos),
            pl.BlockSpec((None, 1, half), pos),
            pl.BlockSpec((tm, half), lambda i: (0, 0)),
            pl.BlockSpec((tm, half), lambda i: (0, 0)),
        ],
        out_specs=[
            pl.BlockSpec((tm, heads * dk), lambda i: (i, 0)),
            pl.BlockSpec((tm, heads * dk), lambda i: (i, 0)),
            pl.BlockSpec((tm, heads * dv), lambda i: (i, 0)),
            pl.BlockSpec((tm, heads * dv), lambda i: (i, 0)),
        ],
        out_shape=[
            jax.ShapeDtypeStruct((T, heads * dk), BF16),
            jax.ShapeDtypeStruct((T, heads * dk), BF16),
            jax.ShapeDtypeStruct((T, heads * dv), BF16),
            jax.ShapeDtypeStruct((T, heads * dv), BF16),
        ],
        compiler_params=pltpu.CompilerParams(
            dimension_semantics=("arbitrary",), vmem_limit_bytes=_vmem_limit(est)),
        name="ret_in_proj",
    )(h, norm_g, w_bf16, cos_b, sin_b, cos_r, sin_r)


def _ret_core_kernel(q_ref, k_ref, v_ref, g_ref, dm_ref, qd_ref, kd_ref, cd_ref, y_ref, state_ref, *, heads):
    @pl.when(pl.program_id(1) == 0)
    def _():
        state_ref[...] = jnp.zeros_like(state_ref)

    dk = q_ref.shape[1] // heads
    dv = v_ref.shape[1] // heads
    for h in range(heads):
        q = q_ref[:, h * dk:(h + 1) * dk]
        k = k_ref[:, h * dk:(h + 1) * dk]
        v = v_ref[:, h * dv:(h + 1) * dv]
        s = lax.dot_general(q, k, (((1,), (1,)), ((), ())), preferred_element_type=F32) * dm_ref[h]
        o = jnp.dot(s.astype(BF16), v, preferred_element_type=F32)
        state = state_ref[h]
        o = o + qd_ref[h] * jnp.dot(q, state.astype(BF16), preferred_element_type=F32)
        ks = (k.astype(F32) * kd_ref[h]).astype(BF16)
        state_ref[h] = state * cd_ref[h] + lax.dot_general(
            ks, v, (((0,), (0,)), ((), ())), preferred_element_type=F32)
        o = o * _rms_scale(o)
        y_ref[:, h * dv:(h + 1) * dv] = (_silu(g_ref[:, h * dv:(h + 1) * dv].astype(F32)) * o).astype(BF16)


def _ret_decay_tables(heads, chunk, dv):
    log_gamma = jnp.log1p(-jnp.power(2.0, -5.0 - jnp.arange(heads, dtype=F32)))
    idx = jnp.arange(chunk, dtype=F32)
    diff = idx[:, None] - idx[None, :]
    dmask = jnp.where(diff >= 0, jnp.exp(log_gamma[:, None, None] * jnp.maximum(diff, 0.0)), 0.0)
    qd = jnp.exp(log_gamma[:, None] * (idx + 1.0))[:, :, None]
    kd = jnp.exp(log_gamma[:, None] * (chunk - 1.0 - idx))[:, :, None]
    cd = jnp.broadcast_to(jnp.exp(log_gamma * chunk)[:, None, None], (heads, 1, dv))
    return dmask, qd, kd, cd


def _ret_core(q, k, v, gate, batch, seq):
    T = q.shape[0]
    heads = RET_HEADS
    dk = q.shape[1] // heads
    dv = v.shape[1] // heads
    cc = RET_KERNEL_CHUNK
    nc = seq // cc
    dmask, qd, kd, cd = _ret_decay_tables(heads, cc, dv)
    row = lambda b, c: (b * nc + c, 0)
    tab = lambda b, c: (0, 0, 0)
    return pl.pallas_call(
        functools.partial(_ret_core_kernel, heads=heads),
        grid=(batch, nc),
        in_specs=[
            pl.BlockSpec((cc, heads * dk), row),
            pl.BlockSpec((cc, heads * dk), row),
            pl.BlockSpec((cc, heads * dv), row),
            pl.BlockSpec((cc, heads * dv), row),
            pl.BlockSpec((heads, cc, cc), tab),
            pl.BlockSpec((heads, cc, 1), tab),
            pl.BlockSpec((heads, cc, 1), tab),
            pl.BlockSpec((heads, 1, dv), tab),
        ],
        out_specs=pl.BlockSpec((cc, heads * dv), row),
        out_shape=jax.ShapeDtypeStruct((T, heads * dv), BF16),
        scratch_shapes=[pltpu.VMEM((heads, dk, dv), F32)],
        compiler_params=pltpu.CompilerParams(dimension_semantics=("arbitrary", "arbitrary")),
        name="ret_core",
    )(q, k, v, gate, dmask, qd, kd, cd)


def _proj_router_kernel(y_ref, w_ref, r_ref, g_ref, wr_ref, br_ref, haug_ref, route_ref, counts_ref, run_ref,
                        *, groups, experts, d_model):
    @pl.when(pl.program_id(0) == 0)
    def _():
        run_ref[...] = jnp.zeros_like(run_ref)

    h = r_ref[...] + jnp.dot(y_ref[...], w_ref[...], preferred_element_type=F32)
    xn = h * _rms_scale(h) * g_ref[...]
    xh = xn.astype(BF16)
    xl = (xn - xh.astype(F32)).astype(BF16)
    hi_both = jnp.dot(xh, wr_ref[...], preferred_element_type=F32)
    lo_hi = jnp.dot(xl, wr_ref[:, :LANES], preferred_element_type=F32)
    logits = hi_both[:, :LANES] + (hi_both[:, LANES:] + lo_hi) + br_ref[...]
    lt = logits.T[:ROUTE_ROWS]
    tm = lt.shape[1]
    row = lax.broadcasted_iota(jnp.int32, lt.shape, 0)
    neg = -jnp.inf

    def argmax_first(vals):
        m = jnp.max(vals, axis=0, keepdims=True)
        first = jnp.min(jnp.where(vals == m, row, ROUTE_ROWS), axis=0, keepdims=True)
        return m, first

    gl = jnp.where(row < groups, lt, neg)
    gmax, gidx = argmax_first(gl)
    gsum = jnp.sum(jnp.where(row < groups, jnp.exp(gl - gmax), 0.0), axis=0, keepdims=True)
    g_val = 1.0 / gsum
    e0 = groups + experts * gidx
    el = jnp.where((row >= e0) & (row < e0 + experts), lt, neg)
    m1, i1 = argmax_first(el)
    m2, i2 = argmax_first(jnp.where(row == i1, neg, el))
    t = jnp.exp(m2 - m1)
    w1 = 1.0 / (1.0 + t)
    w2 = t / (1.0 + t)
    first_is_lo = i1 < i2
    lo = jnp.minimum(i1, i2) - e0
    hi = jnp.maximum(i1, i2) - e0
    w_lo = g_val * jnp.where(first_is_lo, w1, w2)
    w_hi = g_val * jnp.where(first_is_lo, w2, w1)
    lex = (lo * (7 - lo)) // 2 + (hi - lo - 1)
    pair = jnp.where(lex == 2, 4, jnp.where(lex == 3, 2, jnp.where(lex == 4, 3, lex)))
    cls = gidx * N_PAIRS + pair

    onehot = row == cls
    oh = jnp.where(onehot, 1.0, 0.0)
    before = lax.broadcasted_iota(jnp.int32, (tm, tm), 0) < lax.broadcasted_iota(jnp.int32, (tm, tm), 1)
    prefix = jnp.dot(oh.astype(BF16), jnp.where(before, 1.0, 0.0).astype(BF16), preferred_element_type=F32)
    rank = jnp.sum(jnp.where(onehot, prefix + run_ref[:, 0:1], 0.0), axis=0, keepdims=True)
    run_new = run_ref[...] + jnp.sum(oh, axis=1, keepdims=True)
    run_ref[...] = run_new
    counts_ref[...] = run_new

    r8 = lax.broadcasted_iota(jnp.int32, route_ref.shape, 0)
    route_ref[...] = jnp.where(r8 == 0, cls, jnp.where(r8 == 1, rank.astype(jnp.int32), 0))
    rl = lax.broadcasted_iota(jnp.int32, (LANES, tm), 0)
    meta_t = jnp.where(rl == 0, w_lo, jnp.where(rl == 1, w_hi, 0.0))
    haug_ref[:, :d_model] = h
    haug_ref[:, d_model:] = meta_t.T


def _proj_router(y, w_bf16, resid, norm_g, w_rg, b_rg, w_re, b_re, name, tm=512):
    T, K = y.shape
    D = w_bf16.shape[1]
    G, E = N_GROUPS, EXPERTS_PER_GROUP
    n_used = G + G * E
    wr = jnp.concatenate([w_rg, jnp.transpose(w_re, (1, 0, 2)).reshape(D, G * E)], axis=1)
    wr = jnp.pad(wr, ((0, 0), (0, LANES - n_used)))
    wr_hi = wr.astype(BF16)
    wr = jnp.concatenate([wr_hi, (wr - wr_hi.astype(F32)).astype(BF16)], axis=1)
    br = jnp.pad(jnp.concatenate([b_rg, b_re.reshape(G * E)]), (0, LANES - n_used))[None, :]
    est = 2 * K * D * 2 + 2 * tm * K * 2 + 6 * tm * D * 4 + 2 * D * LANES * 4 + 4 * tm * tm * 4
    const = lambda i: (0, 0)
    return pl.pallas_call(
        functools.partial(_proj_router_kernel, groups=G, experts=E, d_model=D),
        grid=(T // tm,),
        in_specs=[
            pl.BlockSpec((tm, K), lambda i: (i, 0)),
            pl.BlockSpec((K, D), const),
            pl.BlockSpec((tm, D), lambda i: (i, 0)),
            pl.BlockSpec((1, D), const),
            pl.BlockSpec((D, 2 * LANES), const),
            pl.BlockSpec((1, LANES), const),
        ],
        out_specs=[
            pl.BlockSpec((tm, D + LANES), lambda i: (i, 0)),
            pl.BlockSpec((8, tm), lambda i: (0, i)),
            pl.BlockSpec((ROUTE_ROWS, LANES), const),
        ],
        out_shape=[
            jax.ShapeDtypeStruct((T, D + LANES), F32),
            jax.ShapeDtypeStruct((8, T), jnp.int32),
            jax.ShapeDtypeStruct((ROUTE_ROWS, LANES), F32),
        ],
        scratch_shapes=[pltpu.VMEM((ROUTE_ROWS, LANES), F32)],
        compiler_params=pltpu.CompilerParams(
            dimension_semantics=("arbitrary",), vmem_limit_bytes=_vmem_limit(est)),
        name=name,
    )(y, w_bf16, resid, norm_g, wr, br)


def _tile_schedule(counts_blk, tm, n_tiles):
    counts = counts_blk[:N_CLASSES, 0].astype(jnp.int32)
    tiles_per = (counts + tm - 1) // tm
    tile_end = jnp.cumsum(tiles_per)
    tile_begin = tile_end - tiles_per
    cstart = jnp.cumsum(counts) - counts
    t = jnp.arange(n_tiles, dtype=jnp.int32)
    live = t < tile_end[-1]
    tq = jnp.where(live, t, tile_end[-1] - 1)
    tcls = jnp.sum((tile_end[None, :] <= tq[:, None]).astype(jnp.int32), axis=1)
    k = tq - tile_begin[tcls]
    nvalid = jnp.where(live, jnp.clip(counts[tcls] - k * tm, 0, tm), 0).astype(jnp.int32)
    base = jnp.where(live, cstart[tcls] + k * tm, 0).astype(jnp.int32)
    pair = tcls % N_PAIRS
    tile_g = (tcls // N_PAIRS).astype(jnp.int32)
    tile_lo = jnp.asarray(PAIR_LO, jnp.int32)[pair]
    tile_hi = jnp.asarray(PAIR_HI, jnp.int32)[pair]
    cstart_pad = jnp.pad(cstart, (0, ROUTE_ROWS - N_CLASSES)).astype(jnp.int32)
    return tile_g, tile_lo, tile_hi, nvalid, base, cstart_pad


def _moe_kernel(tg_ref, tlo_ref, thi_ref, tnv_ref, tbase_ref, cs_ref,
                h_hbm, route_ref, gn_ref, fn_ref, wg_lo, wu_lo, wd_lo, wg_hi, wu_hi, wd_hi,
                out_hbm, rt_ref, pos_smem, pos_vmem, x0, x1, x2, o0, o1, o2, w_in_bf, w_out_bf,
                gsem, ssem, psem, *, tm, d_model, n_tok, final_norm):
    i = pl.program_id(0)
    xbufs = (x0, x1, x2)
    obufs = (o0, o1, o2)
    depth = len(xbufs)

    def issue_gather(t, sl):
        base = tbase_ref[t]
        for r in range(tm):
            tok = rt_ref[base + r]
            pltpu.make_async_copy(h_hbm.at[pl.ds(tok, 1)], xbufs[sl].at[pl.ds(r, 1)], gsem.at[sl]).start()

    def scatter_row(sl, r, tok):
        return pltpu.make_async_copy(obufs[sl].at[pl.ds(r, 1)], out_hbm.at[pl.ds(tok, 1)], ssem.at[sl])

    def issue_scatter(sl, base, n_valid):
        for r in range(tm):
            @pl.when(r < n_valid)
            def _():
                scatter_row(sl, r, rt_ref[base + r]).start()

    def wait_gather(sl):
        pltpu.make_async_copy(h_hbm.at[pl.ds(0, tm)], xbufs[sl], gsem.at[sl]).wait()

    def wait_scatter(sl, n_valid):
        aligned = pl.multiple_of((n_valid // SUBLANES) * SUBLANES, SUBLANES)

        @pl.when(aligned > 0)
        def _():
            rows = pl.ds(0, aligned)
            pltpu.make_async_copy(obufs[sl].at[rows], out_hbm.at[rows], ssem.at[sl]).wait()

        def single(r, c):
            scatter_row(sl, r, 0).wait()
            return c
        lax.fori_loop(aligned, n_valid, single, 0)

    def live(t):
        return tnv_ref[jnp.maximum(t, 0)] > 0

    @pl.when(i == 0)
    def _():
        cls2 = route_ref[0]
        pos = route_ref[1]
        for c in range(N_CLASSES):
            pos = pos + jnp.where(cls2 == c, cs_ref[c], 0)
        pos_vmem[...] = pos
        to_smem = pltpu.make_async_copy(pos_vmem, pos_smem, psem)
        to_smem.start()
        to_smem.wait()

        def place(row, c):
            for col in range(LANES):
                rt_ref[pos_smem[row, col]] = row * LANES + col
            return c
        lax.fori_loop(0, n_tok // LANES, place, 0)
        for r in range(tm):
            rt_ref[n_tok + r] = 0
        issue_gather(0, 0)
        issue_gather(1, 1)

    prev = jnp.maximum(i - 1, 0)
    group_changed = (i == 0) | (tg_ref[i] != tg_ref[prev])
    for which, t_ref, (wg, wu, wd) in ((0, tlo_ref, (wg_lo, wu_lo, wd_lo)), (1, thi_ref, (wg_hi, wu_hi, wd_hi))):
        @pl.when(group_changed | (t_ref[i] != t_ref[prev]))
        def _():
            w_in_bf[2 * which] = wg[...].astype(BF16)
            w_in_bf[2 * which + 1] = wu[...].astype(BF16)
            w_out_bf[which] = wd[...].astype(BF16)

    def tile(cur):
        far = (cur + depth - 1) % depth
        @pl.when((i <= 1) | live(i - 2))
        def _():
            wait_gather(cur)

        @pl.when((i >= depth) & live(i - depth))
        def _():
            wait_scatter(cur, tnv_ref[jnp.maximum(i - depth, 0)])

        prev_base = tbase_ref[prev]
        prev_valid = jnp.where(i == 0, 0, tnv_ref[prev])

        @pl.when(live(i))
        def _():
            issue_gather(i + 2, far)
            xa = xbufs[cur][...]
            x = xa[:, :d_model]
            xn = (x * _rms_scale(x) * gn_ref[...]).astype(BF16)

            def expert(which, w):
                a = jnp.dot(xn, w_in_bf[2 * which], preferred_element_type=F32)
                b = jnp.dot(xn, w_in_bf[2 * which + 1], preferred_element_type=F32)
                return jnp.dot((_silu(a) * b * w).astype(BF16), w_out_bf[which], preferred_element_type=F32)

            y = expert(0, xa[:, d_model:d_model + 1]) + expert(1, xa[:, d_model + 1:d_model + 2])
            o = x + y
            if final_norm:
                o = o * _rms_scale(o) * fn_ref[...]
            obufs[cur][...] = o
            issue_scatter(far, prev_base, prev_valid)

        @pl.when(jnp.logical_not(live(i)) & (i >= 1) & live(i - 1))
        def _():
            issue_scatter(far, prev_base, prev_valid)

    for residue in range(depth):
        @pl.when(i % depth == residue)
        def _():
            tile(residue)


def _moe_layer(h_aug, route, counts_blk, ffn_norm_g, final_norm_g, wg, wu, wd, layer, final_norm):
    T, DA = h_aug.shape
    D = DA - LANES
    F = wg.shape[-1]
    tm = MOE_TILE
    assert T % (LANES * SUBLANES) == 0 and T >= 2 * tm
    n_tiles = T // tm + N_CLASSES + 3
    tile_g, tile_lo, tile_hi, nvalid, base, cstart = _tile_schedule(counts_blk, tm, n_tiles)
    route2d = route[:2].reshape(2, T // LANES, LANES)

    def w_spec(shape, which):
        def index(i, tg, tlo, thi, *_):
            return (layer, tg[i], (tlo if which == 0 else thi)[i], 0, 0)
        return pl.BlockSpec((None, None, None) + shape, index)

    est = 2 * 6 * D * F * 4 + 6 * D * F * 2 + 3 * tm * DA * 4 + 3 * tm * D * 4 + 8 * tm * F * 4
    return pl.pallas_call(
        functools.partial(_moe_kernel, tm=tm, d_model=D, n_tok=T, final_norm=final_norm),
        grid_spec=pltpu.PrefetchScalarGridSpec(
            num_scalar_prefetch=6,
            grid=(n_tiles,),
            in_specs=[
                pl.BlockSpec(memory_space=pl.ANY),
                pl.BlockSpec((2, T // LANES, LANES), lambda i, *_: (0, 0, 0)),
                pl.BlockSpec((1, D), lambda i, *_: (0, 0)),
                pl.BlockSpec((1, D), lambda i, *_: (0, 0)),
                w_spec((D, F), 0), w_spec((D, F), 0), w_spec((F, D), 0),
                w_spec((D, F), 1), w_spec((D, F), 1), w_spec((F, D), 1),
            ],
            out_specs=pl.BlockSpec(memory_space=pl.ANY),
            scratch_shapes=[
                pltpu.SMEM((T + tm,), jnp.int32),
                pltpu.SMEM((T // LANES, LANES), jnp.int32),
                pltpu.VMEM((T // LANES, LANES), jnp.int32),
                pltpu.VMEM((tm, DA), F32), pltpu.VMEM((tm, DA), F32), pltpu.VMEM((tm, DA), F32),
                pltpu.VMEM((tm, D), F32), pltpu.VMEM((tm, D), F32), pltpu.VMEM((tm, D), F32),
                pltpu.VMEM((4, D, F), BF16),
                pltpu.VMEM((2, F, D), BF16),
                pltpu.SemaphoreType.DMA((3,)),
                pltpu.SemaphoreType.DMA((3,)),
                pltpu.SemaphoreType.DMA(()),
            ],
        ),
        out_shape=jax.ShapeDtypeStruct((T, D), F32),
        compiler_params=pltpu.CompilerParams(
            dimension_semantics=("arbitrary",), vmem_limit_bytes=_vmem_limit(est)),
        name="moe_experts",
    )(tile_g, tile_lo, tile_hi, nvalid, base, cstart,
      h_aug, route2d, ffn_norm_g, final_norm_g, wg, wu, wd, wg, wu, wd)


def _partial_rotary(xh, cosf, sinp, sinm, half):
    lanes = xh.shape[-1]
    return xh * cosf + pltpu.roll(xh, half, 1) * sinp + pltpu.roll(xh, lanes - half, 1) * sinm


def _tile_cos_sin(cb_ref, sb_ref, cr_ref, sr_ref):
    cb, sb, cr, sr = cb_ref[...], sb_ref[...], cr_ref[...], sr_ref[...]
    return cb * cr - sb * sr, sb * cr + cb * sr


def _kvq_kernel(h_ref, gkv_ref, gq_ref, wkv_ref, wq_ref, cb_ref, sb_ref, cr_ref, sr_ref,
                k_ref, vt_ref, qt_ref, km_ref, *, heads, dh, half):
    x = h_ref[...]
    xr = x * _rms_scale(x)
    xkv = (xr * gkv_ref[...]).astype(BF16)
    xq = (xr * gq_ref[...]).astype(BF16)
    cosf, sin_all = _tile_cos_sin(cb_ref, sb_ref, cr_ref, sr_ref)
    lane = lax.broadcasted_iota(jnp.int32, sin_all.shape, 1)
    sinp = jnp.where(lane >= half, sin_all, 0.0)
    sinm = jnp.where(lane < half, -sin_all, 0.0)
    width = heads * dh
    q_scale = dh ** -0.5 * LOG2_E
    pair_w = 2 * dh
    for c0 in range(0, width, pair_w):
        kk = jnp.dot(xkv, wkv_ref[:, c0:c0 + pair_w], preferred_element_type=F32)
        vv = jnp.dot(xkv, wkv_ref[:, width + c0:width + c0 + pair_w], preferred_element_type=F32)
        qq = jnp.dot(xq, wq_ref[:, c0:c0 + pair_w], preferred_element_type=F32)
        for u in range(2):
            hh = c0 // dh + u
            cols = slice(hh * dh, (hh + 1) * dh)
            kh = _partial_rotary(kk[:, u * dh:(u + 1) * dh], cosf, sinp, sinm, half)
            k_ref[hh] = kh.astype(BF16)
            km_ref[:, cols] = jnp.mean(kh, axis=0, keepdims=True)
            vt_ref[hh, :dh] = vv[:, u * dh:(u + 1) * dh].T.astype(BF16)
            vt_ref[hh, dh:] = jnp.ones((VT_ONES_ROWS, vv.shape[0]), BF16)
            qh = _partial_rotary(qq[:, u * dh:(u + 1) * dh], cosf, sinp, sinm, half)
            qt_ref[hh] = (qh * q_scale).T.astype(BF16)


def _angle_tables(inv_freq, seq, tm):
    base = jnp.arange(0, seq, tm, dtype=F32)[:, None] * inv_freq[None, :]
    offs = jnp.arange(tm, dtype=F32)[:, None] * inv_freq[None, :]
    return jnp.cos(base)[:, None, :], jnp.sin(base)[:, None, :], jnp.cos(offs), jnp.sin(offs)


def _rope_tables(seq, dh, tm):
    rot = dh // ROPE_FRACTION
    half = rot // 2
    inv = 1.0 / (ROPE_THETA ** (jnp.arange(half, dtype=F32) / half))
    inv_lanes = jnp.concatenate([inv, inv, jnp.zeros((dh - rot,), F32)])
    return _angle_tables(inv_lanes, seq, tm) + (half,)


def _kvq_proj(h, kv_norm_g, q_norm_g, wkv_bf16, wq_bf16, seq):
    T, D = h.shape
    heads = ATT_HEADS
    dh = D // heads
    tm = MOBA_BLOCK
    n_pos_tiles = seq // tm
    cos_b, sin_b, cos_r, sin_r, half = _rope_tables(seq, dh, tm)
    width = heads * dh
    est = 2 * (D * 2 * width + D * width) * 2 + 2 * tm * D * 4 + 6 * tm * width * 2 + 8 * tm * dh * 4
    pos = lambda i: (i % n_pos_tiles, 0, 0)
    const = lambda i: (0, 0)
    per_block = lambda i: (0, i, 0, 0)
    return pl.pallas_call(
        functools.partial(_kvq_kernel, heads=heads, dh=dh, half=half),
        grid=(T // tm,),
        in_specs=[
            pl.BlockSpec((tm, D), lambda i: (i, 0)),
            pl.BlockSpec((1, D), const),
            pl.BlockSpec((1, D), const),
            pl.BlockSpec((D, 2 * width), const),
            pl.BlockSpec((D, width), const),
            pl.BlockSpec((None, 1, dh), pos),
            pl.BlockSpec((None, 1, dh), pos),
            pl.BlockSpec((tm, dh), const),
            pl.BlockSpec((tm, dh), const),
        ],
        out_specs=[
            pl.BlockSpec((heads, None, tm, dh), per_block),
            pl.BlockSpec((heads, None, dh + VT_ONES_ROWS, tm), per_block),
            pl.BlockSpec((heads, None, dh, tm), per_block),
            pl.BlockSpec((None, 1, width), lambda i: (i, 0, 0)),
        ],
        out_shape=[
            jax.ShapeDtypeStruct((heads, T // tm, tm, dh), BF16),
            jax.ShapeDtypeStruct((heads, T // tm, dh + VT_ONES_ROWS, tm), BF16),
            jax.ShapeDtypeStruct((heads, T // tm, dh, tm), BF16),
            jax.ShapeDtypeStruct((T // tm, 1, width), F32),
        ],
        compiler_params=pltpu.CompilerParams(
            dimension_semantics=("arbitrary",), vmem_limit_bytes=_vmem_limit(est)),
        name="kvq_proj",
    )(h, kv_norm_g, q_norm_g, wkv_bf16, wq_bf16, cos_b, sin_b, cos_r, sin_r)


def _moba_kernel(qt_ref, k_ref, vt_ref, km_ref, o_ref, sel_ref, acc_ref, sa_ref, sb_ref, pa_ref, pb_ref,
                 *, topk, group, dh):
    qi = pl.program_id(2)
    n_blk = km_ref.shape[0]
    blk = sa_ref.shape[1]
    neg = -jnp.inf
    chunks = [slice(c, c + MOBA_KEY_CHUNK) for c in range(0, blk, MOBA_KEY_CHUNK)]

    def scores(g, j):
        return jnp.dot(k_ref[g, j], qt_ref[g], preferred_element_type=F32)

    def stage_scores(s_buf, j):
        jc = jnp.minimum(j, n_blk - 1)
        for g in range(group):
            s_buf[g] = scores(g, jc) + sel_ref[g, pl.ds(jc, 1), :]

    def stage_softmax(s_buf, p_buf, ms):
        new_m, alphas = [], []
        for g in range(group):
            part = s_buf[g, chunks[0], :]
            for c in chunks[1:]:
                part = jnp.maximum(part, s_buf[g, c, :])
            m_blk = jnp.max(part, axis=0, keepdims=True)
            m_new = m_blk if ms is None else jnp.maximum(ms[g], m_blk)
            for c in chunks:
                p_buf[g, c, :] = jnp.exp2(s_buf[g, c, :] - m_new).astype(BF16)
            new_m.append(m_new)
            alphas.append(None if ms is None else jnp.exp2(ms[g] - m_new))
        return tuple(new_m), alphas

    def stage_pv(p_buf, j, alphas):
        jc = jnp.minimum(j, n_blk - 1)
        for g in range(group):
            pv = jnp.dot(vt_ref[g, jc], p_buf[g], preferred_element_type=F32)
            acc_ref[g] = pv if alphas[g] is None else alphas[g] * acc_ref[g] + pv

    for g in range(group):
        gate = jnp.dot(km_ref[:, g * dh:(g + 1) * dh], qt_ref[g].astype(F32), preferred_element_type=F32,
                       precision=lax.Precision.HIGHEST)
        blk_id = lax.broadcasted_iota(jnp.int32, gate.shape, 0)
        gv = jnp.where(blk_id < qi, gate, neg)
        bias = jnp.full(gate.shape, neg, F32)
        for _ in range(topk):
            m = jnp.max(gv, axis=0, keepdims=True)
            first = jnp.min(jnp.where(gv == m, blk_id, n_blk), axis=0, keepdims=True)
            bias = jnp.where((blk_id == first) & (m > neg), 0.0, bias)
            gv = jnp.where(blk_id == first, neg, gv)
        sel_ref[g] = bias

    key_id = lax.broadcasted_iota(jnp.int32, (blk, blk), 0)
    qry_id = lax.broadcasted_iota(jnp.int32, (blk, blk), 1)
    for g in range(group):
        sb_ref[g] = jnp.where(key_id <= qry_id, scores(g, qi), neg)
    stage_scores(sa_ref, 0)
    ms, alphas = stage_softmax(sb_ref, pb_ref, None)
    stage_pv(pb_ref, qi, alphas)

    def body(t, ms):
        j = 2 * t
        ms, alphas = stage_softmax(sa_ref, pa_ref, ms)
        stage_scores(sb_ref, j + 1)
        stage_pv(pa_ref, j, alphas)
        ms, alphas = stage_softmax(sb_ref, pb_ref, ms)
        stage_scores(sa_ref, j + 2)
        stage_pv(pb_ref, j + 1, alphas)
        return ms

    lax.fori_loop(0, (qi + 1) // 2, body, ms)
    for g in range(group):
        acc = acc_ref[g]
        o_ref[:, g * dh:(g + 1) * dh] = (acc[:dh] / acc[dh:dh + 1]).T.astype(BF16)


def _moba_attention(qt, k, vt, k_mean, batch, seq):
    heads, n_tiles, dh, blk = qt.shape
    dv_rows = vt.shape[2]
    n_blk = seq // blk
    topk = min(MOBA_TOPK, n_blk)
    width = heads * dh
    group = MOBA_HEAD_GROUP
    km = k_mean.reshape(batch, n_blk, width)
    per_seq = lambda b, h, i: (h, b, 0, 0)
    est = 2 * 2 * group * seq * dh * 2 + 2 * group * blk * dh * 4 + 16 * blk * blk * 4
    return pl.pallas_call(
        functools.partial(_moba_kernel, topk=topk, group=group, dh=dh),
        grid=(batch, heads // group, n_blk),
        in_specs=[
            pl.BlockSpec((group, None, dh, blk), lambda b, h, i: (h, b * n_blk + i, 0, 0)),
            pl.BlockSpec((group, n_blk, blk, dh), per_seq),
            pl.BlockSpec((group, n_blk, dv_rows, blk), per_seq),
            pl.BlockSpec((None, n_blk, group * dh), lambda b, h, i: (b, 0, h)),
        ],
        out_specs=pl.BlockSpec((blk, group * dh), lambda b, h, i: (b * n_blk + i, h)),
        out_shape=jax.ShapeDtypeStruct((n_tiles * blk, width), BF16),
        scratch_shapes=[pltpu.VMEM((group, n_blk, blk), F32), pltpu.VMEM((group, dv_rows, blk), F32),
                        pltpu.VMEM((group, blk, blk), F32), pltpu.VMEM((group, blk, blk), F32),
                        pltpu.VMEM((group, blk, blk), BF16), pltpu.VMEM((group, blk, blk), BF16)],
        compiler_params=pltpu.CompilerParams(
            dimension_semantics=("arbitrary", "arbitrary", "arbitrary"), vmem_limit_bytes=_vmem_limit(est)),
        name="moba_attention",
    )(qt, k, vt, km)


def kernel(x, ret_norm, ret_w_in, ret_w_out, kv_norm, w_kv, attn_norm, w_q, w_o, ffn_norm, router_group_w, router_group_b, router_expert_w, router_expert_b, expert_w_gate, expert_w_up, expert_w_down, final_norm):
    B, S, D = x.shape
    T = B * S
    assert S % MOBA_BLOCK == 0 and S % RET_KERNEL_CHUNK == 0 and T % MOE_TILE == 0
    assert ret_norm.shape[0] == 1 and attn_norm.shape[0] == 1 and ffn_norm.shape[0] == 2
    h = x.reshape(T, D)
    final_g = final_norm[None, :]

    def proj_router(y, w, resid, layer, name):
        return _proj_router(y, w.astype(BF16), resid, ffn_norm[layer][None, :], router_group_w[layer],
                            router_group_b[layer], router_expert_w[layer], router_expert_b[layer], name)

    def moe(h_aug, route, counts, layer, last):
        return _moe_layer(h_aug, route, counts, ffn_norm[layer][None, :], final_g,
                          expert_w_gate, expert_w_up, expert_w_down, layer, final_norm=last)

    q, k, v, gate = _ret_in_proj(h, ret_norm[0][None, :], ret_w_in[0].astype(BF16), S)
    y = _ret_core(q, k, v, gate, B, S)
    h = moe(*proj_router(y, ret_w_out[0], h, 0, "ret_out_proj_router"), 0, False)

    k2, vt2, qt2, k_mean = _kvq_proj(h, kv_norm[None, :], attn_norm[0][None, :],
                                     w_kv.astype(BF16), w_q[0].astype(BF16), S)
    o = _moba_attention(qt2, k2, vt2, k_mean, B, S)
    h = moe(*proj_router(o, w_o[0], h, 1, "attn_out_proj_router"), 1, True)
    return h.reshape(B, S, D)
```

```python
import functools

import jax
import jax.numpy as jnp
from jax import lax
from jax.experimental import pallas as pl
from jax.experimental.pallas import tpu as pltpu

F32 = jnp.float32
BF16 = jnp.bfloat16

NORM_EPS = 1e-6
RET_HEADS = 4
RET_ROT_BASE = 10000.0
ATT_HEADS = 8
ROPE_FRACTION = 4
ROPE_THETA = 500000.0
MOBA_BLOCK = 256
MOBA_TOPK = 3
N_GROUPS = 4
EXPERTS_PER_GROUP = 4
PAIR_LO = (0, 0, 1, 1, 0, 2)
PAIR_HI = (1, 2, 2, 3, 3, 3)
N_PAIRS = len(PAIR_LO)
N_CLASSES = N_GROUPS * N_PAIRS

LANES = 128
SUBLANES = 8
RET_KERNEL_CHUNK = 256
MOE_TILE = 128
ROUTE_ROWS = 32
MOBA_HEAD_GROUP = 4
MOBA_KEY_CHUNK = 32
VT_ONES_ROWS = 16
LOG2_E = 1.4426950408889634
V7X_VMEM_BYTES = 64 * 1024 * 1024


def _vmem_limit(estimate_bytes):
    return int(min(V7X_VMEM_BYTES - 8 * 1024 * 1024, max(32 * 1024 * 1024, estimate_bytes * 5 // 4)))


def _rms_scale(x):
    return lax.rsqrt(jnp.mean(x * x, axis=-1, keepdims=True) + NORM_EPS)


def _silu(a):
    return a * jax.nn.sigmoid(a)


def _ret_in_kernel(x_ref, g_ref, w_ref, cb_ref, sb_ref, cr_ref, sr_ref, q_ref, k_ref, v_ref, gate_ref,
                   *, heads, dk, dv):
    x = x_ref[...]
    xn = (x * _rms_scale(x) * g_ref[...]).astype(BF16)
    cos, sin = _tile_cos_sin(cb_ref, sb_ref, cr_ref, sr_ref)
    half = dk // 2
    k_scale = dk ** -0.5

    def rot_store(dst_ref, col0, h, scale):
        p = jnp.dot(xn, w_ref[:, col0 + h * dk:col0 + (h + 1) * dk], preferred_element_type=F32)
        x1 = p[:, :half]
        x2 = p[:, half:]
        dst_ref[:, h * dk:h * dk + half] = ((x1 * cos - x2 * sin) * scale).astype(BF16)
        dst_ref[:, h * dk + half:(h + 1) * dk] = ((x2 * cos + x1 * sin) * scale).astype(BF16)

    for h in range(heads):
        rot_store(q_ref, 0, h, 1.0)
        rot_store(k_ref, heads * dk, h, k_scale)
    v0 = 2 * heads * dk
    g0 = v0 + heads * dv
    for h in range(heads):
        v_ref[:, h * dv:(h + 1) * dv] = jnp.dot(
            xn, w_ref[:, v0 + h * dv:v0 + (h + 1) * dv], preferred_element_type=F32).astype(BF16)
        gate_ref[:, h * dv:(h + 1) * dv] = jnp.dot(
            xn, w_ref[:, g0 + h * dv:g0 + (h + 1) * dv], preferred_element_type=F32).astype(BF16)


def _ret_in_proj(h, norm_g, w_bf16, seq, tm=256):
    T, D = h.shape
    heads = RET_HEADS
    dk = D // heads
    dv = 2 * dk
    n_cols = w_bf16.shape[1]
    n_pos_tiles = seq // tm
    half = dk // 2
    inv = 1.0 / (RET_ROT_BASE ** (jnp.arange(half, dtype=F32) / half))
    cos_b, sin_b, cos_r, sin_r = _angle_tables(inv, seq, tm)
    pos = lambda i: (i % n_pos_tiles, 0, 0)
    est = 2 * D * n_cols * 2 + 2 * tm * D * 4 + 2 * tm * n_cols * 2 + 4 * tm * LANES * 4
    return pl.pallas_call(
        functools.partial(_ret_in_kernel, heads=heads, dk=dk, dv=dv),
        grid=(T // tm,),
        in_specs=[
            pl.BlockSpec((tm, D), lambda i: (i, 0)),
            pl.BlockSpec((1, D), lambda i: (0, 0)),
            pl.BlockSpec((D, n_cols), lambda i: (0, 0)),
            pl.BlockSpec((None, 1, half), pos),
            pl.BlockSpec((None, 1, half), pos),
            pl.BlockSpec((tm, half), lambda i: (0, 0)),
            pl.BlockSpec((tm, half), lambda i: (0, 0)),
        ],
        out_specs=[
            pl.BlockSpec((tm, heads * dk), lambda i: (i, 0)),
            pl.BlockSpec((tm, heads * dk), lambda i: (i, 0)),
            pl.BlockSpec((tm, heads * dv), lambda i: (i, 0)),
            pl.BlockSpec((tm, heads * dv), lambda i: (i, 0)),
        ],
        out_shape=[
            jax.ShapeDtypeStruct((T, heads * dk), BF16),
            jax.ShapeDtypeStruct((T, heads * dk), BF16),
            jax.ShapeDtypeStruct((T, heads * dv), BF16),
            jax.ShapeDtypeStruct((T, heads * dv), BF16),
        ],
        compiler_params=pltpu.CompilerParams(
            dimension_semantics=("arbitrary",), vmem_limit_bytes=_vmem_limit(est)),
        name="ret_in_proj",
    )(h, norm_g, w_bf16, cos_b, sin_b, cos_r, sin_r)


def _ret_core_kernel(q_ref, k_ref, v_ref, g_ref, dm_ref, qd_ref, kd_ref, cd_ref, y_ref, state_ref, *, heads):
    @pl.when(pl.program_id(1) == 0)
    def _():
        state_ref[...] = jnp.zeros_like(state_ref)

    dk = q_ref.shape[1] // heads
    dv = v_ref.shape[1] // heads
    for h in range(heads):
        q = q_ref[:, h * dk:(h + 1) * dk]
        k = k_ref[:, h * dk:(h + 1) * dk]
        v = v_ref[:, h * dv:(h + 1) * dv]
        s = lax.dot_general(q, k, (((1,), (1,)), ((), ())), preferred_element_type=F32) * dm_ref[h]
        o = jnp.dot(s.astype(BF16), v, preferred_element_type=F32)
        state = state_ref[h]
        o = o + qd_ref[h] * jnp.dot(q, state.astype(BF16), preferred_element_type=F32)
        ks = (k.astype(F32) * kd_ref[h]).astype(BF16)
        state_ref[h] = state * cd_ref[h] + lax.dot_general(
            ks, v, (((0,), (0,)), ((), ())), preferred_element_type=F32)
        o = o * _rms_scale(o)
        y_ref[:, h * dv:(h + 1) * dv] = (_silu(g_ref[:, h * dv:(h + 1) * dv].astype(F32)) * o).astype(BF16)


def _ret_decay_tables(heads, chunk, dv):
    log_gamma = jnp.log1p(-jnp.power(2.0, -5.0 - jnp.arange(heads, dtype=F32)))
    idx = jnp.arange(chunk, dtype=F32)
    diff = idx[:, None] - idx[None, :]
    dmask = jnp.where(diff >= 0, jnp.exp(log_gamma[:, None, None] * jnp.maximum(diff, 0.0)), 0.0)
    qd = jnp.exp(log_gamma[:, None] * (idx + 1.0))[:, :, None]
    kd = jnp.exp(log_gamma[:, None] * (chunk - 1.0 - idx))[:, :, None]
    cd = jnp.broadcast_to(jnp.exp(log_gamma * chunk)[:, None, None], (heads, 1, dv))
    return dmask, qd, kd, cd


def _ret_core(q, k, v, gate, batch, seq):
    T = q.shape[0]
    heads = RET_HEADS
    dk = q.shape[1] // heads
    dv = v.shape[1] // heads
    cc = RET_KERNEL_CHUNK
    nc = seq // cc
    dmask, qd, kd, cd = _ret_decay_tables(heads, cc, dv)
    row = lambda b, c: (b * nc + c, 0)
    tab = lambda b, c: (0, 0, 0)
    return pl.pallas_call(
        functools.partial(_ret_core_kernel, heads=heads),
        grid=(batch, nc),
        in_specs=[
            pl.BlockSpec((cc, heads * dk), row),
            pl.BlockSpec((cc, heads * dk), row),
            pl.BlockSpec((cc, heads * dv), row),
            pl.BlockSpec((cc, heads * dv), row),
            pl.BlockSpec((heads, cc, cc), tab),
            pl.BlockSpec((heads, cc, 1), tab),
            pl.BlockSpec((heads, cc, 1), tab),
            pl.BlockSpec((heads, 1, dv), tab),
        ],
        out_specs=pl.BlockSpec((cc, heads * dv), row),
        out_shape=jax.ShapeDtypeStruct((T, heads * dv), BF16),
        scratch_shapes=[pltpu.VMEM((heads, dk, dv), F32)],
        compiler_params=pltpu.CompilerParams(dimension_semantics=("arbitrary", "arbitrary")),
        name="ret_core",
    )(q, k, v, gate, dmask, qd, kd, cd)


def _proj_router_kernel(y_ref, w_ref, r_ref, g_ref, wr_ref, br_ref, haug_ref, route_ref, counts_ref, run_ref,
                        *, groups, experts, d_model):
    @pl.when(pl.program_id(0) == 0)
    def _():
        run_ref[...] = jnp.zeros_like(run_ref)

    h = r_ref[...] + jnp.dot(y_ref[...], w_ref[...], preferred_element_type=F32)
    xn = h * _rms_scale(h) * g_ref[...]
    xh = xn.astype(BF16)
    xl = (xn - xh.astype(F32)).astype(BF16)
    hi_both = jnp.dot(xh, wr_ref[...], preferred_element_type=F32)
    lo_hi = jnp.dot(xl, wr_ref[:, :LANES], preferred_element_type=F32)
    logits = hi_both[:, :LANES] + (hi_both[:, LANES:] + lo_hi) + br_ref[...]
    lt = logits.T[:ROUTE_ROWS]
    tm = lt.shape[1]
    row = lax.broadcasted_iota(jnp.int32, lt.shape, 0)
    neg = -jnp.inf

    def argmax_first(vals):
        m = jnp.max(vals, axis=0, keepdims=True)
        first = jnp.min(jnp.where(vals == m, row, ROUTE_ROWS), axis=0, keepdims=True)
        return m, first

    gl = jnp.where(row < groups, lt, neg)
    gmax, gidx = argmax_first(gl)
    gsum = jnp.sum(jnp.where(row < groups, jnp.exp(gl - gmax), 0.0), axis=0, keepdims=True)
    g_val = 1.0 / gsum
    e0 = groups + experts * gidx
    el = jnp.where((row >= e0) & (row < e0 + experts), lt, neg)
    m1, i1 = argmax_first(el)
    m2, i2 = argmax_first(jnp.where(row == i1, neg, el))
    t = jnp.exp(m2 - m1)
    w1 = 1.0 / (1.0 + t)
    w2 = t / (1.0 + t)
    first_is_lo = i1 < i2
    lo = jnp.minimum(i1, i2) - e0
    hi = jnp.maximum(i1, i2) - e0
    w_lo = g_val * jnp.where(first_is_lo, w1, w2)
    w_hi = g_val * jnp.where(first_is_lo, w2, w1)
    lex = (lo * (7 - lo)) // 2 + (hi - lo - 1)
    pair = jnp.where(lex == 2, 4, jnp.where(lex == 3, 2, jnp.where(lex == 4, 3, lex)))
    cls = gidx * N_PAIRS + pair

    onehot = row == cls
    oh = jnp.where(onehot, 1.0, 0.0)
    before = lax.broadcasted_iota(jnp.int32, (tm, tm), 0) < lax.broadcasted_iota(jnp.int32, (tm, tm), 1)
    prefix = jnp.dot(oh.astype(BF16), jnp.where(before, 1.0, 0.0).astype(BF16), preferred_element_type=F32)
    rank = jnp.sum(jnp.where(onehot, prefix + run_ref[:, 0:1], 0.0), axis=0, keepdims=True)
    run_new = run_ref[...] + jnp.sum(oh, axis=1, keepdims=True)
    run_ref[...] = run_new
    counts_ref[...] = run_new

    r8 = lax.broadcasted_iota(jnp.int32, route_ref.shape, 0)
    route_ref[...] = jnp.where(r8 == 0, cls, jnp.where(r8 == 1, rank.astype(jnp.int32), 0))
    rl = lax.broadcasted_iota(jnp.int32, (LANES, tm), 0)
    meta_t = jnp.where(rl == 0, w_lo, jnp.where(rl == 1, w_hi, 0.0))
    haug_ref[:, :d_model] = h
    haug_ref[:, d_model:] = meta_t.T


def _proj_router(y, w_bf16, resid, norm_g, w_rg, b_rg, w_re, b_re, name, tm=512):
    T, K = y.shape
    D = w_bf16.shape[1]
    G, E = N_GROUPS, EXPERTS_PER_GROUP
    n_used = G + G * E
    wr = jnp.concatenate([w_rg, jnp.transpose(w_re, (1, 0, 2)).reshape(D, G * E)], axis=1)
    wr = jnp.pad(wr, ((0, 0), (0, LANES - n_used)))
    wr_hi = wr.astype(BF16)
    wr = jnp.concatenate([wr_hi, (wr - wr_hi.astype(F32)).astype(BF16)], axis=1)
    br = jnp.pad(jnp.concatenate([b_rg, b_re.reshape(G * E)]), (0, LANES - n_used))[None, :]
    est = 2 * K * D * 2 + 2 * tm * K * 2 + 6 * tm * D * 4 + 2 * D * LANES * 4 + 4 * tm * tm * 4
    const = lambda i: (0, 0)
    return pl.pallas_call(
        functools.partial(_proj_router_kernel, groups=G, experts=E, d_model=D),
        grid=(T // tm,),
        in_specs=[
            pl.BlockSpec((tm, K), lambda i: (i, 0)),
            pl.BlockSpec((K, D), const),
            pl.BlockSpec((tm, D), lambda i: (i, 0)),
            pl.BlockSpec((1, D), const),
            pl.BlockSpec((D, 2 * LANES), const),
            pl.BlockSpec((1, LANES), const),
        ],
        out_specs=[
            pl.BlockSpec((tm, D + LANES), lambda i: (i, 0)),
            pl.BlockSpec((8, tm), lambda i: (0, i)),
            pl.BlockSpec((ROUTE_ROWS, LANES), const),
        ],
        out_shape=[
            jax.ShapeDtypeStruct((T, D + LANES), F32),
            jax.ShapeDtypeStruct((8, T), jnp.int32),
            jax.ShapeDtypeStruct((ROUTE_ROWS, LANES), F32),
        ],
        scratch_shapes=[pltpu.VMEM((ROUTE_ROWS, LANES), F32)],
        compiler_params=pltpu.CompilerParams(
            dimension_semantics=("arbitrary",), vmem_limit_bytes=_vmem_limit(est)),
        name=name,
    )(y, w_bf16, resid, norm_g, wr, br)


def _tile_schedule(counts_blk, tm, n_tiles):
    counts = counts_blk[:N_CLASSES, 0].astype(jnp.int32)
    tiles_per = (counts + tm - 1) // tm
    tile_end = jnp.cumsum(tiles_per)
    tile_begin = tile_end - tiles_per
    cstart = jnp.cumsum(counts) - counts
    t = jnp.arange(n_tiles, dtype=jnp.int32)
    live = t < tile_end[-1]
    tq = jnp.where(live, t, tile_end[-1] - 1)
    tcls = jnp.sum((tile_end[None, :] <= tq[:, None]).astype(jnp.int32), axis=1)
    onehot = tcls[:, None] == jnp.arange(N_CLASSES, dtype=jnp.int32)[None, :]
    pick = lambda table: jnp.sum(jnp.where(onehot, jnp.asarray(table, jnp.int32)[None, :], 0), axis=1)
    k = tq - pick(tile_begin)
    nvalid = jnp.where(live, jnp.clip(pick(counts) - k * tm, 0, tm), 0).astype(jnp.int32)
    base = jnp.where(live, pick(cstart) + k * tm, 0).astype(jnp.int32)
    tile_g = pick([c // N_PAIRS for c in range(N_CLASSES)])
    tile_lo = pick([PAIR_LO[c % N_PAIRS] for c in range(N_CLASSES)])
    tile_hi = pick([PAIR_HI[c % N_PAIRS] for c in range(N_CLASSES)])
    cstart_pad = jnp.pad(cstart, (0, ROUTE_ROWS - N_CLASSES)).astype(jnp.int32)
    return tile_g, tile_lo, tile_hi, nvalid, base, cstart_pad


def _moe_kernel(tg_ref, tlo_ref, thi_ref, tnv_ref, tbase_ref, cs_ref,
                h_hbm, route_ref, gn_ref, fn_ref, wg_lo, wu_lo, wd_lo, wg_hi, wu_hi, wd_hi,
                out_hbm, rt_ref, pos_smem, pos_vmem, x0, x1, x2, o0, o1, o2, w_in_bf, w_out_bf,
                gsem, ssem, psem, *, tm, d_model, n_tok, final_norm):
    i = pl.program_id(0)
    xbufs = (x0, x1, x2)
    obufs = (o0, o1, o2)
    depth = len(xbufs)

    def issue_gather(t, sl):
        base = tbase_ref[t]
        for r in range(tm):
            tok = rt_ref[base + r]
            pltpu.make_async_copy(h_hbm.at[pl.ds(tok, 1)], xbufs[sl].at[pl.ds(r, 1)], gsem.at[sl]).start()

    def scatter_row(sl, r, tok):
        return pltpu.make_async_copy(obufs[sl].at[pl.ds(r, 1)], out_hbm.at[pl.ds(tok, 1)], ssem.at[sl])

    def issue_scatter(sl, base, n_valid):
        for r in range(tm):
            @pl.when(r < n_valid)
            def _():
                scatter_row(sl, r, rt_ref[base + r]).start()

    def wait_gather(sl):
        pltpu.make_async_copy(h_hbm.at[pl.ds(0, tm)], xbufs[sl], gsem.at[sl]).wait()

    def wait_scatter(sl, n_valid):
        aligned = pl.multiple_of((n_valid // SUBLANES) * SUBLANES, SUBLANES)

        @pl.when(aligned > 0)
        def _():
            rows = pl.ds(0, aligned)
            pltpu.make_async_copy(obufs[sl].at[rows], out_hbm.at[rows], ssem.at[sl]).wait()

        def single(r, c):
            scatter_row(sl, r, 0).wait()
            return c
        lax.fori_loop(aligned, n_valid, single, 0)

    def live(t):
        return tnv_ref[jnp.maximum(t, 0)] > 0

    @pl.when(i == 0)
    def _():
        cls2 = route_ref[0]
        pos = route_ref[1]
        for c in range(N_CLASSES):
            pos = pos + jnp.where(cls2 == c, cs_ref[c], 0)
        pos_vmem[...] = pos
        to_smem = pltpu.make_async_copy(pos_vmem, pos_smem, psem)
        to_smem.start()
        to_smem.wait()

        def place(row, c):
            for col in range(LANES):
                rt_ref[pos_smem[row, col]] = row * LANES + col
            return c
        lax.fori_loop(0, n_tok // LANES, place, 0)
        for r in range(tm):
            rt_ref[n_tok + r] = 0
        issue_gather(0, 0)
        issue_gather(1, 1)

    prev = jnp.maximum(i - 1, 0)
    group_changed = (i == 0) | (tg_ref[i] != tg_ref[prev])
    for which, t_ref, (wg, wu, wd) in ((0, tlo_ref, (wg_lo, wu_lo, wd_lo)), (1, thi_ref, (wg_hi, wu_hi, wd_hi))):
        @pl.when(group_changed | (t_ref[i] != t_ref[prev]))
        def _():
            w_in_bf[2 * which] = wg[...].astype(BF16)
            w_in_bf[2 * which + 1] = wu[...].astype(BF16)
            w_out_bf[which] = wd[...].astype(BF16)

    def tile(cur):
        far = (cur + depth - 1) % depth
        @pl.when((i <= 1) | live(i - 2))
        def _():
            wait_gather(cur)

        @pl.when((i >= depth) & live(i - depth))
        def _():
            wait_scatter(cur, tnv_ref[jnp.maximum(i - depth, 0)])

        prev_base = tbase_ref[prev]
        prev_valid = jnp.where(i == 0, 0, tnv_ref[prev])

        @pl.when(live(i))
        def _():
            issue_gather(i + 2, far)
            xa = xbufs[cur][...]
            x = xa[:, :d_model]
            xn = (x * _rms_scale(x) * gn_ref[...]).astype(BF16)

            def expert(which, w):
                a = jnp.dot(xn, w_in_bf[2 * which], preferred_element_type=F32)
                b = jnp.dot(xn, w_in_bf[2 * which + 1], preferred_element_type=F32)
                return jnp.dot((_silu(a) * b * w).astype(BF16), w_out_bf[which], preferred_element_type=F32)

            y = expert(0, xa[:, d_model:d_model + 1]) + expert(1, xa[:, d_model + 1:d_model + 2])
            obufs[cur][...] = x + y
            issue_scatter(far, prev_base, prev_valid)
            if final_norm:
                for r0 in range(0, tm, 4 * SUBLANES):
                    rows = slice(r0, r0 + 4 * SUBLANES)
                    o = obufs[cur][rows, :]
                    obufs[cur][rows, :] = o * _rms_scale(o) * fn_ref[...]

        @pl.when(jnp.logical_not(live(i)) & (i >= 1) & live(i - 1))
        def _():
            issue_scatter(far, prev_base, prev_valid)

    for residue in range(depth):
        @pl.when(i % depth == residue)
        def _():
            tile(residue)


def _moe_layer(h_aug, route, counts_blk, ffn_norm_g, final_norm_g, wg, wu, wd, layer, final_norm):
    T, DA = h_aug.shape
    D = DA - LANES
    F = wg.shape[-1]
    tm = MOE_TILE
    assert T % (LANES * SUBLANES) == 0 and T >= 2 * tm
    n_tiles = T // tm + N_CLASSES + 3
    tile_g, tile_lo, tile_hi, nvalid, base, cstart = _tile_schedule(counts_blk, tm, n_tiles)
    route2d = route[:2].reshape(2, T // LANES, LANES)

    def w_spec(shape, which):
        def index(i, tg, tlo, thi, *_):
            return (layer, tg[i], (tlo if which == 0 else thi)[i], 0, 0)
        return pl.BlockSpec((None, None, None) + shape, index)

    est = 2 * 6 * D * F * 4 + 6 * D * F * 2 + 3 * tm * DA * 4 + 3 * tm * D * 4 + 8 * tm * F * 4
    return pl.pallas_call(
        functools.partial(_moe_kernel, tm=tm, d_model=D, n_tok=T, final_norm=final_norm),
        grid_spec=pltpu.PrefetchScalarGridSpec(
            num_scalar_prefetch=6,
            grid=(n_tiles,),
            in_specs=[
                pl.BlockSpec(memory_space=pl.ANY),
                pl.BlockSpec((2, T // LANES, LANES), lambda i, *_: (0, 0, 0)),
                pl.BlockSpec((1, D), lambda i, *_: (0, 0)),
                pl.BlockSpec((1, D), lambda i, *_: (0, 0)),
                w_spec((D, F), 0), w_spec((D, F), 0), w_spec((F, D), 0),
                w_spec((D, F), 1), w_spec((D, F), 1), w_spec((F, D), 1),
            ],
            out_specs=pl.BlockSpec(memory_space=pl.ANY),
            scratch_shapes=[
                pltpu.SMEM((T + tm,), jnp.int32),
                pltpu.SMEM((T // LANES, LANES), jnp.int32),
                pltpu.VMEM((T // LANES, LANES), jnp.int32),
                pltpu.VMEM((tm, DA), F32), pltpu.VMEM((tm, DA), F32), pltpu.VMEM((tm, DA), F32),
                pltpu.VMEM((tm, D), F32), pltpu.VMEM((tm, D), F32), pltpu.VMEM((tm, D), F32),
                pltpu.VMEM((4, D, F), BF16),
                pltpu.VMEM((2, F, D), BF16),
                pltpu.SemaphoreType.DMA((3,)),
                pltpu.SemaphoreType.DMA((3,)),
                pltpu.SemaphoreType.DMA(()),
            ],
        ),
        out_shape=jax.ShapeDtypeStruct((T, D), F32),
        compiler_params=pltpu.CompilerParams(
            dimension_semantics=("arbitrary",), vmem_limit_bytes=_vmem_limit(est)),
        name="moe_experts",
    )(tile_g, tile_lo, tile_hi, nvalid, base, cstart,
      h_aug, route2d, ffn_norm_g, final_norm_g, wg, wu, wd, wg, wu, wd)


def _partial_rotary(xh, cosf, sinp, sinm, half):
    lanes = xh.shape[-1]
    return xh * cosf + pltpu.roll(xh, half, 1) * sinp + pltpu.roll(xh, lanes - half, 1) * sinm


def _tile_cos_sin(cb_ref, sb_ref, cr_ref, sr_ref):
    cb, sb, cr, sr = cb_ref[...], sb_ref[...], cr_ref[...], sr_ref[...]
    return cb * cr - sb * sr, sb * cr + cb * sr


def _kvq_kernel(h_ref, gkv_ref, gq_ref, wkv_ref, wq_ref, cb_ref, sb_ref, cr_ref, sr_ref,
                k_ref, vt_ref, qt_ref, km_ref, *, heads, dh, half):
    x = h_ref[...]
    xr = x * _rms_scale(x)
    xkv = (xr * gkv_ref[...]).astype(BF16)
    xq = (xr * gq_ref[...]).astype(BF16)
    cosf, sin_all = _tile_cos_sin(cb_ref, sb_ref, cr_ref, sr_ref)
    lane = lax.broadcasted_iota(jnp.int32, sin_all.shape, 1)
    sinp = jnp.where(lane >= half, sin_all, 0.0)
    sinm = jnp.where(lane < half, -sin_all, 0.0)
    width = heads * dh
    q_scale = dh ** -0.5 * LOG2_E
    pair_w = 2 * dh
    for c0 in range(0, width, pair_w):
        kk = jnp.dot(xkv, wkv_ref[:, c0:c0 + pair_w], preferred_element_type=F32)
        vv = jnp.dot(xkv, wkv_ref[:, width + c0:width + c0 + pair_w], preferred_element_type=F32)
        qq = jnp.dot(xq, wq_ref[:, c0:c0 + pair_w], preferred_element_type=F32)
        for u in range(2):
            hh = c0 // dh + u
            cols = slice(hh * dh, (hh + 1) * dh)
            kh = _partial_rotary(kk[:, u * dh:(u + 1) * dh], cosf, sinp, sinm, half)
            k_ref[hh] = kh.astype(BF16)
            km_ref[:, cols] = jnp.mean(kh, axis=0, keepdims=True)
            vt_ref[hh, :dh] = vv[:, u * dh:(u + 1) * dh].T.astype(BF16)
            vt_ref[hh, dh:] = jnp.ones((VT_ONES_ROWS, vv.shape[0]), BF16)
            qh = _partial_rotary(qq[:, u * dh:(u + 1) * dh], cosf, sinp, sinm, half)
            qt_ref[hh] = (qh * q_scale).T.astype(BF16)


def _angle_tables(inv_freq, seq, tm):
    base = jnp.arange(0, seq, tm, dtype=F32)[:, None] * inv_freq[None, :]
    offs = jnp.arange(tm, dtype=F32)[:, None] * inv_freq[None, :]
    return jnp.cos(base)[:, None, :], jnp.sin(base)[:, None, :], jnp.cos(offs), jnp.sin(offs)


def _rope_tables(seq, dh, tm):
    rot = dh // ROPE_FRACTION
    half = rot // 2
    inv = 1.0 / (ROPE_THETA ** (jnp.arange(half, dtype=F32) / half))
    inv_lanes = jnp.concatenate([inv, inv, jnp.zeros((dh - rot,), F32)])
    return _angle_tables(inv_lanes, seq, tm) + (half,)


def _kvq_proj(h, kv_norm_g, q_norm_g, wkv_bf16, wq_bf16, seq):
    T, D = h.shape
    heads = ATT_HEADS
    dh = D // heads
    tm = MOBA_BLOCK
    n_pos_tiles = seq // tm
    cos_b, sin_b, cos_r, sin_r, half = _rope_tables(seq, dh, tm)
    width = heads * dh
    est = 2 * (D * 2 * width + D * width) * 2 + 2 * tm * D * 4 + 6 * tm * width * 2 + 8 * tm * dh * 4
    pos = lambda i: (i % n_pos_tiles, 0, 0)
    const = lambda i: (0, 0)
    per_block = lambda i: (0, i, 0, 0)
    return pl.pallas_call(
        functools.partial(_kvq_kernel, heads=heads, dh=dh, half=half),
        grid=(T // tm,),
        in_specs=[
            pl.BlockSpec((tm, D), lambda i: (i, 0)),
            pl.BlockSpec((1, D), const),
            pl.BlockSpec((1, D), const),
            pl.BlockSpec((D, 2 * width), const),
            pl.BlockSpec((D, width), const),
            pl.BlockSpec((None, 1, dh), pos),
            pl.BlockSpec((None, 1, dh), pos),
            pl.BlockSpec((tm, dh), const),
            pl.BlockSpec((tm, dh), const),
        ],
        out_specs=[
            pl.BlockSpec((heads, None, tm, dh), per_block),
            pl.BlockSpec((heads, None, dh + VT_ONES_ROWS, tm), per_block),
            pl.BlockSpec((heads, None, dh, tm), per_block),
            pl.BlockSpec((None, 1, width), lambda i: (i, 0, 0)),
        ],
        out_shape=[
            jax.ShapeDtypeStruct((heads, T // tm, tm, dh), BF16),
            jax.ShapeDtypeStruct((heads, T // tm, dh + VT_ONES_ROWS, tm), BF16),
            jax.ShapeDtypeStruct((heads, T // tm, dh, tm), BF16),
            jax.ShapeDtypeStruct((T // tm, 1, width), F32),
        ],
        compiler_params=pltpu.CompilerParams(
            dimension_semantics=("arbitrary",), vmem_limit_bytes=_vmem_limit(est)),
        name="kvq_proj",
    )(h, kv_norm_g, q_norm_g, wkv_bf16, wq_bf16, cos_b, sin_b, cos_r, sin_r)


def _moba_kernel(qt_ref, k_ref, vt_ref, km_ref, o_ref, sel_ref, acc_ref, sa_ref, sb_ref, pa_ref, pb_ref,
                 *, topk, group, dh):
    qi = pl.program_id(2)
    n_blk = km_ref.shape[0]
    blk = sa_ref.shape[1]
    neg = -jnp.inf
    chunks = [slice(c, c + MOBA_KEY_CHUNK) for c in range(0, blk, MOBA_KEY_CHUNK)]

    def scores(g, j):
        return jnp.dot(k_ref[g, j], qt_ref[g], preferred_element_type=F32)

    def stage_scores(s_buf, j):
        jc = jnp.minimum(j, n_blk - 1)
        for g in range(group):
            s_buf[g] = scores(g, jc)

    def stage_softmax(s_buf, p_buf, ms, j):
        new_m, alphas = [], []
        for g in range(group):
            part = s_buf[g, chunks[0], :]
            for c in chunks[1:]:
                part = jnp.maximum(part, s_buf[g, c, :])
            m_blk = jnp.max(part, axis=0, keepdims=True)
            if j is None:
                m_new = shift = m_blk
            else:
                bias = sel_ref[g, pl.ds(jnp.minimum(j, n_blk - 1), 1), :]
                m_new = jnp.maximum(ms[g], m_blk + bias)
                shift = m_new - bias
            for c in chunks:
                p_buf[g, c, :] = jnp.exp2(s_buf[g, c, :] - shift).astype(BF16)
            new_m.append(m_new)
            alphas.append(None if j is None else jnp.exp2(ms[g] - m_new))
        return tuple(new_m), alphas

    def stage_pv(p_buf, j, alphas):
        jc = jnp.minimum(j, n_blk - 1)
        for g in range(group):
            pv = jnp.dot(vt_ref[g, jc], p_buf[g], preferred_element_type=F32)
            acc_ref[g] = pv if alphas[g] is None else alphas[g] * acc_ref[g] + pv

    for g in range(group):
        km = km_ref[:, g * dh:(g + 1) * dh]
        km_hi = km.astype(BF16)
        km_lo = (km - km_hi.astype(F32)).astype(BF16)
        gate = (jnp.dot(km_hi, qt_ref[g], preferred_element_type=F32)
                + jnp.dot(km_lo, qt_ref[g], preferred_element_type=F32))
        blk_id = lax.broadcasted_iota(jnp.int32, gate.shape, 0)
        gv = jnp.where(blk_id < qi, gate, neg)
        bias = jnp.full(gate.shape, neg, F32)
        for _ in range(topk):
            m = jnp.max(gv, axis=0, keepdims=True)
            first = jnp.min(jnp.where(gv == m, blk_id, n_blk), axis=0, keepdims=True)
            bias = jnp.where((blk_id == first) & (m > neg), 0.0, bias)
            gv = jnp.where(blk_id == first, neg, gv)
        sel_ref[g] = bias

    key_id = lax.broadcasted_iota(jnp.int32, (blk, blk), 0)
    qry_id = lax.broadcasted_iota(jnp.int32, (blk, blk), 1)
    for g in range(group):
        sb_ref[g] = jnp.where(key_id <= qry_id, scores(g, qi), neg)
    stage_scores(sa_ref, 0)
    ms, alphas = stage_softmax(sb_ref, pb_ref, None, None)
    stage_pv(pb_ref, qi, alphas)

    def body(t, ms):
        j = 2 * t
        ms, alphas = stage_softmax(sa_ref, pa_ref, ms, j)
        stage_scores(sb_ref, j + 1)
        stage_pv(pa_ref, j, alphas)
        ms, alphas = stage_softmax(sb_ref, pb_ref, ms, j + 1)
        stage_scores(sa_ref, j + 2)
        stage_pv(pb_ref, j + 1, alphas)
        return ms

    lax.fori_loop(0, (qi + 1) // 2, body, ms)
    for g in range(group):
        acc = acc_ref[g]
        o_ref[:, g * dh:(g + 1) * dh] = (acc[:dh] / acc[dh:dh + 1]).T.astype(BF16)


def _moba_attention(qt, k, vt, k_mean, batch, seq):
    heads, n_tiles, dh, blk = qt.shape
    dv_rows = vt.shape[2]
    n_blk = seq // blk
    topk = min(MOBA_TOPK, n_blk)
    width = heads * dh
    group = MOBA_HEAD_GROUP
    km = k_mean.reshape(batch, n_blk, width)
    per_seq = lambda b, h, i: (h, b, 0, 0)
    est = 2 * 2 * group * seq * dh * 2 + 2 * group * blk * dh * 4 + 16 * blk * blk * 4
    return pl.pallas_call(
        functools.partial(_moba_kernel, topk=topk, group=group, dh=dh),
        grid=(batch, heads // group, n_blk),
        in_specs=[
            pl.BlockSpec((group, None, dh, blk), lambda b, h, i: (h, b * n_blk + i, 0, 0)),
            pl.BlockSpec((group, n_blk, blk, dh), per_seq),
            pl.BlockSpec((group, n_blk, dv_rows, blk), per_seq),
            pl.BlockSpec((None, n_blk, group * dh), lambda b, h, i: (b, 0, h)),
        ],
        out_specs=pl.BlockSpec((blk, group * dh), lambda b, h, i: (b * n_blk + i, h)),
        out_shape=jax.ShapeDtypeStruct((n_tiles * blk, width), BF16),
        scratch_shapes=[pltpu.VMEM((group, n_blk, blk), F32), pltpu.VMEM((group, dv_rows, blk), F32),
                        pltpu.VMEM((group, blk, blk), F32), pltpu.VMEM((group, blk, blk), F32),
                        pltpu.VMEM((group, blk, blk), BF16), pltpu.VMEM((group, blk, blk), BF16)],
        compiler_params=pltpu.CompilerParams(
            dimension_semantics=("arbitrary", "arbitrary", "arbitrary"), vmem_limit_bytes=_vmem_limit(est)),
        name="moba_attention",
    )(qt, k, vt, km)


def kernel(x, ret_norm, ret_w_in, ret_w_out, kv_norm, w_kv, attn_norm, w_q, w_o, ffn_norm, router_group_w, router_group_b, router_expert_w, router_expert_b, expert_w_gate, expert_w_up, expert_w_down, final_norm):
    B, S, D = x.shape
    T = B * S
    assert S % MOBA_BLOCK == 0 and S % RET_KERNEL_CHUNK == 0 and T % MOE_TILE == 0
    assert ret_norm.shape[0] == 1 and attn_norm.shape[0] == 1 and ffn_norm.shape[0] == 2
    h = x.reshape(T, D)
    final_g = final_norm[None, :]

    def proj_router(y, w, resid, layer, name):
        return _proj_router(y, w.astype(BF16), resid, ffn_norm[layer][None, :], router_group_w[layer],
                            router_group_b[layer], router_expert_w[layer], router_expert_b[layer], name)

    def moe(h_aug, route, counts, layer, last):
        return _moe_layer(h_aug, route, counts, ffn_norm[layer][None, :], final_g,
                          expert_w_gate, expert_w_up, expert_w_down, layer, final_norm=last)

    q, k, v, gate = _ret_in_proj(h, ret_norm[0][None, :], ret_w_in[0].astype(BF16), S)
    y = _ret_core(q, k, v, gate, B, S)
    h = moe(*proj_router(y, ret_w_out[0], h, 0, "ret_out_proj_router"), 0, False)

    k2, vt2, qt2, k_mean = _kvq_proj(h, kv_norm[None, :], attn_norm[0][None, :],
                                     w_kv.astype(BF16), w_q[0].astype(BF16), S)
    o = _moba_attention(qt2, k2, vt2, k_mean, B, S)
    h = moe(*proj_router(o, w_o[0], h, 1, "attn_out_proj_router"), 1, True)
    return h.reshape(B, S, D)
```

```python
import functools

import jax
import jax.numpy as jnp
from jax import lax
from jax.experimental import pallas as pl
from jax.experimental.pallas import tpu as pltpu

F32 = jnp.float32
BF16 = jnp.bfloat16

NORM_EPS = 1e-6
RET_HEADS = 4
RET_ROT_BASE = 10000.0
ATT_HEADS = 8
ROPE_FRACTION = 4
ROPE_THETA = 500000.0
MOBA_BLOCK = 256
MOBA_TOPK = 3
N_GROUPS = 4
EXPERTS_PER_GROUP = 4
PAIR_LO = (0, 0, 1, 1, 0, 2)
PAIR_HI = (1, 2, 2, 3, 3, 3)
N_PAIRS = len(PAIR_LO)
N_CLASSES = N_GROUPS * N_PAIRS

LANES = 128
SUBLANES = 8
RET_KERNEL_CHUNK = 256
MOE_TILE = 128
ROUTE_ROWS = 32
MOBA_HEAD_GROUP = 4
MOBA_KEY_CHUNK = 32
VT_ONES_ROWS = 16
LOG2_E = 1.4426950408889634
V7X_VMEM_BYTES = 64 * 1024 * 1024


def _vmem_limit(estimate_bytes):
    return int(min(V7X_VMEM_BYTES - 8 * 1024 * 1024, max(32 * 1024 * 1024, estimate_bytes * 5 // 4)))


def _rms_scale(x):
    return lax.rsqrt(jnp.mean(x * x, axis=-1, keepdims=True) + NORM_EPS)


def _silu(a):
    return a * jax.nn.sigmoid(a)


def _ret_in_kernel(x_ref, g_ref, w_ref, cb_ref, sb_ref, cr_ref, sr_ref, qdec_ref, kdec_ref,
                   q_ref, kt_ref, v_ref, gate_ref, *, heads, dk, dv):
    x = x_ref[...]
    xn = (x * _rms_scale(x) * g_ref[...]).astype(BF16)
    cos, sin = _tile_cos_sin(cb_ref, sb_ref, cr_ref, sr_ref)
    half = dk // 2

    def rotated(col0, h, dec):
        p = jnp.dot(xn, w_ref[:, col0 + h * dk:col0 + (h + 1) * dk], preferred_element_type=F32)
        x1 = p[:, :half]
        x2 = p[:, half:]
        cos_h = cos * dec
        sin_h = sin * dec
        return x1 * cos_h - x2 * sin_h, x2 * cos_h + x1 * sin_h

    for h in range(heads):
        lo, hi = rotated(0, h, qdec_ref[h])
        q_ref[:, h * dk:h * dk + half] = lo.astype(BF16)
        q_ref[:, h * dk + half:(h + 1) * dk] = hi.astype(BF16)
        lo, hi = rotated(heads * dk, h, kdec_ref[h])
        kt_ref[h, :half] = lo.astype(BF16).T
        kt_ref[h, half:] = hi.astype(BF16).T
    v0 = 2 * heads * dk
    g0 = v0 + heads * dv
    for h in range(heads):
        v_ref[:, h * dv:(h + 1) * dv] = jnp.dot(
            xn, w_ref[:, v0 + h * dv:v0 + (h + 1) * dv], preferred_element_type=F32).astype(BF16)
        gate_ref[:, h * dv:(h + 1) * dv] = jnp.dot(
            xn, w_ref[:, g0 + h * dv:g0 + (h + 1) * dv], preferred_element_type=F32).astype(BF16)


def _ret_in_proj(h, norm_g, w_bf16, seq, tm=256):
    T, D = h.shape
    heads = RET_HEADS
    dk = D // heads
    dv = 2 * dk
    n_cols = w_bf16.shape[1]
    n_pos_tiles = seq // tm
    half = dk // 2
    inv = 1.0 / (RET_ROT_BASE ** (jnp.arange(half, dtype=F32) / half))
    cos_b, sin_b, cos_r, sin_r = _angle_tables(inv, seq, tm)
    assert tm == RET_KERNEL_CHUNK
    log_gamma = _ret_log_gamma(heads)
    idx = jnp.arange(tm, dtype=F32)
    q_dec = jnp.broadcast_to(jnp.exp(log_gamma[:, None] * idx)[:, :, None], (heads, tm, half))
    k_dec = jnp.broadcast_to((jnp.exp(-log_gamma[:, None] * idx) * dk ** -0.5)[:, :, None], (heads, tm, half))
    pos = lambda i: (i % n_pos_tiles, 0, 0)
    est = 2 * D * n_cols * 2 + 2 * tm * D * 4 + 2 * tm * n_cols * 2 + (4 + 4 * heads) * tm * LANES * 4
    return pl.pallas_call(
        functools.partial(_ret_in_kernel, heads=heads, dk=dk, dv=dv),
        grid=(T // tm,),
        in_specs=[
            pl.BlockSpec((tm, D), lambda i: (i, 0)),
            pl.BlockSpec((1, D), lambda i: (0, 0)),
            pl.BlockSpec((D, n_cols), lambda i: (0, 0)),
            pl.BlockSpec((None, 1, half), pos),
            pl.BlockSpec((None, 1, half), pos),
            pl.BlockSpec((tm, half), lambda i: (0, 0)),
            pl.BlockSpec((tm, half), lambda i: (0, 0)),
            pl.BlockSpec((heads, tm, half), lambda i: (0, 0, 0)),
            pl.BlockSpec((heads, tm, half), lambda i: (0, 0, 0)),
        ],
        out_specs=[
            pl.BlockSpec((tm, heads * dk), lambda i: (i, 0)),
            pl.BlockSpec((heads, None, dk, tm), lambda i: (0, i, 0, 0)),
            pl.BlockSpec((tm, heads * dv), lambda i: (i, 0)),
            pl.BlockSpec((tm, heads * dv), lambda i: (i, 0)),
        ],
        out_shape=[
            jax.ShapeDtypeStruct((T, heads * dk), BF16),
            jax.ShapeDtypeStruct((heads, T // tm, dk, tm), BF16),
            jax.ShapeDtypeStruct((T, heads * dv), BF16),
            jax.ShapeDtypeStruct((T, heads * dv), BF16),
        ],
        compiler_params=pltpu.CompilerParams(
            dimension_semantics=("arbitrary",), vmem_limit_bytes=_vmem_limit(est)),
        name="ret_in_proj",
    )(h, norm_g, w_bf16, cos_b, sin_b, cos_r, sin_r, q_dec, k_dec)


def _ret_log_gamma(heads):
    return jnp.log1p(-jnp.power(2.0, -5.0 - jnp.arange(heads, dtype=F32)))


def _ret_core_kernel(q_ref, kt_ref, v_ref, g_ref, cd_ref, y_ref, state_ref, *, heads):
    @pl.when(pl.program_id(1) == 0)
    def _():
        state_ref[...] = jnp.zeros_like(state_ref)

    cc = q_ref.shape[0]
    dk = q_ref.shape[1] // heads
    dv = v_ref.shape[1] // heads
    causal = lax.broadcasted_iota(jnp.int32, (cc, cc), 0) >= lax.broadcasted_iota(jnp.int32, (cc, cc), 1)
    for h in range(heads):
        q = q_ref[:, h * dk:(h + 1) * dk]
        kt = kt_ref[h]
        v = v_ref[:, h * dv:(h + 1) * dv]
        s = jnp.where(causal, jnp.dot(q, kt, preferred_element_type=F32), 0.0)
        u = state_ref[h]
        o = (jnp.dot(s.astype(BF16), v, preferred_element_type=F32)
             + jnp.dot(q, u.astype(BF16), preferred_element_type=F32))
        state_ref[h] = cd_ref[h] * (u + jnp.dot(kt, v, preferred_element_type=F32))
        o = o * _rms_scale(o)
        y_ref[:, h * dv:(h + 1) * dv] = (_silu(g_ref[:, h * dv:(h + 1) * dv].astype(F32)) * o).astype(BF16)


def _ret_core(q, kt, v, gate, batch, seq):
    T = q.shape[0]
    heads = RET_HEADS
    dk = q.shape[1] // heads
    dv = v.shape[1] // heads
    cc = RET_KERNEL_CHUNK
    nc = seq // cc
    chunk_decay = jnp.broadcast_to(jnp.exp(_ret_log_gamma(heads) * cc)[:, None, None], (heads, 1, dv))
    row = lambda b, c: (b * nc + c, 0)
    return pl.pallas_call(
        functools.partial(_ret_core_kernel, heads=heads),
        grid=(batch, nc),
        in_specs=[
            pl.BlockSpec((cc, heads * dk), row),
            pl.BlockSpec((heads, None, dk, cc), lambda b, c: (0, b * nc + c, 0, 0)),
            pl.BlockSpec((cc, heads * dv), row),
            pl.BlockSpec((cc, heads * dv), row),
            pl.BlockSpec((heads, 1, dv), lambda b, c: (0, 0, 0)),
        ],
        out_specs=pl.BlockSpec((cc, heads * dv), row),
        out_shape=jax.ShapeDtypeStruct((T, heads * dv), BF16),
        scratch_shapes=[pltpu.VMEM((heads, dk, dv), F32)],
        compiler_params=pltpu.CompilerParams(dimension_semantics=("arbitrary", "arbitrary")),
        name="ret_core",
    )(q, kt, v, gate, chunk_decay)


def _proj_router_kernel(y_ref, w_ref, r_ref, g_ref, wr_ref, br_ref, haug_ref, route_ref, counts_ref, run_ref,
                        *, groups, experts, d_model):
    @pl.when(pl.program_id(0) == 0)
    def _():
        run_ref[...] = jnp.zeros_like(run_ref)

    h = r_ref[...] + jnp.dot(y_ref[...], w_ref[...], preferred_element_type=F32)
    xn = h * _rms_scale(h) * g_ref[...]
    xh = xn.astype(BF16)
    xl = (xn - xh.astype(F32)).astype(BF16)
    hi_both = jnp.dot(xh, wr_ref[...], preferred_element_type=F32)
    lo_hi = jnp.dot(xl, wr_ref[:, :LANES], preferred_element_type=F32)
    logits = hi_both[:, :LANES] + (hi_both[:, LANES:] + lo_hi) + br_ref[...]
    lt = logits.T[:ROUTE_ROWS]
    tm = lt.shape[1]
    row = lax.broadcasted_iota(jnp.int32, lt.shape, 0)
    neg = -jnp.inf

    def argmax_first(vals):
        m = jnp.max(vals, axis=0, keepdims=True)
        first = jnp.min(jnp.where(vals == m, row, ROUTE_ROWS), axis=0, keepdims=True)
        return m, first

    gl = jnp.where(row < groups, lt, neg)
    gmax, gidx = argmax_first(gl)
    gsum = jnp.sum(jnp.where(row < groups, jnp.exp(gl - gmax), 0.0), axis=0, keepdims=True)
    g_val = 1.0 / gsum
    e0 = groups + experts * gidx
    el = jnp.where((row >= e0) & (row < e0 + experts), lt, neg)
    m1, i1 = argmax_first(el)
    m2, i2 = argmax_first(jnp.where(row == i1, neg, el))
    t = jnp.exp(m2 - m1)
    w1 = 1.0 / (1.0 + t)
    w2 = t / (1.0 + t)
    first_is_lo = i1 < i2
    lo = jnp.minimum(i1, i2) - e0
    hi = jnp.maximum(i1, i2) - e0
    w_lo = g_val * jnp.where(first_is_lo, w1, w2)
    w_hi = g_val * jnp.where(first_is_lo, w2, w1)
    lex = (lo * (7 - lo)) // 2 + (hi - lo - 1)
    pair = jnp.where(lex == 2, 4, jnp.where(lex == 3, 2, jnp.where(lex == 4, 3, lex)))
    cls = gidx * N_PAIRS + pair

    onehot = row == cls
    oh = jnp.where(onehot, 1.0, 0.0)
    before = lax.broadcasted_iota(jnp.int32, (tm, tm), 0) < lax.broadcasted_iota(jnp.int32, (tm, tm), 1)
    prefix = jnp.dot(oh.astype(BF16), jnp.where(before, 1.0, 0.0).astype(BF16), preferred_element_type=F32)
    rank = jnp.sum(jnp.where(onehot, prefix + run_ref[:, 0:1], 0.0), axis=0, keepdims=True)
    run_new = run_ref[...] + jnp.sum(oh, axis=1, keepdims=True)
    run_ref[...] = run_new
    counts_ref[...] = run_new

    r8 = lax.broadcasted_iota(jnp.int32, route_ref.shape, 0)
    route_ref[...] = jnp.where(r8 == 0, cls, jnp.where(r8 == 1, rank.astype(jnp.int32), 0))
    rl = lax.broadcasted_iota(jnp.int32, (LANES, tm), 0)
    meta_t = jnp.where(rl == 0, w_lo, jnp.where(rl == 1, w_hi, 0.0))
    haug_ref[:, :d_model] = h
    haug_ref[:, d_model:] = meta_t.T


def _proj_router(y, w_bf16, resid, norm_g, w_rg, b_rg, w_re, b_re, name, tm=512):
    T, K = y.shape
    D = w_bf16.shape[1]
    G, E = N_GROUPS, EXPERTS_PER_GROUP
    n_used = G + G * E
    wr = jnp.concatenate([w_rg, jnp.transpose(w_re, (1, 0, 2)).reshape(D, G * E)], axis=1)
    wr = jnp.pad(wr, ((0, 0), (0, LANES - n_used)))
    wr_hi = wr.astype(BF16)
    wr = jnp.concatenate([wr_hi, (wr - wr_hi.astype(F32)).astype(BF16)], axis=1)
    br = jnp.pad(jnp.concatenate([b_rg, b_re.reshape(G * E)]), (0, LANES - n_used))[None, :]
    est = 2 * K * D * 2 + 2 * tm * K * 2 + 6 * tm * D * 4 + 2 * D * LANES * 4 + 4 * tm * tm * 4
    const = lambda i: (0, 0)
    return pl.pallas_call(
        functools.partial(_proj_router_kernel, groups=G, experts=E, d_model=D),
        grid=(T // tm,),
        in_specs=[
            pl.BlockSpec((tm, K), lambda i: (i, 0)),
            pl.BlockSpec((K, D), const),
            pl.BlockSpec((tm, D), lambda i: (i, 0)),
            pl.BlockSpec((1, D), const),
            pl.BlockSpec((D, 2 * LANES), const),
            pl.BlockSpec((1, LANES), const),
        ],
        out_specs=[
            pl.BlockSpec((tm, D + LANES), lambda i: (i, 0)),
            pl.BlockSpec((8, tm), lambda i: (0, i)),
            pl.BlockSpec((ROUTE_ROWS, LANES), const),
        ],
        out_shape=[
            jax.ShapeDtypeStruct((T, D + LANES), F32),
            jax.ShapeDtypeStruct((8, T), jnp.int32),
            jax.ShapeDtypeStruct((ROUTE_ROWS, LANES), F32),
        ],
        scratch_shapes=[pltpu.VMEM((ROUTE_ROWS, LANES), F32)],
        compiler_params=pltpu.CompilerParams(
            dimension_semantics=("arbitrary",), vmem_limit_bytes=_vmem_limit(est)),
        name=name,
    )(y, w_bf16, resid, norm_g, wr, br)


def _tile_schedule(counts_blk, tm, n_tiles):
    counts = counts_blk[:N_CLASSES, 0].astype(jnp.int32)
    tiles_per = (counts + tm - 1) // tm
    tile_end = jnp.cumsum(tiles_per)
    tile_begin = tile_end - tiles_per
    cstart = jnp.cumsum(counts) - counts
    t = jnp.arange(n_tiles, dtype=jnp.int32)
    live = t < tile_end[-1]
    tq = jnp.where(live, t, tile_end[-1] - 1)
    tcls = jnp.sum((tile_end[None, :] <= tq[:, None]).astype(jnp.int32), axis=1)
    onehot = tcls[:, None] == jnp.arange(N_CLASSES, dtype=jnp.int32)[None, :]
    pick = lambda table: jnp.sum(jnp.where(onehot, jnp.asarray(table, jnp.int32)[None, :], 0), axis=1)
    k = tq - pick(tile_begin)
    nvalid = jnp.where(live, jnp.clip(pick(counts) - k * tm, 0, tm), 0).astype(jnp.int32)
    base = jnp.where(live, pick(cstart) + k * tm, 0).astype(jnp.int32)
    tile_g = pick([c // N_PAIRS for c in range(N_CLASSES)])
    tile_lo = pick([PAIR_LO[c % N_PAIRS] for c in range(N_CLASSES)])
    tile_hi = pick([PAIR_HI[c % N_PAIRS] for c in range(N_CLASSES)])
    cstart_pad = jnp.pad(cstart, (0, ROUTE_ROWS - N_CLASSES)).astype(jnp.int32)
    return tile_g, tile_lo, tile_hi, nvalid, base, cstart_pad


def _moe_kernel(tg_ref, tlo_ref, thi_ref, tnv_ref, tbase_ref, cs_ref,
                h_hbm, route_ref, gn_ref, fn_ref, wg_lo, wu_lo, wd_lo, wg_hi, wu_hi, wd_hi,
                out_hbm, rt_ref, pos_smem, pos_vmem, x0, x1, x2, o0, o1, o2, w_in_bf, w_out_bf,
                gsem, ssem, psem, *, tm, d_model, n_tok, final_norm):
    i = pl.program_id(0)
    xbufs = (x0, x1, x2)
    obufs = (o0, o1, o2)
    depth = len(xbufs)

    def issue_gather(t, sl):
        base = tbase_ref[t]
        for r in range(tm):
            tok = rt_ref[base + r]
            pltpu.make_async_copy(h_hbm.at[pl.ds(tok, 1)], xbufs[sl].at[pl.ds(r, 1)], gsem.at[sl]).start()

    def scatter_row(sl, r, tok):
        return pltpu.make_async_copy(obufs[sl].at[pl.ds(r, 1)], out_hbm.at[pl.ds(tok, 1)], ssem.at[sl])

    def issue_scatter(sl, base, n_valid):
        for r in range(tm):
            @pl.when(r < n_valid)
            def _():
                scatter_row(sl, r, rt_ref[base + r]).start()

    def wait_gather(sl):
        pltpu.make_async_copy(h_hbm.at[pl.ds(0, tm)], xbufs[sl], gsem.at[sl]).wait()

    def wait_scatter(sl, n_valid):
        aligned = pl.multiple_of((n_valid // SUBLANES) * SUBLANES, SUBLANES)

        @pl.when(aligned > 0)
        def _():
            rows = pl.ds(0, aligned)
            pltpu.make_async_copy(obufs[sl].at[rows], out_hbm.at[rows], ssem.at[sl]).wait()

        def single(r, c):
            scatter_row(sl, r, 0).wait()
            return c
        lax.fori_loop(aligned, n_valid, single, 0)

    def live(t):
        return tnv_ref[jnp.maximum(t, 0)] > 0

    @pl.when(i == 0)
    def _():
        cls2 = route_ref[0]
        pos = route_ref[1]
        for c in range(N_CLASSES):
            pos = pos + jnp.where(cls2 == c, cs_ref[c], 0)
        pos_vmem[...] = pos
        to_smem = pltpu.make_async_copy(pos_vmem, pos_smem, psem)
        to_smem.start()
        to_smem.wait()

        def place(row, c):
            for col in range(LANES):
                rt_ref[pos_smem[row, col]] = row * LANES + col
            return c
        lax.fori_loop(0, n_tok // LANES, place, 0)
        for r in range(tm):
            rt_ref[n_tok + r] = 0
        issue_gather(0, 0)
        issue_gather(1, 1)

    prev = jnp.maximum(i - 1, 0)
    group_changed = (i == 0) | (tg_ref[i] != tg_ref[prev])
    for which, t_ref, (wg, wu, wd) in ((0, tlo_ref, (wg_lo, wu_lo, wd_lo)), (1, thi_ref, (wg_hi, wu_hi, wd_hi))):
        @pl.when(group_changed | (t_ref[i] != t_ref[prev]))
        def _():
            w_in_bf[2 * which] = wg[...].astype(BF16)
            w_in_bf[2 * which + 1] = wu[...].astype(BF16)
            w_out_bf[which] = wd[...].astype(BF16)

    def tile(cur):
        far = (cur + depth - 1) % depth
        @pl.when((i <= 1) | live(i - 2))
        def _():
            wait_gather(cur)

        @pl.when((i >= depth) & live(i - depth))
        def _():
            wait_scatter(cur, tnv_ref[jnp.maximum(i - depth, 0)])

        prev_base = tbase_ref[prev]
        prev_valid = jnp.where(i == 0, 0, tnv_ref[prev])

        @pl.when(live(i))
        def _():
            issue_gather(i + 2, far)
            xa = xbufs[cur][...]
            x = xa[:, :d_model]
            xn = (x * _rms_scale(x) * gn_ref[...]).astype(BF16)

            def expert(which, w):
                a = jnp.dot(xn, w_in_bf[2 * which], preferred_element_type=F32)
                b = jnp.dot(xn, w_in_bf[2 * which + 1], preferred_element_type=F32)
                return jnp.dot((_silu(a) * b * w).astype(BF16), w_out_bf[which], preferred_element_type=F32)

            y = expert(0, xa[:, d_model:d_model + 1]) + expert(1, xa[:, d_model + 1:d_model + 2])
            obufs[cur][...] = x + y
            issue_scatter(far, prev_base, prev_valid)
            if final_norm:
                for r0 in range(0, tm, 4 * SUBLANES):
                    rows = slice(r0, r0 + 4 * SUBLANES)
                    o = obufs[cur][rows, :]
                    obufs[cur][rows, :] = o * _rms_scale(o) * fn_ref[...]

        @pl.when(jnp.logical_not(live(i)) & (i >= 1) & live(i - 1))
        def _():
            issue_scatter(far, prev_base, prev_valid)

    for residue in range(depth):
        @pl.when(i % depth == residue)
        def _():
            tile(residue)


def _moe_layer(h_aug, route, counts_blk, ffn_norm_g, final_norm_g, wg, wu, wd, layer, final_norm):
    T, DA = h_aug.shape
    D = DA - LANES
    F = wg.shape[-1]
    tm = MOE_TILE
    assert T % (LANES * SUBLANES) == 0 and T >= 2 * tm
    n_tiles = T // tm + N_CLASSES + 3
    tile_g, tile_lo, tile_hi, nvalid, base, cstart = _tile_schedule(counts_blk, tm, n_tiles)
    route2d = route[:2].reshape(2, T // LANES, LANES)

    def w_spec(shape, which):
        def index(i, tg, tlo, thi, *_):
            return (layer, tg[i], (tlo if which == 0 else thi)[i], 0, 0)
        return pl.BlockSpec((None, None, None) + shape, index)

    est = 2 * 6 * D * F * 4 + 6 * D * F * 2 + 3 * tm * DA * 4 + 3 * tm * D * 4 + 8 * tm * F * 4
    return pl.pallas_call(
        functools.partial(_moe_kernel, tm=tm, d_model=D, n_tok=T, final_norm=final_norm),
        grid_spec=pltpu.PrefetchScalarGridSpec(
            num_scalar_prefetch=6,
            grid=(n_tiles,),
            in_specs=[
                pl.BlockSpec(memory_space=pl.ANY),
                pl.BlockSpec((2, T // LANES, LANES), lambda i, *_: (0, 0, 0)),
                pl.BlockSpec((1, D), lambda i, *_: (0, 0)),
                pl.BlockSpec((1, D), lambda i, *_: (0, 0)),
                w_spec((D, F), 0), w_spec((D, F), 0), w_spec((F, D), 0),
                w_spec((D, F), 1), w_spec((D, F), 1), w_spec((F, D), 1),
            ],
            out_specs=pl.BlockSpec(memory_space=pl.ANY),
            scratch_shapes=[
                pltpu.SMEM((T + tm,), jnp.int32),
                pltpu.SMEM((T // LANES, LANES), jnp.int32),
                pltpu.VMEM((T // LANES, LANES), jnp.int32),
                pltpu.VMEM((tm, DA), F32), pltpu.VMEM((tm, DA), F32), pltpu.VMEM((tm, DA), F32),
                pltpu.VMEM((tm, D), F32), pltpu.VMEM((tm, D), F32), pltpu.VMEM((tm, D), F32),
                pltpu.VMEM((4, D, F), BF16),
                pltpu.VMEM((2, F, D), BF16),
                pltpu.SemaphoreType.DMA((3,)),
                pltpu.SemaphoreType.DMA((3,)),
                pltpu.SemaphoreType.DMA(()),
            ],
        ),
        out_shape=jax.ShapeDtypeStruct((T, D), F32),
        compiler_params=pltpu.CompilerParams(
            dimension_semantics=("arbitrary",), vmem_limit_bytes=_vmem_limit(est)),
        name="moe_experts",
    )(tile_g, tile_lo, tile_hi, nvalid, base, cstart,
      h_aug, route2d, ffn_norm_g, final_norm_g, wg, wu, wd, wg, wu, wd)


def _partial_rotary(xh, cosf, sinp, sinm, half):
    lanes = xh.shape[-1]
    return xh * cosf + pltpu.roll(xh, half, 1) * sinp + pltpu.roll(xh, lanes - half, 1) * sinm


def _tile_cos_sin(cb_ref, sb_ref, cr_ref, sr_ref):
    cb, sb, cr, sr = cb_ref[...], sb_ref[...], cr_ref[...], sr_ref[...]
    return cb * cr - sb * sr, sb * cr + cb * sr


def _kvq_kernel(h_ref, gkv_ref, gq_ref, wkv_ref, wq_ref, cb_ref, sb_ref, cr_ref, sr_ref,
                k_ref, vt_ref, qt_ref, km_ref, *, heads, dh, half):
    x = h_ref[...]
    xr = x * _rms_scale(x)
    xkv = (xr * gkv_ref[...]).astype(BF16)
    xq = (xr * gq_ref[...]).astype(BF16)
    cosf, sin_all = _tile_cos_sin(cb_ref, sb_ref, cr_ref, sr_ref)
    lane = lax.broadcasted_iota(jnp.int32, sin_all.shape, 1)
    sinp = jnp.where(lane >= half, sin_all, 0.0)
    sinm = jnp.where(lane < half, -sin_all, 0.0)
    width = heads * dh
    q_scale = dh ** -0.5 * LOG2_E
    pair_w = 2 * dh
    for c0 in range(0, width, pair_w):
        kk = jnp.dot(xkv, wkv_ref[:, c0:c0 + pair_w], preferred_element_type=F32)
        vv = jnp.dot(xkv, wkv_ref[:, width + c0:width + c0 + pair_w], preferred_element_type=F32)
        qq = jnp.dot(xq, wq_ref[:, c0:c0 + pair_w], preferred_element_type=F32)
        for u in range(2):
            hh = c0 // dh + u
            cols = slice(hh * dh, (hh + 1) * dh)
            kh = _partial_rotary(kk[:, u * dh:(u + 1) * dh], cosf, sinp, sinm, half)
            k_ref[hh] = kh.astype(BF16)
            km_ref[:, cols] = jnp.mean(kh, axis=0, keepdims=True)
            vt_ref[hh, :dh] = vv[:, u * dh:(u + 1) * dh].astype(BF16).T
            vt_ref[hh, dh:] = jnp.ones((VT_ONES_ROWS, vv.shape[0]), BF16)
            qh = _partial_rotary(qq[:, u * dh:(u + 1) * dh], cosf, sinp, sinm, half)
            qt_ref[hh] = (qh * q_scale).astype(BF16).T


def _angle_tables(inv_freq, seq, tm):
    base = jnp.arange(0, seq, tm, dtype=F32)[:, None] * inv_freq[None, :]
    offs = jnp.arange(tm, dtype=F32)[:, None] * inv_freq[None, :]
    return jnp.cos(base)[:, None, :], jnp.sin(base)[:, None, :], jnp.cos(offs), jnp.sin(offs)


def _rope_tables(seq, dh, tm):
    rot = dh // ROPE_FRACTION
    half = rot // 2
    inv = 1.0 / (ROPE_THETA ** (jnp.arange(half, dtype=F32) / half))
    inv_lanes = jnp.concatenate([inv, inv, jnp.zeros((dh - rot,), F32)])
    return _angle_tables(inv_lanes, seq, tm) + (half,)


def _kvq_proj(h, kv_norm_g, q_norm_g, wkv_bf16, wq_bf16, seq):
    T, D = h.shape
    heads = ATT_HEADS
    dh = D // heads
    tm = MOBA_BLOCK
    n_pos_tiles = seq // tm
    cos_b, sin_b, cos_r, sin_r, half = _rope_tables(seq, dh, tm)
    width = heads * dh
    est = 2 * (D * 2 * width + D * width) * 2 + 2 * tm * D * 4 + 6 * tm * width * 2 + 8 * tm * dh * 4
    pos = lambda i: (i % n_pos_tiles, 0, 0)
    const = lambda i: (0, 0)
    per_block = lambda i: (0, i, 0, 0)
    return pl.pallas_call(
        functools.partial(_kvq_kernel, heads=heads, dh=dh, half=half),
        grid=(T // tm,),
        in_specs=[
            pl.BlockSpec((tm, D), lambda i: (i, 0)),
            pl.BlockSpec((1, D), const),
            pl.BlockSpec((1, D), const),
            pl.BlockSpec((D, 2 * width), const),
            pl.BlockSpec((D, width), const),
            pl.BlockSpec((None, 1, dh), pos),
            pl.BlockSpec((None, 1, dh), pos),
            pl.BlockSpec((tm, dh), const),
            pl.BlockSpec((tm, dh), const),
        ],
        out_specs=[
            pl.BlockSpec((heads, None, tm, dh), per_block),
            pl.BlockSpec((heads, None, dh + VT_ONES_ROWS, tm), per_block),
            pl.BlockSpec((heads, None, dh, tm), per_block),
            pl.BlockSpec((None, 1, width), lambda i: (i, 0, 0)),
        ],
        out_shape=[
            jax.ShapeDtypeStruct((heads, T // tm, tm, dh), BF16),
            jax.ShapeDtypeStruct((heads, T // tm, dh + VT_ONES_ROWS, tm), BF16),
            jax.ShapeDtypeStruct((heads, T // tm, dh, tm), BF16),
            jax.ShapeDtypeStruct((T // tm, 1, width), F32),
        ],
        compiler_params=pltpu.CompilerParams(
            dimension_semantics=("arbitrary",), vmem_limit_bytes=_vmem_limit(est)),
        name="kvq_proj",
    )(h, kv_norm_g, q_norm_g, wkv_bf16, wq_bf16, cos_b, sin_b, cos_r, sin_r)


def _moba_kernel(qt_ref, k_ref, vt_ref, km_ref, o_ref, sel_ref, acc_ref, sa_ref, sb_ref, pa_ref, pb_ref,
                 *, topk, group, dh):
    qi = pl.program_id(2)
    n_blk = km_ref.shape[0]
    blk = sa_ref.shape[1]
    neg = -jnp.inf
    chunks = [slice(c, c + MOBA_KEY_CHUNK) for c in range(0, blk, MOBA_KEY_CHUNK)]

    def scores(g, j):
        return jnp.dot(k_ref[g, j], qt_ref[g], preferred_element_type=F32)

    def stage_scores(s_buf, j):
        jc = jnp.minimum(j, n_blk - 1)
        for g in range(group):
            s_buf[g] = scores(g, jc)

    def stage_softmax(s_buf, p_buf, ms, j):
        new_m, alphas = [], []
        for g in range(group):
            part = s_buf[g, chunks[0], :]
            for c in chunks[1:]:
                part = jnp.maximum(part, s_buf[g, c, :])
            m_blk = jnp.max(part, axis=0, keepdims=True)
            if j is None:
                m_new = shift = m_blk
            else:
                bias = sel_ref[g, pl.ds(jnp.minimum(j, n_blk - 1), 1), :]
                m_new = jnp.maximum(ms[g], m_blk + bias)
                shift = m_new - bias
            for c in chunks:
                p_buf[g, c, :] = jnp.exp2(s_buf[g, c, :] - shift).astype(BF16)
            new_m.append(m_new)
            alphas.append(None if j is None else jnp.exp2(ms[g] - m_new))
        return tuple(new_m), alphas

    def stage_pv(p_buf, j, alphas):
        jc = jnp.minimum(j, n_blk - 1)
        for g in range(group):
            pv = jnp.dot(vt_ref[g, jc], p_buf[g], preferred_element_type=F32)
            acc_ref[g] = pv if alphas[g] is None else alphas[g] * acc_ref[g] + pv

    for g in range(group):
        km = km_ref[:, g * dh:(g + 1) * dh]
        km_hi = km.astype(BF16)
        km_lo = (km - km_hi.astype(F32)).astype(BF16)
        gate = (jnp.dot(km_hi, qt_ref[g], preferred_element_type=F32)
                + jnp.dot(km_lo, qt_ref[g], preferred_element_type=F32))
        blk_id = lax.broadcasted_iota(jnp.int32, gate.shape, 0)
        gv = jnp.where(blk_id < qi, gate, neg)
        bias = jnp.full(gate.shape, neg, F32)
        for _ in range(topk):
            m = jnp.max(gv, axis=0, keepdims=True)
            first = jnp.min(jnp.where(gv == m, blk_id, n_blk), axis=0, keepdims=True)
            bias = jnp.where((blk_id == first) & (m > neg), 0.0, bias)
            gv = jnp.where(blk_id == first, neg, gv)
        sel_ref[g] = bias

    key_id = lax.broadcasted_iota(jnp.int32, (blk, blk), 0)
    qry_id = lax.broadcasted_iota(jnp.int32, (blk, blk), 1)
    for g in range(group):
        sb_ref[g] = jnp.where(key_id <= qry_id, scores(g, qi), neg)
    stage_scores(sa_ref, 0)
    ms, alphas = stage_softmax(sb_ref, pb_ref, None, None)
    stage_pv(pb_ref, qi, alphas)

    def body(t, ms):
        j = 2 * t
        ms, alphas = stage_softmax(sa_ref, pa_ref, ms, j)
        stage_scores(sb_ref, j + 1)
        stage_pv(pa_ref, j, alphas)
        ms, alphas = stage_softmax(sb_ref, pb_ref, ms, j + 1)
        stage_scores(sa_ref, j + 2)
        stage_pv(pb_ref, j + 1, alphas)
        return ms

    lax.fori_loop(0, (qi + 1) // 2, body, ms)
    for g in range(group):
        acc = acc_ref[g]
        o_ref[:, g * dh:(g + 1) * dh] = (acc[:dh] / acc[dh:dh + 1]).astype(BF16).T


def _moba_attention(qt, k, vt, k_mean, batch, seq):
    heads, n_tiles, dh, blk = qt.shape
    dv_rows = vt.shape[2]
    n_blk = seq // blk
    topk = min(MOBA_TOPK, n_blk)
    width = heads * dh
    group = MOBA_HEAD_GROUP
    km = k_mean.reshape(batch, n_blk, width)
    per_seq = lambda b, h, i: (h, b, 0, 0)
    est = 2 * group * seq * (dh + dv_rows) * 2 + 6 * group * blk * blk * 4 + 4 * group * blk * dh * 4
    return pl.pallas_call(
        functools.partial(_moba_kernel, topk=topk, group=group, dh=dh),
        grid=(batch, heads // group, n_blk),
        in_specs=[
            pl.BlockSpec((group, None, dh, blk), lambda b, h, i: (h, b * n_blk + i, 0, 0)),
            pl.BlockSpec((group, n_blk, blk, dh), per_seq),
            pl.BlockSpec((group, n_blk, dv_rows, blk), per_seq),
            pl.BlockSpec((None, n_blk, group * dh), lambda b, h, i: (b, 0, h)),
        ],
        out_specs=pl.BlockSpec((blk, group * dh), lambda b, h, i: (b * n_blk + i, h)),
        out_shape=jax.ShapeDtypeStruct((n_tiles * blk, width), BF16),
        scratch_shapes=[pltpu.VMEM((group, n_blk, blk), F32), pltpu.VMEM((group, dv_rows, blk), F32),
                        pltpu.VMEM((group, blk, blk), F32), pltpu.VMEM((group, blk, blk), F32),
                        pltpu.VMEM((group, blk, blk), BF16), pltpu.VMEM((group, blk, blk), BF16)],
        compiler_params=pltpu.CompilerParams(
            dimension_semantics=("arbitrary", "arbitrary", "arbitrary"), vmem_limit_bytes=_vmem_limit(est)),
        name="moba_attention",
    )(qt, k, vt, km)


def kernel(x, ret_norm, ret_w_in, ret_w_out, kv_norm, w_kv, attn_norm, w_q, w_o, ffn_norm, router_group_w, router_group_b, router_expert_w, router_expert_b, expert_w_gate, expert_w_up, expert_w_down, final_norm):
    B, S, D = x.shape
    T = B * S
    assert S % MOBA_BLOCK == 0 and S % RET_KERNEL_CHUNK == 0 and T % MOE_TILE == 0
    assert ret_norm.shape[0] == 1 and attn_norm.shape[0] == 1 and ffn_norm.shape[0] == 2
    h = x.reshape(T, D)
    final_g = final_norm[None, :]

    def proj_router(y, w, resid, layer, name):
        return _proj_router(y, w.astype(BF16), resid, ffn_norm[layer][None, :], router_group_w[layer],
                            router_group_b[layer], router_expert_w[layer], router_expert_b[layer], name)

    def moe(h_aug, route, counts, layer, last):
        return _moe_layer(h_aug, route, counts, ffn_norm[layer][None, :], final_g,
                          expert_w_gate, expert_w_up, expert_w_down, layer, final_norm=last)

    q, kt, v, gate = _ret_in_proj(h, ret_norm[0][None, :], ret_w_in[0].astype(BF16), S)
    y = _ret_core(q, kt, v, gate, B, S)
    h = moe(*proj_router(y, ret_w_out[0], h, 0, "ret_out_proj_router"), 0, False)

    k2, vt2, qt2, k_mean = _kvq_proj(h, kv_norm[None, :], attn_norm[0][None, :],
                                     w_kv.astype(BF16), w_q[0].astype(BF16), S)
    o = _moba_attention(qt2, k2, vt2, k_mean, B, S)
    h = moe(*proj_router(o, w_o[0], h, 1, "attn_out_proj_router"), 1, True)
    return h.reshape(B, S, D)
```

```python
import functools

import jax
import jax.numpy as jnp
from jax import lax
from jax.experimental import pallas as pl
from jax.experimental.pallas import tpu as pltpu

F32 = jnp.float32
BF16 = jnp.bfloat16

NORM_EPS = 1e-6
RET_HEADS = 4
RET_ROT_BASE = 10000.0
ATT_HEADS = 8
ROPE_FRACTION = 4
ROPE_THETA = 500000.0
MOBA_BLOCK = 256
MOBA_TOPK = 3
N_GROUPS = 4
EXPERTS_PER_GROUP = 4
PAIR_LO = (0, 0, 1, 1, 0, 2)
PAIR_HI = (1, 2, 2, 3, 3, 3)
N_PAIRS = len(PAIR_LO)
N_CLASSES = N_GROUPS * N_PAIRS

LANES = 128
SUBLANES = 8
RET_KERNEL_CHUNK = 256
MOE_TILE = 256
ROUTE_ROWS = 32
MOBA_HEAD_GROUP = 4
MOBA_KEY_CHUNK = 32
VT_ONES_ROWS = 16
LOG2_E = 1.4426950408889634
V7X_VMEM_BYTES = 64 * 1024 * 1024


def _vmem_limit(estimate_bytes):
    return int(min(V7X_VMEM_BYTES - 8 * 1024 * 1024, max(32 * 1024 * 1024, estimate_bytes * 5 // 4)))


def _rms_scale(x):
    return lax.rsqrt(jnp.mean(x * x, axis=-1, keepdims=True) + NORM_EPS)


def _silu(a):
    return a * jax.nn.sigmoid(a)


def _ret_in_kernel(x_ref, g_ref, w_ref, cb_ref, sb_ref, cr_ref, sr_ref, qdec_ref, kdec_ref,
                   q_ref, kt_ref, v_ref, gate_ref, *, heads, dk, dv):
    x = x_ref[...]
    xn = (x * _rms_scale(x) * g_ref[...]).astype(BF16)
    cos, sin = _tile_cos_sin(cb_ref, sb_ref, cr_ref, sr_ref)
    half = dk // 2

    def rotated(col0, h, dec):
        p = jnp.dot(xn, w_ref[:, col0 + h * dk:col0 + (h + 1) * dk], preferred_element_type=F32)
        x1 = p[:, :half]
        x2 = p[:, half:]
        cos_h = cos * dec
        sin_h = sin * dec
        return x1 * cos_h - x2 * sin_h, x2 * cos_h + x1 * sin_h

    for h in range(heads):
        lo, hi = rotated(0, h, qdec_ref[h])
        q_ref[:, h * dk:h * dk + half] = lo.astype(BF16)
        q_ref[:, h * dk + half:(h + 1) * dk] = hi.astype(BF16)
        lo, hi = rotated(heads * dk, h, kdec_ref[h])
        kt_ref[h, :half] = lo.astype(BF16).T
        kt_ref[h, half:] = hi.astype(BF16).T
    v0 = 2 * heads * dk
    g0 = v0 + heads * dv
    for h in range(heads):
        v_ref[:, h * dv:(h + 1) * dv] = jnp.dot(
            xn, w_ref[:, v0 + h * dv:v0 + (h + 1) * dv], preferred_element_type=F32).astype(BF16)
        gate_ref[:, h * dv:(h + 1) * dv] = jnp.dot(
            xn, w_ref[:, g0 + h * dv:g0 + (h + 1) * dv], preferred_element_type=F32).astype(BF16)


def _ret_in_proj(h, norm_g, w_bf16, seq, tm=256):
    T, D = h.shape
    heads = RET_HEADS
    dk = D // heads
    dv = 2 * dk
    n_cols = w_bf16.shape[1]
    n_pos_tiles = seq // tm
    half = dk // 2
    inv = 1.0 / (RET_ROT_BASE ** (jnp.arange(half, dtype=F32) / half))
    cos_b, sin_b, cos_r, sin_r = _angle_tables(inv, seq, tm)
    assert tm == RET_KERNEL_CHUNK
    log_gamma = _ret_log_gamma(heads)
    idx = jnp.arange(tm, dtype=F32)
    q_dec = jnp.broadcast_to(jnp.exp(log_gamma[:, None] * idx)[:, :, None], (heads, tm, half))
    k_dec = jnp.broadcast_to((jnp.exp(-log_gamma[:, None] * idx) * dk ** -0.5)[:, :, None], (heads, tm, half))
    pos = lambda i: (i % n_pos_tiles, 0, 0)
    est = 2 * D * n_cols * 2 + 2 * tm * D * 4 + 2 * tm * n_cols * 2 + (4 + 4 * heads) * tm * LANES * 4
    return pl.pallas_call(
        functools.partial(_ret_in_kernel, heads=heads, dk=dk, dv=dv),
        grid=(T // tm,),
        in_specs=[
            pl.BlockSpec((tm, D), lambda i: (i, 0)),
            pl.BlockSpec((1, D), lambda i: (0, 0)),
            pl.BlockSpec((D, n_cols), lambda i: (0, 0)),
            pl.BlockSpec((None, 1, half), pos),
            pl.BlockSpec((None, 1, half), pos),
            pl.BlockSpec((tm, half), lambda i: (0, 0)),
            pl.BlockSpec((tm, half), lambda i: (0, 0)),
            pl.BlockSpec((heads, tm, half), lambda i: (0, 0, 0)),
            pl.BlockSpec((heads, tm, half), lambda i: (0, 0, 0)),
        ],
        out_specs=[
            pl.BlockSpec((tm, heads * dk), lambda i: (i, 0)),
            pl.BlockSpec((heads, None, dk, tm), lambda i: (0, i, 0, 0)),
            pl.BlockSpec((tm, heads * dv), lambda i: (i, 0)),
            pl.BlockSpec((tm, heads * dv), lambda i: (i, 0)),
        ],
        out_shape=[
            jax.ShapeDtypeStruct((T, heads * dk), BF16),
            jax.ShapeDtypeStruct((heads, T // tm, dk, tm), BF16),
            jax.ShapeDtypeStruct((T, heads * dv), BF16),
            jax.ShapeDtypeStruct((T, heads * dv), BF16),
        ],
        compiler_params=pltpu.CompilerParams(
            dimension_semantics=("arbitrary",), vmem_limit_bytes=_vmem_limit(est)),
        name="ret_in_proj",
    )(h, norm_g, w_bf16, cos_b, sin_b, cos_r, sin_r, q_dec, k_dec)


def _ret_log_gamma(heads):
    return jnp.log1p(-jnp.power(2.0, -5.0 - jnp.arange(heads, dtype=F32)))


def _ret_core_kernel(q_ref, kt_ref, v_ref, g_ref, cd_ref, y_ref, state_ref, *, heads):
    @pl.when(pl.program_id(1) == 0)
    def _():
        state_ref[...] = jnp.zeros_like(state_ref)

    cc = q_ref.shape[0]
    dk = q_ref.shape[1] // heads
    dv = v_ref.shape[1] // heads
    causal = lax.broadcasted_iota(jnp.int32, (cc, cc), 0) >= lax.broadcasted_iota(jnp.int32, (cc, cc), 1)
    for h in range(heads):
        q = q_ref[:, h * dk:(h + 1) * dk]
        kt = kt_ref[h]
        v = v_ref[:, h * dv:(h + 1) * dv]
        s = jnp.where(causal, jnp.dot(q, kt, preferred_element_type=F32), 0.0)
        u = state_ref[h]
        o = (jnp.dot(s.astype(BF16), v, preferred_element_type=F32)
             + jnp.dot(q, u.astype(BF16), preferred_element_type=F32))
        state_ref[h] = cd_ref[h] * (u + jnp.dot(kt, v, preferred_element_type=F32))
        o = o * _rms_scale(o)
        y_ref[:, h * dv:(h + 1) * dv] = (_silu(g_ref[:, h * dv:(h + 1) * dv].astype(F32)) * o).astype(BF16)


def _ret_core(q, kt, v, gate, batch, seq):
    T = q.shape[0]
    heads = RET_HEADS
    dk = q.shape[1] // heads
    dv = v.shape[1] // heads
    cc = RET_KERNEL_CHUNK
    nc = seq // cc
    chunk_decay = jnp.broadcast_to(jnp.exp(_ret_log_gamma(heads) * cc)[:, None, None], (heads, 1, dv))
    row = lambda b, c: (b * nc + c, 0)
    return pl.pallas_call(
        functools.partial(_ret_core_kernel, heads=heads),
        grid=(batch, nc),
        in_specs=[
            pl.BlockSpec((cc, heads * dk), row),
            pl.BlockSpec((heads, None, dk, cc), lambda b, c: (0, b * nc + c, 0, 0)),
            pl.BlockSpec((cc, heads * dv), row),
            pl.BlockSpec((cc, heads * dv), row),
            pl.BlockSpec((heads, 1, dv), lambda b, c: (0, 0, 0)),
        ],
        out_specs=pl.BlockSpec((cc, heads * dv), row),
        out_shape=jax.ShapeDtypeStruct((T, heads * dv), BF16),
        scratch_shapes=[pltpu.VMEM((heads, dk, dv), F32)],
        compiler_params=pltpu.CompilerParams(dimension_semantics=("arbitrary", "arbitrary")),
        name="ret_core",
    )(q, kt, v, gate, chunk_decay)


def _proj_router_kernel(y_ref, w_ref, r_ref, g_ref, wr_ref, br_ref, haug_ref, route_ref, counts_ref, run_ref,
                        *, groups, experts, d_model):
    @pl.when(pl.program_id(0) == 0)
    def _():
        run_ref[...] = jnp.zeros_like(run_ref)

    h = r_ref[...] + jnp.dot(y_ref[...], w_ref[...], preferred_element_type=F32)
    xn = h * _rms_scale(h) * g_ref[...]
    xh = xn.astype(BF16)
    xl = (xn - xh.astype(F32)).astype(BF16)
    hi_both = jnp.dot(xh, wr_ref[...], preferred_element_type=F32)
    lo_hi = jnp.dot(xl, wr_ref[:, :LANES], preferred_element_type=F32)
    logits = hi_both[:, :LANES] + (hi_both[:, LANES:] + lo_hi) + br_ref[...]
    lt = logits.T[:ROUTE_ROWS]
    tm = lt.shape[1]
    row = lax.broadcasted_iota(jnp.int32, lt.shape, 0)
    neg = -jnp.inf

    def argmax_first(vals):
        m = jnp.max(vals, axis=0, keepdims=True)
        first = jnp.min(jnp.where(vals == m, row, ROUTE_ROWS), axis=0, keepdims=True)
        return m, first

    gl = jnp.where(row < groups, lt, neg)
    gmax, gidx = argmax_first(gl)
    gsum = jnp.sum(jnp.where(row < groups, jnp.exp(gl - gmax), 0.0), axis=0, keepdims=True)
    g_val = 1.0 / gsum
    e0 = groups + experts * gidx
    el = jnp.where((row >= e0) & (row < e0 + experts), lt, neg)
    m1, i1 = argmax_first(el)
    m2, i2 = argmax_first(jnp.where(row == i1, neg, el))
    t = jnp.exp(m2 - m1)
    w1 = 1.0 / (1.0 + t)
    w2 = t / (1.0 + t)
    first_is_lo = i1 < i2
    lo = jnp.minimum(i1, i2) - e0
    hi = jnp.maximum(i1, i2) - e0
    w_lo = g_val * jnp.where(first_is_lo, w1, w2)
    w_hi = g_val * jnp.where(first_is_lo, w2, w1)
    lex = (lo * (7 - lo)) // 2 + (hi - lo - 1)
    pair = jnp.where(lex == 2, 4, jnp.where(lex == 3, 2, jnp.where(lex == 4, 3, lex)))
    cls = gidx * N_PAIRS + pair

    onehot = row == cls
    oh = jnp.where(onehot, 1.0, 0.0)
    before = lax.broadcasted_iota(jnp.int32, (tm, tm), 0) < lax.broadcasted_iota(jnp.int32, (tm, tm), 1)
    prefix = jnp.dot(oh.astype(BF16), jnp.where(before, 1.0, 0.0).astype(BF16), preferred_element_type=F32)
    rank = jnp.sum(jnp.where(onehot, prefix + run_ref[:, 0:1], 0.0), axis=0, keepdims=True)
    run_new = run_ref[...] + jnp.sum(oh, axis=1, keepdims=True)
    run_ref[...] = run_new
    counts_ref[...] = run_new

    r8 = lax.broadcasted_iota(jnp.int32, route_ref.shape, 0)
    route_ref[...] = jnp.where(r8 == 0, cls, jnp.where(r8 == 1, rank.astype(jnp.int32), 0))
    rl = lax.broadcasted_iota(jnp.int32, (LANES, tm), 0)
    meta_t = jnp.where(rl == 0, w_lo, jnp.where(rl == 1, w_hi, 0.0))
    haug_ref[:, :d_model] = h
    haug_ref[:, d_model:] = meta_t.T


def _proj_router(y, w_bf16, resid, norm_g, w_rg, b_rg, w_re, b_re, name, tm=512):
    T, K = y.shape
    D = w_bf16.shape[1]
    G, E = N_GROUPS, EXPERTS_PER_GROUP
    n_used = G + G * E
    wr = jnp.concatenate([w_rg, jnp.transpose(w_re, (1, 0, 2)).reshape(D, G * E)], axis=1)
    wr = jnp.pad(wr, ((0, 0), (0, LANES - n_used)))
    wr_hi = wr.astype(BF16)
    wr = jnp.concatenate([wr_hi, (wr - wr_hi.astype(F32)).astype(BF16)], axis=1)
    br = jnp.pad(jnp.concatenate([b_rg, b_re.reshape(G * E)]), (0, LANES - n_used))[None, :]
    est = 2 * K * D * 2 + 2 * tm * K * 2 + 6 * tm * D * 4 + 2 * D * LANES * 4 + 4 * tm * tm * 4
    const = lambda i: (0, 0)
    return pl.pallas_call(
        functools.partial(_proj_router_kernel, groups=G, experts=E, d_model=D),
        grid=(T // tm,),
        in_specs=[
            pl.BlockSpec((tm, K), lambda i: (i, 0)),
            pl.BlockSpec((K, D), const),
            pl.BlockSpec((tm, D), lambda i: (i, 0)),
            pl.BlockSpec((1, D), const),
            pl.BlockSpec((D, 2 * LANES), const),
            pl.BlockSpec((1, LANES), const),
        ],
        out_specs=[
            pl.BlockSpec((tm, D + LANES), lambda i: (i, 0)),
            pl.BlockSpec((8, tm), lambda i: (0, i)),
            pl.BlockSpec((ROUTE_ROWS, LANES), const),
        ],
        out_shape=[
            jax.ShapeDtypeStruct((T, D + LANES), F32),
            jax.ShapeDtypeStruct((8, T), jnp.int32),
            jax.ShapeDtypeStruct((ROUTE_ROWS, LANES), F32),
        ],
        scratch_shapes=[pltpu.VMEM((ROUTE_ROWS, LANES), F32)],
        compiler_params=pltpu.CompilerParams(
            dimension_semantics=("arbitrary",), vmem_limit_bytes=_vmem_limit(est)),
        name=name,
    )(y, w_bf16, resid, norm_g, wr, br)


def _tile_schedule(counts_blk, tm, n_tiles):
    counts = counts_blk[:N_CLASSES, 0].astype(jnp.int32)
    tiles_per = (counts + tm - 1) // tm
    tile_end = jnp.cumsum(tiles_per)
    tile_begin = tile_end - tiles_per
    cstart = jnp.cumsum(counts) - counts
    t = jnp.arange(n_tiles, dtype=jnp.int32)
    live = t < tile_end[-1]
    tq = jnp.where(live, t, tile_end[-1] - 1)
    tcls = jnp.sum((tile_end[None, :] <= tq[:, None]).astype(jnp.int32), axis=1)
    onehot = tcls[:, None] == jnp.arange(N_CLASSES, dtype=jnp.int32)[None, :]
    pick = lambda table: jnp.sum(jnp.where(onehot, jnp.asarray(table, jnp.int32)[None, :], 0), axis=1)
    k = tq - pick(tile_begin)
    nvalid = jnp.where(live, jnp.clip(pick(counts) - k * tm, 0, tm), 0).astype(jnp.int32)
    base = jnp.where(live, pick(cstart) + k * tm, 0).astype(jnp.int32)
    tile_g = pick([c // N_PAIRS for c in range(N_CLASSES)])
    tile_lo = pick([PAIR_LO[c % N_PAIRS] for c in range(N_CLASSES)])
    tile_hi = pick([PAIR_HI[c % N_PAIRS] for c in range(N_CLASSES)])
    cstart_pad = jnp.pad(cstart, (0, ROUTE_ROWS - N_CLASSES)).astype(jnp.int32)
    return tile_g, tile_lo, tile_hi, nvalid, base, cstart_pad


def _moe_kernel(tg_ref, tlo_ref, thi_ref, tnv_ref, tbase_ref, cs_ref,
                h_hbm, route_ref, gn_ref, fn_ref, wg_lo, wu_lo, wd_lo, wg_hi, wu_hi, wd_hi,
                out_hbm, rt_ref, pos_smem, pos_vmem, x0, x1, x2, o0, o1, o2, w_in_bf, w_out_bf,
                gsem, ssem, psem, *, tm, d_model, n_tok, final_norm):
    i = pl.program_id(0)
    xbufs = (x0, x1, x2)
    obufs = (o0, o1, o2)
    depth = len(xbufs)

    def issue_gather(t, sl):
        base = tbase_ref[t]
        for r in range(tm):
            tok = rt_ref[base + r]
            pltpu.make_async_copy(h_hbm.at[pl.ds(tok, 1)], xbufs[sl].at[pl.ds(r, 1)], gsem.at[sl]).start()

    def scatter_row(sl, r, tok):
        return pltpu.make_async_copy(obufs[sl].at[pl.ds(r, 1)], out_hbm.at[pl.ds(tok, 1)], ssem.at[sl])

    def issue_scatter(sl, base, n_valid):
        for r in range(tm):
            @pl.when(r < n_valid)
            def _():
                scatter_row(sl, r, rt_ref[base + r]).start()

    def wait_gather(sl):
        pltpu.make_async_copy(h_hbm.at[pl.ds(0, tm)], xbufs[sl], gsem.at[sl]).wait()

    def wait_scatter(sl, n_valid):
        aligned = pl.multiple_of((n_valid // SUBLANES) * SUBLANES, SUBLANES)

        @pl.when(aligned > 0)
        def _():
            rows = pl.ds(0, aligned)
            pltpu.make_async_copy(obufs[sl].at[rows], out_hbm.at[rows], ssem.at[sl]).wait()

        def single(r, c):
            scatter_row(sl, r, 0).wait()
            return c
        lax.fori_loop(aligned, n_valid, single, 0)

    def live(t):
        return tnv_ref[jnp.maximum(t, 0)] > 0

    @pl.when(i == 0)
    def _():
        cls2 = route_ref[0]
        pos = route_ref[1]
        for c in range(N_CLASSES):
            pos = pos + jnp.where(cls2 == c, cs_ref[c], 0)
        pos_vmem[...] = pos
        to_smem = pltpu.make_async_copy(pos_vmem, pos_smem, psem)
        to_smem.start()
        to_smem.wait()

        def place(row, c):
            for col in range(LANES):
                rt_ref[pos_smem[row, col]] = row * LANES + col
            return c
        lax.fori_loop(0, n_tok // LANES, place, 0)
        for r in range(tm):
            rt_ref[n_tok + r] = 0
        issue_gather(0, 0)
        issue_gather(1, 1)

    prev = jnp.maximum(i - 1, 0)
    group_changed = (i == 0) | (tg_ref[i] != tg_ref[prev])
    for which, t_ref, (wg, wu, wd) in ((0, tlo_ref, (wg_lo, wu_lo, wd_lo)), (1, thi_ref, (wg_hi, wu_hi, wd_hi))):
        @pl.when(group_changed | (t_ref[i] != t_ref[prev]))
        def _():
            w_in_bf[2 * which] = wg[...].astype(BF16)
            w_in_bf[2 * which + 1] = wu[...].astype(BF16)
            w_out_bf[which] = wd[...].astype(BF16)

    def tile(cur):
        far = (cur + depth - 1) % depth
        @pl.when((i <= 1) | live(i - 2))
        def _():
            wait_gather(cur)

        @pl.when((i >= depth) & live(i - depth))
        def _():
            wait_scatter(cur, tnv_ref[jnp.maximum(i - depth, 0)])

        prev_base = tbase_ref[prev]
        prev_valid = jnp.where(i == 0, 0, tnv_ref[prev])

        @pl.when(live(i))
        def _():
            issue_scatter(far, prev_base, prev_valid)
            issue_gather(i + 2, far)
            xa = xbufs[cur][...]
            x = xa[:, :d_model]
            xn = (x * _rms_scale(x) * gn_ref[...]).astype(BF16)

            def expert(which, w):
                a = jnp.dot(xn, w_in_bf[2 * which], preferred_element_type=F32)
                b = jnp.dot(xn, w_in_bf[2 * which + 1], preferred_element_type=F32)
                return jnp.dot((_silu(a) * b * w).astype(BF16), w_out_bf[which], preferred_element_type=F32)

            y = expert(0, xa[:, d_model:d_model + 1]) + expert(1, xa[:, d_model + 1:d_model + 2])
            obufs[cur][...] = x + y
            if final_norm:
                for r0 in range(0, tm, 4 * SUBLANES):
                    rows = slice(r0, r0 + 4 * SUBLANES)
                    o = obufs[cur][rows, :]
                    obufs[cur][rows, :] = o * _rms_scale(o) * fn_ref[...]

        @pl.when(jnp.logical_not(live(i)) & (i >= 1) & live(i - 1))
        def _():
            issue_scatter(far, prev_base, prev_valid)

    for residue in range(depth):
        @pl.when(i % depth == residue)
        def _():
            tile(residue)


def _moe_layer(h_aug, route, counts_blk, ffn_norm_g, final_norm_g, wg, wu, wd, layer, final_norm):
    T, DA = h_aug.shape
    D = DA - LANES
    F = wg.shape[-1]
    tm = MOE_TILE
    assert T % (LANES * SUBLANES) == 0 and T >= 2 * tm
    n_tiles = T // tm + N_CLASSES + 3
    tile_g, tile_lo, tile_hi, nvalid, base, cstart = _tile_schedule(counts_blk, tm, n_tiles)
    route2d = route[:2].reshape(2, T // LANES, LANES)

    def w_spec(shape, which):
        def index(i, tg, tlo, thi, *_):
            return (layer, tg[i], (tlo if which == 0 else thi)[i], 0, 0)
        return pl.BlockSpec((None, None, None) + shape, index)

    est = 2 * 6 * D * F * 4 + 6 * D * F * 2 + 3 * tm * DA * 4 + 3 * tm * D * 4 + 8 * tm * F * 4
    return pl.pallas_call(
        functools.partial(_moe_kernel, tm=tm, d_model=D, n_tok=T, final_norm=final_norm),
        grid_spec=pltpu.PrefetchScalarGridSpec(
            num_scalar_prefetch=6,
            grid=(n_tiles,),
            in_specs=[
                pl.BlockSpec(memory_space=pl.ANY),
                pl.BlockSpec((2, T // LANES, LANES), lambda i, *_: (0, 0, 0)),
                pl.BlockSpec((1, D), lambda i, *_: (0, 0)),
                pl.BlockSpec((1, D), lambda i, *_: (0, 0)),
                w_spec((D, F), 0), w_spec((D, F), 0), w_spec((F, D), 0),
                w_spec((D, F), 1), w_spec((D, F), 1), w_spec((F, D), 1),
            ],
            out_specs=pl.BlockSpec(memory_space=pl.ANY),
            scratch_shapes=[
                pltpu.SMEM((T + tm,), jnp.int32),
                pltpu.SMEM((T // LANES, LANES), jnp.int32),
                pltpu.VMEM((T // LANES, LANES), jnp.int32),
                pltpu.VMEM((tm, DA), F32), pltpu.VMEM((tm, DA), F32), pltpu.VMEM((tm, DA), F32),
                pltpu.VMEM((tm, D), F32), pltpu.VMEM((tm, D), F32), pltpu.VMEM((tm, D), F32),
                pltpu.VMEM((4, D, F), BF16),
                pltpu.VMEM((2, F, D), BF16),
                pltpu.SemaphoreType.DMA((3,)),
                pltpu.SemaphoreType.DMA((3,)),
                pltpu.SemaphoreType.DMA(()),
            ],
        ),
        out_shape=jax.ShapeDtypeStruct((T, D), F32),
        compiler_params=pltpu.CompilerParams(
            dimension_semantics=("arbitrary",), vmem_limit_bytes=_vmem_limit(est)),
        name="moe_experts",
    )(tile_g, tile_lo, tile_hi, nvalid, base, cstart,
      h_aug, route2d, ffn_norm_g, final_norm_g, wg, wu, wd, wg, wu, wd)


def _partial_rotary(xh, cosf, sinp, sinm, half):
    lanes = xh.shape[-1]
    return xh * cosf + pltpu.roll(xh, half, 1) * sinp + pltpu.roll(xh, lanes - half, 1) * sinm


def _tile_cos_sin(cb_ref, sb_ref, cr_ref, sr_ref):
    cb, sb, cr, sr = cb_ref[...], sb_ref[...], cr_ref[...], sr_ref[...]
    return cb * cr - sb * sr, sb * cr + cb * sr


def _kvq_kernel(h_ref, gkv_ref, gq_ref, wkv_ref, wq_ref, cb_ref, sb_ref, cr_ref, sr_ref,
                k_ref, vt_ref, qt_ref, km_ref, *, heads, dh, half):
    x = h_ref[...]
    xr = x * _rms_scale(x)
    xkv = (xr * gkv_ref[...]).astype(BF16)
    xq = (xr * gq_ref[...]).astype(BF16)
    cosf, sin_all = _tile_cos_sin(cb_ref, sb_ref, cr_ref, sr_ref)
    lane = lax.broadcasted_iota(jnp.int32, sin_all.shape, 1)
    sinp = jnp.where(lane >= half, sin_all, 0.0)
    sinm = jnp.where(lane < half, -sin_all, 0.0)
    width = heads * dh
    q_scale = dh ** -0.5 * LOG2_E
    pair_w = 2 * dh
    for c0 in range(0, width, pair_w):
        kk = jnp.dot(xkv, wkv_ref[:, c0:c0 + pair_w], preferred_element_type=F32)
        vv = jnp.dot(xkv, wkv_ref[:, width + c0:width + c0 + pair_w], preferred_element_type=F32)
        qq = jnp.dot(xq, wq_ref[:, c0:c0 + pair_w], preferred_element_type=F32)
        for u in range(2):
            hh = c0 // dh + u
            cols = slice(hh * dh, (hh + 1) * dh)
            kh = _partial_rotary(kk[:, u * dh:(u + 1) * dh], cosf, sinp, sinm, half)
            k_ref[hh] = kh.astype(BF16)
            km_ref[:, cols] = jnp.mean(kh, axis=0, keepdims=True)
            vt_ref[hh, :dh] = vv[:, u * dh:(u + 1) * dh].astype(BF16).T
            vt_ref[hh, dh:] = jnp.ones((VT_ONES_ROWS, vv.shape[0]), BF16)
            qh = _partial_rotary(qq[:, u * dh:(u + 1) * dh], cosf, sinp, sinm, half)
            qt_ref[hh] = (qh * q_scale).astype(BF16).T


def _angle_tables(inv_freq, seq, tm):
    base = jnp.arange(0, seq, tm, dtype=F32)[:, None] * inv_freq[None, :]
    offs = jnp.arange(tm, dtype=F32)[:, None] * inv_freq[None, :]
    return jnp.cos(base)[:, None, :], jnp.sin(base)[:, None, :], jnp.cos(offs), jnp.sin(offs)


def _rope_tables(seq, dh, tm):
    rot = dh // ROPE_FRACTION
    half = rot // 2
    inv = 1.0 / (ROPE_THETA ** (jnp.arange(half, dtype=F32) / half))
    inv_lanes = jnp.concatenate([inv, inv, jnp.zeros((dh - rot,), F32)])
    return _angle_tables(inv_lanes, seq, tm) + (half,)


def _kvq_proj(h, kv_norm_g, q_norm_g, wkv_bf16, wq_bf16, seq):
    T, D = h.shape
    heads = ATT_HEADS
    dh = D // heads
    tm = MOBA_BLOCK
    n_pos_tiles = seq // tm
    cos_b, sin_b, cos_r, sin_r, half = _rope_tables(seq, dh, tm)
    width = heads * dh
    est = 2 * (D * 2 * width + D * width) * 2 + 2 * tm * D * 4 + 6 * tm * width * 2 + 8 * tm * dh * 4
    pos = lambda i: (i % n_pos_tiles, 0, 0)
    const = lambda i: (0, 0)
    per_block = lambda i: (0, i, 0, 0)
    return pl.pallas_call(
        functools.partial(_kvq_kernel, heads=heads, dh=dh, half=half),
        grid=(T // tm,),
        in_specs=[
            pl.BlockSpec((tm, D), lambda i: (i, 0)),
            pl.BlockSpec((1, D), const),
            pl.BlockSpec((1, D), const),
            pl.BlockSpec((D, 2 * width), const),
            pl.BlockSpec((D, width), const),
            pl.BlockSpec((None, 1, dh), pos),
            pl.BlockSpec((None, 1, dh), pos),
            pl.BlockSpec((tm, dh), const),
            pl.BlockSpec((tm, dh), const),
        ],
        out_specs=[
            pl.BlockSpec((heads, None, tm, dh), per_block),
            pl.BlockSpec((heads, None, dh + VT_ONES_ROWS, tm), per_block),
            pl.BlockSpec((heads, None, dh, tm), per_block),
            pl.BlockSpec((None, 1, width), lambda i: (i, 0, 0)),
        ],
        out_shape=[
            jax.ShapeDtypeStruct((heads, T // tm, tm, dh), BF16),
            jax.ShapeDtypeStruct((heads, T // tm, dh + VT_ONES_ROWS, tm), BF16),
            jax.ShapeDtypeStruct((heads, T // tm, dh, tm), BF16),
            jax.ShapeDtypeStruct((T // tm, 1, width), F32),
        ],
        compiler_params=pltpu.CompilerParams(
            dimension_semantics=("arbitrary",), vmem_limit_bytes=_vmem_limit(est)),
        name="kvq_proj",
    )(h, kv_norm_g, q_norm_g, wkv_bf16, wq_bf16, cos_b, sin_b, cos_r, sin_r)


def _moba_kernel(qt_ref, k_ref, vt_ref, km_ref, o_ref, sel_ref, acc_ref, sa_ref, sb_ref, pa_ref, pb_ref,
                 *, topk, group, dh):
    qi = pl.program_id(2)
    n_blk = km_ref.shape[0]
    blk = sa_ref.shape[1]
    neg = -jnp.inf
    chunks = [slice(c, c + MOBA_KEY_CHUNK) for c in range(0, blk, MOBA_KEY_CHUNK)]

    def scores(g, j):
        return jnp.dot(k_ref[g, j], qt_ref[g], preferred_element_type=F32)

    def stage_scores(s_buf, j):
        jc = jnp.minimum(j, n_blk - 1)
        for g in range(group):
            s_buf[g] = scores(g, jc)

    def stage_softmax(s_buf, p_buf, ms, j):
        new_m, alphas = [], []
        for g in range(group):
            part = s_buf[g, chunks[0], :]
            for c in chunks[1:]:
                part = jnp.maximum(part, s_buf[g, c, :])
            m_blk = jnp.max(part, axis=0, keepdims=True)
            if j is None:
                m_new = shift = m_blk
            else:
                bias = sel_ref[g, pl.ds(jnp.minimum(j, n_blk - 1), 1), :]
                m_new = jnp.maximum(ms[g], m_blk + bias)
                shift = m_new - bias
            for c in chunks:
                p_buf[g, c, :] = jnp.exp2(s_buf[g, c, :] - shift).astype(BF16)
            new_m.append(m_new)
            alphas.append(None if j is None else jnp.exp2(ms[g] - m_new))
        return tuple(new_m), alphas

    def stage_pv(p_buf, j, alphas):
        jc = jnp.minimum(j, n_blk - 1)
        for g in range(group):
            pv = jnp.dot(vt_ref[g, jc], p_buf[g], preferred_element_type=F32)
            acc_ref[g] = pv if alphas[g] is None else alphas[g] * acc_ref[g] + pv

    for g in range(group):
        km = km_ref[:, g * dh:(g + 1) * dh]
        km_hi = km.astype(BF16)
        km_lo = (km - km_hi.astype(F32)).astype(BF16)
        gate = (jnp.dot(km_hi, qt_ref[g], preferred_element_type=F32)
                + jnp.dot(km_lo, qt_ref[g], preferred_element_type=F32))
        blk_id = lax.broadcasted_iota(jnp.int32, gate.shape, 0)
        gv = jnp.where(blk_id < qi, gate, neg)
        bias = jnp.full(gate.shape, neg, F32)
        for _ in range(topk):
            m = jnp.max(gv, axis=0, keepdims=True)
            first = jnp.min(jnp.where(gv == m, blk_id, n_blk), axis=0, keepdims=True)
            bias = jnp.where((blk_id == first) & (m > neg), 0.0, bias)
            gv = jnp.where(blk_id == first, neg, gv)
        sel_ref[g] = bias

    key_id = lax.broadcasted_iota(jnp.int32, (blk, blk), 0)
    qry_id = lax.broadcasted_iota(jnp.int32, (blk, blk), 1)
    for g in range(group):
        sb_ref[g] = jnp.where(key_id <= qry_id, scores(g, qi), neg)
    stage_scores(sa_ref, 0)
    ms, alphas = stage_softmax(sb_ref, pb_ref, None, None)
    stage_pv(pb_ref, qi, alphas)

    def body(t, ms):
        j = 2 * t
        ms, alphas = stage_softmax(sa_ref, pa_ref, ms, j)
        stage_scores(sb_ref, j + 1)
        stage_pv(pa_ref, j, alphas)
        ms, alphas = stage_softmax(sb_ref, pb_ref, ms, j + 1)
        stage_scores(sa_ref, j + 2)
        stage_pv(pb_ref, j + 1, alphas)
        return ms

    lax.fori_loop(0, (qi + 1) // 2, body, ms)
    for g in range(group):
        acc = acc_ref[g]
        o_ref[:, g * dh:(g + 1) * dh] = (acc[:dh] / acc[dh:dh + 1]).astype(BF16).T


def _moba_attention(qt, k, vt, k_mean, batch, seq):
    heads, n_tiles, dh, blk = qt.shape
    dv_rows = vt.shape[2]
    n_blk = seq // blk
    topk = min(MOBA_TOPK, n_blk)
    width = heads * dh
    group = MOBA_HEAD_GROUP
    km = k_mean.reshape(batch, n_blk, width)
    per_seq = lambda b, h, i: (h, b, 0, 0)
    est = 2 * group * seq * (dh + dv_rows) * 2 + 6 * group * blk * blk * 4 + 4 * group * blk * dh * 4
    return pl.pallas_call(
        functools.partial(_moba_kernel, topk=topk, group=group, dh=dh),
        grid=(batch, heads // group, n_blk),
        in_specs=[
            pl.BlockSpec((group, None, dh, blk), lambda b, h, i: (h, b * n_blk + i, 0, 0)),
            pl.BlockSpec((group, n_blk, blk, dh), per_seq),
            pl.BlockSpec((group, n_blk, dv_rows, blk), per_seq),
            pl.BlockSpec((None, n_blk, group * dh), lambda b, h, i: (b, 0, h)),
        ],
        out_specs=pl.BlockSpec((blk, group * dh), lambda b, h, i: (b * n_blk + i, h)),
        out_shape=jax.ShapeDtypeStruct((n_tiles * blk, width), BF16),
        scratch_shapes=[pltpu.VMEM((group, n_blk, blk), F32), pltpu.VMEM((group, dv_rows, blk), F32),
                        pltpu.VMEM((group, blk, blk), F32), pltpu.VMEM((group, blk, blk), F32),
                        pltpu.VMEM((group, blk, blk), BF16), pltpu.VMEM((group, blk, blk), BF16)],
        compiler_params=pltpu.CompilerParams(
            dimension_semantics=("arbitrary", "arbitrary", "arbitrary"), vmem_limit_bytes=_vmem_limit(est)),
        name="moba_attention",
    )(qt, k, vt, km)


def kernel(x, ret_norm, ret_w_in, ret_w_out, kv_norm, w_kv, attn_norm, w_q, w_o, ffn_norm, router_group_w, router_group_b, router_expert_w, router_expert_b, expert_w_gate, expert_w_up, expert_w_down, final_norm):
    B, S, D = x.shape
    T = B * S
    assert S % MOBA_BLOCK == 0 and S % RET_KERNEL_CHUNK == 0 and T % MOE_TILE == 0
    assert ret_norm.shape[0] == 1 and attn_norm.shape[0] == 1 and ffn_norm.shape[0] == 2
    h = x.reshape(T, D)
    final_g = final_norm[None, :]

    def proj_router(y, w, resid, layer, name):
        return _proj_router(y, w.astype(BF16), resid, ffn_norm[layer][None, :], router_group_w[layer],
                            router_group_b[layer], router_expert_w[layer], router_expert_b[layer], name)

    def moe(h_aug, route, counts, layer, last):
        return _moe_layer(h_aug, route, counts, ffn_norm[layer][None, :], final_g,
                          expert_w_gate, expert_w_up, expert_w_down, layer, final_norm=last)

    q, kt, v, gate = _ret_in_proj(h, ret_norm[0][None, :], ret_w_in[0].astype(BF16), S)
    y = _ret_core(q, kt, v, gate, B, S)
    h = moe(*proj_router(y, ret_w_out[0], h, 0, "ret_out_proj_router"), 0, False)

    k2, vt2, qt2, k_mean = _kvq_proj(h, kv_norm[None, :], attn_norm[0][None, :],
                                     w_kv.astype(BF16), w_q[0].astype(BF16), S)
    o = _moba_attention(qt2, k2, vt2, k_mean, B, S)
    h = moe(*proj_router(o, w_o[0], h, 1, "attn_out_proj_router"), 1, True)
    return h.reshape(B, S, D)
```

```python
import functools

import jax
import jax.numpy as jnp
from jax import lax
from jax.experimental import pallas as pl
from jax.experimental.pallas import tpu as pltpu

F32 = jnp.float32
BF16 = jnp.bfloat16

NORM_EPS = 1e-6
RET_HEADS = 4
RET_ROT_BASE = 10000.0
ATT_HEADS = 8
ROPE_FRACTION = 4
ROPE_THETA = 500000.0
MOBA_BLOCK = 256
MOBA_TOPK = 3
N_GROUPS = 4
EXPERTS_PER_GROUP = 4
PAIR_LO = (0, 0, 1, 1, 0, 2)
PAIR_HI = (1, 2, 2, 3, 3, 3)
N_PAIRS = len(PAIR_LO)
N_CLASSES = N_GROUPS * N_PAIRS

LANES = 128
SUBLANES = 8
RET_KERNEL_CHUNK = 256
MOE_TILE = 256
ROUTE_ROWS = 32
MOBA_HEAD_GROUP = 4
MOBA_KEY_CHUNK = 32
VT_ONES_ROWS = 16
LOG2_E = 1.4426950408889634
V7X_VMEM_BYTES = 64 * 1024 * 1024


def _vmem_limit(estimate_bytes):
    return int(min(V7X_VMEM_BYTES - 8 * 1024 * 1024, max(32 * 1024 * 1024, estimate_bytes * 5 // 4)))


def _rms_scale(x):
    return lax.rsqrt(jnp.mean(x * x, axis=-1, keepdims=True) + NORM_EPS)


def _silu(a):
    return a * jax.nn.sigmoid(a)


def _ret_in_kernel(x_ref, g_ref, w_ref, cb_ref, sb_ref, cr_ref, sr_ref, qdec_ref, kdec_ref,
                   q_ref, kt_ref, v_ref, gate_ref, *, heads, dk, dv):
    x = x_ref[...]
    xn = (x * _rms_scale(x) * g_ref[...]).astype(BF16)
    cos, sin = _tile_cos_sin(cb_ref, sb_ref, cr_ref, sr_ref)
    half = dk // 2

    def rotated(col0, h, dec):
        p = jnp.dot(xn, w_ref[:, col0 + h * dk:col0 + (h + 1) * dk], preferred_element_type=F32)
        x1 = p[:, :half]
        x2 = p[:, half:]
        cos_h = cos * dec
        sin_h = sin * dec
        return x1 * cos_h - x2 * sin_h, x2 * cos_h + x1 * sin_h

    for h in range(heads):
        lo, hi = rotated(0, h, qdec_ref[h])
        q_ref[:, h * dk:h * dk + half] = lo.astype(BF16)
        q_ref[:, h * dk + half:(h + 1) * dk] = hi.astype(BF16)
        lo, hi = rotated(heads * dk, h, kdec_ref[h])
        kt_ref[h, :half] = lo.astype(BF16).T
        kt_ref[h, half:] = hi.astype(BF16).T
    v0 = 2 * heads * dk
    g0 = v0 + heads * dv
    for h in range(heads):
        v_ref[:, h * dv:(h + 1) * dv] = jnp.dot(
            xn, w_ref[:, v0 + h * dv:v0 + (h + 1) * dv], preferred_element_type=F32).astype(BF16)
        gate_ref[:, h * dv:(h + 1) * dv] = jnp.dot(
            xn, w_ref[:, g0 + h * dv:g0 + (h + 1) * dv], preferred_element_type=F32).astype(BF16)


def _ret_in_proj(h, norm_g, w_bf16, seq, tm=256):
    T, D = h.shape
    heads = RET_HEADS
    dk = D // heads
    dv = 2 * dk
    n_cols = w_bf16.shape[1]
    n_pos_tiles = seq // tm
    half = dk // 2
    inv = 1.0 / (RET_ROT_BASE ** (jnp.arange(half, dtype=F32) / half))
    cos_b, sin_b, cos_r, sin_r = _angle_tables(inv, seq, tm)
    assert tm == RET_KERNEL_CHUNK
    log_gamma = _ret_log_gamma(heads)
    idx = jnp.arange(tm, dtype=F32)
    q_dec = jnp.broadcast_to(jnp.exp(log_gamma[:, None] * idx)[:, :, None], (heads, tm, half))
    k_dec = jnp.broadcast_to((jnp.exp(-log_gamma[:, None] * idx) * dk ** -0.5)[:, :, None], (heads, tm, half))
    pos = lambda i: (i % n_pos_tiles, 0, 0)
    est = 2 * D * n_cols * 2 + 2 * tm * D * 4 + 2 * tm * n_cols * 2 + (4 + 4 * heads) * tm * LANES * 4
    return pl.pallas_call(
        functools.partial(_ret_in_kernel, heads=heads, dk=dk, dv=dv),
        grid=(T // tm,),
        in_specs=[
            pl.BlockSpec((tm, D), lambda i: (i, 0)),
            pl.BlockSpec((1, D), lambda i: (0, 0)),
            pl.BlockSpec((D, n_cols), lambda i: (0, 0)),
            pl.BlockSpec((None, 1, half), pos),
            pl.BlockSpec((None, 1, half), pos),
            pl.BlockSpec((tm, half), lambda i: (0, 0)),
            pl.BlockSpec((tm, half), lambda i: (0, 0)),
            pl.BlockSpec((heads, tm, half), lambda i: (0, 0, 0)),
            pl.BlockSpec((heads, tm, half), lambda i: (0, 0, 0)),
        ],
        out_specs=[
            pl.BlockSpec((tm, heads * dk), lambda i: (i, 0)),
            pl.BlockSpec((heads, None, dk, tm), lambda i: (0, i, 0, 0)),
            pl.BlockSpec((tm, heads * dv), lambda i: (i, 0)),
            pl.BlockSpec((tm, heads * dv), lambda i: (i, 0)),
        ],
        out_shape=[
            jax.ShapeDtypeStruct((T, heads * dk), BF16),
            jax.ShapeDtypeStruct((heads, T // tm, dk, tm), BF16),
            jax.ShapeDtypeStruct((T, heads * dv), BF16),
            jax.ShapeDtypeStruct((T, heads * dv), BF16),
        ],
        compiler_params=pltpu.CompilerParams(
            dimension_semantics=("arbitrary",), vmem_limit_bytes=_vmem_limit(est)),
        name="ret_in_proj",
    )(h, norm_g, w_bf16, cos_b, sin_b, cos_r, sin_r, q_dec, k_dec)


def _ret_log_gamma(heads):
    return jnp.log1p(-jnp.power(2.0, -5.0 - jnp.arange(heads, dtype=F32)))


def _ret_core_kernel(q_ref, kt_ref, v_ref, g_ref, cd_ref, y_ref, state_ref, *, heads):
    @pl.when(pl.program_id(1) == 0)
    def _():
        state_ref[...] = jnp.zeros_like(state_ref)

    cc = q_ref.shape[0]
    dk = q_ref.shape[1] // heads
    dv = v_ref.shape[1] // heads
    causal = lax.broadcasted_iota(jnp.int32, (cc, cc), 0) >= lax.broadcasted_iota(jnp.int32, (cc, cc), 1)
    for h in range(heads):
        q = q_ref[:, h * dk:(h + 1) * dk]
        kt = kt_ref[h]
        v = v_ref[:, h * dv:(h + 1) * dv]
        s = jnp.where(causal, jnp.dot(q, kt, preferred_element_type=F32), 0.0)
        u = state_ref[h]
        o = (jnp.dot(s.astype(BF16), v, preferred_element_type=F32)
             + jnp.dot(q, u.astype(BF16), preferred_element_type=F32))
        state_ref[h] = cd_ref[h] * (u + jnp.dot(kt, v, preferred_element_type=F32))
        o = o * _rms_scale(o)
        y_ref[:, h * dv:(h + 1) * dv] = (_silu(g_ref[:, h * dv:(h + 1) * dv].astype(F32)) * o).astype(BF16)


def _ret_core(q, kt, v, gate, batch, seq):
    T = q.shape[0]
    heads = RET_HEADS
    dk = q.shape[1] // heads
    dv = v.shape[1] // heads
    cc = RET_KERNEL_CHUNK
    nc = seq // cc
    chunk_decay = jnp.broadcast_to(jnp.exp(_ret_log_gamma(heads) * cc)[:, None, None], (heads, 1, dv))
    row = lambda b, c: (b * nc + c, 0)
    return pl.pallas_call(
        functools.partial(_ret_core_kernel, heads=heads),
        grid=(batch, nc),
        in_specs=[
            pl.BlockSpec((cc, heads * dk), row),
            pl.BlockSpec((heads, None, dk, cc), lambda b, c: (0, b * nc + c, 0, 0)),
            pl.BlockSpec((cc, heads * dv), row),
            pl.BlockSpec((cc, heads * dv), row),
            pl.BlockSpec((heads, 1, dv), lambda b, c: (0, 0, 0)),
        ],
        out_specs=pl.BlockSpec((cc, heads * dv), row),
        out_shape=jax.ShapeDtypeStruct((T, heads * dv), BF16),
        scratch_shapes=[pltpu.VMEM((heads, dk, dv), F32)],
        compiler_params=pltpu.CompilerParams(dimension_semantics=("arbitrary", "arbitrary")),
        name="ret_core",
    )(q, kt, v, gate, chunk_decay)


def _proj_router_kernel(y_ref, w_ref, r_ref, g_ref, wr_ref, br_ref, haug_ref, route_ref, counts_ref, run_ref,
                        *, groups, experts, d_model):
    @pl.when(pl.program_id(0) == 0)
    def _():
        run_ref[...] = jnp.zeros_like(run_ref)

    h = r_ref[...] + jnp.dot(y_ref[...], w_ref[...], preferred_element_type=F32)
    xn = h * _rms_scale(h) * g_ref[...]
    xh = xn.astype(BF16)
    xl = (xn - xh.astype(F32)).astype(BF16)
    hi_both = jnp.dot(xh, wr_ref[...], preferred_element_type=F32)
    lo_hi = jnp.dot(xl, wr_ref[:, :LANES], preferred_element_type=F32)
    logits = hi_both[:, :LANES] + (hi_both[:, LANES:] + lo_hi) + br_ref[...]
    lt = logits.T[:ROUTE_ROWS]
    tm = lt.shape[1]
    row = lax.broadcasted_iota(jnp.int32, lt.shape, 0)
    neg = -jnp.inf

    def argmax_first(vals):
        m = jnp.max(vals, axis=0, keepdims=True)
        first = jnp.min(jnp.where(vals == m, row, ROUTE_ROWS), axis=0, keepdims=True)
        return m, first

    gl = jnp.where(row < groups, lt, neg)
    gmax, gidx = argmax_first(gl)
    gsum = jnp.sum(jnp.where(row < groups, jnp.exp(gl - gmax), 0.0), axis=0, keepdims=True)
    g_val = 1.0 / gsum
    e0 = groups + experts * gidx
    el = jnp.where((row >= e0) & (row < e0 + experts), lt, neg)
    m1, i1 = argmax_first(el)
    m2, i2 = argmax_first(jnp.where(row == i1, neg, el))
    t = jnp.exp(m2 - m1)
    w1 = 1.0 / (1.0 + t)
    w2 = t / (1.0 + t)
    first_is_lo = i1 < i2
    lo = jnp.minimum(i1, i2) - e0
    hi = jnp.maximum(i1, i2) - e0
    w_lo = g_val * jnp.where(first_is_lo, w1, w2)
    w_hi = g_val * jnp.where(first_is_lo, w2, w1)
    lex = (lo * (7 - lo)) // 2 + (hi - lo - 1)
    pair = jnp.where(lex == 2, 4, jnp.where(lex == 3, 2, jnp.where(lex == 4, 3, lex)))
    cls = gidx * N_PAIRS + pair

    onehot = row == cls
    oh = jnp.where(onehot, 1.0, 0.0)
    before = lax.broadcasted_iota(jnp.int32, (tm, tm), 0) < lax.broadcasted_iota(jnp.int32, (tm, tm), 1)
    prefix = jnp.dot(oh.astype(BF16), jnp.where(before, 1.0, 0.0).astype(BF16), preferred_element_type=F32)
    rank = jnp.sum(jnp.where(onehot, prefix + run_ref[:, 0:1], 0.0), axis=0, keepdims=True)
    run_new = run_ref[...] + jnp.sum(oh, axis=1, keepdims=True)
    run_ref[...] = run_new
    counts_ref[...] = run_new

    r8 = lax.broadcasted_iota(jnp.int32, route_ref.shape, 0)
    route_ref[...] = jnp.where(r8 == 0, cls, jnp.where(r8 == 1, rank.astype(jnp.int32), 0))
    rl = lax.broadcasted_iota(jnp.int32, (LANES, tm), 0)
    meta_t = jnp.where(rl == 0, w_lo, jnp.where(rl == 1, w_hi, 0.0))
    haug_ref[:, :d_model] = h
    haug_ref[:, d_model:] = meta_t.T


def _proj_router(y, w_bf16, resid, norm_g, w_rg, b_rg, w_re, b_re, name, tm=512):
    T, K = y.shape
    D = w_bf16.shape[1]
    G, E = N_GROUPS, EXPERTS_PER_GROUP
    n_used = G + G * E
    wr = jnp.concatenate([w_rg, jnp.transpose(w_re, (1, 0, 2)).reshape(D, G * E)], axis=1)
    wr = jnp.pad(wr, ((0, 0), (0, LANES - n_used)))
    wr_hi = wr.astype(BF16)
    wr = jnp.concatenate([wr_hi, (wr - wr_hi.astype(F32)).astype(BF16)], axis=1)
    br = jnp.pad(jnp.concatenate([b_rg, b_re.reshape(G * E)]), (0, LANES - n_used))[None, :]
    est = 2 * K * D * 2 + 2 * tm * K * 2 + 6 * tm * D * 4 + 2 * D * LANES * 4 + 4 * tm * tm * 4
    const = lambda i: (0, 0)
    return pl.pallas_call(
        functools.partial(_proj_router_kernel, groups=G, experts=E, d_model=D),
        grid=(T // tm,),
        in_specs=[
            pl.BlockSpec((tm, K), lambda i: (i, 0)),
            pl.BlockSpec((K, D), const),
            pl.BlockSpec((tm, D), lambda i: (i, 0)),
            pl.BlockSpec((1, D), const),
            pl.BlockSpec((D, 2 * LANES), const),
            pl.BlockSpec((1, LANES), const),
        ],
        out_specs=[
            pl.BlockSpec((tm, D + LANES), lambda i: (i, 0)),
            pl.BlockSpec((8, tm), lambda i: (0, i)),
            pl.BlockSpec((ROUTE_ROWS, LANES), const),
        ],
        out_shape=[
            jax.ShapeDtypeStruct((T, D + LANES), F32),
            jax.ShapeDtypeStruct((8, T), jnp.int32),
            jax.ShapeDtypeStruct((ROUTE_ROWS, LANES), F32),
        ],
        scratch_shapes=[pltpu.VMEM((ROUTE_ROWS, LANES), F32)],
        compiler_params=pltpu.CompilerParams(
            dimension_semantics=("arbitrary",), vmem_limit_bytes=_vmem_limit(est)),
        name=name,
    )(y, w_bf16, resid, norm_g, wr, br)


def _tile_schedule(counts_blk, tm, n_tiles):
    counts = counts_blk[:N_CLASSES, 0].astype(jnp.int32)
    tiles_per = (counts + tm - 1) // tm
    tile_end = jnp.cumsum(tiles_per)
    tile_begin = tile_end - tiles_per
    cstart = jnp.cumsum(counts) - counts
    t = jnp.arange(n_tiles, dtype=jnp.int32)
    live = t < tile_end[-1]
    tq = jnp.where(live, t, tile_end[-1] - 1)
    tcls = jnp.sum((tile_end[None, :] <= tq[:, None]).astype(jnp.int32), axis=1)
    onehot = tcls[:, None] == jnp.arange(N_CLASSES, dtype=jnp.int32)[None, :]
    pick = lambda table: jnp.sum(jnp.where(onehot, jnp.asarray(table, jnp.int32)[None, :], 0), axis=1)
    k = tq - pick(tile_begin)
    nvalid = jnp.where(live, jnp.clip(pick(counts) - k * tm, 0, tm), 0).astype(jnp.int32)
    base = jnp.where(live, pick(cstart) + k * tm, 0).astype(jnp.int32)
    tile_g = pick([c // N_PAIRS for c in range(N_CLASSES)])
    tile_lo = pick([PAIR_LO[c % N_PAIRS] for c in range(N_CLASSES)])
    tile_hi = pick([PAIR_HI[c % N_PAIRS] for c in range(N_CLASSES)])
    cstart_pad = jnp.pad(cstart, (0, ROUTE_ROWS - N_CLASSES)).astype(jnp.int32)
    return tile_g, tile_lo, tile_hi, nvalid, base, cstart_pad


def _moe_kernel(tg_ref, tlo_ref, thi_ref, tnv_ref, tbase_ref, cs_ref,
                h_hbm, route_ref, gn_ref, fn_ref, wg_lo, wu_lo, wd_lo, wg_hi, wu_hi, wd_hi,
                out_hbm, rt_ref, pos_smem, pos_vmem, x0, x1, x2, o0, o1, o2, w_in_bf, w_out_bf,
                gsem, ssem, psem, *, tm, d_model, n_tok, final_norm):
    i = pl.program_id(0)
    xbufs = (x0, x1, x2)
    obufs = (o0, o1, o2)
    depth = len(xbufs)

    def issue_gather(t, sl):
        base = tbase_ref[t]
        for r in range(tm):
            tok = rt_ref[base + r]
            pltpu.make_async_copy(h_hbm.at[pl.ds(tok, 1)], xbufs[sl].at[pl.ds(r, 1)], gsem.at[sl]).start()

    def scatter_row(sl, r, tok):
        return pltpu.make_async_copy(obufs[sl].at[pl.ds(r, 1)], out_hbm.at[pl.ds(tok, 1)], ssem.at[sl])

    def issue_scatter(sl, base, n_valid):
        for r in range(tm):
            @pl.when(r < n_valid)
            def _():
                scatter_row(sl, r, rt_ref[base + r]).start()

    def wait_gather(sl):
        pltpu.make_async_copy(h_hbm.at[pl.ds(0, tm)], xbufs[sl], gsem.at[sl]).wait()

    def wait_scatter(sl, n_valid):
        aligned = pl.multiple_of((n_valid // SUBLANES) * SUBLANES, SUBLANES)

        @pl.when(aligned > 0)
        def _():
            rows = pl.ds(0, aligned)
            pltpu.make_async_copy(obufs[sl].at[rows], out_hbm.at[rows], ssem.at[sl]).wait()

        def single(r, c):
            scatter_row(sl, r, 0).wait()
            return c
        lax.fori_loop(aligned, n_valid, single, 0)

    def live(t):
        return tnv_ref[jnp.maximum(t, 0)] > 0

    @pl.when(i == 0)
    def _():
        cls2 = route_ref[0]
        pos = route_ref[1]
        for c in range(N_CLASSES):
            pos = pos + jnp.where(cls2 == c, cs_ref[c], 0)
        pos_vmem[...] = pos
        to_smem = pltpu.make_async_copy(pos_vmem, pos_smem, psem)
        to_smem.start()
        to_smem.wait()

        def place(row, c):
            for col in range(LANES):
                rt_ref[pos_smem[row, col]] = row * LANES + col
            return c
        lax.fori_loop(0, n_tok // LANES, place, 0)
        for r in range(tm):
            rt_ref[n_tok + r] = 0
        issue_gather(0, 0)
        issue_gather(1, 1)

    prev = jnp.maximum(i - 1, 0)
    group_changed = (i == 0) | (tg_ref[i] != tg_ref[prev])
    for which, t_ref, (wg, wu, wd) in ((0, tlo_ref, (wg_lo, wu_lo, wd_lo)), (1, thi_ref, (wg_hi, wu_hi, wd_hi))):
        @pl.when(group_changed | (t_ref[i] != t_ref[prev]))
        def _():
            w_in_bf[2 * which] = wg[...].astype(BF16)
            w_in_bf[2 * which + 1] = wu[...].astype(BF16)
            w_out_bf[which] = wd[...].astype(BF16)

    def tile(cur):
        far = (cur + depth - 1) % depth
        @pl.when((i <= 1) | live(i - 2))
        def _():
            wait_gather(cur)

        @pl.when((i >= depth) & live(i - depth))
        def _():
            wait_scatter(cur, tnv_ref[jnp.maximum(i - depth, 0)])

        prev_base = tbase_ref[prev]
        prev_valid = jnp.where(i == 0, 0, tnv_ref[prev])

        @pl.when(live(i))
        def _():
            issue_scatter(far, prev_base, prev_valid)
            issue_gather(i + 2, far)
            xa = xbufs[cur][...]
            x = xa[:, :d_model]
            xn = (x * _rms_scale(x) * gn_ref[...]).astype(BF16)

            def expert(which, w):
                a = jnp.dot(xn, w_in_bf[2 * which], preferred_element_type=F32)
                b = jnp.dot(xn, w_in_bf[2 * which + 1], preferred_element_type=F32)
                return jnp.dot((_silu(a) * b * w).astype(BF16), w_out_bf[which], preferred_element_type=F32)

            y = expert(0, xa[:, d_model:d_model + 1]) + expert(1, xa[:, d_model + 1:d_model + 2])
            obufs[cur][...] = x + y
            if final_norm:
                for r0 in range(0, tm, 4 * SUBLANES):
                    rows = slice(r0, r0 + 4 * SUBLANES)
                    o = obufs[cur][rows, :]
                    obufs[cur][rows, :] = o * _rms_scale(o) * fn_ref[...]

        @pl.when(jnp.logical_not(live(i)) & (i >= 1) & live(i - 1))
        def _():
            issue_scatter(far, prev_base, prev_valid)

    for residue in range(depth):
        @pl.when(i % depth == residue)
        def _():
            tile(residue)


def _moe_layer(h_aug, route, counts_blk, ffn_norm_g, final_norm_g, wg, wu, wd, layer, final_norm):
    T, DA = h_aug.shape
    D = DA - LANES
    F = wg.shape[-1]
    tm = MOE_TILE
    assert T % (LANES * SUBLANES) == 0 and T >= 2 * tm
    n_tiles = T // tm + N_CLASSES + 3
    tile_g, tile_lo, tile_hi, nvalid, base, cstart = _tile_schedule(counts_blk, tm, n_tiles)
    route2d = route[:2].reshape(2, T // LANES, LANES)

    def w_spec(shape, which):
        def index(i, tg, tlo, thi, *_):
            return (layer, tg[i], (tlo if which == 0 else thi)[i], 0, 0)
        return pl.BlockSpec((None, None, None) + shape, index)

    est = 2 * 6 * D * F * 4 + 6 * D * F * 2 + 3 * tm * DA * 4 + 3 * tm * D * 4 + 8 * tm * F * 4
    return pl.pallas_call(
        functools.partial(_moe_kernel, tm=tm, d_model=D, n_tok=T, final_norm=final_norm),
        grid_spec=pltpu.PrefetchScalarGridSpec(
            num_scalar_prefetch=6,
            grid=(n_tiles,),
            in_specs=[
                pl.BlockSpec(memory_space=pl.ANY),
                pl.BlockSpec((2, T // LANES, LANES), lambda i, *_: (0, 0, 0)),
                pl.BlockSpec((1, D), lambda i, *_: (0, 0)),
                pl.BlockSpec((1, D), lambda i, *_: (0, 0)),
                w_spec((D, F), 0), w_spec((D, F), 0), w_spec((F, D), 0),
                w_spec((D, F), 1), w_spec((D, F), 1), w_spec((F, D), 1),
            ],
            out_specs=pl.BlockSpec(memory_space=pl.ANY),
            scratch_shapes=[
                pltpu.SMEM((T + tm,), jnp.int32),
                pltpu.SMEM((T // LANES, LANES), jnp.int32),
                pltpu.VMEM((T // LANES, LANES), jnp.int32),
                pltpu.VMEM((tm, DA), F32), pltpu.VMEM((tm, DA), F32), pltpu.VMEM((tm, DA), F32),
                pltpu.VMEM((tm, D), F32), pltpu.VMEM((tm, D), F32), pltpu.VMEM((tm, D), F32),
                pltpu.VMEM((4, D, F), BF16),
                pltpu.VMEM((2, F, D), BF16),
                pltpu.SemaphoreType.DMA((3,)),
                pltpu.SemaphoreType.DMA((3,)),
                pltpu.SemaphoreType.DMA(()),
            ],
        ),
        out_shape=jax.ShapeDtypeStruct((T, D), F32),
        compiler_params=pltpu.CompilerParams(
            dimension_semantics=("arbitrary",), vmem_limit_bytes=_vmem_limit(est)),
        name="moe_experts",
    )(tile_g, tile_lo, tile_hi, nvalid, base, cstart,
      h_aug, route2d, ffn_norm_g, final_norm_g, wg, wu, wd, wg, wu, wd)


def _partial_rotary(xh, cosf, sinp, sinm, half):
    lanes = xh.shape[-1]
    return xh * cosf + pltpu.roll(xh, half, 1) * sinp + pltpu.roll(xh, lanes - half, 1) * sinm


def _tile_cos_sin(cb_ref, sb_ref, cr_ref, sr_ref):
    cb, sb, cr, sr = cb_ref[...], sb_ref[...], cr_ref[...], sr_ref[...]
    return cb * cr - sb * sr, sb * cr + cb * sr


def _kvq_kernel(h_ref, gkv_ref, gq_ref, wkv_ref, wq_ref, cb_ref, sb_ref, cr_ref, sr_ref,
                k_ref, vt_ref, qt_ref, km_ref, *, heads, dh, half):
    x = h_ref[...]
    xr = x * _rms_scale(x)
    xkv = (xr * gkv_ref[...]).astype(BF16)
    xq = (xr * gq_ref[...]).astype(BF16)
    cosf, sin_all = _tile_cos_sin(cb_ref, sb_ref, cr_ref, sr_ref)
    lane = lax.broadcasted_iota(jnp.int32, sin_all.shape, 1)
    sinp = jnp.where(lane >= half, sin_all, 0.0)
    sinm = jnp.where(lane < half, -sin_all, 0.0)
    width = heads * dh
    q_scale = dh ** -0.5 * LOG2_E
    pair_w = 2 * dh
    for c0 in range(0, width, pair_w):
        kk = jnp.dot(xkv, wkv_ref[:, c0:c0 + pair_w], preferred_element_type=F32)
        vv = jnp.dot(xkv, wkv_ref[:, width + c0:width + c0 + pair_w], preferred_element_type=F32)
        qq = jnp.dot(xq, wq_ref[:, c0:c0 + pair_w], preferred_element_type=F32)
        for u in range(2):
            hh = c0 // dh + u
            cols = slice(hh * dh, (hh + 1) * dh)
            kh = _partial_rotary(kk[:, u * dh:(u + 1) * dh], cosf, sinp, sinm, half)
            k_ref[hh] = kh.astype(BF16)
            km_ref[:, cols] = jnp.mean(kh, axis=0, keepdims=True)
            vt_ref[hh, :dh] = vv[:, u * dh:(u + 1) * dh].astype(BF16).T
            vt_ref[hh, dh:] = jnp.ones((VT_ONES_ROWS, vv.shape[0]), BF16)
            qh = _partial_rotary(qq[:, u * dh:(u + 1) * dh], cosf, sinp, sinm, half)
            qt_ref[hh] = (qh * q_scale).astype(BF16).T


def _angle_tables(inv_freq, seq, tm):
    base = jnp.arange(0, seq, tm, dtype=F32)[:, None] * inv_freq[None, :]
    offs = jnp.arange(tm, dtype=F32)[:, None] * inv_freq[None, :]
    return jnp.cos(base)[:, None, :], jnp.sin(base)[:, None, :], jnp.cos(offs), jnp.sin(offs)


def _rope_tables(seq, dh, tm):
    rot = dh // ROPE_FRACTION
    half = rot // 2
    inv = 1.0 / (ROPE_THETA ** (jnp.arange(half, dtype=F32) / half))
    inv_lanes = jnp.concatenate([inv, inv, jnp.zeros((dh - rot,), F32)])
    return _angle_tables(inv_lanes, seq, tm) + (half,)


def _kvq_proj(h, kv_norm_g, q_norm_g, wkv_bf16, wq_bf16, seq):
    T, D = h.shape
    heads = ATT_HEADS
    dh = D // heads
    tm = MOBA_BLOCK
    n_pos_tiles = seq // tm
    cos_b, sin_b, cos_r, sin_r, half = _rope_tables(seq, dh, tm)
    width = heads * dh
    est = 2 * (D * 2 * width + D * width) * 2 + 2 * tm * D * 4 + 6 * tm * width * 2 + 8 * tm * dh * 4
    pos = lambda i: (i % n_pos_tiles, 0, 0)
    const = lambda i: (0, 0)
    per_block = lambda i: (0, i, 0, 0)
    return pl.pallas_call(
        functools.partial(_kvq_kernel, heads=heads, dh=dh, half=half),
        grid=(T // tm,),
        in_specs=[
            pl.BlockSpec((tm, D), lambda i: (i, 0)),
            pl.BlockSpec((1, D), const),
            pl.BlockSpec((1, D), const),
            pl.BlockSpec((D, 2 * width), const),
            pl.BlockSpec((D, width), const),
            pl.BlockSpec((None, 1, dh), pos),
            pl.BlockSpec((None, 1, dh), pos),
            pl.BlockSpec((tm, dh), const),
            pl.BlockSpec((tm, dh), const),
        ],
        out_specs=[
            pl.BlockSpec((heads, None, tm, dh), per_block),
            pl.BlockSpec((heads, None, dh + VT_ONES_ROWS, tm), per_block),
            pl.BlockSpec((heads, None, dh, tm), per_block),
            pl.BlockSpec((None, 1, width), lambda i: (i, 0, 0)),
        ],
        out_shape=[
            jax.ShapeDtypeStruct((heads, T // tm, tm, dh), BF16),
            jax.ShapeDtypeStruct((heads, T // tm, dh + VT_ONES_ROWS, tm), BF16),
            jax.ShapeDtypeStruct((heads, T // tm, dh, tm), BF16),
            jax.ShapeDtypeStruct((T // tm, 1, width), F32),
        ],
        compiler_params=pltpu.CompilerParams(
            dimension_semantics=("arbitrary",), vmem_limit_bytes=_vmem_limit(est)),
        name="kvq_proj",
    )(h, kv_norm_g, q_norm_g, wkv_bf16, wq_bf16, cos_b, sin_b, cos_r, sin_r)


def _moba_kernel(qt_ref, k_ref, vt_ref, km_ref, o_ref, sel_ref, acc_ref, sa_ref, sb_ref, pa_ref, pb_ref,
                 *, topk, group, dh):
    qi = pl.program_id(2)
    n_blk = km_ref.shape[0]
    blk = sa_ref.shape[1]
    neg = -jnp.inf
    chunks = [slice(c, c + MOBA_KEY_CHUNK) for c in range(0, blk, MOBA_KEY_CHUNK)]

    def scores(g, j):
        return jnp.dot(k_ref[g, j], qt_ref[g], preferred_element_type=F32)

    def stage_scores(s_buf, j):
        jc = jnp.minimum(j, n_blk - 1)
        for g in range(group):
            s_buf[g] = scores(g, jc).astype(BF16)

    def stage_softmax(s_buf, p_buf, ms, j):
        new_m, alphas = [], []
        for g in range(group):
            part = s_buf[g, chunks[0], :]
            for c in chunks[1:]:
                part = jnp.maximum(part, s_buf[g, c, :])
            m_blk = jnp.max(part, axis=0, keepdims=True).astype(F32)
            if j is None:
                m_new = shift = m_blk
            else:
                bias = sel_ref[g, pl.ds(jnp.minimum(j, n_blk - 1), 1), :]
                m_new = jnp.maximum(ms[g], m_blk + bias)
                shift = m_new - bias
            shift = shift.astype(BF16)
            for c in chunks:
                p_buf[g, c, :] = jnp.exp2(s_buf[g, c, :] - shift)
            new_m.append(m_new)
            alphas.append(None if j is None else jnp.exp2(ms[g] - m_new))
        return tuple(new_m), alphas

    def stage_pv(p_buf, j, alphas):
        jc = jnp.minimum(j, n_blk - 1)
        for g in range(group):
            pv = jnp.dot(vt_ref[g, jc], p_buf[g], preferred_element_type=F32)
            acc_ref[g] = pv if alphas[g] is None else alphas[g] * acc_ref[g] + pv

    for g in range(group):
        km = km_ref[:, g * dh:(g + 1) * dh]
        km_hi = km.astype(BF16)
        km_lo = (km - km_hi.astype(F32)).astype(BF16)
        gate = (jnp.dot(km_hi, qt_ref[g], preferred_element_type=F32)
                + jnp.dot(km_lo, qt_ref[g], preferred_element_type=F32))
        blk_id = lax.broadcasted_iota(jnp.int32, gate.shape, 0)
        gv = jnp.where(blk_id < qi, gate, neg)
        bias = jnp.full(gate.shape, neg, F32)
        for _ in range(topk):
            m = jnp.max(gv, axis=0, keepdims=True)
            first = jnp.min(jnp.where(gv == m, blk_id, n_blk), axis=0, keepdims=True)
            bias = jnp.where((blk_id == first) & (m > neg), 0.0, bias)
            gv = jnp.where(blk_id == first, neg, gv)
        sel_ref[g] = bias

    key_id = lax.broadcasted_iota(jnp.int32, (blk, blk), 0)
    qry_id = lax.broadcasted_iota(jnp.int32, (blk, blk), 1)
    for g in range(group):
        sb_ref[g] = jnp.where(key_id <= qry_id, scores(g, qi), neg).astype(BF16)
    stage_scores(sa_ref, 0)
    ms, alphas = stage_softmax(sb_ref, pb_ref, None, None)
    stage_pv(pb_ref, qi, alphas)

    def body(t, ms):
        j = 2 * t
        ms, alphas = stage_softmax(sa_ref, pa_ref, ms, j)
        stage_scores(sb_ref, j + 1)
        stage_pv(pa_ref, j, alphas)
        ms, alphas = stage_softmax(sb_ref, pb_ref, ms, j + 1)
        stage_scores(sa_ref, j + 2)
        stage_pv(pb_ref, j + 1, alphas)
        return ms

    lax.fori_loop(0, (qi + 1) // 2, body, ms)
    for g in range(group):
        acc = acc_ref[g]
        o_ref[:, g * dh:(g + 1) * dh] = (acc[:dh] / acc[dh:dh + 1]).astype(BF16).T


def _moba_attention(qt, k, vt, k_mean, batch, seq):
    heads, n_tiles, dh, blk = qt.shape
    dv_rows = vt.shape[2]
    n_blk = seq // blk
    topk = min(MOBA_TOPK, n_blk)
    width = heads * dh
    group = MOBA_HEAD_GROUP
    km = k_mean.reshape(batch, n_blk, width)
    per_seq = lambda b, h, i: (h, b, 0, 0)
    est = 2 * group * seq * (dh + dv_rows) * 2 + 6 * group * blk * blk * 4 + 4 * group * blk * dh * 4
    return pl.pallas_call(
        functools.partial(_moba_kernel, topk=topk, group=group, dh=dh),
        grid=(batch, heads // group, n_blk),
        in_specs=[
            pl.BlockSpec((group, None, dh, blk), lambda b, h, i: (h, b * n_blk + i, 0, 0)),
            pl.BlockSpec((group, n_blk, blk, dh), per_seq),
            pl.BlockSpec((group, n_blk, dv_rows, blk), per_seq),
            pl.BlockSpec((None, n_blk, group * dh), lambda b, h, i: (b, 0, h)),
        ],
        out_specs=pl.BlockSpec((blk, group * dh), lambda b, h, i: (b * n_blk + i, h)),
        out_shape=jax.ShapeDtypeStruct((n_tiles * blk, width), BF16),
        scratch_shapes=[pltpu.VMEM((group, n_blk, blk), F32), pltpu.VMEM((group, dv_rows, blk), F32),
                        pltpu.VMEM((group, blk, blk), BF16), pltpu.VMEM((group, blk, blk), BF16),
                        pltpu.VMEM((group, blk, blk), BF16), pltpu.VMEM((group, blk, blk), BF16)],
        compiler_params=pltpu.CompilerParams(
            dimension_semantics=("arbitrary", "arbitrary", "arbitrary"), vmem_limit_bytes=_vmem_limit(est)),
        name="moba_attention",
    )(qt, k, vt, km)


def kernel(x, ret_norm, ret_w_in, ret_w_out, kv_norm, w_kv, attn_norm, w_q, w_o, ffn_norm, router_group_w, router_group_b, router_expert_w, router_expert_b, expert_w_gate, expert_w_up, expert_w_down, final_norm):
    B, S, D = x.shape
    T = B * S
    assert S % MOBA_BLOCK == 0 and S % RET_KERNEL_CHUNK == 0 and T % MOE_TILE == 0
    assert ret_norm.shape[0] == 1 and attn_norm.shape[0] == 1 and ffn_norm.shape[0] == 2
    h = x.reshape(T, D)
    final_g = final_norm[None, :]

    def proj_router(y, w, resid, layer, name):
        return _proj_router(y, w.astype(BF16), resid, ffn_norm[layer][None, :], router_group_w[layer],
                            router_group_b[layer], router_expert_w[layer], router_expert_b[layer], name)

    def moe(h_aug, route, counts, layer, last):
        return _moe_layer(h_aug, route, counts, ffn_norm[layer][None, :], final_g,
                          expert_w_gate, expert_w_up, expert_w_down, layer, final_norm=last)

    q, kt, v, gate = _ret_in_proj(h, ret_norm[0][None, :], ret_w_in[0].astype(BF16), S)
    y = _ret_core(q, kt, v, gate, B, S)
    h = moe(*proj_router(y, ret_w_out[0], h, 0, "ret_out_proj_router"), 0, False)

    k2, vt2, qt2, k_mean = _kvq_proj(h, kv_norm[None, :], attn_norm[0][None, :],
                                     w_kv.astype(BF16), w_q[0].astype(BF16), S)
    o = _moba_attention(qt2, k2, vt2, k_mean, B, S)
    h = moe(*proj_router(o, w_o[0], h, 1, "attn_out_proj_router"), 1, True)
    return h.reshape(B, S, D)
```

```python
import functools

import jax
import jax.numpy as jnp
from jax import lax
from jax.experimental import pallas as pl
from jax.experimental.pallas import tpu as pltpu

F32 = jnp.float32
BF16 = jnp.bfloat16

NORM_EPS = 1e-6
RET_HEADS = 4
RET_ROT_BASE = 10000.0
ATT_HEADS = 8
ROPE_FRACTION = 4
ROPE_THETA = 500000.0
MOBA_BLOCK = 256
MOBA_TOPK = 3
N_GROUPS = 4
EXPERTS_PER_GROUP = 4
PAIR_LO = (0, 0, 1, 1, 0, 2)
PAIR_HI = (1, 2, 2, 3, 3, 3)
N_PAIRS = len(PAIR_LO)
N_CLASSES = N_GROUPS * N_PAIRS

LANES = 128
SUBLANES = 8
RET_KERNEL_CHUNK = 256
MOE_TILE = 256
ROUTE_ROWS = 32
MOBA_HEAD_GROUP = 8
MOBA_KEY_CHUNK = 32
VT_ONES_ROWS = 16
LOG2_E = 1.4426950408889634
V7X_VMEM_BYTES = 64 * 1024 * 1024


def _vmem_limit(estimate_bytes):
    return int(min(V7X_VMEM_BYTES - 8 * 1024 * 1024, max(32 * 1024 * 1024, estimate_bytes * 5 // 4)))


def _rms_scale(x):
    return lax.rsqrt(jnp.mean(x * x, axis=-1, keepdims=True) + NORM_EPS)


def _silu(a):
    return a * jax.nn.sigmoid(a)


def _ret_in_kernel(x_ref, g_ref, w_ref, cb_ref, sb_ref, cr_ref, sr_ref, qdec_ref, kdec_ref,
                   q_ref, kt_ref, v_ref, gate_ref, *, heads, dk, dv):
    x = x_ref[...]
    xn = (x * _rms_scale(x) * g_ref[...]).astype(BF16)
    cos, sin = _tile_cos_sin(cb_ref, sb_ref, cr_ref, sr_ref)
    half = dk // 2

    def rotated(col0, h, dec):
        p = jnp.dot(xn, w_ref[:, col0 + h * dk:col0 + (h + 1) * dk], preferred_element_type=F32)
        x1 = p[:, :half]
        x2 = p[:, half:]
        cos_h = cos * dec
        sin_h = sin * dec
        return x1 * cos_h - x2 * sin_h, x2 * cos_h + x1 * sin_h

    for h in range(heads):
        lo, hi = rotated(0, h, qdec_ref[h])
        q_ref[:, h * dk:h * dk + half] = lo.astype(BF16)
        q_ref[:, h * dk + half:(h + 1) * dk] = hi.astype(BF16)
        lo, hi = rotated(heads * dk, h, kdec_ref[h])
        kt_ref[h, :half] = lo.astype(BF16).T
        kt_ref[h, half:] = hi.astype(BF16).T
    v0 = 2 * heads * dk
    g0 = v0 + heads * dv
    for h in range(heads):
        v_ref[:, h * dv:(h + 1) * dv] = jnp.dot(
            xn, w_ref[:, v0 + h * dv:v0 + (h + 1) * dv], preferred_element_type=F32).astype(BF16)
        gate_ref[:, h * dv:(h + 1) * dv] = jnp.dot(
            xn, w_ref[:, g0 + h * dv:g0 + (h + 1) * dv], preferred_element_type=F32).astype(BF16)


def _ret_in_proj(h, norm_g, w_bf16, seq, tm=256):
    T, D = h.shape
    heads = RET_HEADS
    dk = D // heads
    dv = 2 * dk
    n_cols = w_bf16.shape[1]
    n_pos_tiles = seq // tm
    half = dk // 2
    inv = 1.0 / (RET_ROT_BASE ** (jnp.arange(half, dtype=F32) / half))
    cos_b, sin_b, cos_r, sin_r = _angle_tables(inv, seq, tm)
    assert tm == RET_KERNEL_CHUNK
    log_gamma = _ret_log_gamma(heads)
    idx = jnp.arange(tm, dtype=F32)
    q_dec = jnp.broadcast_to(jnp.exp(log_gamma[:, None] * idx)[:, :, None], (heads, tm, half))
    k_dec = jnp.broadcast_to((jnp.exp(-log_gamma[:, None] * idx) * dk ** -0.5)[:, :, None], (heads, tm, half))
    pos = lambda i: (i % n_pos_tiles, 0, 0)
    est = 2 * D * n_cols * 2 + 2 * tm * D * 4 + 2 * tm * n_cols * 2 + (4 + 4 * heads) * tm * LANES * 4
    return pl.pallas_call(
        functools.partial(_ret_in_kernel, heads=heads, dk=dk, dv=dv),
        grid=(T // tm,),
        in_specs=[
            pl.BlockSpec((tm, D), lambda i: (i, 0)),
            pl.BlockSpec((1, D), lambda i: (0, 0)),
            pl.BlockSpec((D, n_cols), lambda i: (0, 0)),
            pl.BlockSpec((None, 1, half), pos),
            pl.BlockSpec((None, 1, half), pos),
            pl.BlockSpec((tm, half), lambda i: (0, 0)),
            pl.BlockSpec((tm, half), lambda i: (0, 0)),
            pl.BlockSpec((heads, tm, half), lambda i: (0, 0, 0)),
            pl.BlockSpec((heads, tm, half), lambda i: (0, 0, 0)),
        ],
        out_specs=[
            pl.BlockSpec((tm, heads * dk), lambda i: (i, 0)),
            pl.BlockSpec((heads, None, dk, tm), lambda i: (0, i, 0, 0)),
            pl.BlockSpec((tm, heads * dv), lambda i: (i, 0)),
            pl.BlockSpec((tm, heads * dv), lambda i: (i, 0)),
        ],
        out_shape=[
            jax.ShapeDtypeStruct((T, heads * dk), BF16),
            jax.ShapeDtypeStruct((heads, T // tm, dk, tm), BF16),
            jax.ShapeDtypeStruct((T, heads * dv), BF16),
            jax.ShapeDtypeStruct((T, heads * dv), BF16),
        ],
        compiler_params=pltpu.CompilerParams(
            dimension_semantics=("arbitrary",), vmem_limit_bytes=_vmem_limit(est)),
        name="ret_in_proj",
    )(h, norm_g, w_bf16, cos_b, sin_b, cos_r, sin_r, q_dec, k_dec)


def _ret_log_gamma(heads):
    return jnp.log1p(-jnp.power(2.0, -5.0 - jnp.arange(heads, dtype=F32)))


def _ret_core_kernel(q_ref, kt_ref, v_ref, g_ref, cd_ref, y_ref, state_ref, *, heads):
    @pl.when(pl.program_id(1) == 0)
    def _():
        state_ref[...] = jnp.zeros_like(state_ref)

    cc = q_ref.shape[0]
    dk = q_ref.shape[1] // heads
    dv = v_ref.shape[1] // heads
    causal = lax.broadcasted_iota(jnp.int32, (cc, cc), 0) >= lax.broadcasted_iota(jnp.int32, (cc, cc), 1)
    for h in range(heads):
        q = q_ref[:, h * dk:(h + 1) * dk]
        kt = kt_ref[h]
        v = v_ref[:, h * dv:(h + 1) * dv]
        s = jnp.where(causal, jnp.dot(q, kt, preferred_element_type=F32), 0.0)
        u = state_ref[h]
        o = (jnp.dot(s.astype(BF16), v, preferred_element_type=F32)
             + jnp.dot(q, u.astype(BF16), preferred_element_type=F32))
        state_ref[h] = cd_ref[h] * (u + jnp.dot(kt, v, preferred_element_type=F32))
        o = o * _rms_scale(o)
        y_ref[:, h * dv:(h + 1) * dv] = (_silu(g_ref[:, h * dv:(h + 1) * dv].astype(F32)) * o).astype(BF16)


def _ret_core(q, kt, v, gate, batch, seq):
    T = q.shape[0]
    heads = RET_HEADS
    dk = q.shape[1] // heads
    dv = v.shape[1] // heads
    cc = RET_KERNEL_CHUNK
    nc = seq // cc
    chunk_decay = jnp.broadcast_to(jnp.exp(_ret_log_gamma(heads) * cc)[:, None, None], (heads, 1, dv))
    row = lambda b, c: (b * nc + c, 0)
    return pl.pallas_call(
        functools.partial(_ret_core_kernel, heads=heads),
        grid=(batch, nc),
        in_specs=[
            pl.BlockSpec((cc, heads * dk), row),
            pl.BlockSpec((heads, None, dk, cc), lambda b, c: (0, b * nc + c, 0, 0)),
            pl.BlockSpec((cc, heads * dv), row),
            pl.BlockSpec((cc, heads * dv), row),
            pl.BlockSpec((heads, 1, dv), lambda b, c: (0, 0, 0)),
        ],
        out_specs=pl.BlockSpec((cc, heads * dv), row),
        out_shape=jax.ShapeDtypeStruct((T, heads * dv), BF16),
        scratch_shapes=[pltpu.VMEM((heads, dk, dv), F32)],
        compiler_params=pltpu.CompilerParams(dimension_semantics=("arbitrary", "arbitrary")),
        name="ret_core",
    )(q, kt, v, gate, chunk_decay)


def _proj_router_kernel(y_ref, w_ref, r_ref, g_ref, wr_ref, br_ref, haug_ref, route_ref, counts_ref, run_ref,
                        *, groups, experts, d_model):
    @pl.when(pl.program_id(0) == 0)
    def _():
        run_ref[...] = jnp.zeros_like(run_ref)

    h = r_ref[...] + jnp.dot(y_ref[...], w_ref[...], preferred_element_type=F32)
    xn = h * _rms_scale(h) * g_ref[...]
    xh = xn.astype(BF16)
    xl = (xn - xh.astype(F32)).astype(BF16)
    hi_both = jnp.dot(xh, wr_ref[...], preferred_element_type=F32)
    lo_hi = jnp.dot(xl, wr_ref[:, :LANES], preferred_element_type=F32)
    logits = hi_both[:, :LANES] + (hi_both[:, LANES:] + lo_hi) + br_ref[...]
    lt = logits.T[:ROUTE_ROWS]
    tm = lt.shape[1]
    row = lax.broadcasted_iota(jnp.int32, lt.shape, 0)
    neg = -jnp.inf

    def argmax_first(vals):
        m = jnp.max(vals, axis=0, keepdims=True)
        first = jnp.min(jnp.where(vals == m, row, ROUTE_ROWS), axis=0, keepdims=True)
        return m, first

    gl = jnp.where(row < groups, lt, neg)
    gmax, gidx = argmax_first(gl)
    gsum = jnp.sum(jnp.where(row < groups, jnp.exp(gl - gmax), 0.0), axis=0, keepdims=True)
    g_val = 1.0 / gsum
    e0 = groups + experts * gidx
    el = jnp.where((row >= e0) & (row < e0 + experts), lt, neg)
    m1, i1 = argmax_first(el)
    m2, i2 = argmax_first(jnp.where(row == i1, neg, el))
    t = jnp.exp(m2 - m1)
    w1 = 1.0 / (1.0 + t)
    w2 = t / (1.0 + t)
    first_is_lo = i1 < i2
    lo = jnp.minimum(i1, i2) - e0
    hi = jnp.maximum(i1, i2) - e0
    w_lo = g_val * jnp.where(first_is_lo, w1, w2)
    w_hi = g_val * jnp.where(first_is_lo, w2, w1)
    lex = (lo * (7 - lo)) // 2 + (hi - lo - 1)
    pair = jnp.where(lex == 2, 4, jnp.where(lex == 3, 2, jnp.where(lex == 4, 3, lex)))
    cls = gidx * N_PAIRS + pair

    onehot = row == cls
    oh = jnp.where(onehot, 1.0, 0.0)
    before = lax.broadcasted_iota(jnp.int32, (tm, tm), 0) < lax.broadcasted_iota(jnp.int32, (tm, tm), 1)
    prefix = jnp.dot(oh.astype(BF16), jnp.where(before, 1.0, 0.0).astype(BF16), preferred_element_type=F32)
    rank = jnp.sum(jnp.where(onehot, prefix + run_ref[:, 0:1], 0.0), axis=0, keepdims=True)
    run_new = run_ref[...] + jnp.sum(oh, axis=1, keepdims=True)
    run_ref[...] = run_new
    counts_ref[...] = run_new

    r8 = lax.broadcasted_iota(jnp.int32, route_ref.shape, 0)
    route_ref[...] = jnp.where(r8 == 0, cls, jnp.where(r8 == 1, rank.astype(jnp.int32), 0))
    rl = lax.broadcasted_iota(jnp.int32, (LANES, tm), 0)
    meta_t = jnp.where(rl == 0, w_lo, jnp.where(rl == 1, w_hi, 0.0))
    haug_ref[:, :d_model] = h
    haug_ref[:, d_model:] = meta_t.T


def _proj_router(y, w_bf16, resid, norm_g, w_rg, b_rg, w_re, b_re, name, tm=512):
    T, K = y.shape
    D = w_bf16.shape[1]
    G, E = N_GROUPS, EXPERTS_PER_GROUP
    n_used = G + G * E
    wr = jnp.concatenate([w_rg, jnp.transpose(w_re, (1, 0, 2)).reshape(D, G * E)], axis=1)
    wr = jnp.pad(wr, ((0, 0), (0, LANES - n_used)))
    wr_hi = wr.astype(BF16)
    wr = jnp.concatenate([wr_hi, (wr - wr_hi.astype(F32)).astype(BF16)], axis=1)
    br = jnp.pad(jnp.concatenate([b_rg, b_re.reshape(G * E)]), (0, LANES - n_used))[None, :]
    est = 2 * K * D * 2 + 2 * tm * K * 2 + 6 * tm * D * 4 + 2 * D * LANES * 4 + 4 * tm * tm * 4
    const = lambda i: (0, 0)
    return pl.pallas_call(
        functools.partial(_proj_router_kernel, groups=G, experts=E, d_model=D),
        grid=(T // tm,),
        in_specs=[
            pl.BlockSpec((tm, K), lambda i: (i, 0)),
            pl.BlockSpec((K, D), const),
            pl.BlockSpec((tm, D), lambda i: (i, 0)),
            pl.BlockSpec((1, D), const),
            pl.BlockSpec((D, 2 * LANES), const),
            pl.BlockSpec((1, LANES), const),
        ],
        out_specs=[
            pl.BlockSpec((tm, D + LANES), lambda i: (i, 0)),
            pl.BlockSpec((8, tm), lambda i: (0, i)),
            pl.BlockSpec((ROUTE_ROWS, LANES), const),
        ],
        out_shape=[
            jax.ShapeDtypeStruct((T, D + LANES), F32),
            jax.ShapeDtypeStruct((8, T), jnp.int32),
            jax.ShapeDtypeStruct((ROUTE_ROWS, LANES), F32),
        ],
        scratch_shapes=[pltpu.VMEM((ROUTE_ROWS, LANES), F32)],
        compiler_params=pltpu.CompilerParams(
            dimension_semantics=("arbitrary",), vmem_limit_bytes=_vmem_limit(est)),
        name=name,
    )(y, w_bf16, resid, norm_g, wr, br)


def _tile_schedule(counts_blk, tm, n_tiles):
    counts = counts_blk[:N_CLASSES, 0].astype(jnp.int32)
    tiles_per = (counts + tm - 1) // tm
    tile_end = jnp.cumsum(tiles_per)
    tile_begin = tile_end - tiles_per
    cstart = jnp.cumsum(counts) - counts
    t = jnp.arange(n_tiles, dtype=jnp.int32)
    live = t < tile_end[-1]
    tq = jnp.where(live, t, tile_end[-1] - 1)
    tcls = jnp.sum((tile_end[None, :] <= tq[:, None]).astype(jnp.int32), axis=1)
    onehot = tcls[:, None] == jnp.arange(N_CLASSES, dtype=jnp.int32)[None, :]
    pick = lambda table: jnp.sum(jnp.where(onehot, jnp.asarray(table, jnp.int32)[None, :], 0), axis=1)
    k = tq - pick(tile_begin)
    nvalid = jnp.where(live, jnp.clip(pick(counts) - k * tm, 0, tm), 0).astype(jnp.int32)
    base = jnp.where(live, pick(cstart) + k * tm, 0).astype(jnp.int32)
    tile_g = pick([c // N_PAIRS for c in range(N_CLASSES)])
    tile_lo = pick([PAIR_LO[c % N_PAIRS] for c in range(N_CLASSES)])
    tile_hi = pick([PAIR_HI[c % N_PAIRS] for c in range(N_CLASSES)])
    cstart_pad = jnp.pad(cstart, (0, ROUTE_ROWS - N_CLASSES)).astype(jnp.int32)
    return tile_g, tile_lo, tile_hi, nvalid, base, cstart_pad


def _moe_kernel(tg_ref, tlo_ref, thi_ref, tnv_ref, tbase_ref, cs_ref,
                h_hbm, route_ref, gn_ref, fn_ref, wg_lo, wu_lo, wd_lo, wg_hi, wu_hi, wd_hi,
                out_hbm, rt_ref, pos_smem, pos_vmem, x0, x1, x2, o0, o1, o2, w_in_bf, w_out_bf,
                gsem, ssem, psem, *, tm, d_model, n_tok, final_norm):
    i = pl.program_id(0)
    xbufs = (x0, x1, x2)
    obufs = (o0, o1, o2)
    depth = len(xbufs)

    def issue_gather(t, sl):
        base = tbase_ref[t]
        for r in range(tm):
            tok = rt_ref[base + r]
            pltpu.make_async_copy(h_hbm.at[pl.ds(tok, 1)], xbufs[sl].at[pl.ds(r, 1)], gsem.at[sl]).start()

    def scatter_row(sl, r, tok):
        return pltpu.make_async_copy(obufs[sl].at[pl.ds(r, 1)], out_hbm.at[pl.ds(tok, 1)], ssem.at[sl])

    def issue_scatter(sl, base, n_valid):
        for r in range(tm):
            @pl.when(r < n_valid)
            def _():
                scatter_row(sl, r, rt_ref[base + r]).start()

    def wait_gather(sl):
        pltpu.make_async_copy(h_hbm.at[pl.ds(0, tm)], xbufs[sl], gsem.at[sl]).wait()

    def wait_scatter(sl, n_valid):
        aligned = pl.multiple_of((n_valid // SUBLANES) * SUBLANES, SUBLANES)

        @pl.when(aligned > 0)
        def _():
            rows = pl.ds(0, aligned)
            pltpu.make_async_copy(obufs[sl].at[rows], out_hbm.at[rows], ssem.at[sl]).wait()

        def single(r, c):
            scatter_row(sl, r, 0).wait()
            return c
        lax.fori_loop(aligned, n_valid, single, 0)

    def live(t):
        return tnv_ref[jnp.maximum(t, 0)] > 0

    @pl.when(i == 0)
    def _():
        cls2 = route_ref[0]
        pos = route_ref[1]
        for c in range(N_CLASSES):
            pos = pos + jnp.where(cls2 == c, cs_ref[c], 0)
        pos_vmem[...] = pos
        to_smem = pltpu.make_async_copy(pos_vmem, pos_smem, psem)
        to_smem.start()
        to_smem.wait()

        def place(row, c):
            for col in range(LANES):
                rt_ref[pos_smem[row, col]] = row * LANES + col
            return c
        lax.fori_loop(0, n_tok // LANES, place, 0)
        for r in range(tm):
            rt_ref[n_tok + r] = 0
        issue_gather(0, 0)
        issue_gather(1, 1)

    prev = jnp.maximum(i - 1, 0)
    group_changed = (i == 0) | (tg_ref[i] != tg_ref[prev])
    for which, t_ref, (wg, wu, wd) in ((0, tlo_ref, (wg_lo, wu_lo, wd_lo)), (1, thi_ref, (wg_hi, wu_hi, wd_hi))):
        @pl.when(group_changed | (t_ref[i] != t_ref[prev]))
        def _():
            w_in_bf[2 * which] = wg[...].astype(BF16)
            w_in_bf[2 * which + 1] = wu[...].astype(BF16)
            w_out_bf[which] = wd[...].astype(BF16)

    def tile(cur):
        far = (cur + depth - 1) % depth
        @pl.when((i <= 1) | live(i - 2))
        def _():
            wait_gather(cur)

        @pl.when((i >= depth) & live(i - depth))
        def _():
            wait_scatter(cur, tnv_ref[jnp.maximum(i - depth, 0)])

        prev_base = tbase_ref[prev]
        prev_valid = jnp.where(i == 0, 0, tnv_ref[prev])

        @pl.when(live(i))
        def _():
            issue_scatter(far, prev_base, prev_valid)
            issue_gather(i + 2, far)
            xa = xbufs[cur][...]
            x = xa[:, :d_model]
            xn = (x * _rms_scale(x) * gn_ref[...]).astype(BF16)

            def expert(which, w):
                a = jnp.dot(xn, w_in_bf[2 * which], preferred_element_type=F32)
                b = jnp.dot(xn, w_in_bf[2 * which + 1], preferred_element_type=F32)
                return jnp.dot((_silu(a) * b * w).astype(BF16), w_out_bf[which], preferred_element_type=F32)

            y = expert(0, xa[:, d_model:d_model + 1]) + expert(1, xa[:, d_model + 1:d_model + 2])
            obufs[cur][...] = x + y
            if final_norm:
                for r0 in range(0, tm, 4 * SUBLANES):
                    rows = slice(r0, r0 + 4 * SUBLANES)
                    o = obufs[cur][rows, :]
                    obufs[cur][rows, :] = o * _rms_scale(o) * fn_ref[...]

        @pl.when(jnp.logical_not(live(i)) & (i >= 1) & live(i - 1))
        def _():
            issue_scatter(far, prev_base, prev_valid)

    for residue in range(depth):
        @pl.when(i % depth == residue)
        def _():
            tile(residue)


def _moe_layer(h_aug, route, counts_blk, ffn_norm_g, final_norm_g, wg, wu, wd, layer, final_norm):
    T, DA = h_aug.shape
    D = DA - LANES
    F = wg.shape[-1]
    tm = MOE_TILE
    assert T % (LANES * SUBLANES) == 0 and T >= 2 * tm
    n_tiles = T // tm + N_CLASSES + 3
    tile_g, tile_lo, tile_hi, nvalid, base, cstart = _tile_schedule(counts_blk, tm, n_tiles)
    route2d = route[:2].reshape(2, T // LANES, LANES)

    def w_spec(shape, which):
        def index(i, tg, tlo, thi, *_):
            return (layer, tg[i], (tlo if which == 0 else thi)[i], 0, 0)
        return pl.BlockSpec((None, None, None) + shape, index)

    est = 2 * 6 * D * F * 4 + 6 * D * F * 2 + 3 * tm * DA * 4 + 3 * tm * D * 4 + 8 * tm * F * 4
    return pl.pallas_call(
        functools.partial(_moe_kernel, tm=tm, d_model=D, n_tok=T, final_norm=final_norm),
        grid_spec=pltpu.PrefetchScalarGridSpec(
            num_scalar_prefetch=6,
            grid=(n_tiles,),
            in_specs=[
                pl.BlockSpec(memory_space=pl.ANY),
                pl.BlockSpec((2, T // LANES, LANES), lambda i, *_: (0, 0, 0)),
                pl.BlockSpec((1, D), lambda i, *_: (0, 0)),
                pl.BlockSpec((1, D), lambda i, *_: (0, 0)),
                w_spec((D, F), 0), w_spec((D, F), 0), w_spec((F, D), 0),
                w_spec((D, F), 1), w_spec((D, F), 1), w_spec((F, D), 1),
            ],
            out_specs=pl.BlockSpec(memory_space=pl.ANY),
            scratch_shapes=[
                pltpu.SMEM((T + tm,), jnp.int32),
                pltpu.SMEM((T // LANES, LANES), jnp.int32),
                pltpu.VMEM((T // LANES, LANES), jnp.int32),
                pltpu.VMEM((tm, DA), F32), pltpu.VMEM((tm, DA), F32), pltpu.VMEM((tm, DA), F32),
                pltpu.VMEM((tm, D), F32), pltpu.VMEM((tm, D), F32), pltpu.VMEM((tm, D), F32),
                pltpu.VMEM((4, D, F), BF16),
                pltpu.VMEM((2, F, D), BF16),
                pltpu.SemaphoreType.DMA((3,)),
                pltpu.SemaphoreType.DMA((3,)),
                pltpu.SemaphoreType.DMA(()),
            ],
        ),
        out_shape=jax.ShapeDtypeStruct((T, D), F32),
        compiler_params=pltpu.CompilerParams(
            dimension_semantics=("arbitrary",), vmem_limit_bytes=_vmem_limit(est)),
        name="moe_experts",
    )(tile_g, tile_lo, tile_hi, nvalid, base, cstart,
      h_aug, route2d, ffn_norm_g, final_norm_g, wg, wu, wd, wg, wu, wd)


def _partial_rotary(xh, cosf, sinp, sinm, half):
    lanes = xh.shape[-1]
    return xh * cosf + pltpu.roll(xh, half, 1) * sinp + pltpu.roll(xh, lanes - half, 1) * sinm


def _tile_cos_sin(cb_ref, sb_ref, cr_ref, sr_ref):
    cb, sb, cr, sr = cb_ref[...], sb_ref[...], cr_ref[...], sr_ref[...]
    return cb * cr - sb * sr, sb * cr + cb * sr


def _kvq_kernel(h_ref, gkv_ref, gq_ref, wkv_ref, wq_ref, cb_ref, sb_ref, cr_ref, sr_ref,
                k_ref, vt_ref, qt_ref, km_ref, *, heads, dh, half):
    x = h_ref[...]
    xr = x * _rms_scale(x)
    xkv = (xr * gkv_ref[...]).astype(BF16)
    xq = (xr * gq_ref[...]).astype(BF16)
    cosf, sin_all = _tile_cos_sin(cb_ref, sb_ref, cr_ref, sr_ref)
    lane = lax.broadcasted_iota(jnp.int32, sin_all.shape, 1)
    sinp = jnp.where(lane >= half, sin_all, 0.0)
    sinm = jnp.where(lane < half, -sin_all, 0.0)
    width = heads * dh
    q_scale = dh ** -0.5 * LOG2_E
    pair_w = 2 * dh
    for c0 in range(0, width, pair_w):
        kk = jnp.dot(xkv, wkv_ref[:, c0:c0 + pair_w], preferred_element_type=F32)
        vv = jnp.dot(xkv, wkv_ref[:, width + c0:width + c0 + pair_w], preferred_element_type=F32)
        qq = jnp.dot(xq, wq_ref[:, c0:c0 + pair_w], preferred_element_type=F32)
        for u in range(2):
            hh = c0 // dh + u
            cols = slice(hh * dh, (hh + 1) * dh)
            kh = _partial_rotary(kk[:, u * dh:(u + 1) * dh], cosf, sinp, sinm, half)
            k_ref[hh] = kh.astype(BF16)
            km_ref[:, cols] = jnp.mean(kh, axis=0, keepdims=True)
            vt_ref[hh, :dh] = vv[:, u * dh:(u + 1) * dh].astype(BF16).T
            vt_ref[hh, dh:] = jnp.ones((VT_ONES_ROWS, vv.shape[0]), BF16)
            qh = _partial_rotary(qq[:, u * dh:(u + 1) * dh], cosf, sinp, sinm, half)
            qt_ref[hh] = (qh * q_scale).astype(BF16).T


def _angle_tables(inv_freq, seq, tm):
    base = jnp.arange(0, seq, tm, dtype=F32)[:, None] * inv_freq[None, :]
    offs = jnp.arange(tm, dtype=F32)[:, None] * inv_freq[None, :]
    return jnp.cos(base)[:, None, :], jnp.sin(base)[:, None, :], jnp.cos(offs), jnp.sin(offs)


def _rope_tables(seq, dh, tm):
    rot = dh // ROPE_FRACTION
    half = rot // 2
    inv = 1.0 / (ROPE_THETA ** (jnp.arange(half, dtype=F32) / half))
    inv_lanes = jnp.concatenate([inv, inv, jnp.zeros((dh - rot,), F32)])
    return _angle_tables(inv_lanes, seq, tm) + (half,)


def _kvq_proj(h, kv_norm_g, q_norm_g, wkv_bf16, wq_bf16, seq):
    T, D = h.shape
    heads = ATT_HEADS
    dh = D // heads
    tm = MOBA_BLOCK
    n_pos_tiles = seq // tm
    cos_b, sin_b, cos_r, sin_r, half = _rope_tables(seq, dh, tm)
    width = heads * dh
    est = 2 * (D * 2 * width + D * width) * 2 + 2 * tm * D * 4 + 6 * tm * width * 2 + 8 * tm * dh * 4
    pos = lambda i: (i % n_pos_tiles, 0, 0)
    const = lambda i: (0, 0)
    per_block = lambda i: (0, i, 0, 0)
    return pl.pallas_call(
        functools.partial(_kvq_kernel, heads=heads, dh=dh, half=half),
        grid=(T // tm,),
        in_specs=[
            pl.BlockSpec((tm, D), lambda i: (i, 0)),
            pl.BlockSpec((1, D), const),
            pl.BlockSpec((1, D), const),
            pl.BlockSpec((D, 2 * width), const),
            pl.BlockSpec((D, width), const),
            pl.BlockSpec((None, 1, dh), pos),
            pl.BlockSpec((None, 1, dh), pos),
            pl.BlockSpec((tm, dh), const),
            pl.BlockSpec((tm, dh), const),
        ],
        out_specs=[
            pl.BlockSpec((heads, None, tm, dh), per_block),
            pl.BlockSpec((heads, None, dh + VT_ONES_ROWS, tm), per_block),
            pl.BlockSpec((heads, None, dh, tm), per_block),
            pl.BlockSpec((None, 1, width), lambda i: (i, 0, 0)),
        ],
        out_shape=[
            jax.ShapeDtypeStruct((heads, T // tm, tm, dh), BF16),
            jax.ShapeDtypeStruct((heads, T // tm, dh + VT_ONES_ROWS, tm), BF16),
            jax.ShapeDtypeStruct((heads, T // tm, dh, tm), BF16),
            jax.ShapeDtypeStruct((T // tm, 1, width), F32),
        ],
        compiler_params=pltpu.CompilerParams(
            dimension_semantics=("arbitrary",), vmem_limit_bytes=_vmem_limit(est)),
        name="kvq_proj",
    )(h, kv_norm_g, q_norm_g, wkv_bf16, wq_bf16, cos_b, sin_b, cos_r, sin_r)


def _moba_kernel(qt_ref, k_ref, vt_ref, km_ref, o_ref, sel_ref, acc_ref, sa_ref, sb_ref, pa_ref, pb_ref,
                 *, topk, group, dh):
    qi = pl.program_id(2)
    n_blk = km_ref.shape[0]
    blk = sa_ref.shape[1]
    neg = -jnp.inf
    chunks = [slice(c, c + MOBA_KEY_CHUNK) for c in range(0, blk, MOBA_KEY_CHUNK)]

    def scores(g, j):
        return jnp.dot(k_ref[g, j], qt_ref[g], preferred_element_type=F32)

    def stage_scores(s_buf, j):
        jc = jnp.minimum(j, n_blk - 1)
        for g in range(group):
            s_buf[g] = scores(g, jc).astype(BF16)

    def stage_softmax(s_buf, p_buf, ms, j):
        new_m, alphas = [], []
        for g in range(group):
            part = s_buf[g, chunks[0], :]
            for c in chunks[1:]:
                part = jnp.maximum(part, s_buf[g, c, :])
            m_blk = jnp.max(part, axis=0, keepdims=True).astype(F32)
            if j is None:
                m_new = shift = m_blk
            else:
                bias = sel_ref[g, pl.ds(jnp.minimum(j, n_blk - 1), 1), :]
                m_new = jnp.maximum(ms[g], m_blk + bias)
                shift = m_new - bias
            shift = shift.astype(BF16)
            for c in chunks:
                p_buf[g, c, :] = jnp.exp2(s_buf[g, c, :] - shift)
            new_m.append(m_new)
            alphas.append(None if j is None else jnp.exp2(ms[g] - m_new))
        return tuple(new_m), alphas

    def stage_pv(p_buf, j, alphas):
        jc = jnp.minimum(j, n_blk - 1)
        for g in range(group):
            pv = jnp.dot(vt_ref[g, jc], p_buf[g], preferred_element_type=F32)
            acc_ref[g] = pv if alphas[g] is None else alphas[g] * acc_ref[g] + pv

    for g in range(group):
        km = km_ref[:, g * dh:(g + 1) * dh]
        km_hi = km.astype(BF16)
        km_lo = (km - km_hi.astype(F32)).astype(BF16)
        gate = (jnp.dot(km_hi, qt_ref[g], preferred_element_type=F32)
                + jnp.dot(km_lo, qt_ref[g], preferred_element_type=F32))
        blk_id = lax.broadcasted_iota(jnp.int32, gate.shape, 0)
        gv = jnp.where(blk_id < qi, gate, neg)
        bias = jnp.full(gate.shape, neg, F32)
        for _ in range(topk):
            m = jnp.max(gv, axis=0, keepdims=True)
            first = jnp.min(jnp.where(gv == m, blk_id, n_blk), axis=0, keepdims=True)
            bias = jnp.where((blk_id == first) & (m > neg), 0.0, bias)
            gv = jnp.where(blk_id == first, neg, gv)
        sel_ref[g] = bias

    key_id = lax.broadcasted_iota(jnp.int32, (blk, blk), 0)
    qry_id = lax.broadcasted_iota(jnp.int32, (blk, blk), 1)
    for g in range(group):
        sb_ref[g] = jnp.where(key_id <= qry_id, scores(g, qi), neg).astype(BF16)
    stage_scores(sa_ref, 0)
    ms, alphas = stage_softmax(sb_ref, pb_ref, None, None)
    stage_pv(pb_ref, qi, alphas)

    def body(t, ms):
        j = 2 * t
        ms, alphas = stage_softmax(sa_ref, pa_ref, ms, j)
        stage_scores(sb_ref, j + 1)
        stage_pv(pa_ref, j, alphas)
        ms, alphas = stage_softmax(sb_ref, pb_ref, ms, j + 1)
        stage_scores(sa_ref, j + 2)
        stage_pv(pb_ref, j + 1, alphas)
        return ms

    lax.fori_loop(0, (qi + 1) // 2, body, ms)
    for g in range(group):
        acc = acc_ref[g]
        o_ref[:, g * dh:(g + 1) * dh] = (acc[:dh] / acc[dh:dh + 1]).astype(BF16).T


def _moba_attention(qt, k, vt, k_mean, batch, seq):
    heads, n_tiles, dh, blk = qt.shape
    dv_rows = vt.shape[2]
    n_blk = seq // blk
    topk = min(MOBA_TOPK, n_blk)
    width = heads * dh
    group = MOBA_HEAD_GROUP
    km = k_mean.reshape(batch, n_blk, width)
    per_seq = lambda b, h, i: (h, b, 0, 0)
    est = group * seq * (dh + dv_rows) * 2 + 2 * group * blk * blk * (2 + 2) + 6 * group * blk * dh * 4
    return pl.pallas_call(
        functools.partial(_moba_kernel, topk=topk, group=group, dh=dh),
        grid=(batch, heads // group, n_blk),
        in_specs=[
            pl.BlockSpec((group, None, dh, blk), lambda b, h, i: (h, b * n_blk + i, 0, 0)),
            pl.BlockSpec((group, n_blk, blk, dh), per_seq, pipeline_mode=pl.Buffered(1)),
            pl.BlockSpec((group, n_blk, dv_rows, blk), per_seq, pipeline_mode=pl.Buffered(1)),
            pl.BlockSpec((None, n_blk, group * dh), lambda b, h, i: (b, 0, h)),
        ],
        out_specs=pl.BlockSpec((blk, group * dh), lambda b, h, i: (b * n_blk + i, h)),
        out_shape=jax.ShapeDtypeStruct((n_tiles * blk, width), BF16),
        scratch_shapes=[pltpu.VMEM((group, n_blk, blk), F32), pltpu.VMEM((group, dv_rows, blk), F32),
                        pltpu.VMEM((group, blk, blk), BF16), pltpu.VMEM((group, blk, blk), BF16),
                        pltpu.VMEM((group, blk, blk), BF16), pltpu.VMEM((group, blk, blk), BF16)],
        compiler_params=pltpu.CompilerParams(
            dimension_semantics=("arbitrary", "arbitrary", "arbitrary"), vmem_limit_bytes=_vmem_limit(est)),
        name="moba_attention",
    )(qt, k, vt, km)


def kernel(x, ret_norm, ret_w_in, ret_w_out, kv_norm, w_kv, attn_norm, w_q, w_o, ffn_norm, router_group_w, router_group_b, router_expert_w, router_expert_b, expert_w_gate, expert_w_up, expert_w_down, final_norm):
    B, S, D = x.shape
    T = B * S
    assert S % MOBA_BLOCK == 0 and S % RET_KERNEL_CHUNK == 0 and T % MOE_TILE == 0
    assert ret_norm.shape[0] == 1 and attn_norm.shape[0] == 1 and ffn_norm.shape[0] == 2
    h = x.reshape(T, D)
    final_g = final_norm[None, :]

    def proj_router(y, w, resid, layer, name):
        return _proj_router(y, w.astype(BF16), resid, ffn_norm[layer][None, :], router_group_w[layer],
                            router_group_b[layer], router_expert_w[layer], router_expert_b[layer], name)

    def moe(h_aug, route, counts, layer, last):
        return _moe_layer(h_aug, route, counts, ffn_norm[layer][None, :], final_g,
                          expert_w_gate, expert_w_up, expert_w_down, layer, final_norm=last)

    q, kt, v, gate = _ret_in_proj(h, ret_norm[0][None, :], ret_w_in[0].astype(BF16), S)
    y = _ret_core(q, kt, v, gate, B, S)
    h = moe(*proj_router(y, ret_w_out[0], h, 0, "ret_out_proj_router"), 0, False)

    k2, vt2, qt2, k_mean = _kvq_proj(h, kv_norm[None, :], attn_norm[0][None, :],
                                     w_kv.astype(BF16), w_q[0].astype(BF16), S)
    o = _moba_attention(qt2, k2, vt2, k_mean, B, S)
    h = moe(*proj_router(o, w_o[0], h, 1, "attn_out_proj_router"), 1, True)
    return h.reshape(B, S, D)
```

```python
import functools

import jax
import jax.numpy as jnp
from jax import lax
from jax.experimental import pallas as pl
from jax.experimental.pallas import tpu as pltpu

F32 = jnp.float32
BF16 = jnp.bfloat16

NORM_EPS = 1e-6
RET_HEADS = 4
RET_ROT_BASE = 10000.0
ATT_HEADS = 8
ROPE_FRACTION = 4
ROPE_THETA = 500000.0
MOBA_BLOCK = 256
MOBA_TOPK = 3
N_GROUPS = 4
EXPERTS_PER_GROUP = 4
PAIR_LO = (0, 0, 1, 1, 0, 2)
PAIR_HI = (1, 2, 2, 3, 3, 3)
N_PAIRS = len(PAIR_LO)
N_CLASSES = N_GROUPS * N_PAIRS

LANES = 128
SUBLANES = 8
RET_KERNEL_CHUNK = 256
MOE_TILE = 256
ROUTE_ROWS = 32
MOBA_HEAD_GROUP = 8
MOBA_KEY_CHUNK = 32
VT_ONES_ROWS = 16
LOG2_E = 1.4426950408889634
V7X_VMEM_BYTES = 64 * 1024 * 1024


def _vmem_limit(estimate_bytes):
    return int(min(V7X_VMEM_BYTES - 8 * 1024 * 1024, max(32 * 1024 * 1024, estimate_bytes * 5 // 4)))


def _rms_scale(x):
    return lax.rsqrt(jnp.mean(x * x, axis=-1, keepdims=True) + NORM_EPS)


def _silu(a):
    return a * jax.nn.sigmoid(a)


def _ret_in_kernel(x_ref, g_ref, w_ref, cb_ref, sb_ref, cr_ref, sr_ref, qdec_ref, kdec_ref,
                   q_ref, kt_ref, v_ref, gate_ref, *, heads, dk, dv):
    x = x_ref[...]
    xn = (x * _rms_scale(x) * g_ref[...]).astype(BF16)
    cos, sin = _tile_cos_sin(cb_ref, sb_ref, cr_ref, sr_ref)
    half = dk // 2

    def rotated(col0, h, dec):
        p = jnp.dot(xn, w_ref[:, col0 + h * dk:col0 + (h + 1) * dk], preferred_element_type=F32)
        x1 = p[:, :half]
        x2 = p[:, half:]
        cos_h = cos * dec
        sin_h = sin * dec
        return x1 * cos_h - x2 * sin_h, x2 * cos_h + x1 * sin_h

    for h in range(heads):
        lo, hi = rotated(0, h, qdec_ref[h])
        q_ref[:, h * dk:h * dk + half] = lo.astype(BF16)
        q_ref[:, h * dk + half:(h + 1) * dk] = hi.astype(BF16)
        lo, hi = rotated(heads * dk, h, kdec_ref[h])
        kt_ref[h, :half] = lo.astype(BF16).T
        kt_ref[h, half:] = hi.astype(BF16).T
    v0 = 2 * heads * dk
    g0 = v0 + heads * dv
    for h in range(heads):
        v_ref[:, h * dv:(h + 1) * dv] = jnp.dot(
            xn, w_ref[:, v0 + h * dv:v0 + (h + 1) * dv], preferred_element_type=F32).astype(BF16)
        gate_ref[:, h * dv:(h + 1) * dv] = jnp.dot(
            xn, w_ref[:, g0 + h * dv:g0 + (h + 1) * dv], preferred_element_type=F32).astype(BF16)


def _ret_in_proj(h, norm_g, w_bf16, seq, tm=256):
    T, D = h.shape
    heads = RET_HEADS
    dk = D // heads
    dv = 2 * dk
    n_cols = w_bf16.shape[1]
    n_pos_tiles = seq // tm
    half = dk // 2
    inv = 1.0 / (RET_ROT_BASE ** (jnp.arange(half, dtype=F32) / half))
    cos_b, sin_b, cos_r, sin_r = _angle_tables(inv, seq, tm)
    assert tm == RET_KERNEL_CHUNK
    log_gamma = _ret_log_gamma(heads)
    idx = jnp.arange(tm, dtype=F32)
    q_dec = jnp.broadcast_to(jnp.exp(log_gamma[:, None] * idx)[:, :, None], (heads, tm, half))
    k_dec = jnp.broadcast_to((jnp.exp(-log_gamma[:, None] * idx) * dk ** -0.5)[:, :, None], (heads, tm, half))
    pos = lambda i: (i % n_pos_tiles, 0, 0)
    est = 2 * D * n_cols * 2 + 2 * tm * D * 4 + 2 * tm * n_cols * 2 + (4 + 4 * heads) * tm * LANES * 4
    return pl.pallas_call(
        functools.partial(_ret_in_kernel, heads=heads, dk=dk, dv=dv),
        grid=(T // tm,),
        in_specs=[
            pl.BlockSpec((tm, D), lambda i: (i, 0)),
            pl.BlockSpec((1, D), lambda i: (0, 0)),
            pl.BlockSpec((D, n_cols), lambda i: (0, 0)),
            pl.BlockSpec((None, 1, half), pos),
            pl.BlockSpec((None, 1, half), pos),
            pl.BlockSpec((tm, half), lambda i: (0, 0)),
            pl.BlockSpec((tm, half), lambda i: (0, 0)),
            pl.BlockSpec((heads, tm, half), lambda i: (0, 0, 0)),
            pl.BlockSpec((heads, tm, half), lambda i: (0, 0, 0)),
        ],
        out_specs=[
            pl.BlockSpec((tm, heads * dk), lambda i: (i, 0)),
            pl.BlockSpec((heads, None, dk, tm), lambda i: (0, i, 0, 0)),
            pl.BlockSpec((tm, heads * dv), lambda i: (i, 0)),
            pl.BlockSpec((tm, heads * dv), lambda i: (i, 0)),
        ],
        out_shape=[
            jax.ShapeDtypeStruct((T, heads * dk), BF16),
            jax.ShapeDtypeStruct((heads, T // tm, dk, tm), BF16),
            jax.ShapeDtypeStruct((T, heads * dv), BF16),
            jax.ShapeDtypeStruct((T, heads * dv), BF16),
        ],
        compiler_params=pltpu.CompilerParams(
            dimension_semantics=("arbitrary",), vmem_limit_bytes=_vmem_limit(est)),
        name="ret_in_proj",
    )(h, norm_g, w_bf16, cos_b, sin_b, cos_r, sin_r, q_dec, k_dec)


def _ret_log_gamma(heads):
    return jnp.log1p(-jnp.power(2.0, -5.0 - jnp.arange(heads, dtype=F32)))


def _ret_core_kernel(q_ref, kt_ref, v_ref, g_ref, cd_ref, y_ref, state_ref, *, heads):
    @pl.when(pl.program_id(1) == 0)
    def _():
        state_ref[...] = jnp.zeros_like(state_ref)

    cc = q_ref.shape[0]
    dk = q_ref.shape[1] // heads
    dv = v_ref.shape[1] // heads
    causal = lax.broadcasted_iota(jnp.int32, (cc, cc), 0) >= lax.broadcasted_iota(jnp.int32, (cc, cc), 1)
    for h in range(heads):
        q = q_ref[:, h * dk:(h + 1) * dk]
        kt = kt_ref[h]
        v = v_ref[:, h * dv:(h + 1) * dv]
        s = jnp.where(causal, jnp.dot(q, kt, preferred_element_type=F32), 0.0)
        u = state_ref[h]
        o = (jnp.dot(s.astype(BF16), v, preferred_element_type=F32)
             + jnp.dot(q, u.astype(BF16), preferred_element_type=F32))
        state_ref[h] = cd_ref[h] * (u + jnp.dot(kt, v, preferred_element_type=F32))
        o = o * _rms_scale(o)
        y_ref[:, h * dv:(h + 1) * dv] = (_silu(g_ref[:, h * dv:(h + 1) * dv].astype(F32)) * o).astype(BF16)


def _ret_core(q, kt, v, gate, batch, seq):
    T = q.shape[0]
    heads = RET_HEADS
    dk = q.shape[1] // heads
    dv = v.shape[1] // heads
    cc = RET_KERNEL_CHUNK
    nc = seq // cc
    chunk_decay = jnp.broadcast_to(jnp.exp(_ret_log_gamma(heads) * cc)[:, None, None], (heads, 1, dv))
    row = lambda b, c: (b * nc + c, 0)
    return pl.pallas_call(
        functools.partial(_ret_core_kernel, heads=heads),
        grid=(batch, nc),
        in_specs=[
            pl.BlockSpec((cc, heads * dk), row),
            pl.BlockSpec((heads, None, dk, cc), lambda b, c: (0, b * nc + c, 0, 0)),
            pl.BlockSpec((cc, heads * dv), row),
            pl.BlockSpec((cc, heads * dv), row),
            pl.BlockSpec((heads, 1, dv), lambda b, c: (0, 0, 0)),
        ],
        out_specs=pl.BlockSpec((cc, heads * dv), row),
        out_shape=jax.ShapeDtypeStruct((T, heads * dv), BF16),
        scratch_shapes=[pltpu.VMEM((heads, dk, dv), F32)],
        compiler_params=pltpu.CompilerParams(dimension_semantics=("arbitrary", "arbitrary")),
        name="ret_core",
    )(q, kt, v, gate, chunk_decay)


def _proj_router_kernel(y_ref, w_ref, r_ref, g_ref, wr_ref, br_ref, haug_ref, route_ref, counts_ref, run_ref,
                        *, groups, experts, d_model):
    @pl.when(pl.program_id(0) == 0)
    def _():
        run_ref[...] = jnp.zeros_like(run_ref)

    h = r_ref[...] + jnp.dot(y_ref[...], w_ref[...], preferred_element_type=F32)
    xn = h * _rms_scale(h) * g_ref[...]
    xh = xn.astype(BF16)
    xl = (xn - xh.astype(F32)).astype(BF16)
    hi_both = jnp.dot(xh, wr_ref[...], preferred_element_type=F32)
    lo_hi = jnp.dot(xl, wr_ref[:, :LANES], preferred_element_type=F32)
    logits = hi_both[:, :LANES] + (hi_both[:, LANES:] + lo_hi) + br_ref[...]
    lt = logits.T[:ROUTE_ROWS]
    tm = lt.shape[1]
    row = lax.broadcasted_iota(jnp.int32, lt.shape, 0)
    neg = -jnp.inf

    def argmax_first(vals):
        m = jnp.max(vals, axis=0, keepdims=True)
        first = jnp.min(jnp.where(vals == m, row, ROUTE_ROWS), axis=0, keepdims=True)
        return m, first

    gl = jnp.where(row < groups, lt, neg)
    gmax, gidx = argmax_first(gl)
    gsum = jnp.sum(jnp.where(row < groups, jnp.exp(gl - gmax), 0.0), axis=0, keepdims=True)
    g_val = 1.0 / gsum
    e0 = groups + experts * gidx
    el = jnp.where((row >= e0) & (row < e0 + experts), lt, neg)
    m1, i1 = argmax_first(el)
    m2, i2 = argmax_first(jnp.where(row == i1, neg, el))
    t = jnp.exp(m2 - m1)
    w1 = 1.0 / (1.0 + t)
    w2 = t / (1.0 + t)
    first_is_lo = i1 < i2
    lo = jnp.minimum(i1, i2) - e0
    hi = jnp.maximum(i1, i2) - e0
    w_lo = g_val * jnp.where(first_is_lo, w1, w2)
    w_hi = g_val * jnp.where(first_is_lo, w2, w1)
    lex = (lo * (7 - lo)) // 2 + (hi - lo - 1)
    pair = jnp.where(lex == 2, 4, jnp.where(lex == 3, 2, jnp.where(lex == 4, 3, lex)))
    cls = gidx * N_PAIRS + pair

    onehot = row == cls
    oh = jnp.where(onehot, 1.0, 0.0)
    before = lax.broadcasted_iota(jnp.int32, (tm, tm), 0) < lax.broadcasted_iota(jnp.int32, (tm, tm), 1)
    prefix = jnp.dot(oh.astype(BF16), jnp.where(before, 1.0, 0.0).astype(BF16), preferred_element_type=F32)
    rank = jnp.sum(jnp.where(onehot, prefix + run_ref[:, 0:1], 0.0), axis=0, keepdims=True)
    run_new = run_ref[...] + jnp.sum(oh, axis=1, keepdims=True)
    run_ref[...] = run_new
    counts_ref[...] = run_new

    r8 = lax.broadcasted_iota(jnp.int32, route_ref.shape, 0)
    route_ref[...] = jnp.where(r8 == 0, cls, jnp.where(r8 == 1, rank.astype(jnp.int32), 0))
    rl = lax.broadcasted_iota(jnp.int32, (LANES, tm), 0)
    meta_t = jnp.where(rl == 0, w_lo, jnp.where(rl == 1, w_hi, 0.0))
    haug_ref[:, :d_model] = h
    haug_ref[:, d_model:] = meta_t.T


def _proj_router(y, w_bf16, resid, norm_g, w_rg, b_rg, w_re, b_re, name, tm=512):
    T, K = y.shape
    D = w_bf16.shape[1]
    G, E = N_GROUPS, EXPERTS_PER_GROUP
    n_used = G + G * E
    wr = jnp.concatenate([w_rg, jnp.transpose(w_re, (1, 0, 2)).reshape(D, G * E)], axis=1)
    wr = jnp.pad(wr, ((0, 0), (0, LANES - n_used)))
    wr_hi = wr.astype(BF16)
    wr = jnp.concatenate([wr_hi, (wr - wr_hi.astype(F32)).astype(BF16)], axis=1)
    br = jnp.pad(jnp.concatenate([b_rg, b_re.reshape(G * E)]), (0, LANES - n_used))[None, :]
    est = 2 * K * D * 2 + 2 * tm * K * 2 + 6 * tm * D * 4 + 2 * D * LANES * 4 + 4 * tm * tm * 4
    const = lambda i: (0, 0)
    return pl.pallas_call(
        functools.partial(_proj_router_kernel, groups=G, experts=E, d_model=D),
        grid=(T // tm,),
        in_specs=[
            pl.BlockSpec((tm, K), lambda i: (i, 0)),
            pl.BlockSpec((K, D), const),
            pl.BlockSpec((tm, D), lambda i: (i, 0)),
            pl.BlockSpec((1, D), const),
            pl.BlockSpec((D, 2 * LANES), const),
            pl.BlockSpec((1, LANES), const),
        ],
        out_specs=[
            pl.BlockSpec((tm, D + LANES), lambda i: (i, 0)),
            pl.BlockSpec((8, tm), lambda i: (0, i)),
            pl.BlockSpec((ROUTE_ROWS, LANES), const),
        ],
        out_shape=[
            jax.ShapeDtypeStruct((T, D + LANES), F32),
            jax.ShapeDtypeStruct((8, T), jnp.int32),
            jax.ShapeDtypeStruct((ROUTE_ROWS, LANES), F32),
        ],
        scratch_shapes=[pltpu.VMEM((ROUTE_ROWS, LANES), F32)],
        compiler_params=pltpu.CompilerParams(
            dimension_semantics=("arbitrary",), vmem_limit_bytes=_vmem_limit(est)),
        name=name,
    )(y, w_bf16, resid, norm_g, wr, br)


def _tile_schedule(counts_blk, tm, n_tiles):
    counts = counts_blk[:N_CLASSES, 0].astype(jnp.int32)
    tiles_per = (counts + tm - 1) // tm
    tile_end = jnp.cumsum(tiles_per)
    tile_begin = tile_end - tiles_per
    cstart = jnp.cumsum(counts) - counts
    t = jnp.arange(n_tiles, dtype=jnp.int32)
    live = t < tile_end[-1]
    tq = jnp.where(live, t, tile_end[-1] - 1)
    tcls = jnp.sum((tile_end[None, :] <= tq[:, None]).astype(jnp.int32), axis=1)
    onehot = tcls[:, None] == jnp.arange(N_CLASSES, dtype=jnp.int32)[None, :]
    pick = lambda table: jnp.sum(jnp.where(onehot, jnp.asarray(table, jnp.int32)[None, :], 0), axis=1)
    k = tq - pick(tile_begin)
    nvalid = jnp.where(live, jnp.clip(pick(counts) - k * tm, 0, tm), 0).astype(jnp.int32)
    base = jnp.where(live, pick(cstart) + k * tm, 0).astype(jnp.int32)
    tile_g = pick([c // N_PAIRS for c in range(N_CLASSES)])
    tile_lo = pick([PAIR_LO[c % N_PAIRS] for c in range(N_CLASSES)])
    tile_hi = pick([PAIR_HI[c % N_PAIRS] for c in range(N_CLASSES)])
    cstart_pad = jnp.pad(cstart, (0, ROUTE_ROWS - N_CLASSES)).astype(jnp.int32)
    return tile_g, tile_lo, tile_hi, nvalid, base, cstart_pad


def _moe_kernel(tg_ref, tlo_ref, thi_ref, tnv_ref, tbase_ref, cs_ref,
                h_hbm, route_ref, gn_ref, fn_ref, wg_lo, wu_lo, wd_lo, wg_hi, wu_hi, wd_hi,
                out_hbm, rt_ref, pos_smem, pos_vmem, x0, x1, x2, o0, o1, o2, w_in_bf, w_out_bf,
                gsem, ssem, psem, *, tm, d_model, n_tok, final_norm):
    i = pl.program_id(0)
    xbufs = (x0, x1, x2)
    obufs = (o0, o1, o2)
    depth = len(xbufs)

    def issue_gather(t, sl):
        base = tbase_ref[t]
        for r in range(tm):
            tok = rt_ref[base + r]
            pltpu.make_async_copy(h_hbm.at[pl.ds(tok, 1)], xbufs[sl].at[pl.ds(r, 1)], gsem.at[sl]).start()

    def scatter_row(sl, r, tok):
        return pltpu.make_async_copy(obufs[sl].at[pl.ds(r, 1)], out_hbm.at[pl.ds(tok, 1)], ssem.at[sl])

    def issue_scatter(sl, base, n_valid):
        for r in range(tm):
            @pl.when(r < n_valid)
            def _():
                scatter_row(sl, r, rt_ref[base + r]).start()

    def wait_gather(sl):
        pltpu.make_async_copy(h_hbm.at[pl.ds(0, tm)], xbufs[sl], gsem.at[sl]).wait()

    def wait_scatter(sl, n_valid):
        aligned = pl.multiple_of((n_valid // SUBLANES) * SUBLANES, SUBLANES)

        @pl.when(aligned > 0)
        def _():
            rows = pl.ds(0, aligned)
            pltpu.make_async_copy(obufs[sl].at[rows], out_hbm.at[rows], ssem.at[sl]).wait()

        def single(r, c):
            scatter_row(sl, r, 0).wait()
            return c
        lax.fori_loop(aligned, n_valid, single, 0)

    def live(t):
        return tnv_ref[jnp.maximum(t, 0)] > 0

    @pl.when(i == 0)
    def _():
        cls2 = route_ref[0]
        pos = route_ref[1]
        for c in range(N_CLASSES):
            pos = pos + jnp.where(cls2 == c, cs_ref[c], 0)
        pos_vmem[...] = pos
        to_smem = pltpu.make_async_copy(pos_vmem, pos_smem, psem)
        to_smem.start()
        to_smem.wait()

        def place(row, c):
            for col in range(LANES):
                rt_ref[pos_smem[row, col]] = row * LANES + col
            return c
        lax.fori_loop(0, n_tok // LANES, place, 0)
        for r in range(tm):
            rt_ref[n_tok + r] = 0
        issue_gather(0, 0)
        issue_gather(1, 1)

    prev = jnp.maximum(i - 1, 0)
    group_changed = (i == 0) | (tg_ref[i] != tg_ref[prev])
    for which, t_ref, (wg, wu, wd) in ((0, tlo_ref, (wg_lo, wu_lo, wd_lo)), (1, thi_ref, (wg_hi, wu_hi, wd_hi))):
        @pl.when(group_changed | (t_ref[i] != t_ref[prev]))
        def _():
            w_in_bf[2 * which] = wg[...].astype(BF16)
            w_in_bf[2 * which + 1] = wu[...].astype(BF16)
            w_out_bf[which] = wd[...].astype(BF16)

    def tile(cur):
        far = (cur + depth - 1) % depth
        @pl.when((i <= 1) | live(i - 2))
        def _():
            wait_gather(cur)

        @pl.when((i >= depth) & live(i - depth))
        def _():
            wait_scatter(cur, tnv_ref[jnp.maximum(i - depth, 0)])

        prev_base = tbase_ref[prev]
        prev_valid = jnp.where(i == 0, 0, tnv_ref[prev])

        @pl.when(live(i))
        def _():
            issue_scatter(far, prev_base, prev_valid)
            issue_gather(i + 2, far)
            xa = xbufs[cur][...]
            x = xa[:, :d_model]
            xn = (x * _rms_scale(x) * gn_ref[...]).astype(BF16)

            def expert(which, w):
                a = jnp.dot(xn, w_in_bf[2 * which], preferred_element_type=F32)
                b = jnp.dot(xn, w_in_bf[2 * which + 1], preferred_element_type=F32)
                return jnp.dot((_silu(a) * b * w).astype(BF16), w_out_bf[which], preferred_element_type=F32)

            y = expert(0, xa[:, d_model:d_model + 1]) + expert(1, xa[:, d_model + 1:d_model + 2])
            obufs[cur][...] = x + y
            if final_norm:
                for r0 in range(0, tm, 4 * SUBLANES):
                    rows = slice(r0, r0 + 4 * SUBLANES)
                    o = obufs[cur][rows, :]
                    obufs[cur][rows, :] = o * _rms_scale(o) * fn_ref[...]

        @pl.when(jnp.logical_not(live(i)) & (i >= 1) & live(i - 1))
        def _():
            issue_scatter(far, prev_base, prev_valid)

    for residue in range(depth):
        @pl.when(i % depth == residue)
        def _():
            tile(residue)


def _moe_layer(h_aug, route, counts_blk, ffn_norm_g, final_norm_g, wg, wu, wd, layer, final_norm):
    T, DA = h_aug.shape
    D = DA - LANES
    F = wg.shape[-1]
    tm = MOE_TILE
    assert T % (LANES * SUBLANES) == 0 and T >= 2 * tm
    n_tiles = T // tm + N_CLASSES + 3
    tile_g, tile_lo, tile_hi, nvalid, base, cstart = _tile_schedule(counts_blk, tm, n_tiles)
    route2d = route[:2].reshape(2, T // LANES, LANES)

    def w_spec(shape, which):
        def index(i, tg, tlo, thi, *_):
            return (layer, tg[i], (tlo if which == 0 else thi)[i], 0, 0)
        return pl.BlockSpec((None, None, None) + shape, index)

    est = 2 * 6 * D * F * 4 + 6 * D * F * 2 + 3 * tm * DA * 4 + 3 * tm * D * 4 + 8 * tm * F * 4
    return pl.pallas_call(
        functools.partial(_moe_kernel, tm=tm, d_model=D, n_tok=T, final_norm=final_norm),
        grid_spec=pltpu.PrefetchScalarGridSpec(
            num_scalar_prefetch=6,
            grid=(n_tiles,),
            in_specs=[
                pl.BlockSpec(memory_space=pl.ANY),
                pl.BlockSpec((2, T // LANES, LANES), lambda i, *_: (0, 0, 0)),
                pl.BlockSpec((1, D), lambda i, *_: (0, 0)),
                pl.BlockSpec((1, D), lambda i, *_: (0, 0)),
                w_spec((D, F), 0), w_spec((D, F), 0), w_spec((F, D), 0),
                w_spec((D, F), 1), w_spec((D, F), 1), w_spec((F, D), 1),
            ],
            out_specs=pl.BlockSpec(memory_space=pl.ANY),
            scratch_shapes=[
                pltpu.SMEM((T + tm,), jnp.int32),
                pltpu.SMEM((T // LANES, LANES), jnp.int32),
                pltpu.VMEM((T // LANES, LANES), jnp.int32),
                pltpu.VMEM((tm, DA), F32), pltpu.VMEM((tm, DA), F32), pltpu.VMEM((tm, DA), F32),
                pltpu.VMEM((tm, D), F32), pltpu.VMEM((tm, D), F32), pltpu.VMEM((tm, D), F32),
                pltpu.VMEM((4, D, F), BF16),
                pltpu.VMEM((2, F, D), BF16),
                pltpu.SemaphoreType.DMA((3,)),
                pltpu.SemaphoreType.DMA((3,)),
                pltpu.SemaphoreType.DMA(()),
            ],
        ),
        out_shape=jax.ShapeDtypeStruct((T, D), F32),
        compiler_params=pltpu.CompilerParams(
            dimension_semantics=("arbitrary",), vmem_limit_bytes=_vmem_limit(est)),
        name="moe_experts",
    )(tile_g, tile_lo, tile_hi, nvalid, base, cstart,
      h_aug, route2d, ffn_norm_g, final_norm_g, wg, wu, wd, wg, wu, wd)


def _partial_rotary(xh, cosf, sinp, sinm, half):
    lanes = xh.shape[-1]
    return xh * cosf + pltpu.roll(xh, half, 1) * sinp + pltpu.roll(xh, lanes - half, 1) * sinm


def _tile_cos_sin(cb_ref, sb_ref, cr_ref, sr_ref):
    cb, sb, cr, sr = cb_ref[...], sb_ref[...], cr_ref[...], sr_ref[...]
    return cb * cr - sb * sr, sb * cr + cb * sr


def _kvq_kernel(h_ref, gkv_ref, gq_ref, wkv_ref, wq_ref, cb_ref, sb_ref, cr_ref, sr_ref,
                k_ref, vt_ref, qt_ref, km_ref, *, heads, dh, half):
    x = h_ref[...]
    xr = x * _rms_scale(x)
    xkv = (xr * gkv_ref[...]).astype(BF16)
    xq = (xr * gq_ref[...]).astype(BF16)
    cosf, sin_all = _tile_cos_sin(cb_ref, sb_ref, cr_ref, sr_ref)
    lane = lax.broadcasted_iota(jnp.int32, sin_all.shape, 1)
    sinp = jnp.where(lane >= half, sin_all, 0.0)
    sinm = jnp.where(lane < half, -sin_all, 0.0)
    width = heads * dh
    q_scale = dh ** -0.5 * LOG2_E
    pair_w = 2 * dh
    for c0 in range(0, width, pair_w):
        kk = jnp.dot(xkv, wkv_ref[:, c0:c0 + pair_w], preferred_element_type=F32)
        vv = jnp.dot(xkv, wkv_ref[:, width + c0:width + c0 + pair_w], preferred_element_type=F32)
        qq = jnp.dot(xq, wq_ref[:, c0:c0 + pair_w], preferred_element_type=F32)
        for u in range(2):
            hh = c0 // dh + u
            cols = slice(hh * dh, (hh + 1) * dh)
            kh = _partial_rotary(kk[:, u * dh:(u + 1) * dh], cosf, sinp, sinm, half)
            k_ref[hh] = kh.astype(BF16)
            km_ref[:, cols] = jnp.mean(kh, axis=0, keepdims=True)
            vt_ref[hh, :dh] = vv[:, u * dh:(u + 1) * dh].astype(BF16).T
            vt_ref[hh, dh:] = jnp.ones((VT_ONES_ROWS, vv.shape[0]), BF16)
            qh = _partial_rotary(qq[:, u * dh:(u + 1) * dh], cosf, sinp, sinm, half)
            qt_ref[hh] = (qh * q_scale).astype(BF16).T


def _angle_tables(inv_freq, seq, tm):
    base = jnp.arange(0, seq, tm, dtype=F32)[:, None] * inv_freq[None, :]
    offs = jnp.arange(tm, dtype=F32)[:, None] * inv_freq[None, :]
    return jnp.cos(base)[:, None, :], jnp.sin(base)[:, None, :], jnp.cos(offs), jnp.sin(offs)


def _rope_tables(seq, dh, tm):
    rot = dh // ROPE_FRACTION
    half = rot // 2
    inv = 1.0 / (ROPE_THETA ** (jnp.arange(half, dtype=F32) / half))
    inv_lanes = jnp.concatenate([inv, inv, jnp.zeros((dh - rot,), F32)])
    return _angle_tables(inv_lanes, seq, tm) + (half,)


def _kvq_proj(h, kv_norm_g, q_norm_g, wkv_bf16, wq_bf16, seq):
    T, D = h.shape
    heads = ATT_HEADS
    dh = D // heads
    tm = MOBA_BLOCK
    n_pos_tiles = seq // tm
    cos_b, sin_b, cos_r, sin_r, half = _rope_tables(seq, dh, tm)
    width = heads * dh
    est = 2 * (D * 2 * width + D * width) * 2 + 2 * tm * D * 4 + 6 * tm * width * 2 + 8 * tm * dh * 4
    pos = lambda i: (i % n_pos_tiles, 0, 0)
    const = lambda i: (0, 0)
    per_block = lambda i: (0, i, 0, 0)
    return pl.pallas_call(
        functools.partial(_kvq_kernel, heads=heads, dh=dh, half=half),
        grid=(T // tm,),
        in_specs=[
            pl.BlockSpec((tm, D), lambda i: (i, 0)),
            pl.BlockSpec((1, D), const),
            pl.BlockSpec((1, D), const),
            pl.BlockSpec((D, 2 * width), const),
            pl.BlockSpec((D, width), const),
            pl.BlockSpec((None, 1, dh), pos),
            pl.BlockSpec((None, 1, dh), pos),
            pl.BlockSpec((tm, dh), const),
            pl.BlockSpec((tm, dh), const),
        ],
        out_specs=[
            pl.BlockSpec((heads, None, tm, dh), per_block),
            pl.BlockSpec((heads, None, dh + VT_ONES_ROWS, tm), per_block),
            pl.BlockSpec((heads, None, dh, tm), per_block),
            pl.BlockSpec((None, 1, width), lambda i: (i, 0, 0)),
        ],
        out_shape=[
            jax.ShapeDtypeStruct((heads, T // tm, tm, dh), BF16),
            jax.ShapeDtypeStruct((heads, T // tm, dh + VT_ONES_ROWS, tm), BF16),
            jax.ShapeDtypeStruct((heads, T // tm, dh, tm), BF16),
            jax.ShapeDtypeStruct((T // tm, 1, width), F32),
        ],
        compiler_params=pltpu.CompilerParams(
            dimension_semantics=("arbitrary",), vmem_limit_bytes=_vmem_limit(est)),
        name="kvq_proj",
    )(h, kv_norm_g, q_norm_g, wkv_bf16, wq_bf16, cos_b, sin_b, cos_r, sin_r)


def _moba_kernel(qt_ref, k_ref, vt_ref, km_ref, o_ref, sel_ref, acc_ref, s0_ref, s1_ref, p_ref,
                 *, topk, group, dh):
    qi = pl.program_id(2)
    n_blk = km_ref.shape[0]
    blk = s0_ref.shape[1] // 2
    neg = -jnp.inf
    halves = (slice(0, blk), slice(blk, 2 * blk))

    def chunks(h):
        return [slice(c, c + MOBA_KEY_CHUNK) for c in range(h * blk, (h + 1) * blk, MOBA_KEY_CHUNK)]

    def stage_pair(s_buf, j):
        jp = jnp.minimum(j, n_blk - 2)
        for g in range(group):
            keys = k_ref[g, pl.ds(jp, 2)].reshape(2 * blk, dh)
            s_buf[g] = jnp.dot(keys, qt_ref[g], preferred_element_type=F32).astype(BF16)

    def stage_softmax(s_buf, h, ms, j):
        rows = chunks(h)
        new_m, alphas = [], []
        for g in range(group):
            part = s_buf[g, rows[0], :]
            for c in rows[1:]:
                part = jnp.maximum(part, s_buf[g, c, :])
            m_blk = jnp.max(part, axis=0, keepdims=True).astype(F32)
            if j is None:
                m_new = shift = m_blk
            else:
                bias = sel_ref[g, pl.ds(j, 1), :]
                m_new = jnp.maximum(ms[g], m_blk + bias)
                shift = m_new - bias
            shift = shift.astype(BF16)
            for c in rows:
                p_ref[g, c, :] = jnp.exp2(s_buf[g, c, :] - shift)
            new_m.append(m_new)
            alphas.append(None if j is None else jnp.exp2(ms[g] - m_new))
        return tuple(new_m), alphas

    def stage_pv(h, j, alphas):
        for g in range(group):
            pv = jnp.dot(vt_ref[g, j], p_ref[g, halves[h], :], preferred_element_type=F32)
            acc_ref[g] = pv if alphas[g] is None else alphas[g] * acc_ref[g] + pv

    for g in range(group):
        km = km_ref[:, g * dh:(g + 1) * dh]
        km_hi = km.astype(BF16)
        km_lo = (km - km_hi.astype(F32)).astype(BF16)
        gate = (jnp.dot(km_hi, qt_ref[g], preferred_element_type=F32)
                + jnp.dot(km_lo, qt_ref[g], preferred_element_type=F32))
        blk_id = lax.broadcasted_iota(jnp.int32, gate.shape, 0)
        gv = jnp.where(blk_id < qi, gate, neg)
        bias = jnp.full(gate.shape, neg, F32)
        for _ in range(topk):
            m = jnp.max(gv, axis=0, keepdims=True)
            first = jnp.min(jnp.where(gv == m, blk_id, n_blk), axis=0, keepdims=True)
            bias = jnp.where((blk_id == first) & (m > neg), 0.0, bias)
            gv = jnp.where(blk_id == first, neg, gv)
        sel_ref[g] = bias

    key_id = lax.broadcasted_iota(jnp.int32, (blk, blk), 0)
    qry_id = lax.broadcasted_iota(jnp.int32, (blk, blk), 1)
    for g in range(group):
        own = jnp.dot(k_ref[g, qi], qt_ref[g], preferred_element_type=F32)
        s1_ref[g, halves[0], :] = jnp.where(key_id <= qry_id, own, neg).astype(BF16)
    stage_pair(s0_ref, 0)
    ms, alphas = stage_softmax(s1_ref, 0, None, None)
    stage_pv(0, qi, alphas)

    def trip(t, ms, cur, nxt):
        stage_pair(nxt, 2 * t + 2)
        for h in range(2):
            ms, alphas = stage_softmax(cur, h, ms, 2 * t + h)
            stage_pv(h, 2 * t + h, alphas)
        return ms

    def body(t, ms):
        return lax.cond(t % 2 == 0,
                        lambda m: trip(t, m, s0_ref, s1_ref),
                        lambda m: trip(t, m, s1_ref, s0_ref), ms)

    lax.fori_loop(0, (qi + 1) // 2, body, ms)
    for g in range(group):
        acc = acc_ref[g]
        o_ref[:, g * dh:(g + 1) * dh] = (acc[:dh] / acc[dh:dh + 1]).astype(BF16).T


def _moba_attention(qt, k, vt, k_mean, batch, seq):
    heads, n_tiles, dh, blk = qt.shape
    dv_rows = vt.shape[2]
    n_blk = seq // blk
    topk = min(MOBA_TOPK, n_blk)
    width = heads * dh
    group = MOBA_HEAD_GROUP
    km = k_mean.reshape(batch, n_blk, width)
    per_seq = lambda b, h, i: (h, b, 0, 0)
    est = group * seq * (dh + dv_rows) * 2 + 2 * group * blk * blk * (2 + 2) + 6 * group * blk * dh * 4
    return pl.pallas_call(
        functools.partial(_moba_kernel, topk=topk, group=group, dh=dh),
        grid=(batch, heads // group, n_blk),
        in_specs=[
            pl.BlockSpec((group, None, dh, blk), lambda b, h, i: (h, b * n_blk + i, 0, 0)),
            pl.BlockSpec((group, n_blk, blk, dh), per_seq, pipeline_mode=pl.Buffered(1)),
            pl.BlockSpec((group, n_blk, dv_rows, blk), per_seq, pipeline_mode=pl.Buffered(1)),
            pl.BlockSpec((None, n_blk, group * dh), lambda b, h, i: (b, 0, h)),
        ],
        out_specs=pl.BlockSpec((blk, group * dh), lambda b, h, i: (b * n_blk + i, h)),
        out_shape=jax.ShapeDtypeStruct((n_tiles * blk, width), BF16),
        scratch_shapes=[pltpu.VMEM((group, n_blk, blk), F32), pltpu.VMEM((group, dv_rows, blk), F32),
                        pltpu.VMEM((group, 2 * blk, blk), BF16), pltpu.VMEM((group, 2 * blk, blk), BF16),
                        pltpu.VMEM((group, 2 * blk, blk), BF16)],
        compiler_params=pltpu.CompilerParams(
            dimension_semantics=("arbitrary", "arbitrary", "arbitrary"), vmem_limit_bytes=_vmem_limit(est)),
        name="moba_attention",
    )(qt, k, vt, km)


def kernel(x, ret_norm, ret_w_in, ret_w_out, kv_norm, w_kv, attn_norm, w_q, w_o, ffn_norm, router_group_w, router_group_b, router_expert_w, router_expert_b, expert_w_gate, expert_w_up, expert_w_down, final_norm):
    B, S, D = x.shape
    T = B * S
    assert S % MOBA_BLOCK == 0 and S % RET_KERNEL_CHUNK == 0 and T % MOE_TILE == 0
    assert ret_norm.shape[0] == 1 and attn_norm.shape[0] == 1 and ffn_norm.shape[0] == 2
    h = x.reshape(T, D)
    final_g = final_norm[None, :]

    def proj_router(y, w, resid, layer, name):
        return _proj_router(y, w.astype(BF16), resid, ffn_norm[layer][None, :], router_group_w[layer],
                            router_group_b[layer], router_expert_w[layer], router_expert_b[layer], name)

    def moe(h_aug, route, counts, layer, last):
        return _moe_layer(h_aug, route, counts, ffn_norm[layer][None, :], final_g,
                          expert_w_gate, expert_w_up, expert_w_down, layer, final_norm=last)

    q, kt, v, gate = _ret_in_proj(h, ret_norm[0][None, :], ret_w_in[0].astype(BF16), S)
    y = _ret_core(q, kt, v, gate, B, S)
    h = moe(*proj_router(y, ret_w_out[0], h, 0, "ret_out_proj_router"), 0, False)

    k2, vt2, qt2, k_mean = _kvq_proj(h, kv_norm[None, :], attn_norm[0][None, :],
                                     w_kv.astype(BF16), w_q[0].astype(BF16), S)
    o = _moba_attention(qt2, k2, vt2, k_mean, B, S)
    h = moe(*proj_router(o, w_o[0], h, 1, "attn_out_proj_router"), 1, True)
    return h.reshape(B, S, D)
```

```python
import functools

import jax
import jax.numpy as jnp
from jax import lax
from jax.experimental import pallas as pl
from jax.experimental.pallas import tpu as pltpu

F32 = jnp.float32
BF16 = jnp.bfloat16

NORM_EPS = 1e-6
RET_HEADS = 4
RET_ROT_BASE = 10000.0
ATT_HEADS = 8
ROPE_FRACTION = 4
ROPE_THETA = 500000.0
MOBA_BLOCK = 256
MOBA_TOPK = 3
N_GROUPS = 4
EXPERTS_PER_GROUP = 4
PAIR_LO = (0, 0, 1, 1, 0, 2)
PAIR_HI = (1, 2, 2, 3, 3, 3)
N_PAIRS = len(PAIR_LO)
N_CLASSES = N_GROUPS * N_PAIRS

LANES = 128
SUBLANES = 8
RET_KERNEL_CHUNK = 256
RET_CHUNKS_PER_TILE = 2
MOE_TILE = 256
ROUTE_ROWS = 32
MOBA_HEAD_GROUP = 8
KVQ_BLOCKS_PER_TILE = 2
MOBA_KEY_CHUNK = 32
VT_ONES_ROWS = 16
LOG2_E = 1.4426950408889634
V7X_VMEM_BYTES = 64 * 1024 * 1024
V7X_VMEM_RESERVED_BYTES = 8 * 1024 * 1024
DEFAULT_SCOPED_VMEM_BYTES = 32 * 1024 * 1024


def _vmem_limit(estimate_bytes):
    wanted = max(DEFAULT_SCOPED_VMEM_BYTES, estimate_bytes * 5 // 4)
    return int(min(V7X_VMEM_BYTES - V7X_VMEM_RESERVED_BYTES, wanted))


def _rms_scale(x):
    return lax.rsqrt(jnp.mean(x * x, axis=-1, keepdims=True) + NORM_EPS)


def _silu(a):
    return a * jax.nn.sigmoid(a)


def _ret_in_kernel(x_ref, g_ref, w_ref, cb_ref, sb_ref, cr_ref, sr_ref, qdec_ref, kdec_ref,
                   q_ref, kt_ref, v_ref, gate_ref, *, heads, dk, dv):
    x = x_ref[...]
    xn = (x * _rms_scale(x) * g_ref[...]).astype(BF16)
    cos, sin = _tile_cos_sin(cb_ref, sb_ref, cr_ref, sr_ref)
    half = dk // 2

    def rotated(col0, h, dec):
        p = jnp.dot(xn, w_ref[:, col0 + h * dk:col0 + (h + 1) * dk], preferred_element_type=F32)
        x1 = p[:, :half]
        x2 = p[:, half:]
        cos_h = cos * dec
        sin_h = sin * dec
        return x1 * cos_h - x2 * sin_h, x2 * cos_h + x1 * sin_h

    for h in range(heads):
        lo, hi = rotated(0, h, qdec_ref[h])
        q_ref[:, h * dk:h * dk + half] = lo.astype(BF16)
        q_ref[:, h * dk + half:(h + 1) * dk] = hi.astype(BF16)
        lo, hi = rotated(heads * dk, h, kdec_ref[h])
        kt_ref[h, :half] = lo.astype(BF16).T
        kt_ref[h, half:] = hi.astype(BF16).T
    v0 = 2 * heads * dk
    g0 = v0 + heads * dv
    for h in range(heads):
        v_ref[:, h * dv:(h + 1) * dv] = jnp.dot(
            xn, w_ref[:, v0 + h * dv:v0 + (h + 1) * dv], preferred_element_type=F32).astype(BF16)
        gate_ref[:, h * dv:(h + 1) * dv] = jnp.dot(
            xn, w_ref[:, g0 + h * dv:g0 + (h + 1) * dv], preferred_element_type=F32).astype(BF16)


def _ret_in_proj(h, norm_g, w_bf16, seq, tm=256):
    T, D = h.shape
    heads = RET_HEADS
    dk = D // heads
    dv = 2 * dk
    n_cols = w_bf16.shape[1]
    n_pos_tiles = seq // tm
    half = dk // 2
    inv = 1.0 / (RET_ROT_BASE ** (jnp.arange(half, dtype=F32) / half))
    cos_b, sin_b, cos_r, sin_r = _angle_tables(inv, seq, tm)
    assert tm == RET_KERNEL_CHUNK
    log_gamma = _ret_log_gamma(heads)
    idx = jnp.arange(tm, dtype=F32)
    q_dec = jnp.broadcast_to(jnp.exp(log_gamma[:, None] * idx)[:, :, None], (heads, tm, half))
    k_dec = jnp.broadcast_to((jnp.exp(-log_gamma[:, None] * idx) * dk ** -0.5)[:, :, None], (heads, tm, half))
    pos = lambda i: (i % n_pos_tiles, 0, 0)
    est = 2 * D * n_cols * 2 + 2 * tm * D * 4 + 2 * tm * n_cols * 2 + (4 + 4 * heads) * tm * LANES * 4
    return pl.pallas_call(
        functools.partial(_ret_in_kernel, heads=heads, dk=dk, dv=dv),
        grid=(T // tm,),
        in_specs=[
            pl.BlockSpec((tm, D), lambda i: (i, 0)),
            pl.BlockSpec((1, D), lambda i: (0, 0)),
            pl.BlockSpec((D, n_cols), lambda i: (0, 0)),
            pl.BlockSpec((None, 1, half), pos),
            pl.BlockSpec((None, 1, half), pos),
            pl.BlockSpec((tm, half), lambda i: (0, 0)),
            pl.BlockSpec((tm, half), lambda i: (0, 0)),
            pl.BlockSpec((heads, tm, half), lambda i: (0, 0, 0)),
            pl.BlockSpec((heads, tm, half), lambda i: (0, 0, 0)),
        ],
        out_specs=[
            pl.BlockSpec((tm, heads * dk), lambda i: (i, 0)),
            pl.BlockSpec((heads, None, dk, tm), lambda i: (0, i, 0, 0)),
            pl.BlockSpec((tm, heads * dv), lambda i: (i, 0)),
            pl.BlockSpec((tm, heads * dv), lambda i: (i, 0)),
        ],
        out_shape=[
            jax.ShapeDtypeStruct((T, heads * dk), BF16),
            jax.ShapeDtypeStruct((heads, T // tm, dk, tm), BF16),
            jax.ShapeDtypeStruct((T, heads * dv), BF16),
            jax.ShapeDtypeStruct((T, heads * dv), BF16),
        ],
        compiler_params=pltpu.CompilerParams(
            dimension_semantics=("arbitrary",), vmem_limit_bytes=_vmem_limit(est)),
        name="ret_in_proj",
    )(h, norm_g, w_bf16, cos_b, sin_b, cos_r, sin_r, q_dec, k_dec)


def _ret_log_gamma(heads):
    return jnp.log1p(-jnp.power(2.0, -5.0 - jnp.arange(heads, dtype=F32)))


def _ret_core_kernel(q_ref, kt_ref, v_ref, g_ref, cd_ref, y_ref, state_ref, *, heads):
    @pl.when(pl.program_id(1) == 0)
    def _():
        state_ref[...] = jnp.zeros_like(state_ref)

    n_sub, cc = kt_ref.shape[1], kt_ref.shape[3]
    dk = q_ref.shape[1] // heads
    dv = v_ref.shape[1] // heads
    causal = lax.broadcasted_iota(jnp.int32, (cc, cc), 0) >= lax.broadcasted_iota(jnp.int32, (cc, cc), 1)
    for c in range(n_sub):
        rows = slice(c * cc, (c + 1) * cc)
        for h in range(heads):
            q = q_ref[rows, h * dk:(h + 1) * dk]
            kt = kt_ref[h, c]
            v = v_ref[rows, h * dv:(h + 1) * dv]
            s = jnp.where(causal, jnp.dot(q, kt, preferred_element_type=F32), 0.0)
            u = state_ref[h]
            o = (jnp.dot(s.astype(BF16), v, preferred_element_type=F32)
                 + jnp.dot(q, u.astype(BF16), preferred_element_type=F32))
            state_ref[h] = cd_ref[h] * (u + jnp.dot(kt, v, preferred_element_type=F32))
            o = o * _rms_scale(o)
            gate = g_ref[rows, h * dv:(h + 1) * dv].astype(F32)
            y_ref[rows, h * dv:(h + 1) * dv] = (_silu(gate) * o).astype(BF16)


def _ret_core(q, kt, v, gate, batch, seq):
    T = q.shape[0]
    heads = RET_HEADS
    dk = q.shape[1] // heads
    dv = v.shape[1] // heads
    cc = RET_KERNEL_CHUNK
    n_sub = RET_CHUNKS_PER_TILE
    tm = n_sub * cc
    assert seq % tm == 0
    nc = seq // tm
    chunk_decay = jnp.broadcast_to(jnp.exp(_ret_log_gamma(heads) * cc)[:, None, None], (heads, 1, dv))
    row = lambda b, c: (b * nc + c, 0)
    return pl.pallas_call(
        functools.partial(_ret_core_kernel, heads=heads),
        grid=(batch, nc),
        in_specs=[
            pl.BlockSpec((tm, heads * dk), row),
            pl.BlockSpec((heads, n_sub, dk, cc), lambda b, c: (0, b * nc + c, 0, 0)),
            pl.BlockSpec((tm, heads * dv), row),
            pl.BlockSpec((tm, heads * dv), row),
            pl.BlockSpec((heads, 1, dv), lambda b, c: (0, 0, 0)),
        ],
        out_specs=pl.BlockSpec((tm, heads * dv), row),
        out_shape=jax.ShapeDtypeStruct((T, heads * dv), BF16),
        scratch_shapes=[pltpu.VMEM((heads, dk, dv), F32)],
        compiler_params=pltpu.CompilerParams(dimension_semantics=("arbitrary", "arbitrary")),
        name="ret_core",
    )(q, kt, v, gate, chunk_decay)


def _proj_router_kernel(y_ref, w_ref, r_ref, g_ref, wr_ref, br_ref, haug_ref, route_ref, counts_ref, run_ref,
                        *, groups, experts, d_model):
    @pl.when(pl.program_id(0) == 0)
    def _():
        run_ref[...] = jnp.zeros_like(run_ref)

    h = r_ref[...] + jnp.dot(y_ref[...], w_ref[...], preferred_element_type=F32)
    xn = h * _rms_scale(h) * g_ref[...]
    xh = xn.astype(BF16)
    xl = (xn - xh.astype(F32)).astype(BF16)
    hi_both = jnp.dot(xh, wr_ref[...], preferred_element_type=F32)
    lo_hi = jnp.dot(xl, wr_ref[:, :LANES], preferred_element_type=F32)
    logits = hi_both[:, :LANES] + (hi_both[:, LANES:] + lo_hi) + br_ref[...]
    lt = logits.T[:ROUTE_ROWS]
    tm = lt.shape[1]
    row = lax.broadcasted_iota(jnp.int32, lt.shape, 0)
    neg = -jnp.inf

    def argmax_first(vals):
        m = jnp.max(vals, axis=0, keepdims=True)
        first = jnp.min(jnp.where(vals == m, row, ROUTE_ROWS), axis=0, keepdims=True)
        return m, first

    gl = jnp.where(row < groups, lt, neg)
    gmax, gidx = argmax_first(gl)
    gsum = jnp.sum(jnp.where(row < groups, jnp.exp(gl - gmax), 0.0), axis=0, keepdims=True)
    g_val = 1.0 / gsum
    e0 = groups + experts * gidx
    el = jnp.where((row >= e0) & (row < e0 + experts), lt, neg)
    m1, i1 = argmax_first(el)
    m2, i2 = argmax_first(jnp.where(row == i1, neg, el))
    t = jnp.exp(m2 - m1)
    w1 = 1.0 / (1.0 + t)
    w2 = t / (1.0 + t)
    first_is_lo = i1 < i2
    lo = jnp.minimum(i1, i2) - e0
    hi = jnp.maximum(i1, i2) - e0
    w_lo = g_val * jnp.where(first_is_lo, w1, w2)
    w_hi = g_val * jnp.where(first_is_lo, w2, w1)
    lex = (lo * (7 - lo)) // 2 + (hi - lo - 1)
    pair = jnp.where(lex == 2, 4, jnp.where(lex == 3, 2, jnp.where(lex == 4, 3, lex)))
    cls = gidx * N_PAIRS + pair

    onehot = row == cls
    oh = jnp.where(onehot, 1.0, 0.0)
    before = lax.broadcasted_iota(jnp.int32, (tm, tm), 0) < lax.broadcasted_iota(jnp.int32, (tm, tm), 1)
    prefix = jnp.dot(oh.astype(BF16), jnp.where(before, 1.0, 0.0).astype(BF16), preferred_element_type=F32)
    rank = jnp.sum(jnp.where(onehot, prefix + run_ref[:, 0:1], 0.0), axis=0, keepdims=True)
    run_new = run_ref[...] + jnp.sum(oh, axis=1, keepdims=True)
    run_ref[...] = run_new
    counts_ref[...] = run_new

    r8 = lax.broadcasted_iota(jnp.int32, route_ref.shape, 0)
    route_ref[...] = jnp.where(r8 == 0, cls, jnp.where(r8 == 1, rank.astype(jnp.int32), 0))
    rl = lax.broadcasted_iota(jnp.int32, (LANES, tm), 0)
    meta_t = jnp.where(rl == 0, w_lo, jnp.where(rl == 1, w_hi, 0.0))
    haug_ref[:, :d_model] = h
    haug_ref[:, d_model:] = meta_t.T


def _proj_router(y, w_bf16, resid, norm_g, w_rg, b_rg, w_re, b_re, name, tm=512):
    T, K = y.shape
    D = w_bf16.shape[1]
    G, E = N_GROUPS, EXPERTS_PER_GROUP
    n_used = G + G * E
    wr = jnp.concatenate([w_rg, jnp.transpose(w_re, (1, 0, 2)).reshape(D, G * E)], axis=1)
    wr = jnp.pad(wr, ((0, 0), (0, LANES - n_used)))
    wr_hi = wr.astype(BF16)
    wr = jnp.concatenate([wr_hi, (wr - wr_hi.astype(F32)).astype(BF16)], axis=1)
    br = jnp.pad(jnp.concatenate([b_rg, b_re.reshape(G * E)]), (0, LANES - n_used))[None, :]
    est = 2 * K * D * 2 + 2 * tm * K * 2 + 6 * tm * D * 4 + 2 * D * LANES * 4 + 4 * tm * tm * 4
    const = lambda i: (0, 0)
    return pl.pallas_call(
        functools.partial(_proj_router_kernel, groups=G, experts=E, d_model=D),
        grid=(T // tm,),
        in_specs=[
            pl.BlockSpec((tm, K), lambda i: (i, 0)),
            pl.BlockSpec((K, D), const),
            pl.BlockSpec((tm, D), lambda i: (i, 0)),
            pl.BlockSpec((1, D), const),
            pl.BlockSpec((D, 2 * LANES), const),
            pl.BlockSpec((1, LANES), const),
        ],
        out_specs=[
            pl.BlockSpec((tm, D + LANES), lambda i: (i, 0)),
            pl.BlockSpec((8, tm), lambda i: (0, i)),
            pl.BlockSpec((ROUTE_ROWS, LANES), const),
        ],
        out_shape=[
            jax.ShapeDtypeStruct((T, D + LANES), F32),
            jax.ShapeDtypeStruct((8, T), jnp.int32),
            jax.ShapeDtypeStruct((ROUTE_ROWS, LANES), F32),
        ],
        scratch_shapes=[pltpu.VMEM((ROUTE_ROWS, LANES), F32)],
        compiler_params=pltpu.CompilerParams(
            dimension_semantics=("arbitrary",), vmem_limit_bytes=_vmem_limit(est)),
        name=name,
    )(y, w_bf16, resid, norm_g, wr, br)


def _tile_schedule(counts_blk, tm, n_tiles):
    counts = counts_blk[:N_CLASSES, 0].astype(jnp.int32)
    tiles_per = (counts + tm - 1) // tm
    tile_end = jnp.cumsum(tiles_per)
    tile_begin = tile_end - tiles_per
    cstart = jnp.cumsum(counts) - counts
    t = jnp.arange(n_tiles, dtype=jnp.int32)
    live = t < tile_end[-1]
    tq = jnp.where(live, t, tile_end[-1] - 1)
    tcls = jnp.sum((tile_end[None, :] <= tq[:, None]).astype(jnp.int32), axis=1)
    onehot = tcls[:, None] == jnp.arange(N_CLASSES, dtype=jnp.int32)[None, :]
    pick = lambda table: jnp.sum(jnp.where(onehot, jnp.asarray(table, jnp.int32)[None, :], 0), axis=1)
    k = tq - pick(tile_begin)
    nvalid = jnp.where(live, jnp.clip(pick(counts) - k * tm, 0, tm), 0).astype(jnp.int32)
    base = jnp.where(live, pick(cstart) + k * tm, 0).astype(jnp.int32)
    tile_g = pick([c // N_PAIRS for c in range(N_CLASSES)])
    tile_lo = pick([PAIR_LO[c % N_PAIRS] for c in range(N_CLASSES)])
    tile_hi = pick([PAIR_HI[c % N_PAIRS] for c in range(N_CLASSES)])
    cstart_pad = jnp.pad(cstart, (0, ROUTE_ROWS - N_CLASSES)).astype(jnp.int32)
    return tile_g, tile_lo, tile_hi, nvalid, base, cstart_pad


def _moe_kernel(tg_ref, tlo_ref, thi_ref, tnv_ref, tbase_ref, cs_ref,
                h_hbm, route_ref, gn_ref, fn_ref, wg_lo, wu_lo, wd_lo, wg_hi, wu_hi, wd_hi,
                out_hbm, rt_ref, pos_smem, pos_vmem, x0, x1, x2, o0, o1, o2, w_in_bf, w_out_bf,
                gsem, ssem, psem, *, tm, d_model, n_tok, final_norm):
    i = pl.program_id(0)
    xbufs = (x0, x1, x2)
    obufs = (o0, o1, o2)
    depth = len(xbufs)

    def issue_gather(t, sl):
        base = tbase_ref[t]
        for r in range(tm):
            tok = rt_ref[base + r]
            pltpu.make_async_copy(h_hbm.at[pl.ds(tok, 1)], xbufs[sl].at[pl.ds(r, 1)], gsem.at[sl]).start()

    def scatter_row(sl, r, tok):
        return pltpu.make_async_copy(obufs[sl].at[pl.ds(r, 1)], out_hbm.at[pl.ds(tok, 1)], ssem.at[sl])

    def issue_scatter(sl, base, n_valid):
        for r in range(tm):
            @pl.when(r < n_valid)
            def _():
                scatter_row(sl, r, rt_ref[base + r]).start()

    def wait_gather(sl):
        pltpu.make_async_copy(h_hbm.at[pl.ds(0, tm)], xbufs[sl], gsem.at[sl]).wait()

    def wait_scatter(sl, n_valid):
        aligned = pl.multiple_of((n_valid // SUBLANES) * SUBLANES, SUBLANES)

        @pl.when(aligned > 0)
        def _():
            rows = pl.ds(0, aligned)
            pltpu.make_async_copy(obufs[sl].at[rows], out_hbm.at[rows], ssem.at[sl]).wait()

        def single(r, c):
            scatter_row(sl, r, 0).wait()
            return c
        lax.fori_loop(aligned, n_valid, single, 0)

    def live(t):
        return tnv_ref[jnp.maximum(t, 0)] > 0

    @pl.when(i == 0)
    def _():
        cls2 = route_ref[0]
        pos = route_ref[1]
        for c in range(N_CLASSES):
            pos = pos + jnp.where(cls2 == c, cs_ref[c], 0)
        pos_vmem[...] = pos
        to_smem = pltpu.make_async_copy(pos_vmem, pos_smem, psem)
        to_smem.start()
        to_smem.wait()

        def place(row, c):
            for col in range(LANES):
                rt_ref[pos_smem[row, col]] = row * LANES + col
            return c
        lax.fori_loop(0, n_tok // LANES, place, 0)
        for r in range(tm):
            rt_ref[n_tok + r] = 0
        issue_gather(0, 0)
        issue_gather(1, 1)

    prev = jnp.maximum(i - 1, 0)
    group_changed = (i == 0) | (tg_ref[i] != tg_ref[prev])
    for which, t_ref, (wg, wu, wd) in ((0, tlo_ref, (wg_lo, wu_lo, wd_lo)), (1, thi_ref, (wg_hi, wu_hi, wd_hi))):
        @pl.when(group_changed | (t_ref[i] != t_ref[prev]))
        def _():
            w_in_bf[2 * which] = wg[...].astype(BF16)
            w_in_bf[2 * which + 1] = wu[...].astype(BF16)
            w_out_bf[which] = wd[...].astype(BF16)

    def tile(cur):
        far = (cur + depth - 1) % depth
        @pl.when((i <= 1) | live(i - 2))
        def _():
            wait_gather(cur)

        @pl.when((i >= depth) & live(i - depth))
        def _():
            wait_scatter(cur, tnv_ref[jnp.maximum(i - depth, 0)])

        prev_base = tbase_ref[prev]
        prev_valid = jnp.where(i == 0, 0, tnv_ref[prev])

        @pl.when(live(i))
        def _():
            issue_scatter(far, prev_base, prev_valid)
            issue_gather(i + 2, far)
            xa = xbufs[cur][...]
            x = xa[:, :d_model]
            xn = (x * _rms_scale(x) * gn_ref[...]).astype(BF16)

            def expert(which, w):
                a = jnp.dot(xn, w_in_bf[2 * which], preferred_element_type=F32)
                b = jnp.dot(xn, w_in_bf[2 * which + 1], preferred_element_type=F32)
                return jnp.dot((_silu(a) * b * w).astype(BF16), w_out_bf[which], preferred_element_type=F32)

            y = expert(0, xa[:, d_model:d_model + 1]) + expert(1, xa[:, d_model + 1:d_model + 2])
            obufs[cur][...] = x + y
            if final_norm:
                for r0 in range(0, tm, 4 * SUBLANES):
                    rows = slice(r0, r0 + 4 * SUBLANES)
                    o = obufs[cur][rows, :]
                    obufs[cur][rows, :] = o * _rms_scale(o) * fn_ref[...]

        @pl.when(jnp.logical_not(live(i)) & (i >= 1) & live(i - 1))
        def _():
            issue_scatter(far, prev_base, prev_valid)

    for residue in range(depth):
        @pl.when(i % depth == residue)
        def _():
            tile(residue)


def _moe_layer(h_aug, route, counts_blk, ffn_norm_g, final_norm_g, wg, wu, wd, layer, final_norm):
    T, DA = h_aug.shape
    D = DA - LANES
    F = wg.shape[-1]
    tm = MOE_TILE
    assert T % (LANES * SUBLANES) == 0 and T >= 2 * tm
    n_tiles = T // tm + N_CLASSES + 3
    tile_g, tile_lo, tile_hi, nvalid, base, cstart = _tile_schedule(counts_blk, tm, n_tiles)
    route2d = route[:2].reshape(2, T // LANES, LANES)

    def w_spec(shape, which):
        def index(i, tg, tlo, thi, *_):
            return (layer, tg[i], (tlo if which == 0 else thi)[i], 0, 0)
        return pl.BlockSpec((None, None, None) + shape, index)

    est = 2 * 6 * D * F * 4 + 6 * D * F * 2 + 3 * tm * DA * 4 + 3 * tm * D * 4 + 8 * tm * F * 4
    return pl.pallas_call(
        functools.partial(_moe_kernel, tm=tm, d_model=D, n_tok=T, final_norm=final_norm),
        grid_spec=pltpu.PrefetchScalarGridSpec(
            num_scalar_prefetch=6,
            grid=(n_tiles,),
            in_specs=[
                pl.BlockSpec(memory_space=pl.ANY),
                pl.BlockSpec((2, T // LANES, LANES), lambda i, *_: (0, 0, 0)),
                pl.BlockSpec((1, D), lambda i, *_: (0, 0)),
                pl.BlockSpec((1, D), lambda i, *_: (0, 0)),
                w_spec((D, F), 0), w_spec((D, F), 0), w_spec((F, D), 0),
                w_spec((D, F), 1), w_spec((D, F), 1), w_spec((F, D), 1),
            ],
            out_specs=pl.BlockSpec(memory_space=pl.ANY),
            scratch_shapes=[
                pltpu.SMEM((T + tm,), jnp.int32),
                pltpu.SMEM((T // LANES, LANES), jnp.int32),
                pltpu.VMEM((T // LANES, LANES), jnp.int32),
                pltpu.VMEM((tm, DA), F32), pltpu.VMEM((tm, DA), F32), pltpu.VMEM((tm, DA), F32),
                pltpu.VMEM((tm, D), F32), pltpu.VMEM((tm, D), F32), pltpu.VMEM((tm, D), F32),
                pltpu.VMEM((4, D, F), BF16),
                pltpu.VMEM((2, F, D), BF16),
                pltpu.SemaphoreType.DMA((3,)),
                pltpu.SemaphoreType.DMA((3,)),
                pltpu.SemaphoreType.DMA(()),
            ],
        ),
        out_shape=jax.ShapeDtypeStruct((T, D), F32),
        compiler_params=pltpu.CompilerParams(
            dimension_semantics=("arbitrary",), vmem_limit_bytes=_vmem_limit(est)),
        name="moe_experts",
    )(tile_g, tile_lo, tile_hi, nvalid, base, cstart,
      h_aug, route2d, ffn_norm_g, final_norm_g, wg, wu, wd, wg, wu, wd)


def _partial_rotary(xh, cosf, sinp, sinm, half):
    lanes = xh.shape[-1]
    return xh * cosf + pltpu.roll(xh, half, 1) * sinp + pltpu.roll(xh, lanes - half, 1) * sinm


def _tile_cos_sin(cb_ref, sb_ref, cr_ref, sr_ref):
    cb, sb, cr, sr = cb_ref[...], sb_ref[...], cr_ref[...], sr_ref[...]
    return cb * cr - sb * sr, sb * cr + cb * sr


def _kvq_kernel(h_ref, gkv_ref, gq_ref, wkv_ref, wq_ref, cb_ref, sb_ref, cr_ref, sr_ref,
                k_ref, vt_ref, qt_ref, km_ref, *, heads, dh, half):
    x = h_ref[...]
    xr = x * _rms_scale(x)
    xkv = (xr * gkv_ref[...]).astype(BF16)
    xq = (xr * gq_ref[...]).astype(BF16)
    cosf, sin_all = _tile_cos_sin(cb_ref, sb_ref, cr_ref, sr_ref)
    lane = lax.broadcasted_iota(jnp.int32, sin_all.shape, 1)
    sinp = jnp.where(lane >= half, sin_all, 0.0)
    sinm = jnp.where(lane < half, -sin_all, 0.0)
    width = heads * dh
    q_scale = dh ** -0.5 * LOG2_E
    pair_w = 2 * dh
    blk = k_ref.shape[2]
    n_sub = x.shape[0] // blk
    for c0 in range(0, width, pair_w):
        kk = jnp.dot(xkv, wkv_ref[:, c0:c0 + pair_w], preferred_element_type=F32)
        vv = jnp.dot(xkv, wkv_ref[:, width + c0:width + c0 + pair_w], preferred_element_type=F32)
        qq = jnp.dot(xq, wq_ref[:, c0:c0 + pair_w], preferred_element_type=F32)
        for u in range(2):
            hh = c0 // dh + u
            cols = slice(hh * dh, (hh + 1) * dh)
            kh = _partial_rotary(kk[:, u * dh:(u + 1) * dh], cosf, sinp, sinm, half)
            qh = _partial_rotary(qq[:, u * dh:(u + 1) * dh], cosf, sinp, sinm, half) * q_scale
            vh = vv[:, u * dh:(u + 1) * dh]
            for b in range(n_sub):
                rows = slice(b * blk, (b + 1) * blk)
                k_ref[hh, b] = kh[rows].astype(BF16)
                km_ref[b, :, cols] = jnp.mean(kh[rows], axis=0, keepdims=True)
                vt_ref[hh, b, :dh] = vh[rows].astype(BF16).T
                vt_ref[hh, b, dh:] = jnp.ones((VT_ONES_ROWS, blk), BF16)
                qt_ref[hh, b] = qh[rows].astype(BF16).T


def _angle_tables(inv_freq, seq, tm):
    base = jnp.arange(0, seq, tm, dtype=F32)[:, None] * inv_freq[None, :]
    offs = jnp.arange(tm, dtype=F32)[:, None] * inv_freq[None, :]
    return jnp.cos(base)[:, None, :], jnp.sin(base)[:, None, :], jnp.cos(offs), jnp.sin(offs)


def _rope_tables(seq, dh, tm):
    rot = dh // ROPE_FRACTION
    half = rot // 2
    inv = 1.0 / (ROPE_THETA ** (jnp.arange(half, dtype=F32) / half))
    inv_lanes = jnp.concatenate([inv, inv, jnp.zeros((dh - rot,), F32)])
    return _angle_tables(inv_lanes, seq, tm) + (half,)


def _kvq_proj(h, kv_norm_g, q_norm_g, wkv_bf16, wq_bf16, seq):
    T, D = h.shape
    heads = ATT_HEADS
    dh = D // heads
    blk = MOBA_BLOCK
    n_sub = KVQ_BLOCKS_PER_TILE
    tm = n_sub * blk
    assert seq % tm == 0
    n_pos_tiles = seq // tm
    cos_b, sin_b, cos_r, sin_r, half = _rope_tables(seq, dh, tm)
    width = heads * dh
    est = 2 * (D * 2 * width + D * width) * 2 + 2 * tm * D * 4 + 6 * tm * width * 2 + 8 * tm * dh * 4
    pos = lambda i: (i % n_pos_tiles, 0, 0)
    const = lambda i: (0, 0)
    per_block = lambda i: (0, i, 0, 0)
    return pl.pallas_call(
        functools.partial(_kvq_kernel, heads=heads, dh=dh, half=half),
        grid=(T // tm,),
        in_specs=[
            pl.BlockSpec((tm, D), lambda i: (i, 0)),
            pl.BlockSpec((1, D), const),
            pl.BlockSpec((1, D), const),
            pl.BlockSpec((D, 2 * width), const),
            pl.BlockSpec((D, width), const),
            pl.BlockSpec((None, 1, dh), pos),
            pl.BlockSpec((None, 1, dh), pos),
            pl.BlockSpec((tm, dh), const),
            pl.BlockSpec((tm, dh), const),
        ],
        out_specs=[
            pl.BlockSpec((heads, n_sub, blk, dh), per_block),
            pl.BlockSpec((heads, n_sub, dh + VT_ONES_ROWS, blk), per_block),
            pl.BlockSpec((heads, n_sub, dh, blk), per_block),
            pl.BlockSpec((n_sub, 1, width), lambda i: (i, 0, 0)),
        ],
        out_shape=[
            jax.ShapeDtypeStruct((heads, T // blk, blk, dh), BF16),
            jax.ShapeDtypeStruct((heads, T // blk, dh + VT_ONES_ROWS, blk), BF16),
            jax.ShapeDtypeStruct((heads, T // blk, dh, blk), BF16),
            jax.ShapeDtypeStruct((T // blk, 1, width), F32),
        ],
        compiler_params=pltpu.CompilerParams(
            dimension_semantics=("arbitrary",), vmem_limit_bytes=_vmem_limit(est)),
        name="kvq_proj",
    )(h, kv_norm_g, q_norm_g, wkv_bf16, wq_bf16, cos_b, sin_b, cos_r, sin_r)


def _moba_kernel(qt_ref, k_ref, vt_ref, km_ref, o_ref, sel_ref, acc_ref, s0_ref, s1_ref, p_ref,
                 *, topk, group, dh):
    qi = pl.program_id(2)
    n_blk = km_ref.shape[0]
    blk = s0_ref.shape[1] // 2
    neg = -jnp.inf
    halves = (slice(0, blk), slice(blk, 2 * blk))

    def chunks(h):
        return [slice(c, c + MOBA_KEY_CHUNK) for c in range(h * blk, (h + 1) * blk, MOBA_KEY_CHUNK)]

    def stage_pair(s_buf, j):
        jp = jnp.minimum(j, n_blk - 2)
        for g in range(group):
            keys = k_ref[g, pl.ds(jp, 2)].reshape(2 * blk, dh)
            s_buf[g] = jnp.dot(keys, qt_ref[g], preferred_element_type=F32).astype(BF16)

    def stage_softmax(s_buf, h, ms, j):
        rows = chunks(h)
        new_m, alphas = [], []
        for g in range(group):
            part = s_buf[g, rows[0], :]
            for c in rows[1:]:
                part = jnp.maximum(part, s_buf[g, c, :])
            m_blk = jnp.max(part, axis=0, keepdims=True).astype(F32)
            if j is None:
                m_new = shift = m_blk
            else:
                bias = sel_ref[g, pl.ds(j, 1), :]
                m_new = jnp.maximum(ms[g], m_blk + bias)
                shift = m_new - bias
            shift = shift.astype(BF16)
            for c in rows:
                p_ref[g, c, :] = jnp.exp2(s_buf[g, c, :] - shift)
            new_m.append(m_new)
            alphas.append(None if j is None else jnp.exp2(ms[g] - m_new))
        return tuple(new_m), alphas

    def stage_pv(h, j, alphas):
        for g in range(group):
            pv = jnp.dot(vt_ref[g, j], p_ref[g, halves[h], :], preferred_element_type=F32)
            acc_ref[g] = pv if alphas[g] is None else alphas[g] * acc_ref[g] + pv

    for g in range(group):
        km = km_ref[:, g * dh:(g + 1) * dh]
        km_hi = km.astype(BF16)
        km_lo = (km - km_hi.astype(F32)).astype(BF16)
        gate = (jnp.dot(km_hi, qt_ref[g], preferred_element_type=F32)
                + jnp.dot(km_lo, qt_ref[g], preferred_element_type=F32))
        blk_id = lax.broadcasted_iota(jnp.int32, gate.shape, 0)
        gv = jnp.where(blk_id < qi, gate, neg)
        bias = jnp.full(gate.shape, neg, F32)
        for _ in range(topk):
            m = jnp.max(gv, axis=0, keepdims=True)
            first = jnp.min(jnp.where(gv == m, blk_id, n_blk), axis=0, keepdims=True)
            bias = jnp.where((blk_id == first) & (m > neg), 0.0, bias)
            gv = jnp.where(blk_id == first, neg, gv)
        sel_ref[g] = bias

    key_id = lax.broadcasted_iota(jnp.int32, (blk, blk), 0)
    qry_id = lax.broadcasted_iota(jnp.int32, (blk, blk), 1)
    for g in range(group):
        own = jnp.dot(k_ref[g, qi], qt_ref[g], preferred_element_type=F32)
        s1_ref[g, halves[0], :] = jnp.where(key_id <= qry_id, own, neg).astype(BF16)
    stage_pair(s0_ref, 0)
    ms, alphas = stage_softmax(s1_ref, 0, None, None)
    stage_pv(0, qi, alphas)

    def trip(t, ms, cur, nxt):
        stage_pair(nxt, 2 * t + 2)
        for h in range(2):
            ms, alphas = stage_softmax(cur, h, ms, 2 * t + h)
            stage_pv(h, 2 * t + h, alphas)
        return ms

    def body(t, ms):
        return lax.cond(t % 2 == 0,
                        lambda m: trip(t, m, s0_ref, s1_ref),
                        lambda m: trip(t, m, s1_ref, s0_ref), ms)

    lax.fori_loop(0, (qi + 1) // 2, body, ms)
    for g in range(group):
        acc = acc_ref[g]
        o_ref[:, g * dh:(g + 1) * dh] = (acc[:dh] / acc[dh:dh + 1]).astype(BF16).T


def _moba_attention(qt, k, vt, k_mean, batch, seq):
    heads, n_tiles, dh, blk = qt.shape
    dv_rows = vt.shape[2]
    n_blk = seq // blk
    topk = min(MOBA_TOPK, n_blk)
    width = heads * dh
    group = MOBA_HEAD_GROUP
    km = k_mean.reshape(batch, n_blk, width)
    per_seq = lambda b, h, i: (h, b, 0, 0)
    est = group * seq * (dh + dv_rows) * 2 + 2 * group * blk * blk * (2 + 2) + 6 * group * blk * dh * 4
    return pl.pallas_call(
        functools.partial(_moba_kernel, topk=topk, group=group, dh=dh),
        grid=(batch, heads // group, n_blk),
        in_specs=[
            pl.BlockSpec((group, None, dh, blk), lambda b, h, i: (h, b * n_blk + i, 0, 0)),
            pl.BlockSpec((group, n_blk, blk, dh), per_seq, pipeline_mode=pl.Buffered(1)),
            pl.BlockSpec((group, n_blk, dv_rows, blk), per_seq, pipeline_mode=pl.Buffered(1)),
            pl.BlockSpec((None, n_blk, group * dh), lambda b, h, i: (b, 0, h)),
        ],
        out_specs=pl.BlockSpec((blk, group * dh), lambda b, h, i: (b * n_blk + i, h)),
        out_shape=jax.ShapeDtypeStruct((n_tiles * blk, width), BF16),
        scratch_shapes=[pltpu.VMEM((group, n_blk, blk), F32), pltpu.VMEM((group, dv_rows, blk), F32),
                        pltpu.VMEM((group, 2 * blk, blk), BF16), pltpu.VMEM((group, 2 * blk, blk), BF16),
                        pltpu.VMEM((group, 2 * blk, blk), BF16)],
        compiler_params=pltpu.CompilerParams(
            dimension_semantics=("arbitrary", "arbitrary", "arbitrary"), vmem_limit_bytes=_vmem_limit(est)),
        name="moba_attention",
    )(qt, k, vt, km)


def kernel(x, ret_norm, ret_w_in, ret_w_out, kv_norm, w_kv, attn_norm, w_q, w_o, ffn_norm, router_group_w, router_group_b, router_expert_w, router_expert_b, expert_w_gate, expert_w_up, expert_w_down, final_norm):
    B, S, D = x.shape
    T = B * S
    assert S % MOBA_BLOCK == 0 and S % RET_KERNEL_CHUNK == 0 and T % MOE_TILE == 0
    assert ret_norm.shape[0] == 1 and attn_norm.shape[0] == 1 and ffn_norm.shape[0] == 2
    h = x.reshape(T, D)
    final_g = final_norm[None, :]

    def proj_router(y, w, resid, layer, name):
        return _proj_router(y, w.astype(BF16), resid, ffn_norm[layer][None, :], router_group_w[layer],
                            router_group_b[layer], router_expert_w[layer], router_expert_b[layer], name)

    def moe(h_aug, route, counts, layer, last):
        return _moe_layer(h_aug, route, counts, ffn_norm[layer][None, :], final_g,
                          expert_w_gate, expert_w_up, expert_w_down, layer, final_norm=last)

    q, kt, v, gate = _ret_in_proj(h, ret_norm[0][None, :], ret_w_in[0].astype(BF16), S)
    y = _ret_core(q, kt, v, gate, B, S)
    h = moe(*proj_router(y, ret_w_out[0], h, 0, "ret_out_proj_router"), 0, False)

    k2, vt2, qt2, k_mean = _kvq_proj(h, kv_norm[None, :], attn_norm[0][None, :],
                                     w_kv.astype(BF16), w_q[0].astype(BF16), S)
    o = _moba_attention(qt2, k2, vt2, k_mean, B, S)
    h = moe(*proj_router(o, w_o[0], h, 1, "attn_out_proj_router"), 1, True)
    return h.reshape(B, S, D)
```

```python
import functools

import jax
import jax.numpy as jnp
from jax import lax
from jax.experimental import pallas as pl
from jax.experimental.pallas import tpu as pltpu

F32 = jnp.float32
BF16 = jnp.bfloat16

NORM_EPS = 1e-6
RET_HEADS = 4
RET_ROT_BASE = 10000.0
ATT_HEADS = 8
ROPE_FRACTION = 4
ROPE_THETA = 500000.0
MOBA_BLOCK = 256
MOBA_TOPK = 3
N_GROUPS = 4
EXPERTS_PER_GROUP = 4
PAIR_LO = (0, 0, 1, 1, 0, 2)
PAIR_HI = (1, 2, 2, 3, 3, 3)
N_PAIRS = len(PAIR_LO)
N_CLASSES = N_GROUPS * N_PAIRS

LANES = 128
SUBLANES = 8
RET_KERNEL_CHUNK = 256
RET_CHUNKS_PER_TILE = 2
MOE_TILE = 256
ROUTE_ROWS = 32
MOBA_HEAD_GROUP = 8
KVQ_BLOCKS_PER_TILE = 2
MOBA_KEY_CHUNK = 32
VT_ONES_ROWS = 16
LOG2_E = 1.4426950408889634
V7X_VMEM_BYTES = 64 * 1024 * 1024
V7X_VMEM_RESERVED_BYTES = 8 * 1024 * 1024
DEFAULT_SCOPED_VMEM_BYTES = 32 * 1024 * 1024


def _vmem_limit(estimate_bytes):
    wanted = max(DEFAULT_SCOPED_VMEM_BYTES, estimate_bytes * 5 // 4)
    return int(min(V7X_VMEM_BYTES - V7X_VMEM_RESERVED_BYTES, wanted))


def _rms_scale(x):
    return lax.rsqrt(jnp.mean(x * x, axis=-1, keepdims=True) + NORM_EPS)


def _silu(a):
    return a * jax.nn.sigmoid(a)


def _ret_in_kernel(x_ref, g_ref, w_ref, cb_ref, sb_ref, cr_ref, sr_ref, qdec_ref, kdec_ref,
                   q_ref, kt_ref, v_ref, gate_ref, *, heads, dk, dv):
    x = x_ref[...]
    xn = (x * _rms_scale(x) * g_ref[...]).astype(BF16)
    cos, sin = _tile_cos_sin(cb_ref, sb_ref, cr_ref, sr_ref)
    half = dk // 2

    def rotated(col0, h, dec):
        p = jnp.dot(xn, w_ref[:, col0 + h * dk:col0 + (h + 1) * dk], preferred_element_type=F32)
        x1 = p[:, :half]
        x2 = p[:, half:]
        cos_h = cos * dec
        sin_h = sin * dec
        return x1 * cos_h - x2 * sin_h, x2 * cos_h + x1 * sin_h

    cc = kt_ref.shape[3]
    for h in range(heads):
        lo, hi = rotated(0, h, qdec_ref[h])
        q_ref[:, h * dk:h * dk + half] = lo.astype(BF16)
        q_ref[:, h * dk + half:(h + 1) * dk] = hi.astype(BF16)
        lo, hi = rotated(heads * dk, h, kdec_ref[h])
        for c in range(kt_ref.shape[1]):
            rows = slice(c * cc, (c + 1) * cc)
            kt_ref[h, c, :half] = lo[rows].astype(BF16).T
            kt_ref[h, c, half:] = hi[rows].astype(BF16).T
    v0 = 2 * heads * dk
    g0 = v0 + heads * dv
    for h in range(heads):
        v_ref[:, h * dv:(h + 1) * dv] = jnp.dot(
            xn, w_ref[:, v0 + h * dv:v0 + (h + 1) * dv], preferred_element_type=F32).astype(BF16)
        gate_ref[:, h * dv:(h + 1) * dv] = jnp.dot(
            xn, w_ref[:, g0 + h * dv:g0 + (h + 1) * dv], preferred_element_type=F32).astype(BF16)


def _ret_in_proj(h, norm_g, w_bf16, seq):
    T, D = h.shape
    heads = RET_HEADS
    dk = D // heads
    dv = 2 * dk
    n_cols = w_bf16.shape[1]
    cc = RET_KERNEL_CHUNK
    n_sub = RET_CHUNKS_PER_TILE
    tm = n_sub * cc
    assert seq % tm == 0
    n_pos_tiles = seq // tm
    half = dk // 2
    inv = 1.0 / (RET_ROT_BASE ** (jnp.arange(half, dtype=F32) / half))
    cos_b, sin_b, cos_r, sin_r = _angle_tables(inv, seq, tm)
    log_gamma = _ret_log_gamma(heads)
    idx = jnp.tile(jnp.arange(cc, dtype=F32), n_sub)
    q_dec = jnp.broadcast_to(jnp.exp(log_gamma[:, None] * idx)[:, :, None], (heads, tm, half))
    k_dec = jnp.broadcast_to((jnp.exp(-log_gamma[:, None] * idx) * dk ** -0.5)[:, :, None], (heads, tm, half))
    pos = lambda i: (i % n_pos_tiles, 0, 0)
    est = D * n_cols * 2 + 2 * tm * D * 4 + 2 * tm * n_cols * 2 + (4 + 4 * heads) * tm * LANES * 4
    return pl.pallas_call(
        functools.partial(_ret_in_kernel, heads=heads, dk=dk, dv=dv),
        grid=(T // tm,),
        in_specs=[
            pl.BlockSpec((tm, D), lambda i: (i, 0)),
            pl.BlockSpec((1, D), lambda i: (0, 0)),
            pl.BlockSpec((D, n_cols), lambda i: (0, 0), pipeline_mode=pl.Buffered(1)),
            pl.BlockSpec((None, 1, half), pos),
            pl.BlockSpec((None, 1, half), pos),
            pl.BlockSpec((tm, half), lambda i: (0, 0)),
            pl.BlockSpec((tm, half), lambda i: (0, 0)),
            pl.BlockSpec((heads, tm, half), lambda i: (0, 0, 0)),
            pl.BlockSpec((heads, tm, half), lambda i: (0, 0, 0)),
        ],
        out_specs=[
            pl.BlockSpec((tm, heads * dk), lambda i: (i, 0)),
            pl.BlockSpec((heads, n_sub, dk, cc), lambda i: (0, i, 0, 0)),
            pl.BlockSpec((tm, heads * dv), lambda i: (i, 0)),
            pl.BlockSpec((tm, heads * dv), lambda i: (i, 0)),
        ],
        out_shape=[
            jax.ShapeDtypeStruct((T, heads * dk), BF16),
            jax.ShapeDtypeStruct((heads, T // cc, dk, cc), BF16),
            jax.ShapeDtypeStruct((T, heads * dv), BF16),
            jax.ShapeDtypeStruct((T, heads * dv), BF16),
        ],
        compiler_params=pltpu.CompilerParams(
            dimension_semantics=("arbitrary",), vmem_limit_bytes=_vmem_limit(est)),
        name="ret_in_proj",
    )(h, norm_g, w_bf16, cos_b, sin_b, cos_r, sin_r, q_dec, k_dec)


def _ret_log_gamma(heads):
    return jnp.log1p(-jnp.power(2.0, -5.0 - jnp.arange(heads, dtype=F32)))


def _ret_core_kernel(q_ref, kt_ref, v_ref, g_ref, cd_ref, y_ref, state_ref, *, heads):
    @pl.when(pl.program_id(1) == 0)
    def _():
        state_ref[...] = jnp.zeros_like(state_ref)

    n_sub, cc = kt_ref.shape[1], kt_ref.shape[3]
    dk = q_ref.shape[1] // heads
    dv = v_ref.shape[1] // heads
    causal = lax.broadcasted_iota(jnp.int32, (cc, cc), 0) >= lax.broadcasted_iota(jnp.int32, (cc, cc), 1)
    for c in range(n_sub):
        rows = slice(c * cc, (c + 1) * cc)
        for h in range(heads):
            q = q_ref[rows, h * dk:(h + 1) * dk]
            kt = kt_ref[h, c]
            v = v_ref[rows, h * dv:(h + 1) * dv]
            s = jnp.where(causal, jnp.dot(q, kt, preferred_element_type=F32), 0.0)
            u = state_ref[h]
            o = (jnp.dot(s.astype(BF16), v, preferred_element_type=F32)
                 + jnp.dot(q, u.astype(BF16), preferred_element_type=F32))
            state_ref[h] = cd_ref[h] * (u + jnp.dot(kt, v, preferred_element_type=F32))
            o = o * _rms_scale(o)
            gate = g_ref[rows, h * dv:(h + 1) * dv].astype(F32)
            y_ref[rows, h * dv:(h + 1) * dv] = (_silu(gate) * o).astype(BF16)


def _ret_core(q, kt, v, gate, batch, seq):
    T = q.shape[0]
    heads = RET_HEADS
    dk = q.shape[1] // heads
    dv = v.shape[1] // heads
    cc = RET_KERNEL_CHUNK
    n_sub = RET_CHUNKS_PER_TILE
    tm = n_sub * cc
    assert seq % tm == 0
    nc = seq // tm
    chunk_decay = jnp.broadcast_to(jnp.exp(_ret_log_gamma(heads) * cc)[:, None, None], (heads, 1, dv))
    row = lambda b, c: (b * nc + c, 0)
    return pl.pallas_call(
        functools.partial(_ret_core_kernel, heads=heads),
        grid=(batch, nc),
        in_specs=[
            pl.BlockSpec((tm, heads * dk), row),
            pl.BlockSpec((heads, n_sub, dk, cc), lambda b, c: (0, b * nc + c, 0, 0)),
            pl.BlockSpec((tm, heads * dv), row),
            pl.BlockSpec((tm, heads * dv), row),
            pl.BlockSpec((heads, 1, dv), lambda b, c: (0, 0, 0)),
        ],
        out_specs=pl.BlockSpec((tm, heads * dv), row),
        out_shape=jax.ShapeDtypeStruct((T, heads * dv), BF16),
        scratch_shapes=[pltpu.VMEM((heads, dk, dv), F32)],
        compiler_params=pltpu.CompilerParams(dimension_semantics=("arbitrary", "arbitrary")),
        name="ret_core",
    )(q, kt, v, gate, chunk_decay)


def _proj_router_kernel(y_ref, w_ref, r_ref, g_ref, wr_ref, br_ref, haug_ref, route_ref, counts_ref, run_ref,
                        *, groups, experts, d_model):
    @pl.when(pl.program_id(0) == 0)
    def _():
        run_ref[...] = jnp.zeros_like(run_ref)

    h = r_ref[...] + jnp.dot(y_ref[...], w_ref[...], preferred_element_type=F32)
    xn = h * _rms_scale(h) * g_ref[...]
    xh = xn.astype(BF16)
    xl = (xn - xh.astype(F32)).astype(BF16)
    hi_both = jnp.dot(xh, wr_ref[...], preferred_element_type=F32)
    lo_hi = jnp.dot(xl, wr_ref[:, :LANES], preferred_element_type=F32)
    logits = hi_both[:, :LANES] + (hi_both[:, LANES:] + lo_hi) + br_ref[...]
    lt = logits.T[:ROUTE_ROWS]
    tm = lt.shape[1]
    row = lax.broadcasted_iota(jnp.int32, lt.shape, 0)
    neg = -jnp.inf

    def argmax_first(vals):
        m = jnp.max(vals, axis=0, keepdims=True)
        first = jnp.min(jnp.where(vals == m, row, ROUTE_ROWS), axis=0, keepdims=True)
        return m, first

    gl = jnp.where(row < groups, lt, neg)
    gmax, gidx = argmax_first(gl)
    gsum = jnp.sum(jnp.where(row < groups, jnp.exp(gl - gmax), 0.0), axis=0, keepdims=True)
    g_val = 1.0 / gsum
    e0 = groups + experts * gidx
    el = jnp.where((row >= e0) & (row < e0 + experts), lt, neg)
    m1, i1 = argmax_first(el)
    m2, i2 = argmax_first(jnp.where(row == i1, neg, el))
    t = jnp.exp(m2 - m1)
    w1 = 1.0 / (1.0 + t)
    w2 = t / (1.0 + t)
    first_is_lo = i1 < i2
    lo = jnp.minimum(i1, i2) - e0
    hi = jnp.maximum(i1, i2) - e0
    w_lo = g_val * jnp.where(first_is_lo, w1, w2)
    w_hi = g_val * jnp.where(first_is_lo, w2, w1)
    lex = (lo * (7 - lo)) // 2 + (hi - lo - 1)
    pair = jnp.where(lex == 2, 4, jnp.where(lex == 3, 2, jnp.where(lex == 4, 3, lex)))
    cls = gidx * N_PAIRS + pair

    onehot = row == cls
    oh = jnp.where(onehot, 1.0, 0.0)
    before = lax.broadcasted_iota(jnp.int32, (tm, tm), 0) < lax.broadcasted_iota(jnp.int32, (tm, tm), 1)
    prefix = jnp.dot(oh.astype(BF16), jnp.where(before, 1.0, 0.0).astype(BF16), preferred_element_type=F32)
    rank = jnp.sum(jnp.where(onehot, prefix + run_ref[:, 0:1], 0.0), axis=0, keepdims=True)
    run_new = run_ref[...] + jnp.sum(oh, axis=1, keepdims=True)
    run_ref[...] = run_new
    counts_ref[...] = run_new

    r8 = lax.broadcasted_iota(jnp.int32, route_ref.shape, 0)
    route_ref[...] = jnp.where(r8 == 0, cls, jnp.where(r8 == 1, rank.astype(jnp.int32), 0))
    rl = lax.broadcasted_iota(jnp.int32, (LANES, tm), 0)
    meta_t = jnp.where(rl == 0, w_lo, jnp.where(rl == 1, w_hi, 0.0))
    haug_ref[:, :d_model] = h
    haug_ref[:, d_model:] = meta_t.T


def _proj_router(y, w_bf16, resid, norm_g, w_rg, b_rg, w_re, b_re, name, tm=512):
    T, K = y.shape
    D = w_bf16.shape[1]
    G, E = N_GROUPS, EXPERTS_PER_GROUP
    n_used = G + G * E
    wr = jnp.concatenate([w_rg, jnp.transpose(w_re, (1, 0, 2)).reshape(D, G * E)], axis=1)
    wr = jnp.pad(wr, ((0, 0), (0, LANES - n_used)))
    wr_hi = wr.astype(BF16)
    wr = jnp.concatenate([wr_hi, (wr - wr_hi.astype(F32)).astype(BF16)], axis=1)
    br = jnp.pad(jnp.concatenate([b_rg, b_re.reshape(G * E)]), (0, LANES - n_used))[None, :]
    est = 2 * K * D * 2 + 2 * tm * K * 2 + 6 * tm * D * 4 + 2 * D * LANES * 4 + 4 * tm * tm * 4
    const = lambda i: (0, 0)
    return pl.pallas_call(
        functools.partial(_proj_router_kernel, groups=G, experts=E, d_model=D),
        grid=(T // tm,),
        in_specs=[
            pl.BlockSpec((tm, K), lambda i: (i, 0)),
            pl.BlockSpec((K, D), const),
            pl.BlockSpec((tm, D), lambda i: (i, 0)),
            pl.BlockSpec((1, D), const),
            pl.BlockSpec((D, 2 * LANES), const),
            pl.BlockSpec((1, LANES), const),
        ],
        out_specs=[
            pl.BlockSpec((tm, D + LANES), lambda i: (i, 0)),
            pl.BlockSpec((8, tm), lambda i: (0, i)),
            pl.BlockSpec((ROUTE_ROWS, LANES), const),
        ],
        out_shape=[
            jax.ShapeDtypeStruct((T, D + LANES), F32),
            jax.ShapeDtypeStruct((8, T), jnp.int32),
            jax.ShapeDtypeStruct((ROUTE_ROWS, LANES), F32),
        ],
        scratch_shapes=[pltpu.VMEM((ROUTE_ROWS, LANES), F32)],
        compiler_params=pltpu.CompilerParams(
            dimension_semantics=("arbitrary",), vmem_limit_bytes=_vmem_limit(est)),
        name=name,
    )(y, w_bf16, resid, norm_g, wr, br)


def _tile_schedule(counts_blk, tm, n_tiles):
    counts = counts_blk[:N_CLASSES, 0].astype(jnp.int32)
    tiles_per = (counts + tm - 1) // tm
    tile_end = jnp.cumsum(tiles_per)
    tile_begin = tile_end - tiles_per
    cstart = jnp.cumsum(counts) - counts
    t = jnp.arange(n_tiles, dtype=jnp.int32)
    live = t < tile_end[-1]
    tq = jnp.where(live, t, tile_end[-1] - 1)
    tcls = jnp.sum((tile_end[None, :] <= tq[:, None]).astype(jnp.int32), axis=1)
    onehot = tcls[:, None] == jnp.arange(N_CLASSES, dtype=jnp.int32)[None, :]
    pick = lambda table: jnp.sum(jnp.where(onehot, jnp.asarray(table, jnp.int32)[None, :], 0), axis=1)
    k = tq - pick(tile_begin)
    nvalid = jnp.where(live, jnp.clip(pick(counts) - k * tm, 0, tm), 0).astype(jnp.int32)
    base = jnp.where(live, pick(cstart) + k * tm, 0).astype(jnp.int32)
    tile_g = pick([c // N_PAIRS for c in range(N_CLASSES)])
    tile_lo = pick([PAIR_LO[c % N_PAIRS] for c in range(N_CLASSES)])
    tile_hi = pick([PAIR_HI[c % N_PAIRS] for c in range(N_CLASSES)])
    cstart_pad = jnp.pad(cstart, (0, ROUTE_ROWS - N_CLASSES)).astype(jnp.int32)
    return tile_g, tile_lo, tile_hi, nvalid, base, cstart_pad


def _moe_kernel(tg_ref, tlo_ref, thi_ref, tnv_ref, tbase_ref, cs_ref,
                h_hbm, route_ref, gn_ref, fn_ref, wg_lo, wu_lo, wd_lo, wg_hi, wu_hi, wd_hi,
                out_hbm, rt_ref, pos_smem, pos_vmem, x0, x1, x2, o0, o1, o2, w_in_bf, w_out_bf,
                gsem, ssem, psem, *, tm, d_model, n_tok, final_norm):
    i = pl.program_id(0)
    xbufs = (x0, x1, x2)
    obufs = (o0, o1, o2)
    depth = len(xbufs)

    def issue_gather(t, sl):
        base = tbase_ref[t]
        for r in range(tm):
            tok = rt_ref[base + r]
            pltpu.make_async_copy(h_hbm.at[pl.ds(tok, 1)], xbufs[sl].at[pl.ds(r, 1)], gsem.at[sl]).start()

    def scatter_row(sl, r, tok):
        return pltpu.make_async_copy(obufs[sl].at[pl.ds(r, 1)], out_hbm.at[pl.ds(tok, 1)], ssem.at[sl])

    def issue_scatter(sl, base, n_valid):
        for r in range(tm):
            @pl.when(r < n_valid)
            def _():
                scatter_row(sl, r, rt_ref[base + r]).start()

    def wait_gather(sl):
        pltpu.make_async_copy(h_hbm.at[pl.ds(0, tm)], xbufs[sl], gsem.at[sl]).wait()

    def wait_scatter(sl, n_valid):
        aligned = pl.multiple_of((n_valid // SUBLANES) * SUBLANES, SUBLANES)

        @pl.when(aligned > 0)
        def _():
            rows = pl.ds(0, aligned)
            pltpu.make_async_copy(obufs[sl].at[rows], out_hbm.at[rows], ssem.at[sl]).wait()

        def single(r, c):
            scatter_row(sl, r, 0).wait()
            return c
        lax.fori_loop(aligned, n_valid, single, 0)

    def live(t):
        return tnv_ref[jnp.maximum(t, 0)] > 0

    @pl.when(i == 0)
    def _():
        cls2 = route_ref[0]
        pos = route_ref[1]
        for c in range(N_CLASSES):
            pos = pos + jnp.where(cls2 == c, cs_ref[c], 0)
        pos_vmem[...] = pos
        to_smem = pltpu.make_async_copy(pos_vmem, pos_smem, psem)
        to_smem.start()
        to_smem.wait()

        def place(row, c):
            for col in range(LANES):
                rt_ref[pos_smem[row, col]] = row * LANES + col
            return c
        lax.fori_loop(0, n_tok // LANES, place, 0)
        for r in range(tm):
            rt_ref[n_tok + r] = 0
        issue_gather(0, 0)
        issue_gather(1, 1)

    prev = jnp.maximum(i - 1, 0)
    group_changed = (i == 0) | (tg_ref[i] != tg_ref[prev])
    for which, t_ref, (wg, wu, wd) in ((0, tlo_ref, (wg_lo, wu_lo, wd_lo)), (1, thi_ref, (wg_hi, wu_hi, wd_hi))):
        @pl.when(group_changed | (t_ref[i] != t_ref[prev]))
        def _():
            w_in_bf[2 * which] = wg[...].astype(BF16)
            w_in_bf[2 * which + 1] = wu[...].astype(BF16)
            w_out_bf[which] = wd[...].astype(BF16)

    def tile(cur):
        far = (cur + depth - 1) % depth
        @pl.when((i <= 1) | live(i - 2))
        def _():
            wait_gather(cur)

        @pl.when((i >= depth) & live(i - depth))
        def _():
            wait_scatter(cur, tnv_ref[jnp.maximum(i - depth, 0)])

        prev_base = tbase_ref[prev]
        prev_valid = jnp.where(i == 0, 0, tnv_ref[prev])

        @pl.when(live(i))
        def _():
            issue_scatter(far, prev_base, prev_valid)
            issue_gather(i + 2, far)
            xa = xbufs[cur][...]
            x = xa[:, :d_model]
            xn = (x * _rms_scale(x) * gn_ref[...]).astype(BF16)

            def expert(which, w):
                a = jnp.dot(xn, w_in_bf[2 * which], preferred_element_type=F32)
                b = jnp.dot(xn, w_in_bf[2 * which + 1], preferred_element_type=F32)
                return jnp.dot((_silu(a) * b * w).astype(BF16), w_out_bf[which], preferred_element_type=F32)

            y = expert(0, xa[:, d_model:d_model + 1]) + expert(1, xa[:, d_model + 1:d_model + 2])
            obufs[cur][...] = x + y
            if final_norm:
                for r0 in range(0, tm, 4 * SUBLANES):
                    rows = slice(r0, r0 + 4 * SUBLANES)
                    o = obufs[cur][rows, :]
                    obufs[cur][rows, :] = o * _rms_scale(o) * fn_ref[...]

        @pl.when(jnp.logical_not(live(i)) & (i >= 1) & live(i - 1))
        def _():
            issue_scatter(far, prev_base, prev_valid)

    for residue in range(depth):
        @pl.when(i % depth == residue)
        def _():
            tile(residue)


def _moe_layer(h_aug, route, counts_blk, ffn_norm_g, final_norm_g, wg, wu, wd, layer, final_norm):
    T, DA = h_aug.shape
    D = DA - LANES
    F = wg.shape[-1]
    tm = MOE_TILE
    assert T % (LANES * SUBLANES) == 0 and T >= 2 * tm
    n_tiles = T // tm + N_CLASSES + 3
    tile_g, tile_lo, tile_hi, nvalid, base, cstart = _tile_schedule(counts_blk, tm, n_tiles)
    route2d = route[:2].reshape(2, T // LANES, LANES)

    def w_spec(shape, which):
        def index(i, tg, tlo, thi, *_):
            return (layer, tg[i], (tlo if which == 0 else thi)[i], 0, 0)
        return pl.BlockSpec((None, None, None) + shape, index)

    est = 2 * 6 * D * F * 4 + 6 * D * F * 2 + 3 * tm * DA * 4 + 3 * tm * D * 4 + 8 * tm * F * 4
    return pl.pallas_call(
        functools.partial(_moe_kernel, tm=tm, d_model=D, n_tok=T, final_norm=final_norm),
        grid_spec=pltpu.PrefetchScalarGridSpec(
            num_scalar_prefetch=6,
            grid=(n_tiles,),
            in_specs=[
                pl.BlockSpec(memory_space=pl.ANY),
                pl.BlockSpec((2, T // LANES, LANES), lambda i, *_: (0, 0, 0)),
                pl.BlockSpec((1, D), lambda i, *_: (0, 0)),
                pl.BlockSpec((1, D), lambda i, *_: (0, 0)),
                w_spec((D, F), 0), w_spec((D, F), 0), w_spec((F, D), 0),
                w_spec((D, F), 1), w_spec((D, F), 1), w_spec((F, D), 1),
            ],
            out_specs=pl.BlockSpec(memory_space=pl.ANY),
            scratch_shapes=[
                pltpu.SMEM((T + tm,), jnp.int32),
                pltpu.SMEM((T // LANES, LANES), jnp.int32),
                pltpu.VMEM((T // LANES, LANES), jnp.int32),
                pltpu.VMEM((tm, DA), F32), pltpu.VMEM((tm, DA), F32), pltpu.VMEM((tm, DA), F32),
                pltpu.VMEM((tm, D), F32), pltpu.VMEM((tm, D), F32), pltpu.VMEM((tm, D), F32),
                pltpu.VMEM((4, D, F), BF16),
                pltpu.VMEM((2, F, D), BF16),
                pltpu.SemaphoreType.DMA((3,)),
                pltpu.SemaphoreType.DMA((3,)),
                pltpu.SemaphoreType.DMA(()),
            ],
        ),
        out_shape=jax.ShapeDtypeStruct((T, D), F32),
        compiler_params=pltpu.CompilerParams(
            dimension_semantics=("arbitrary",), vmem_limit_bytes=_vmem_limit(est)),
        name="moe_experts",
    )(tile_g, tile_lo, tile_hi, nvalid, base, cstart,
      h_aug, route2d, ffn_norm_g, final_norm_g, wg, wu, wd, wg, wu, wd)


def _partial_rotary(xh, cosf, sinp, sinm, half):
    lanes = xh.shape[-1]
    return xh * cosf + pltpu.roll(xh, half, 1) * sinp + pltpu.roll(xh, lanes - half, 1) * sinm


def _tile_cos_sin(cb_ref, sb_ref, cr_ref, sr_ref):
    cb, sb, cr, sr = cb_ref[...], sb_ref[...], cr_ref[...], sr_ref[...]
    return cb * cr - sb * sr, sb * cr + cb * sr


def _kvq_kernel(h_ref, gkv_ref, gq_ref, wkv_ref, wq_ref, cb_ref, sb_ref, cr_ref, sr_ref,
                k_ref, vt_ref, qt_ref, km_ref, *, heads, dh, half):
    x = h_ref[...]
    xr = x * _rms_scale(x)
    xkv = (xr * gkv_ref[...]).astype(BF16)
    xq = (xr * gq_ref[...]).astype(BF16)
    cosf, sin_all = _tile_cos_sin(cb_ref, sb_ref, cr_ref, sr_ref)
    lane = lax.broadcasted_iota(jnp.int32, sin_all.shape, 1)
    sinp = jnp.where(lane >= half, sin_all, 0.0)
    sinm = jnp.where(lane < half, -sin_all, 0.0)
    width = heads * dh
    q_scale = dh ** -0.5 * LOG2_E
    pair_w = 2 * dh
    blk = k_ref.shape[2]
    n_sub = x.shape[0] // blk
    for c0 in range(0, width, pair_w):
        kk = jnp.dot(xkv, wkv_ref[:, c0:c0 + pair_w], preferred_element_type=F32)
        vv = jnp.dot(xkv, wkv_ref[:, width + c0:width + c0 + pair_w], preferred_element_type=F32)
        qq = jnp.dot(xq, wq_ref[:, c0:c0 + pair_w], preferred_element_type=F32)
        for u in range(2):
            hh = c0 // dh + u
            cols = slice(hh * dh, (hh + 1) * dh)
            kh = _partial_rotary(kk[:, u * dh:(u + 1) * dh], cosf, sinp, sinm, half)
            qh = _partial_rotary(qq[:, u * dh:(u + 1) * dh], cosf, sinp, sinm, half) * q_scale
            vh = vv[:, u * dh:(u + 1) * dh]
            for b in range(n_sub):
                rows = slice(b * blk, (b + 1) * blk)
                k_ref[hh, b] = kh[rows].astype(BF16)
                km_ref[b, :, cols] = jnp.mean(kh[rows], axis=0, keepdims=True)
                vt_ref[hh, b, :dh] = vh[rows].astype(BF16).T
                vt_ref[hh, b, dh:] = jnp.ones((VT_ONES_ROWS, blk), BF16)
                qt_ref[hh, b] = qh[rows].astype(BF16).T


def _angle_tables(inv_freq, seq, tm):
    base = jnp.arange(0, seq, tm, dtype=F32)[:, None] * inv_freq[None, :]
    offs = jnp.arange(tm, dtype=F32)[:, None] * inv_freq[None, :]
    return jnp.cos(base)[:, None, :], jnp.sin(base)[:, None, :], jnp.cos(offs), jnp.sin(offs)


def _rope_tables(seq, dh, tm):
    rot = dh // ROPE_FRACTION
    half = rot // 2
    inv = 1.0 / (ROPE_THETA ** (jnp.arange(half, dtype=F32) / half))
    inv_lanes = jnp.concatenate([inv, inv, jnp.zeros((dh - rot,), F32)])
    return _angle_tables(inv_lanes, seq, tm) + (half,)


def _kvq_proj(h, kv_norm_g, q_norm_g, wkv_bf16, wq_bf16, seq):
    T, D = h.shape
    heads = ATT_HEADS
    dh = D // heads
    blk = MOBA_BLOCK
    n_sub = KVQ_BLOCKS_PER_TILE
    tm = n_sub * blk
    assert seq % tm == 0
    n_pos_tiles = seq // tm
    cos_b, sin_b, cos_r, sin_r, half = _rope_tables(seq, dh, tm)
    width = heads * dh
    est = 2 * (D * 2 * width + D * width) * 2 + 2 * tm * D * 4 + 6 * tm * width * 2 + 8 * tm * dh * 4
    pos = lambda i: (i % n_pos_tiles, 0, 0)
    const = lambda i: (0, 0)
    per_block = lambda i: (0, i, 0, 0)
    return pl.pallas_call(
        functools.partial(_kvq_kernel, heads=heads, dh=dh, half=half),
        grid=(T // tm,),
        in_specs=[
            pl.BlockSpec((tm, D), lambda i: (i, 0)),
            pl.BlockSpec((1, D), const),
            pl.BlockSpec((1, D), const),
            pl.BlockSpec((D, 2 * width), const),
            pl.BlockSpec((D, width), const),
            pl.BlockSpec((None, 1, dh), pos),
            pl.BlockSpec((None, 1, dh), pos),
            pl.BlockSpec((tm, dh), const),
            pl.BlockSpec((tm, dh), const),
        ],
        out_specs=[
            pl.BlockSpec((heads, n_sub, blk, dh), per_block),
            pl.BlockSpec((heads, n_sub, dh + VT_ONES_ROWS, blk), per_block),
            pl.BlockSpec((heads, n_sub, dh, blk), per_block),
            pl.BlockSpec((n_sub, 1, width), lambda i: (i, 0, 0)),
        ],
        out_shape=[
            jax.ShapeDtypeStruct((heads, T // blk, blk, dh), BF16),
            jax.ShapeDtypeStruct((heads, T // blk, dh + VT_ONES_ROWS, blk), BF16),
            jax.ShapeDtypeStruct((heads, T // blk, dh, blk), BF16),
            jax.ShapeDtypeStruct((T // blk, 1, width), F32),
        ],
        compiler_params=pltpu.CompilerParams(
            dimension_semantics=("arbitrary",), vmem_limit_bytes=_vmem_limit(est)),
        name="kvq_proj",
    )(h, kv_norm_g, q_norm_g, wkv_bf16, wq_bf16, cos_b, sin_b, cos_r, sin_r)


def _moba_kernel(qt_ref, k_ref, vt_ref, km_ref, o_ref, sel_ref, acc_ref, s0_ref, s1_ref, p_ref,
                 *, topk, group, dh):
    qi = pl.program_id(2)
    n_blk = km_ref.shape[0]
    blk = s0_ref.shape[1] // 2
    neg = -jnp.inf
    halves = (slice(0, blk), slice(blk, 2 * blk))

    def chunks(h):
        return [slice(c, c + MOBA_KEY_CHUNK) for c in range(h * blk, (h + 1) * blk, MOBA_KEY_CHUNK)]

    def stage_pair(s_buf, j):
        jp = jnp.minimum(j, n_blk - 2)
        for g in range(group):
            keys = k_ref[g, pl.ds(jp, 2)].reshape(2 * blk, dh)
            s_buf[g] = jnp.dot(keys, qt_ref[g], preferred_element_type=F32).astype(BF16)

    def stage_softmax(s_buf, h, ms, j):
        rows = chunks(h)
        new_m, alphas = [], []
        for g in range(group):
            part = s_buf[g, rows[0], :]
            for c in rows[1:]:
                part = jnp.maximum(part, s_buf[g, c, :])
            m_blk = jnp.max(part, axis=0, keepdims=True).astype(F32)
            if j is None:
                m_new = shift = m_blk
            else:
                bias = sel_ref[g, pl.ds(j, 1), :]
                m_new = jnp.maximum(ms[g], m_blk + bias)
                shift = m_new - bias
            shift = shift.astype(BF16)
            for c in rows:
                p_ref[g, c, :] = jnp.exp2(s_buf[g, c, :] - shift)
            new_m.append(m_new)
            alphas.append(None if j is None else jnp.exp2(ms[g] - m_new))
        return tuple(new_m), alphas

    def stage_pv(h, j, alphas):
        for g in range(group):
            pv = jnp.dot(vt_ref[g, j], p_ref[g, halves[h], :], preferred_element_type=F32)
            acc_ref[g] = pv if alphas[g] is None else alphas[g] * acc_ref[g] + pv

    for g in range(group):
        km = km_ref[:, g * dh:(g + 1) * dh]
        km_hi = km.astype(BF16)
        km_lo = (km - km_hi.astype(F32)).astype(BF16)
        gate = (jnp.dot(km_hi, qt_ref[g], preferred_element_type=F32)
                + jnp.dot(km_lo, qt_ref[g], preferred_element_type=F32))
        blk_id = lax.broadcasted_iota(jnp.int32, gate.shape, 0)
        gv = jnp.where(blk_id < qi, gate, neg)
        bias = jnp.full(gate.shape, neg, F32)
        for _ in range(topk):
            m = jnp.max(gv, axis=0, keepdims=True)
            first = jnp.min(jnp.where(gv == m, blk_id, n_blk), axis=0, keepdims=True)
            bias = jnp.where((blk_id == first) & (m > neg), 0.0, bias)
            gv = jnp.where(blk_id == first, neg, gv)
        sel_ref[g] = bias

    key_id = lax.broadcasted_iota(jnp.int32, (blk, blk), 0)
    qry_id = lax.broadcasted_iota(jnp.int32, (blk, blk), 1)
    for g in range(group):
        own = jnp.dot(k_ref[g, qi], qt_ref[g], preferred_element_type=F32)
        s1_ref[g, halves[0], :] = jnp.where(key_id <= qry_id, own, neg).astype(BF16)
    stage_pair(s0_ref, 0)
    ms, alphas = stage_softmax(s1_ref, 0, None, None)
    stage_pv(0, qi, alphas)

    def trip(t, ms, cur, nxt):
        stage_pair(nxt, 2 * t + 2)
        for h in range(2):
            ms, alphas = stage_softmax(cur, h, ms, 2 * t + h)
            stage_pv(h, 2 * t + h, alphas)
        return ms

    def body(t, ms):
        return lax.cond(t % 2 == 0,
                        lambda m: trip(t, m, s0_ref, s1_ref),
                        lambda m: trip(t, m, s1_ref, s0_ref), ms)

    lax.fori_loop(0, (qi + 1) // 2, body, ms)
    for g in range(group):
        acc = acc_ref[g]
        o_ref[:, g * dh:(g + 1) * dh] = (acc[:dh] / acc[dh:dh + 1]).astype(BF16).T


def _moba_attention(qt, k, vt, k_mean, batch, seq):
    heads, n_tiles, dh, blk = qt.shape
    dv_rows = vt.shape[2]
    n_blk = seq // blk
    topk = min(MOBA_TOPK, n_blk)
    width = heads * dh
    group = MOBA_HEAD_GROUP
    km = k_mean.reshape(batch, n_blk, width)
    per_seq = lambda b, h, i: (h, b, 0, 0)
    est = group * seq * (dh + dv_rows) * 2 + 2 * group * blk * blk * (2 + 2) + 6 * group * blk * dh * 4
    return pl.pallas_call(
        functools.partial(_moba_kernel, topk=topk, group=group, dh=dh),
        grid=(batch, heads // group, n_blk),
        in_specs=[
            pl.BlockSpec((group, None, dh, blk), lambda b, h, i: (h, b * n_blk + i, 0, 0)),
            pl.BlockSpec((group, n_blk, blk, dh), per_seq, pipeline_mode=pl.Buffered(1)),
            pl.BlockSpec((group, n_blk, dv_rows, blk), per_seq, pipeline_mode=pl.Buffered(1)),
            pl.BlockSpec((None, n_blk, group * dh), lambda b, h, i: (b, 0, h)),
        ],
        out_specs=pl.BlockSpec((blk, group * dh), lambda b, h, i: (b * n_blk + i, h)),
        out_shape=jax.ShapeDtypeStruct((n_tiles * blk, width), BF16),
        scratch_shapes=[pltpu.VMEM((group, n_blk, blk), F32), pltpu.VMEM((group, dv_rows, blk), F32),
                        pltpu.VMEM((group, 2 * blk, blk), BF16), pltpu.VMEM((group, 2 * blk, blk), BF16),
                        pltpu.VMEM((group, 2 * blk, blk), BF16)],
        compiler_params=pltpu.CompilerParams(
            dimension_semantics=("arbitrary", "arbitrary", "arbitrary"), vmem_limit_bytes=_vmem_limit(est)),
        name="moba_attention",
    )(qt, k, vt, km)


def kernel(x, ret_norm, ret_w_in, ret_w_out, kv_norm, w_kv, attn_norm, w_q, w_o, ffn_norm, router_group_w, router_group_b, router_expert_w, router_expert_b, expert_w_gate, expert_w_up, expert_w_down, final_norm):
    B, S, D = x.shape
    T = B * S
    assert S % MOBA_BLOCK == 0 and S % RET_KERNEL_CHUNK == 0 and T % MOE_TILE == 0
    assert ret_norm.shape[0] == 1 and attn_norm.shape[0] == 1 and ffn_norm.shape[0] == 2
    h = x.reshape(T, D)
    final_g = final_norm[None, :]

    def proj_router(y, w, resid, layer, name):
        return _proj_router(y, w.astype(BF16), resid, ffn_norm[layer][None, :], router_group_w[layer],
                            router_group_b[layer], router_expert_w[layer], router_expert_b[layer], name)

    def moe(h_aug, route, counts, layer, last):
        return _moe_layer(h_aug, route, counts, ffn_norm[layer][None, :], final_g,
                          expert_w_gate, expert_w_up, expert_w_down, layer, final_norm=last)

    q, kt, v, gate = _ret_in_proj(h, ret_norm[0][None, :], ret_w_in[0].astype(BF16), S)
    y = _ret_core(q, kt, v, gate, B, S)
    h = moe(*proj_router(y, ret_w_out[0], h, 0, "ret_out_proj_router"), 0, False)

    k2, vt2, qt2, k_mean = _kvq_proj(h, kv_norm[None, :], attn_norm[0][None, :],
                                     w_kv.astype(BF16), w_q[0].astype(BF16), S)
    o = _moba_attention(qt2, k2, vt2, k_mean, B, S)
    h = moe(*proj_router(o, w_o[0], h, 1, "attn_out_proj_router"), 1, True)
    return h.reshape(B, S, D)
```

```python
import functools

import jax
import jax.numpy as jnp
from jax import lax
from jax.experimental import pallas as pl
from jax.experimental.pallas import tpu as pltpu

F32 = jnp.float32
BF16 = jnp.bfloat16

NORM_EPS = 1e-6
RET_HEADS = 4
RET_ROT_BASE = 10000.0
ATT_HEADS = 8
ROPE_FRACTION = 4
ROPE_THETA = 500000.0
MOBA_BLOCK = 256
MOBA_TOPK = 3
N_GROUPS = 4
EXPERTS_PER_GROUP = 4
PAIR_LO = (0, 0, 1, 1, 0, 2)
PAIR_HI = (1, 2, 2, 3, 3, 3)
N_PAIRS = len(PAIR_LO)
N_CLASSES = N_GROUPS * N_PAIRS

LANES = 128
SUBLANES = 8
RET_KERNEL_CHUNK = 256
RET_CHUNKS_PER_TILE = 2
MOE_TILE = 256
ROUTE_ROWS = 32
MOBA_HEAD_GROUP = 8
KVQ_BLOCKS_PER_TILE = 2
MOBA_KEY_CHUNK = 32
VT_ONES_ROWS = 16
LOG2_E = 1.4426950408889634
V7X_VMEM_BYTES = 64 * 1024 * 1024
V7X_VMEM_RESERVED_BYTES = 8 * 1024 * 1024
DEFAULT_SCOPED_VMEM_BYTES = 32 * 1024 * 1024


def _vmem_limit(estimate_bytes):
    wanted = max(DEFAULT_SCOPED_VMEM_BYTES, estimate_bytes * 5 // 4)
    return int(min(V7X_VMEM_BYTES - V7X_VMEM_RESERVED_BYTES, wanted))


def _rms_scale(x):
    return lax.rsqrt(jnp.mean(x * x, axis=-1, keepdims=True) + NORM_EPS)


def _silu(a):
    return a * jax.nn.sigmoid(a)


def _ret_in_kernel(x_ref, g_ref, w_ref, cb_ref, sb_ref, cr_ref, sr_ref, qdec_ref, kdec_ref,
                   q_ref, kt_ref, v_ref, gate_ref, *, heads, dk, dv):
    x = x_ref[...]
    xn = (x * _rms_scale(x) * g_ref[...]).astype(BF16)
    cos, sin = _tile_cos_sin(cb_ref, sb_ref, cr_ref, sr_ref)
    half = dk // 2

    def rotated(col0, h, dec):
        p = jnp.dot(xn, w_ref[:, col0 + h * dk:col0 + (h + 1) * dk], preferred_element_type=F32)
        x1 = p[:, :half]
        x2 = p[:, half:]
        cos_h = cos * dec
        sin_h = sin * dec
        return x1 * cos_h - x2 * sin_h, x2 * cos_h + x1 * sin_h

    cc = kt_ref.shape[3]
    for h in range(heads):
        lo, hi = rotated(0, h, qdec_ref[h])
        q_ref[:, h * dk:h * dk + half] = lo.astype(BF16)
        q_ref[:, h * dk + half:(h + 1) * dk] = hi.astype(BF16)
        lo, hi = rotated(heads * dk, h, kdec_ref[h])
        for c in range(kt_ref.shape[1]):
            rows = slice(c * cc, (c + 1) * cc)
            kt_ref[h, c, :half] = lo[rows].astype(BF16).T
            kt_ref[h, c, half:] = hi[rows].astype(BF16).T
    v0 = 2 * heads * dk
    g0 = v0 + heads * dv
    for h in range(heads):
        v_ref[:, h * dv:(h + 1) * dv] = jnp.dot(
            xn, w_ref[:, v0 + h * dv:v0 + (h + 1) * dv], preferred_element_type=F32).astype(BF16)
        gate_ref[:, h * dv:(h + 1) * dv] = jnp.dot(
            xn, w_ref[:, g0 + h * dv:g0 + (h + 1) * dv], preferred_element_type=F32).astype(BF16)


def _ret_in_proj(h, norm_g, w_bf16, seq):
    T, D = h.shape
    heads = RET_HEADS
    dk = D // heads
    dv = 2 * dk
    n_cols = w_bf16.shape[1]
    cc = RET_KERNEL_CHUNK
    n_sub = RET_CHUNKS_PER_TILE
    tm = n_sub * cc
    assert seq % tm == 0
    n_pos_tiles = seq // tm
    half = dk // 2
    inv = 1.0 / (RET_ROT_BASE ** (jnp.arange(half, dtype=F32) / half))
    cos_b, sin_b, cos_r, sin_r = _angle_tables(inv, seq, tm)
    log_gamma = _ret_log_gamma(heads)
    idx = jnp.tile(jnp.arange(cc, dtype=F32), n_sub)
    q_dec = jnp.broadcast_to(jnp.exp(log_gamma[:, None] * idx)[:, :, None], (heads, tm, half))
    k_dec = jnp.broadcast_to((jnp.exp(-log_gamma[:, None] * idx) * dk ** -0.5)[:, :, None], (heads, tm, half))
    pos = lambda i: (i % n_pos_tiles, 0, 0)
    est = D * n_cols * 2 + 2 * tm * D * 4 + 2 * tm * n_cols * 2 + (4 + 4 * heads) * tm * LANES * 4
    return pl.pallas_call(
        functools.partial(_ret_in_kernel, heads=heads, dk=dk, dv=dv),
        grid=(T // tm,),
        in_specs=[
            pl.BlockSpec((tm, D), lambda i: (i, 0)),
            pl.BlockSpec((1, D), lambda i: (0, 0)),
            pl.BlockSpec((D, n_cols), lambda i: (0, 0), pipeline_mode=pl.Buffered(1)),
            pl.BlockSpec((None, 1, half), pos),
            pl.BlockSpec((None, 1, half), pos),
            pl.BlockSpec((tm, half), lambda i: (0, 0)),
            pl.BlockSpec((tm, half), lambda i: (0, 0)),
            pl.BlockSpec((heads, tm, half), lambda i: (0, 0, 0)),
            pl.BlockSpec((heads, tm, half), lambda i: (0, 0, 0)),
        ],
        out_specs=[
            pl.BlockSpec((tm, heads * dk), lambda i: (i, 0)),
            pl.BlockSpec((heads, n_sub, dk, cc), lambda i: (0, i, 0, 0)),
            pl.BlockSpec((tm, heads * dv), lambda i: (i, 0)),
            pl.BlockSpec((tm, heads * dv), lambda i: (i, 0)),
        ],
        out_shape=[
            jax.ShapeDtypeStruct((T, heads * dk), BF16),
            jax.ShapeDtypeStruct((heads, T // cc, dk, cc), BF16),
            jax.ShapeDtypeStruct((T, heads * dv), BF16),
            jax.ShapeDtypeStruct((T, heads * dv), BF16),
        ],
        compiler_params=pltpu.CompilerParams(
            dimension_semantics=("arbitrary",), vmem_limit_bytes=_vmem_limit(est)),
        name="ret_in_proj",
    )(h, norm_g, w_bf16, cos_b, sin_b, cos_r, sin_r, q_dec, k_dec)


def _ret_log_gamma(heads):
    return jnp.log1p(-jnp.power(2.0, -5.0 - jnp.arange(heads, dtype=F32)))


def _ret_core_kernel(q_ref, kt_ref, v_ref, g_ref, cd_ref, y_ref, state_ref, *, heads):
    @pl.when(pl.program_id(1) == 0)
    def _():
        state_ref[...] = jnp.zeros_like(state_ref)

    n_sub, cc = kt_ref.shape[1], kt_ref.shape[3]
    dk = q_ref.shape[1] // heads
    dv = v_ref.shape[1] // heads
    causal = lax.broadcasted_iota(jnp.int32, (cc, cc), 0) >= lax.broadcasted_iota(jnp.int32, (cc, cc), 1)
    for c in range(n_sub):
        rows = slice(c * cc, (c + 1) * cc)
        for h in range(heads):
            q = q_ref[rows, h * dk:(h + 1) * dk]
            kt = kt_ref[h, c]
            v = v_ref[rows, h * dv:(h + 1) * dv]
            s = jnp.where(causal, jnp.dot(q, kt, preferred_element_type=F32), 0.0)
            u = state_ref[h]
            o = (jnp.dot(s.astype(BF16), v, preferred_element_type=F32)
                 + jnp.dot(q, u.astype(BF16), preferred_element_type=F32))
            state_ref[h] = cd_ref[h] * (u + jnp.dot(kt, v, preferred_element_type=F32))
            o = o * _rms_scale(o)
            gate = g_ref[rows, h * dv:(h + 1) * dv].astype(F32)
            y_ref[rows, h * dv:(h + 1) * dv] = (_silu(gate) * o).astype(BF16)


def _ret_core(q, kt, v, gate, batch, seq):
    T = q.shape[0]
    heads = RET_HEADS
    dk = q.shape[1] // heads
    dv = v.shape[1] // heads
    cc = RET_KERNEL_CHUNK
    n_sub = RET_CHUNKS_PER_TILE
    tm = n_sub * cc
    assert seq % tm == 0
    nc = seq // tm
    chunk_decay = jnp.broadcast_to(jnp.exp(_ret_log_gamma(heads) * cc)[:, None, None], (heads, 1, dv))
    row = lambda b, c: (b * nc + c, 0)
    return pl.pallas_call(
        functools.partial(_ret_core_kernel, heads=heads),
        grid=(batch, nc),
        in_specs=[
            pl.BlockSpec((tm, heads * dk), row),
            pl.BlockSpec((heads, n_sub, dk, cc), lambda b, c: (0, b * nc + c, 0, 0)),
            pl.BlockSpec((tm, heads * dv), row),
            pl.BlockSpec((tm, heads * dv), row),
            pl.BlockSpec((heads, 1, dv), lambda b, c: (0, 0, 0)),
        ],
        out_specs=pl.BlockSpec((tm, heads * dv), row),
        out_shape=jax.ShapeDtypeStruct((T, heads * dv), BF16),
        scratch_shapes=[pltpu.VMEM((heads, dk, dv), F32)],
        compiler_params=pltpu.CompilerParams(dimension_semantics=("arbitrary", "arbitrary")),
        name="ret_core",
    )(q, kt, v, gate, chunk_decay)


def _proj_router_kernel(y_ref, w_ref, r_ref, g_ref, wr_ref, br_ref, haug_ref, route_ref, counts_ref, run_ref,
                        *, groups, experts, d_model):
    @pl.when(pl.program_id(0) == 0)
    def _():
        run_ref[...] = jnp.zeros_like(run_ref)

    h = r_ref[...] + jnp.dot(y_ref[...], w_ref[...], preferred_element_type=F32)
    xn = h * _rms_scale(h) * g_ref[...]
    xh = xn.astype(BF16)
    xl = (xn - xh.astype(F32)).astype(BF16)
    hi_both = jnp.dot(xh, wr_ref[...], preferred_element_type=F32)
    lo_hi = jnp.dot(xl, wr_ref[:, :LANES], preferred_element_type=F32)
    logits = hi_both[:, :LANES] + (hi_both[:, LANES:] + lo_hi) + br_ref[...]
    lt = logits.T[:ROUTE_ROWS]
    tm = lt.shape[1]
    row = lax.broadcasted_iota(jnp.int32, lt.shape, 0)
    neg = -jnp.inf

    def argmax_first(vals):
        m = jnp.max(vals, axis=0, keepdims=True)
        first = jnp.min(jnp.where(vals == m, row, ROUTE_ROWS), axis=0, keepdims=True)
        return m, first

    gl = jnp.where(row < groups, lt, neg)
    gmax, gidx = argmax_first(gl)
    gsum = jnp.sum(jnp.where(row < groups, jnp.exp(gl - gmax), 0.0), axis=0, keepdims=True)
    g_val = 1.0 / gsum
    e0 = groups + experts * gidx
    el = jnp.where((row >= e0) & (row < e0 + experts), lt, neg)
    m1, i1 = argmax_first(el)
    m2, i2 = argmax_first(jnp.where(row == i1, neg, el))
    t = jnp.exp(m2 - m1)
    w1 = 1.0 / (1.0 + t)
    w2 = t / (1.0 + t)
    first_is_lo = i1 < i2
    lo = jnp.minimum(i1, i2) - e0
    hi = jnp.maximum(i1, i2) - e0
    w_lo = g_val * jnp.where(first_is_lo, w1, w2)
    w_hi = g_val * jnp.where(first_is_lo, w2, w1)
    lex = (lo * (7 - lo)) // 2 + (hi - lo - 1)
    pair = jnp.where(lex == 2, 4, jnp.where(lex == 3, 2, jnp.where(lex == 4, 3, lex)))
    cls = gidx * N_PAIRS + pair

    onehot = row == cls
    oh = jnp.where(onehot, 1.0, 0.0)
    before = lax.broadcasted_iota(jnp.int32, (tm, tm), 0) < lax.broadcasted_iota(jnp.int32, (tm, tm), 1)
    prefix = jnp.dot(oh.astype(BF16), jnp.where(before, 1.0, 0.0).astype(BF16), preferred_element_type=F32)
    rank = jnp.sum(jnp.where(onehot, prefix + run_ref[:, 0:1], 0.0), axis=0, keepdims=True)
    run_new = run_ref[...] + jnp.sum(oh, axis=1, keepdims=True)
    run_ref[...] = run_new
    counts_ref[...] = run_new

    r8 = lax.broadcasted_iota(jnp.int32, route_ref.shape, 0)
    route_ref[...] = jnp.where(r8 == 0, cls, jnp.where(r8 == 1, rank.astype(jnp.int32), 0))
    rl = lax.broadcasted_iota(jnp.int32, (LANES, tm), 0)
    meta_t = jnp.where(rl == 0, w_lo, jnp.where(rl == 1, w_hi, 0.0))
    haug_ref[:, :d_model] = h
    haug_ref[:, d_model:] = meta_t.T


def _proj_router(y, w_bf16, resid, norm_g, w_rg, b_rg, w_re, b_re, name, tm=1024):
    T, K = y.shape
    D = w_bf16.shape[1]
    G, E = N_GROUPS, EXPERTS_PER_GROUP
    n_used = G + G * E
    wr = jnp.concatenate([w_rg, jnp.transpose(w_re, (1, 0, 2)).reshape(D, G * E)], axis=1)
    wr = jnp.pad(wr, ((0, 0), (0, LANES - n_used)))
    wr_hi = wr.astype(BF16)
    wr = jnp.concatenate([wr_hi, (wr - wr_hi.astype(F32)).astype(BF16)], axis=1)
    br = jnp.pad(jnp.concatenate([b_rg, b_re.reshape(G * E)]), (0, LANES - n_used))[None, :]
    est = 2 * K * D * 2 + 2 * tm * K * 2 + 6 * tm * D * 4 + 2 * D * LANES * 4 + 4 * tm * tm * 4
    const = lambda i: (0, 0)
    return pl.pallas_call(
        functools.partial(_proj_router_kernel, groups=G, experts=E, d_model=D),
        grid=(T // tm,),
        in_specs=[
            pl.BlockSpec((tm, K), lambda i: (i, 0)),
            pl.BlockSpec((K, D), const),
            pl.BlockSpec((tm, D), lambda i: (i, 0)),
            pl.BlockSpec((1, D), const),
            pl.BlockSpec((D, 2 * LANES), const),
            pl.BlockSpec((1, LANES), const),
        ],
        out_specs=[
            pl.BlockSpec((tm, D + LANES), lambda i: (i, 0)),
            pl.BlockSpec((8, tm), lambda i: (0, i)),
            pl.BlockSpec((ROUTE_ROWS, LANES), const),
        ],
        out_shape=[
            jax.ShapeDtypeStruct((T, D + LANES), F32),
            jax.ShapeDtypeStruct((8, T), jnp.int32),
            jax.ShapeDtypeStruct((ROUTE_ROWS, LANES), F32),
        ],
        scratch_shapes=[pltpu.VMEM((ROUTE_ROWS, LANES), F32)],
        compiler_params=pltpu.CompilerParams(
            dimension_semantics=("arbitrary",), vmem_limit_bytes=_vmem_limit(est)),
        name=name,
    )(y, w_bf16, resid, norm_g, wr, br)


def _tile_schedule(counts_blk, tm, n_tiles):
    counts = counts_blk[:N_CLASSES, 0].astype(jnp.int32)
    tiles_per = (counts + tm - 1) // tm
    tile_end = jnp.cumsum(tiles_per)
    tile_begin = tile_end - tiles_per
    cstart = jnp.cumsum(counts) - counts
    t = jnp.arange(n_tiles, dtype=jnp.int32)
    live = t < tile_end[-1]
    tq = jnp.where(live, t, tile_end[-1] - 1)
    tcls = jnp.sum((tile_end[None, :] <= tq[:, None]).astype(jnp.int32), axis=1)
    onehot = tcls[:, None] == jnp.arange(N_CLASSES, dtype=jnp.int32)[None, :]
    pick = lambda table: jnp.sum(jnp.where(onehot, jnp.asarray(table, jnp.int32)[None, :], 0), axis=1)
    k = tq - pick(tile_begin)
    nvalid = jnp.where(live, jnp.clip(pick(counts) - k * tm, 0, tm), 0).astype(jnp.int32)
    base = jnp.where(live, pick(cstart) + k * tm, 0).astype(jnp.int32)
    tile_g = pick([c // N_PAIRS for c in range(N_CLASSES)])
    tile_lo = pick([PAIR_LO[c % N_PAIRS] for c in range(N_CLASSES)])
    tile_hi = pick([PAIR_HI[c % N_PAIRS] for c in range(N_CLASSES)])
    cstart_pad = jnp.pad(cstart, (0, ROUTE_ROWS - N_CLASSES)).astype(jnp.int32)
    return tile_g, tile_lo, tile_hi, nvalid, base, cstart_pad


def _moe_kernel(tg_ref, tlo_ref, thi_ref, tnv_ref, tbase_ref, cs_ref,
                h_hbm, route_ref, gn_ref, fn_ref, wg_lo, wu_lo, wd_lo, wg_hi, wu_hi, wd_hi,
                out_hbm, rt_ref, pos_smem, pos_vmem, x0, x1, x2, o0, o1, o2, w_in_bf, w_out_bf,
                gsem, ssem, psem, *, tm, d_model, n_tok, final_norm):
    i = pl.program_id(0)
    xbufs = (x0, x1, x2)
    obufs = (o0, o1, o2)
    depth = len(xbufs)

    def issue_gather(t, sl):
        base = tbase_ref[t]
        for r in range(tm):
            tok = rt_ref[base + r]
            pltpu.make_async_copy(h_hbm.at[pl.ds(tok, 1)], xbufs[sl].at[pl.ds(r, 1)], gsem.at[sl]).start()

    def scatter_row(sl, r, tok):
        return pltpu.make_async_copy(obufs[sl].at[pl.ds(r, 1)], out_hbm.at[pl.ds(tok, 1)], ssem.at[sl])

    def issue_scatter(sl, base, n_valid):
        for r in range(tm):
            @pl.when(r < n_valid)
            def _():
                scatter_row(sl, r, rt_ref[base + r]).start()

    def wait_gather(sl):
        pltpu.make_async_copy(h_hbm.at[pl.ds(0, tm)], xbufs[sl], gsem.at[sl]).wait()

    def wait_scatter(sl, n_valid):
        aligned = pl.multiple_of((n_valid // SUBLANES) * SUBLANES, SUBLANES)

        @pl.when(aligned > 0)
        def _():
            rows = pl.ds(0, aligned)
            pltpu.make_async_copy(obufs[sl].at[rows], out_hbm.at[rows], ssem.at[sl]).wait()

        def single(r, c):
            scatter_row(sl, r, 0).wait()
            return c
        lax.fori_loop(aligned, n_valid, single, 0)

    def live(t):
        return tnv_ref[jnp.maximum(t, 0)] > 0

    @pl.when(i == 0)
    def _():
        cls2 = route_ref[0]
        pos = route_ref[1]
        for c in range(N_CLASSES):
            pos = pos + jnp.where(cls2 == c, cs_ref[c], 0)
        pos_vmem[...] = pos
        to_smem = pltpu.make_async_copy(pos_vmem, pos_smem, psem)
        to_smem.start()
        to_smem.wait()

        def place(row, c):
            for col in range(LANES):
                rt_ref[pos_smem[row, col]] = row * LANES + col
            return c
        lax.fori_loop(0, n_tok // LANES, place, 0)
        for r in range(tm):
            rt_ref[n_tok + r] = 0
        issue_gather(0, 0)
        issue_gather(1, 1)

    prev = jnp.maximum(i - 1, 0)
    group_changed = (i == 0) | (tg_ref[i] != tg_ref[prev])
    for which, t_ref, (wg, wu, wd) in ((0, tlo_ref, (wg_lo, wu_lo, wd_lo)), (1, thi_ref, (wg_hi, wu_hi, wd_hi))):
        @pl.when(group_changed | (t_ref[i] != t_ref[prev]))
        def _():
            w_in_bf[2 * which] = wg[...].astype(BF16)
            w_in_bf[2 * which + 1] = wu[...].astype(BF16)
            w_out_bf[which] = wd[...].astype(BF16)

    def tile(cur):
        far = (cur + depth - 1) % depth
        @pl.when((i <= 1) | live(i - 2))
        def _():
            wait_gather(cur)

        @pl.when((i >= depth) & live(i - depth))
        def _():
            wait_scatter(cur, tnv_ref[jnp.maximum(i - depth, 0)])

        prev_base = tbase_ref[prev]
        prev_valid = jnp.where(i == 0, 0, tnv_ref[prev])

        @pl.when(live(i))
        def _():
            issue_scatter(far, prev_base, prev_valid)
            issue_gather(i + 2, far)
            xa = xbufs[cur][...]
            x = xa[:, :d_model]
            xn = (x * _rms_scale(x) * gn_ref[...]).astype(BF16)

            def expert(which, w):
                a = jnp.dot(xn, w_in_bf[2 * which], preferred_element_type=F32)
                b = jnp.dot(xn, w_in_bf[2 * which + 1], preferred_element_type=F32)
                return jnp.dot((_silu(a) * b * w).astype(BF16), w_out_bf[which], preferred_element_type=F32)

            y = expert(0, xa[:, d_model:d_model + 1]) + expert(1, xa[:, d_model + 1:d_model + 2])
            obufs[cur][...] = x + y
            if final_norm:
                for r0 in range(0, tm, 4 * SUBLANES):
                    rows = slice(r0, r0 + 4 * SUBLANES)
                    o = obufs[cur][rows, :]
                    obufs[cur][rows, :] = o * _rms_scale(o) * fn_ref[...]

        @pl.when(jnp.logical_not(live(i)) & (i >= 1) & live(i - 1))
        def _():
            issue_scatter(far, prev_base, prev_valid)

    for residue in range(depth):
        @pl.when(i % depth == residue)
        def _():
            tile(residue)


def _moe_layer(h_aug, route, counts_blk, ffn_norm_g, final_norm_g, wg, wu, wd, layer, final_norm):
    T, DA = h_aug.shape
    D = DA - LANES
    F = wg.shape[-1]
    tm = MOE_TILE
    assert T % (LANES * SUBLANES) == 0 and T >= 2 * tm
    n_tiles = T // tm + N_CLASSES + 3
    tile_g, tile_lo, tile_hi, nvalid, base, cstart = _tile_schedule(counts_blk, tm, n_tiles)
    route2d = route[:2].reshape(2, T // LANES, LANES)

    def w_spec(shape, which):
        def index(i, tg, tlo, thi, *_):
            return (layer, tg[i], (tlo if which == 0 else thi)[i], 0, 0)
        return pl.BlockSpec((None, None, None) + shape, index)

    est = 2 * 6 * D * F * 4 + 6 * D * F * 2 + 3 * tm * DA * 4 + 3 * tm * D * 4 + 8 * tm * F * 4
    return pl.pallas_call(
        functools.partial(_moe_kernel, tm=tm, d_model=D, n_tok=T, final_norm=final_norm),
        grid_spec=pltpu.PrefetchScalarGridSpec(
            num_scalar_prefetch=6,
            grid=(n_tiles,),
            in_specs=[
                pl.BlockSpec(memory_space=pl.ANY),
                pl.BlockSpec((2, T // LANES, LANES), lambda i, *_: (0, 0, 0)),
                pl.BlockSpec((1, D), lambda i, *_: (0, 0)),
                pl.BlockSpec((1, D), lambda i, *_: (0, 0)),
                w_spec((D, F), 0), w_spec((D, F), 0), w_spec((F, D), 0),
                w_spec((D, F), 1), w_spec((D, F), 1), w_spec((F, D), 1),
            ],
            out_specs=pl.BlockSpec(memory_space=pl.ANY),
            scratch_shapes=[
                pltpu.SMEM((T + tm,), jnp.int32),
                pltpu.SMEM((T // LANES, LANES), jnp.int32),
                pltpu.VMEM((T // LANES, LANES), jnp.int32),
                pltpu.VMEM((tm, DA), F32), pltpu.VMEM((tm, DA), F32), pltpu.VMEM((tm, DA), F32),
                pltpu.VMEM((tm, D), F32), pltpu.VMEM((tm, D), F32), pltpu.VMEM((tm, D), F32),
                pltpu.VMEM((4, D, F), BF16),
                pltpu.VMEM((2, F, D), BF16),
                pltpu.SemaphoreType.DMA((3,)),
                pltpu.SemaphoreType.DMA((3,)),
                pltpu.SemaphoreType.DMA(()),
            ],
        ),
        out_shape=jax.ShapeDtypeStruct((T, D), F32),
        compiler_params=pltpu.CompilerParams(
            dimension_semantics=("arbitrary",), vmem_limit_bytes=_vmem_limit(est)),
        name="moe_experts",
    )(tile_g, tile_lo, tile_hi, nvalid, base, cstart,
      h_aug, route2d, ffn_norm_g, final_norm_g, wg, wu, wd, wg, wu, wd)


def _partial_rotary(xh, cosf, sinp, sinm, half):
    lanes = xh.shape[-1]
    return xh * cosf + pltpu.roll(xh, half, 1) * sinp + pltpu.roll(xh, lanes - half, 1) * sinm


def _tile_cos_sin(cb_ref, sb_ref, cr_ref, sr_ref):
    cb, sb, cr, sr = cb_ref[...], sb_ref[...], cr_ref[...], sr_ref[...]
    return cb * cr - sb * sr, sb * cr + cb * sr


def _kvq_kernel(h_ref, gkv_ref, gq_ref, wkv_ref, wq_ref, cb_ref, sb_ref, cr_ref, sr_ref,
                k_ref, vt_ref, qt_ref, km_ref, *, heads, dh, half):
    x = h_ref[...]
    xr = x * _rms_scale(x)
    xkv = (xr * gkv_ref[...]).astype(BF16)
    xq = (xr * gq_ref[...]).astype(BF16)
    cosf, sin_all = _tile_cos_sin(cb_ref, sb_ref, cr_ref, sr_ref)
    lane = lax.broadcasted_iota(jnp.int32, sin_all.shape, 1)
    sinp = jnp.where(lane >= half, sin_all, 0.0)
    sinm = jnp.where(lane < half, -sin_all, 0.0)
    width = heads * dh
    q_scale = dh ** -0.5 * LOG2_E
    pair_w = 2 * dh
    blk = k_ref.shape[2]
    n_sub = x.shape[0] // blk
    for c0 in range(0, width, pair_w):
        kk = jnp.dot(xkv, wkv_ref[:, c0:c0 + pair_w], preferred_element_type=F32)
        vv = jnp.dot(xkv, wkv_ref[:, width + c0:width + c0 + pair_w], preferred_element_type=F32)
        qq = jnp.dot(xq, wq_ref[:, c0:c0 + pair_w], preferred_element_type=F32)
        for u in range(2):
            hh = c0 // dh + u
            cols = slice(hh * dh, (hh + 1) * dh)
            kh = _partial_rotary(kk[:, u * dh:(u + 1) * dh], cosf, sinp, sinm, half)
            qh = _partial_rotary(qq[:, u * dh:(u + 1) * dh], cosf, sinp, sinm, half) * q_scale
            vh = vv[:, u * dh:(u + 1) * dh]
            for b in range(n_sub):
                rows = slice(b * blk, (b + 1) * blk)
                k_ref[hh, b] = kh[rows].astype(BF16)
                km_ref[b, :, cols] = jnp.mean(kh[rows], axis=0, keepdims=True)
                vt_ref[hh, b, :dh] = vh[rows].astype(BF16).T
                vt_ref[hh, b, dh:] = jnp.ones((VT_ONES_ROWS, blk), BF16)
                qt_ref[hh, b] = qh[rows].astype(BF16).T


def _angle_tables(inv_freq, seq, tm):
    base = jnp.arange(0, seq, tm, dtype=F32)[:, None] * inv_freq[None, :]
    offs = jnp.arange(tm, dtype=F32)[:, None] * inv_freq[None, :]
    return jnp.cos(base)[:, None, :], jnp.sin(base)[:, None, :], jnp.cos(offs), jnp.sin(offs)


def _rope_tables(seq, dh, tm):
    rot = dh // ROPE_FRACTION
    half = rot // 2
    inv = 1.0 / (ROPE_THETA ** (jnp.arange(half, dtype=F32) / half))
    inv_lanes = jnp.concatenate([inv, inv, jnp.zeros((dh - rot,), F32)])
    return _angle_tables(inv_lanes, seq, tm) + (half,)


def _kvq_proj(h, kv_norm_g, q_norm_g, wkv_bf16, wq_bf16, seq):
    T, D = h.shape
    heads = ATT_HEADS
    dh = D // heads
    blk = MOBA_BLOCK
    n_sub = KVQ_BLOCKS_PER_TILE
    tm = n_sub * blk
    assert seq % tm == 0
    n_pos_tiles = seq // tm
    cos_b, sin_b, cos_r, sin_r, half = _rope_tables(seq, dh, tm)
    width = heads * dh
    est = 2 * (D * 2 * width + D * width) * 2 + 2 * tm * D * 4 + 6 * tm * width * 2 + 8 * tm * dh * 4
    pos = lambda i: (i % n_pos_tiles, 0, 0)
    const = lambda i: (0, 0)
    per_block = lambda i: (0, i, 0, 0)
    return pl.pallas_call(
        functools.partial(_kvq_kernel, heads=heads, dh=dh, half=half),
        grid=(T // tm,),
        in_specs=[
            pl.BlockSpec((tm, D), lambda i: (i, 0)),
            pl.BlockSpec((1, D), const),
            pl.BlockSpec((1, D), const),
            pl.BlockSpec((D, 2 * width), const),
            pl.BlockSpec((D, width), const),
            pl.BlockSpec((None, 1, dh), pos),
            pl.BlockSpec((None, 1, dh), pos),
            pl.BlockSpec((tm, dh), const),
            pl.BlockSpec((tm, dh), const),
        ],
        out_specs=[
            pl.BlockSpec((heads, n_sub, blk, dh), per_block),
            pl.BlockSpec((heads, n_sub, dh + VT_ONES_ROWS, blk), per_block),
            pl.BlockSpec((heads, n_sub, dh, blk), per_block),
            pl.BlockSpec((n_sub, 1, width), lambda i: (i, 0, 0)),
        ],
        out_shape=[
            jax.ShapeDtypeStruct((heads, T // blk, blk, dh), BF16),
            jax.ShapeDtypeStruct((heads, T // blk, dh + VT_ONES_ROWS, blk), BF16),
            jax.ShapeDtypeStruct((heads, T // blk, dh, blk), BF16),
            jax.ShapeDtypeStruct((T // blk, 1, width), F32),
        ],
        compiler_params=pltpu.CompilerParams(
            dimension_semantics=("arbitrary",), vmem_limit_bytes=_vmem_limit(est)),
        name="kvq_proj",
    )(h, kv_norm_g, q_norm_g, wkv_bf16, wq_bf16, cos_b, sin_b, cos_r, sin_r)


def _moba_kernel(qt_ref, k_ref, vt_ref, km_ref, o_ref, sel_ref, acc_ref, s0_ref, s1_ref, p_ref,
                 *, topk, group, dh):
    qi = pl.program_id(2)
    n_blk = km_ref.shape[0]
    blk = s0_ref.shape[1] // 2
    neg = -jnp.inf
    halves = (slice(0, blk), slice(blk, 2 * blk))

    def chunks(h):
        return [slice(c, c + MOBA_KEY_CHUNK) for c in range(h * blk, (h + 1) * blk, MOBA_KEY_CHUNK)]

    def stage_pair(s_buf, j):
        jp = jnp.minimum(j, n_blk - 2)
        for g in range(group):
            keys = k_ref[g, pl.ds(jp, 2)].reshape(2 * blk, dh)
            s_buf[g] = jnp.dot(keys, qt_ref[g], preferred_element_type=F32).astype(BF16)

    def stage_softmax(s_buf, h, ms, j):
        rows = chunks(h)
        new_m, alphas = [], []
        for g in range(group):
            part = s_buf[g, rows[0], :]
            for c in rows[1:]:
                part = jnp.maximum(part, s_buf[g, c, :])
            m_blk = jnp.max(part, axis=0, keepdims=True).astype(F32)
            if j is None:
                m_new = shift = m_blk
            else:
                bias = sel_ref[g, pl.ds(j, 1), :]
                m_new = jnp.maximum(ms[g], m_blk + bias)
                shift = m_new - bias
            shift = shift.astype(BF16)
            for c in rows:
                p_ref[g, c, :] = jnp.exp2(s_buf[g, c, :] - shift)
            new_m.append(m_new)
            alphas.append(None if j is None else jnp.exp2(ms[g] - m_new))
        return tuple(new_m), alphas

    def stage_pv(h, j, alphas):
        for g in range(group):
            pv = jnp.dot(vt_ref[g, j], p_ref[g, halves[h], :], preferred_element_type=F32)
            acc_ref[g] = pv if alphas[g] is None else alphas[g] * acc_ref[g] + pv

    for g in range(group):
        km = km_ref[:, g * dh:(g + 1) * dh]
        km_hi = km.astype(BF16)
        km_lo = (km - km_hi.astype(F32)).astype(BF16)
        gate = (jnp.dot(km_hi, qt_ref[g], preferred_element_type=F32)
                + jnp.dot(km_lo, qt_ref[g], preferred_element_type=F32))
        blk_id = lax.broadcasted_iota(jnp.int32, gate.shape, 0)
        gv = jnp.where(blk_id < qi, gate, neg)
        bias = jnp.full(gate.shape, neg, F32)
        for _ in range(topk):
            m = jnp.max(gv, axis=0, keepdims=True)
            first = jnp.min(jnp.where(gv == m, blk_id, n_blk), axis=0, keepdims=True)
            bias = jnp.where((blk_id == first) & (m > neg), 0.0, bias)
            gv = jnp.where(blk_id == first, neg, gv)
        sel_ref[g] = bias

    key_id = lax.broadcasted_iota(jnp.int32, (blk, blk), 0)
    qry_id = lax.broadcasted_iota(jnp.int32, (blk, blk), 1)
    for g in range(group):
        own = jnp.dot(k_ref[g, qi], qt_ref[g], preferred_element_type=F32)
        s1_ref[g, halves[0], :] = jnp.where(key_id <= qry_id, own, neg).astype(BF16)
    stage_pair(s0_ref, 0)
    ms, alphas = stage_softmax(s1_ref, 0, None, None)
    stage_pv(0, qi, alphas)

    def trip(t, ms, cur, nxt):
        stage_pair(nxt, 2 * t + 2)
        for h in range(2):
            ms, alphas = stage_softmax(cur, h, ms, 2 * t + h)
            stage_pv(h, 2 * t + h, alphas)
        return ms

    def body(t, ms):
        return lax.cond(t % 2 == 0,
                        lambda m: trip(t, m, s0_ref, s1_ref),
                        lambda m: trip(t, m, s1_ref, s0_ref), ms)

    lax.fori_loop(0, (qi + 1) // 2, body, ms)
    for g in range(group):
        acc = acc_ref[g]
        o_ref[:, g * dh:(g + 1) * dh] = (acc[:dh] / acc[dh:dh + 1]).astype(BF16).T


def _moba_attention(qt, k, vt, k_mean, batch, seq):
    heads, n_tiles, dh, blk = qt.shape
    dv_rows = vt.shape[2]
    n_blk = seq // blk
    topk = min(MOBA_TOPK, n_blk)
    width = heads * dh
    group = MOBA_HEAD_GROUP
    km = k_mean.reshape(batch, n_blk, width)
    per_seq = lambda b, h, i: (h, b, 0, 0)
    est = group * seq * (dh + dv_rows) * 2 + 2 * group * blk * blk * (2 + 2) + 6 * group * blk * dh * 4
    return pl.pallas_call(
        functools.partial(_moba_kernel, topk=topk, group=group, dh=dh),
        grid=(batch, heads // group, n_blk),
        in_specs=[
            pl.BlockSpec((group, None, dh, blk), lambda b, h, i: (h, b * n_blk + i, 0, 0)),
            pl.BlockSpec((group, n_blk, blk, dh), per_seq, pipeline_mode=pl.Buffered(1)),
            pl.BlockSpec((group, n_blk, dv_rows, blk), per_seq, pipeline_mode=pl.Buffered(1)),
            pl.BlockSpec((None, n_blk, group * dh), lambda b, h, i: (b, 0, h)),
        ],
        out_specs=pl.BlockSpec((blk, group * dh), lambda b, h, i: (b * n_blk + i, h)),
        out_shape=jax.ShapeDtypeStruct((n_tiles * blk, width), BF16),
        scratch_shapes=[pltpu.VMEM((group, n_blk, blk), F32), pltpu.VMEM((group, dv_rows, blk), F32),
                        pltpu.VMEM((group, 2 * blk, blk), BF16), pltpu.VMEM((group, 2 * blk, blk), BF16),
                        pltpu.VMEM((group, 2 * blk, blk), BF16)],
        compiler_params=pltpu.CompilerParams(
            dimension_semantics=("arbitrary", "arbitrary", "arbitrary"), vmem_limit_bytes=_vmem_limit(est)),
        name="moba_attention",
    )(qt, k, vt, km)


def kernel(x, ret_norm, ret_w_in, ret_w_out, kv_norm, w_kv, attn_norm, w_q, w_o, ffn_norm, router_group_w, router_group_b, router_expert_w, router_expert_b, expert_w_gate, expert_w_up, expert_w_down, final_norm):
    B, S, D = x.shape
    T = B * S
    assert S % MOBA_BLOCK == 0 and S % RET_KERNEL_CHUNK == 0 and T % MOE_TILE == 0
    assert ret_norm.shape[0] == 1 and attn_norm.shape[0] == 1 and ffn_norm.shape[0] == 2
    h = x.reshape(T, D)
    final_g = final_norm[None, :]

    def proj_router(y, w, resid, layer, name):
        return _proj_router(y, w.astype(BF16), resid, ffn_norm[layer][None, :], router_group_w[layer],
                            router_group_b[layer], router_expert_w[layer], router_expert_b[layer], name)

    def moe(h_aug, route, counts, layer, last):
        return _moe_layer(h_aug, route, counts, ffn_norm[layer][None, :], final_g,
                          expert_w_gate, expert_w_up, expert_w_down, layer, final_norm=last)

    q, kt, v, gate = _ret_in_proj(h, ret_norm[0][None, :], ret_w_in[0].astype(BF16), S)
    y = _ret_core(q, kt, v, gate, B, S)
    h = moe(*proj_router(y, ret_w_out[0], h, 0, "ret_out_proj_router"), 0, False)

    k2, vt2, qt2, k_mean = _kvq_proj(h, kv_norm[None, :], attn_norm[0][None, :],
                                     w_kv.astype(BF16), w_q[0].astype(BF16), S)
    o = _moba_attention(qt2, k2, vt2, k_mean, B, S)
    h = moe(*proj_router(o, w_o[0], h, 1, "attn_out_proj_router"), 1, True)
    return h.reshape(B, S, D)
```

```python
import functools

import jax
import jax.numpy as jnp
from jax import lax
from jax.experimental import pallas as pl
from jax.experimental.pallas import tpu as pltpu

F32 = jnp.float32
BF16 = jnp.bfloat16

NORM_EPS = 1e-6
RET_HEADS = 4
RET_ROT_BASE = 10000.0
ATT_HEADS = 8
ROPE_FRACTION = 4
ROPE_THETA = 500000.0
MOBA_BLOCK = 256
MOBA_TOPK = 3
N_GROUPS = 4
EXPERTS_PER_GROUP = 4
PAIR_LO = (0, 0, 1, 1, 0, 2)
PAIR_HI = (1, 2, 2, 3, 3, 3)
N_PAIRS = len(PAIR_LO)
N_CLASSES = N_GROUPS * N_PAIRS

LANES = 128
SUBLANES = 8
RET_KERNEL_CHUNK = 256
RET_CHUNKS_PER_TILE = 2
MOE_TILE = 256
ROUTE_ROWS = 32
MOBA_HEAD_GROUP = 8
KVQ_BLOCKS_PER_TILE = 2
MOBA_KEY_CHUNK = 32
VT_ONES_ROWS = 16
LOG2_E = 1.4426950408889634
V7X_VMEM_BYTES = 64 * 1024 * 1024
V7X_VMEM_RESERVED_BYTES = 8 * 1024 * 1024
DEFAULT_SCOPED_VMEM_BYTES = 32 * 1024 * 1024


def _vmem_limit(estimate_bytes):
    wanted = max(DEFAULT_SCOPED_VMEM_BYTES, estimate_bytes * 5 // 4)
    return int(min(V7X_VMEM_BYTES - V7X_VMEM_RESERVED_BYTES, wanted))


def _rms_scale(x):
    return lax.rsqrt(jnp.mean(x * x, axis=-1, keepdims=True) + NORM_EPS)


def _silu(a):
    return a * jax.nn.sigmoid(a)


def _ret_in_kernel(x_ref, g_ref, w_ref, cb_ref, sb_ref, cr_ref, sr_ref, qdec_ref, kdec_ref,
                   q_ref, kt_ref, v_ref, gate_ref, *, heads, dk, dv):
    x = x_ref[...]
    xn = (x * _rms_scale(x) * g_ref[...]).astype(BF16)
    cos, sin = _tile_cos_sin(cb_ref, sb_ref, cr_ref, sr_ref)
    half = dk // 2

    def rotated(col0, h, dec):
        p = jnp.dot(xn, w_ref[:, col0 + h * dk:col0 + (h + 1) * dk], preferred_element_type=F32)
        x1 = p[:, :half]
        x2 = p[:, half:]
        cos_h = cos * dec
        sin_h = sin * dec
        return x1 * cos_h - x2 * sin_h, x2 * cos_h + x1 * sin_h

    cc = kt_ref.shape[3]
    for h in range(heads):
        lo, hi = rotated(0, h, qdec_ref[h])
        q_ref[:, h * dk:h * dk + half] = lo.astype(BF16)
        q_ref[:, h * dk + half:(h + 1) * dk] = hi.astype(BF16)
        lo, hi = rotated(heads * dk, h, kdec_ref[h])
        for c in range(kt_ref.shape[1]):
            rows = slice(c * cc, (c + 1) * cc)
            kt_ref[h, c, :half] = lo[rows].astype(BF16).T
            kt_ref[h, c, half:] = hi[rows].astype(BF16).T
    v0 = 2 * heads * dk
    g0 = v0 + heads * dv
    for h in range(heads):
        v_ref[:, h * dv:(h + 1) * dv] = jnp.dot(
            xn, w_ref[:, v0 + h * dv:v0 + (h + 1) * dv], preferred_element_type=F32).astype(BF16)
        gate_ref[:, h * dv:(h + 1) * dv] = jnp.dot(
            xn, w_ref[:, g0 + h * dv:g0 + (h + 1) * dv], preferred_element_type=F32).astype(BF16)


def _ret_in_proj(h, norm_g, w_bf16, seq):
    T, D = h.shape
    heads = RET_HEADS
    dk = D // heads
    dv = 2 * dk
    n_cols = w_bf16.shape[1]
    cc = RET_KERNEL_CHUNK
    n_sub = RET_CHUNKS_PER_TILE
    tm = n_sub * cc
    assert seq % tm == 0
    n_pos_tiles = seq // tm
    half = dk // 2
    inv = 1.0 / (RET_ROT_BASE ** (jnp.arange(half, dtype=F32) / half))
    cos_b, sin_b, cos_r, sin_r = _angle_tables(inv, seq, tm)
    log_gamma = _ret_log_gamma(heads)
    idx = jnp.tile(jnp.arange(cc, dtype=F32), n_sub)
    q_dec = jnp.broadcast_to(jnp.exp(log_gamma[:, None] * idx)[:, :, None], (heads, tm, half))
    k_dec = jnp.broadcast_to((jnp.exp(-log_gamma[:, None] * idx) * dk ** -0.5)[:, :, None], (heads, tm, half))
    pos = lambda i: (i % n_pos_tiles, 0, 0)
    est = D * n_cols * 2 + 2 * tm * D * 4 + 2 * tm * n_cols * 2 + (4 + 4 * heads) * tm * LANES * 4
    return pl.pallas_call(
        functools.partial(_ret_in_kernel, heads=heads, dk=dk, dv=dv),
        grid=(T // tm,),
        in_specs=[
            pl.BlockSpec((tm, D), lambda i: (i, 0)),
            pl.BlockSpec((1, D), lambda i: (0, 0)),
            pl.BlockSpec((D, n_cols), lambda i: (0, 0), pipeline_mode=pl.Buffered(1)),
            pl.BlockSpec((None, 1, half), pos),
            pl.BlockSpec((None, 1, half), pos),
            pl.BlockSpec((tm, half), lambda i: (0, 0)),
            pl.BlockSpec((tm, half), lambda i: (0, 0)),
            pl.BlockSpec((heads, tm, half), lambda i: (0, 0, 0)),
            pl.BlockSpec((heads, tm, half), lambda i: (0, 0, 0)),
        ],
        out_specs=[
            pl.BlockSpec((tm, heads * dk), lambda i: (i, 0)),
            pl.BlockSpec((heads, n_sub, dk, cc), lambda i: (0, i, 0, 0)),
            pl.BlockSpec((tm, heads * dv), lambda i: (i, 0)),
            pl.BlockSpec((tm, heads * dv), lambda i: (i, 0)),
        ],
        out_shape=[
            jax.ShapeDtypeStruct((T, heads * dk), BF16),
            jax.ShapeDtypeStruct((heads, T // cc, dk, cc), BF16),
            jax.ShapeDtypeStruct((T, heads * dv), BF16),
            jax.ShapeDtypeStruct((T, heads * dv), BF16),
        ],
        compiler_params=pltpu.CompilerParams(
            dimension_semantics=("arbitrary",), vmem_limit_bytes=_vmem_limit(est)),
        name="ret_in_proj",
    )(h, norm_g, w_bf16, cos_b, sin_b, cos_r, sin_r, q_dec, k_dec)


def _ret_log_gamma(heads):
    return jnp.log1p(-jnp.power(2.0, -5.0 - jnp.arange(heads, dtype=F32)))


def _ret_core_kernel(q_ref, kt_ref, v_ref, g_ref, cd_ref, y_ref, state_ref, *, heads):
    @pl.when(pl.program_id(1) == 0)
    def _():
        state_ref[...] = jnp.zeros_like(state_ref)

    n_sub, cc = kt_ref.shape[1], kt_ref.shape[3]
    dk = q_ref.shape[1] // heads
    dv = v_ref.shape[1] // heads
    causal = lax.broadcasted_iota(jnp.int32, (cc, cc), 0) >= lax.broadcasted_iota(jnp.int32, (cc, cc), 1)
    for c in range(n_sub):
        rows = slice(c * cc, (c + 1) * cc)
        for h in range(heads):
            q = q_ref[rows, h * dk:(h + 1) * dk]
            kt = kt_ref[h, c]
            v = v_ref[rows, h * dv:(h + 1) * dv]
            s = jnp.where(causal, jnp.dot(q, kt, preferred_element_type=F32), 0.0)
            u = state_ref[h]
            o = (jnp.dot(s.astype(BF16), v, preferred_element_type=F32)
                 + jnp.dot(q, u.astype(BF16), preferred_element_type=F32))
            state_ref[h] = cd_ref[h] * (u + jnp.dot(kt, v, preferred_element_type=F32))
            o = o * _rms_scale(o)
            gate = g_ref[rows, h * dv:(h + 1) * dv].astype(F32)
            y_ref[rows, h * dv:(h + 1) * dv] = (_silu(gate) * o).astype(BF16)


def _ret_core(q, kt, v, gate, batch, seq):
    T = q.shape[0]
    heads = RET_HEADS
    dk = q.shape[1] // heads
    dv = v.shape[1] // heads
    cc = RET_KERNEL_CHUNK
    n_sub = RET_CHUNKS_PER_TILE
    tm = n_sub * cc
    assert seq % tm == 0
    nc = seq // tm
    chunk_decay = jnp.broadcast_to(jnp.exp(_ret_log_gamma(heads) * cc)[:, None, None], (heads, 1, dv))
    row = lambda b, c: (b * nc + c, 0)
    return pl.pallas_call(
        functools.partial(_ret_core_kernel, heads=heads),
        grid=(batch, nc),
        in_specs=[
            pl.BlockSpec((tm, heads * dk), row),
            pl.BlockSpec((heads, n_sub, dk, cc), lambda b, c: (0, b * nc + c, 0, 0)),
            pl.BlockSpec((tm, heads * dv), row),
            pl.BlockSpec((tm, heads * dv), row),
            pl.BlockSpec((heads, 1, dv), lambda b, c: (0, 0, 0)),
        ],
        out_specs=pl.BlockSpec((tm, heads * dv), row),
        out_shape=jax.ShapeDtypeStruct((T, heads * dv), BF16),
        scratch_shapes=[pltpu.VMEM((heads, dk, dv), F32)],
        compiler_params=pltpu.CompilerParams(dimension_semantics=("arbitrary", "arbitrary")),
        name="ret_core",
    )(q, kt, v, gate, chunk_decay)


def _proj_router_kernel(y_ref, w_ref, r_ref, g_ref, wr_ref, br_ref, haug_ref, route_ref, counts_ref, run_ref,
                        *, groups, experts, d_model):
    @pl.when(pl.program_id(0) == 0)
    def _():
        run_ref[...] = jnp.zeros_like(run_ref)

    h = r_ref[...] + jnp.dot(y_ref[...], w_ref[...], preferred_element_type=F32)
    xn = h * _rms_scale(h) * g_ref[...]
    xh = xn.astype(BF16)
    xl = (xn - xh.astype(F32)).astype(BF16)
    hi_both = jnp.dot(xh, wr_ref[...], preferred_element_type=F32)
    lo_hi = jnp.dot(xl, wr_ref[:, :LANES], preferred_element_type=F32)
    logits = hi_both[:, :LANES] + (hi_both[:, LANES:] + lo_hi) + br_ref[...]
    lt = logits.T[:ROUTE_ROWS]
    tm = lt.shape[1]
    row = lax.broadcasted_iota(jnp.int32, lt.shape, 0)
    neg = -jnp.inf

    def argmax_first(vals):
        m = jnp.max(vals, axis=0, keepdims=True)
        first = jnp.min(jnp.where(vals == m, row, ROUTE_ROWS), axis=0, keepdims=True)
        return m, first

    gl = jnp.where(row < groups, lt, neg)
    gmax, gidx = argmax_first(gl)
    gsum = jnp.sum(jnp.where(row < groups, jnp.exp(gl - gmax), 0.0), axis=0, keepdims=True)
    g_val = 1.0 / gsum
    e0 = groups + experts * gidx
    el = jnp.where((row >= e0) & (row < e0 + experts), lt, neg)
    m1, i1 = argmax_first(el)
    m2, i2 = argmax_first(jnp.where(row == i1, neg, el))
    t = jnp.exp(m2 - m1)
    w1 = 1.0 / (1.0 + t)
    w2 = t / (1.0 + t)
    first_is_lo = i1 < i2
    lo = jnp.minimum(i1, i2) - e0
    hi = jnp.maximum(i1, i2) - e0
    w_lo = g_val * jnp.where(first_is_lo, w1, w2)
    w_hi = g_val * jnp.where(first_is_lo, w2, w1)
    lex = (lo * (7 - lo)) // 2 + (hi - lo - 1)
    pair = jnp.where(lex == 2, 4, jnp.where(lex == 3, 2, jnp.where(lex == 4, 3, lex)))
    cls = gidx * N_PAIRS + pair

    onehot = row == cls
    oh = jnp.where(onehot, 1.0, 0.0)
    before = lax.broadcasted_iota(jnp.int32, (tm, tm), 0) < lax.broadcasted_iota(jnp.int32, (tm, tm), 1)
    prefix = jnp.dot(oh.astype(BF16), jnp.where(before, 1.0, 0.0).astype(BF16), preferred_element_type=F32)
    rank = jnp.sum(jnp.where(onehot, prefix + run_ref[:, 0:1], 0.0), axis=0, keepdims=True)
    run_new = run_ref[...] + jnp.sum(oh, axis=1, keepdims=True)
    run_ref[...] = run_new
    counts_ref[...] = run_new

    r8 = lax.broadcasted_iota(jnp.int32, route_ref.shape, 0)
    route_ref[...] = jnp.where(r8 == 0, cls, jnp.where(r8 == 1, rank.astype(jnp.int32), 0))
    rl = lax.broadcasted_iota(jnp.int32, (LANES, tm), 0)
    meta_t = jnp.where(rl == 0, w_lo, jnp.where(rl == 1, w_hi, 0.0))
    haug_ref[:, :d_model] = h
    haug_ref[:, d_model:] = meta_t.T


def _proj_router(y, w_bf16, resid, norm_g, w_rg, b_rg, w_re, b_re, name, tm=1024):
    T, K = y.shape
    D = w_bf16.shape[1]
    G, E = N_GROUPS, EXPERTS_PER_GROUP
    n_used = G + G * E
    wr = jnp.concatenate([w_rg, jnp.transpose(w_re, (1, 0, 2)).reshape(D, G * E)], axis=1)
    wr = jnp.pad(wr, ((0, 0), (0, LANES - n_used)))
    wr_hi = wr.astype(BF16)
    wr = jnp.concatenate([wr_hi, (wr - wr_hi.astype(F32)).astype(BF16)], axis=1)
    br = jnp.pad(jnp.concatenate([b_rg, b_re.reshape(G * E)]), (0, LANES - n_used))[None, :]
    est = 2 * K * D * 2 + 2 * tm * K * 2 + 6 * tm * D * 4 + 2 * D * LANES * 4 + 4 * tm * tm * 4
    const = lambda i: (0, 0)
    return pl.pallas_call(
        functools.partial(_proj_router_kernel, groups=G, experts=E, d_model=D),
        grid=(T // tm,),
        in_specs=[
            pl.BlockSpec((tm, K), lambda i: (i, 0)),
            pl.BlockSpec((K, D), const),
            pl.BlockSpec((tm, D), lambda i: (i, 0)),
            pl.BlockSpec((1, D), const),
            pl.BlockSpec((D, 2 * LANES), const),
            pl.BlockSpec((1, LANES), const),
        ],
        out_specs=[
            pl.BlockSpec((tm, D + LANES), lambda i: (i, 0)),
            pl.BlockSpec((8, tm), lambda i: (0, i)),
            pl.BlockSpec((ROUTE_ROWS, LANES), const),
        ],
        out_shape=[
            jax.ShapeDtypeStruct((T, D + LANES), F32),
            jax.ShapeDtypeStruct((8, T), jnp.int32),
            jax.ShapeDtypeStruct((ROUTE_ROWS, LANES), F32),
        ],
        scratch_shapes=[pltpu.VMEM((ROUTE_ROWS, LANES), F32)],
        compiler_params=pltpu.CompilerParams(
            dimension_semantics=("arbitrary",), vmem_limit_bytes=_vmem_limit(est)),
        name=name,
    )(y, w_bf16, resid, norm_g, wr, br)


def _tile_schedule(counts_blk, tm, n_tiles):
    counts = counts_blk[:N_CLASSES, 0].astype(jnp.int32)
    tiles_per = (counts + tm - 1) // tm
    tile_end = jnp.cumsum(tiles_per)
    tile_begin = tile_end - tiles_per
    cstart = jnp.cumsum(counts) - counts
    t = jnp.arange(n_tiles, dtype=jnp.int32)
    live = t < tile_end[-1]
    tq = jnp.where(live, t, tile_end[-1] - 1)
    tcls = jnp.sum((tile_end[None, :] <= tq[:, None]).astype(jnp.int32), axis=1)
    onehot = tcls[:, None] == jnp.arange(N_CLASSES, dtype=jnp.int32)[None, :]
    pick = lambda table: jnp.sum(jnp.where(onehot, jnp.asarray(table, jnp.int32)[None, :], 0), axis=1)
    k = tq - pick(tile_begin)
    nvalid = jnp.where(live, jnp.clip(pick(counts) - k * tm, 0, tm), 0).astype(jnp.int32)
    base = jnp.where(live, pick(cstart) + k * tm, 0).astype(jnp.int32)
    tile_g = pick([c // N_PAIRS for c in range(N_CLASSES)])
    tile_lo = pick([PAIR_LO[c % N_PAIRS] for c in range(N_CLASSES)])
    tile_hi = pick([PAIR_HI[c % N_PAIRS] for c in range(N_CLASSES)])
    cstart_pad = jnp.pad(cstart, (0, ROUTE_ROWS - N_CLASSES)).astype(jnp.int32)
    return tile_g, tile_lo, tile_hi, nvalid, base, cstart_pad


def _moe_kernel(tg_ref, tlo_ref, thi_ref, tnv_ref, tbase_ref, cs_ref,
                h_hbm, route_ref, gn_ref, fn_ref, wg_lo, wu_lo, wd_lo, wg_hi, wu_hi, wd_hi,
                out_hbm, rt_ref, pos_smem, pos_vmem, x0, x1, x2, o0, o1, o2, w_in_bf, w_out_bf,
                gsem, ssem, psem, *, tm, d_model, n_tok, final_norm):
    i = pl.program_id(0)
    xbufs = (x0, x1, x2)
    obufs = (o0, o1, o2)
    depth = len(xbufs)

    def issue_gather(t, sl):
        base = tbase_ref[t]
        for r in range(tm):
            tok = rt_ref[base + r]
            pltpu.make_async_copy(h_hbm.at[pl.ds(tok, 1)], xbufs[sl].at[pl.ds(r, 1)], gsem.at[sl]).start()

    def scatter_row(sl, r, tok):
        return pltpu.make_async_copy(obufs[sl].at[pl.ds(r, 1)], out_hbm.at[pl.ds(tok, 1)], ssem.at[sl])

    def issue_scatter(sl, base, n_valid):
        for r in range(tm):
            @pl.when(r < n_valid)
            def _():
                scatter_row(sl, r, rt_ref[base + r]).start()

    def wait_gather(sl):
        pltpu.make_async_copy(h_hbm.at[pl.ds(0, tm)], xbufs[sl], gsem.at[sl]).wait()

    def wait_scatter(sl, n_valid):
        aligned = pl.multiple_of((n_valid // SUBLANES) * SUBLANES, SUBLANES)

        @pl.when(aligned > 0)
        def _():
            rows = pl.ds(0, aligned)
            pltpu.make_async_copy(obufs[sl].at[rows], out_hbm.at[rows], ssem.at[sl]).wait()

        def single(r, c):
            scatter_row(sl, r, 0).wait()
            return c
        lax.fori_loop(aligned, n_valid, single, 0)

    def live(t):
        return tnv_ref[jnp.maximum(t, 0)] > 0

    @pl.when(i == 0)
    def _():
        cls2 = route_ref[0]
        pos = route_ref[1]
        for c in range(N_CLASSES):
            pos = pos + jnp.where(cls2 == c, cs_ref[c], 0)
        pos_vmem[...] = pos
        to_smem = pltpu.make_async_copy(pos_vmem, pos_smem, psem)
        to_smem.start()
        to_smem.wait()

        def place(row, c):
            for col in range(LANES):
                rt_ref[pos_smem[row, col]] = row * LANES + col
            return c
        lax.fori_loop(0, n_tok // LANES, place, 0)
        for r in range(tm):
            rt_ref[n_tok + r] = 0
        issue_gather(0, 0)
        issue_gather(1, 1)

    prev = jnp.maximum(i - 1, 0)
    group_changed = (i == 0) | (tg_ref[i] != tg_ref[prev])
    for which, t_ref, (wg, wu, wd) in ((0, tlo_ref, (wg_lo, wu_lo, wd_lo)), (1, thi_ref, (wg_hi, wu_hi, wd_hi))):
        @pl.when(group_changed | (t_ref[i] != t_ref[prev]))
        def _():
            w_in_bf[2 * which] = wg[...].astype(BF16)
            w_in_bf[2 * which + 1] = wu[...].astype(BF16)
            w_out_bf[which] = wd[...].astype(BF16)

    def tile(cur):
        far = (cur + depth - 1) % depth
        @pl.when((i <= 1) | live(i - 2))
        def _():
            wait_gather(cur)

        @pl.when((i >= depth) & live(i - depth))
        def _():
            wait_scatter(cur, tnv_ref[jnp.maximum(i - depth, 0)])

        prev_base = tbase_ref[prev]
        prev_valid = jnp.where(i == 0, 0, tnv_ref[prev])

        @pl.when(live(i))
        def _():
            issue_scatter(far, prev_base, prev_valid)
            issue_gather(i + 2, far)
            xa = xbufs[cur][...]
            x = xa[:, :d_model]
            xn = (x * _rms_scale(x) * gn_ref[...]).astype(BF16)

            def expert(which, w):
                a = jnp.dot(xn, w_in_bf[2 * which], preferred_element_type=F32)
                b = jnp.dot(xn, w_in_bf[2 * which + 1], preferred_element_type=F32)
                return jnp.dot((_silu(a) * b * w).astype(BF16), w_out_bf[which], preferred_element_type=F32)

            y = expert(0, xa[:, d_model:d_model + 1]) + expert(1, xa[:, d_model + 1:d_model + 2])
            obufs[cur][...] = x + y
            if final_norm:
                for r0 in range(0, tm, 4 * SUBLANES):
                    rows = slice(r0, r0 + 4 * SUBLANES)
                    o = obufs[cur][rows, :]
                    obufs[cur][rows, :] = o * _rms_scale(o) * fn_ref[...]

        @pl.when(jnp.logical_not(live(i)) & (i >= 1) & live(i - 1))
        def _():
            issue_scatter(far, prev_base, prev_valid)

    for residue in range(depth):
        @pl.when(i % depth == residue)
        def _():
            tile(residue)


def _moe_layer(h_aug, route, counts_blk, ffn_norm_g, final_norm_g, wg, wu, wd, layer, final_norm):
    T, DA = h_aug.shape
    D = DA - LANES
    F = wg.shape[-1]
    tm = MOE_TILE
    assert T % (LANES * SUBLANES) == 0 and T >= 2 * tm
    n_tiles = T // tm + N_CLASSES + 3
    tile_g, tile_lo, tile_hi, nvalid, base, cstart = _tile_schedule(counts_blk, tm, n_tiles)
    route2d = route[:2].reshape(2, T // LANES, LANES)

    def w_spec(shape, which):
        def index(i, tg, tlo, thi, *_):
            return (layer, tg[i], (tlo if which == 0 else thi)[i], 0, 0)
        return pl.BlockSpec((None, None, None) + shape, index)

    est = 2 * 6 * D * F * 4 + 6 * D * F * 2 + 3 * tm * DA * 4 + 3 * tm * D * 4 + 8 * tm * F * 4
    return pl.pallas_call(
        functools.partial(_moe_kernel, tm=tm, d_model=D, n_tok=T, final_norm=final_norm),
        grid_spec=pltpu.PrefetchScalarGridSpec(
            num_scalar_prefetch=6,
            grid=(n_tiles,),
            in_specs=[
                pl.BlockSpec(memory_space=pl.ANY),
                pl.BlockSpec((2, T // LANES, LANES), lambda i, *_: (0, 0, 0)),
                pl.BlockSpec((1, D), lambda i, *_: (0, 0)),
                pl.BlockSpec((1, D), lambda i, *_: (0, 0)),
                w_spec((D, F), 0), w_spec((D, F), 0), w_spec((F, D), 0),
                w_spec((D, F), 1), w_spec((D, F), 1), w_spec((F, D), 1),
            ],
            out_specs=pl.BlockSpec(memory_space=pl.ANY),
            scratch_shapes=[
                pltpu.SMEM((T + tm,), jnp.int32),
                pltpu.SMEM((T // LANES, LANES), jnp.int32),
                pltpu.VMEM((T // LANES, LANES), jnp.int32),
                pltpu.VMEM((tm, DA), F32), pltpu.VMEM((tm, DA), F32), pltpu.VMEM((tm, DA), F32),
                pltpu.VMEM((tm, D), F32), pltpu.VMEM((tm, D), F32), pltpu.VMEM((tm, D), F32),
                pltpu.VMEM((4, D, F), BF16),
                pltpu.VMEM((2, F, D), BF16),
                pltpu.SemaphoreType.DMA((3,)),
                pltpu.SemaphoreType.DMA((3,)),
                pltpu.SemaphoreType.DMA(()),
            ],
        ),
        out_shape=jax.ShapeDtypeStruct((T, D), F32),
        compiler_params=pltpu.CompilerParams(
            dimension_semantics=("arbitrary",), vmem_limit_bytes=_vmem_limit(est)),
        name="moe_experts",
    )(tile_g, tile_lo, tile_hi, nvalid, base, cstart,
      h_aug, route2d, ffn_norm_g, final_norm_g, wg, wu, wd, wg, wu, wd)


def _partial_rotary(xh, cosf, sinp, sinm, half):
    lanes = xh.shape[-1]
    return xh * cosf + pltpu.roll(xh, half, 1) * sinp + pltpu.roll(xh, lanes - half, 1) * sinm


def _tile_cos_sin(cb_ref, sb_ref, cr_ref, sr_ref):
    cb, sb, cr, sr = cb_ref[...], sb_ref[...], cr_ref[...], sr_ref[...]
    return cb * cr - sb * sr, sb * cr + cb * sr


def _kvq_kernel(h_ref, gkv_ref, gq_ref, wkv_ref, wq_ref, cb_ref, sb_ref, cr_ref, sr_ref,
                k_ref, vt_ref, qt_ref, km_ref, *, heads, dh, half):
    x = h_ref[...]
    xr = x * _rms_scale(x)
    xkv = (xr * gkv_ref[...]).astype(BF16)
    xq = (xr * gq_ref[...]).astype(BF16)
    cosf, sin_all = _tile_cos_sin(cb_ref, sb_ref, cr_ref, sr_ref)
    lane = lax.broadcasted_iota(jnp.int32, sin_all.shape, 1)
    sinp = jnp.where(lane >= half, sin_all, 0.0)
    sinm = jnp.where(lane < half, -sin_all, 0.0)
    width = heads * dh
    q_scale = dh ** -0.5 * LOG2_E
    pair_w = 2 * dh
    blk = k_ref.shape[2]
    n_sub = x.shape[0] // blk
    for c0 in range(0, width, pair_w):
        kk = jnp.dot(xkv, wkv_ref[:, c0:c0 + pair_w], preferred_element_type=F32)
        vv = jnp.dot(xkv, wkv_ref[:, width + c0:width + c0 + pair_w], preferred_element_type=F32)
        qq = jnp.dot(xq, wq_ref[:, c0:c0 + pair_w], preferred_element_type=F32)
        for u in range(2):
            hh = c0 // dh + u
            cols = slice(hh * dh, (hh + 1) * dh)
            kh = _partial_rotary(kk[:, u * dh:(u + 1) * dh], cosf, sinp, sinm, half)
            qh = _partial_rotary(qq[:, u * dh:(u + 1) * dh], cosf, sinp, sinm, half) * q_scale
            vh = vv[:, u * dh:(u + 1) * dh]
            for b in range(n_sub):
                rows = slice(b * blk, (b + 1) * blk)
                k_ref[hh, b] = kh[rows].astype(BF16)
                km_ref[b, :, cols] = jnp.mean(kh[rows], axis=0, keepdims=True)
                vt_ref[hh, b, :dh] = vh[rows].astype(BF16).T
                vt_ref[hh, b, dh:] = jnp.ones((VT_ONES_ROWS, blk), BF16)
                qt_ref[hh, b] = qh[rows].astype(BF16).T


def _angle_tables(inv_freq, seq, tm):
    base = jnp.arange(0, seq, tm, dtype=F32)[:, None] * inv_freq[None, :]
    offs = jnp.arange(tm, dtype=F32)[:, None] * inv_freq[None, :]
    return jnp.cos(base)[:, None, :], jnp.sin(base)[:, None, :], jnp.cos(offs), jnp.sin(offs)


def _rope_tables(seq, dh, tm):
    rot = dh // ROPE_FRACTION
    half = rot // 2
    inv = 1.0 / (ROPE_THETA ** (jnp.arange(half, dtype=F32) / half))
    inv_lanes = jnp.concatenate([inv, inv, jnp.zeros((dh - rot,), F32)])
    return _angle_tables(inv_lanes, seq, tm) + (half,)


def _kvq_proj(h, kv_norm_g, q_norm_g, wkv_bf16, wq_bf16, seq):
    T, D = h.shape
    heads = ATT_HEADS
    dh = D // heads
    blk = MOBA_BLOCK
    n_sub = KVQ_BLOCKS_PER_TILE
    tm = n_sub * blk
    assert seq % tm == 0
    n_pos_tiles = seq // tm
    cos_b, sin_b, cos_r, sin_r, half = _rope_tables(seq, dh, tm)
    width = heads * dh
    est = 2 * (D * 2 * width + D * width) * 2 + 2 * tm * D * 4 + 6 * tm * width * 2 + 8 * tm * dh * 4
    pos = lambda i: (i % n_pos_tiles, 0, 0)
    const = lambda i: (0, 0)
    per_block = lambda i: (0, i, 0, 0)
    return pl.pallas_call(
        functools.partial(_kvq_kernel, heads=heads, dh=dh, half=half),
        grid=(T // tm,),
        in_specs=[
            pl.BlockSpec((tm, D), lambda i: (i, 0)),
            pl.BlockSpec((1, D), const),
            pl.BlockSpec((1, D), const),
            pl.BlockSpec((D, 2 * width), const),
            pl.BlockSpec((D, width), const),
            pl.BlockSpec((None, 1, dh), pos),
            pl.BlockSpec((None, 1, dh), pos),
            pl.BlockSpec((tm, dh), const),
            pl.BlockSpec((tm, dh), const),
        ],
        out_specs=[
            pl.BlockSpec((heads, n_sub, blk, dh), per_block),
            pl.BlockSpec((heads, n_sub, dh + VT_ONES_ROWS, blk), per_block),
            pl.BlockSpec((heads, n_sub, dh, blk), per_block),
            pl.BlockSpec((n_sub, 1, width), lambda i: (i, 0, 0)),
        ],
        out_shape=[
            jax.ShapeDtypeStruct((heads, T // blk, blk, dh), BF16),
            jax.ShapeDtypeStruct((heads, T // blk, dh + VT_ONES_ROWS, blk), BF16),
            jax.ShapeDtypeStruct((heads, T // blk, dh, blk), BF16),
            jax.ShapeDtypeStruct((T // blk, 1, width), F32),
        ],
        compiler_params=pltpu.CompilerParams(
            dimension_semantics=("arbitrary",), vmem_limit_bytes=_vmem_limit(est)),
        name="kvq_proj",
    )(h, kv_norm_g, q_norm_g, wkv_bf16, wq_bf16, cos_b, sin_b, cos_r, sin_r)


def _moba_kernel(qt_ref, k_ref, vt_ref, km_ref, o_ref, sel_ref, acc_ref, s0_ref, s1_ref, p_ref,
                 *, topk, group, dh):
    qi = pl.program_id(2)
    n_blk = km_ref.shape[0]
    blk = s0_ref.shape[1] // 2
    neg = -jnp.inf
    halves = (slice(0, blk), slice(blk, 2 * blk))

    def chunks(h):
        return [slice(c, c + MOBA_KEY_CHUNK) for c in range(h * blk, (h + 1) * blk, MOBA_KEY_CHUNK)]

    def stage_pair(s_buf, j):
        jp = jnp.minimum(j, n_blk - 2)
        for g in range(group):
            keys = k_ref[g, pl.ds(jp, 2)].reshape(2 * blk, dh)
            s_buf[g] = jnp.dot(keys, qt_ref[g], preferred_element_type=F32).astype(BF16)

    def stage_softmax(s_buf, h, ms, j):
        rows = chunks(h)
        new_m, alphas = [], []
        for g in range(group):
            part = s_buf[g, rows[0], :]
            for c in rows[1:]:
                part = jnp.maximum(part, s_buf[g, c, :])
            m_blk = jnp.max(part, axis=0, keepdims=True).astype(F32)
            if j is None:
                m_new = shift = m_blk
            else:
                bias = sel_ref[g, pl.ds(j, 1), :]
                m_new = jnp.maximum(ms[g], m_blk + bias)
                shift = m_new - bias
            shift = shift.astype(BF16)
            for c in rows:
                p_ref[g, c, :] = jnp.exp2(s_buf[g, c, :] - shift)
            new_m.append(m_new)
            alphas.append(None if j is None else jnp.exp2(ms[g] - m_new))
        return tuple(new_m), alphas

    def stage_pv(h, j, alphas):
        for g in range(group):
            pv = jnp.dot(vt_ref[g, j], p_ref[g, halves[h], :], preferred_element_type=F32)
            acc_ref[g] = pv if alphas[g] is None else alphas[g] * acc_ref[g] + pv

    for g in range(group):
        km = km_ref[:, g * dh:(g + 1) * dh]
        km_hi = km.astype(BF16)
        km_lo = (km - km_hi.astype(F32)).astype(BF16)
        gate = (jnp.dot(km_hi, qt_ref[g], preferred_element_type=F32)
                + jnp.dot(km_lo, qt_ref[g], preferred_element_type=F32))
        blk_id = lax.broadcasted_iota(jnp.int32, gate.shape, 0)
        gv = jnp.where(blk_id < qi, gate, neg)
        bias = jnp.full(gate.shape, neg, F32)
        for _ in range(topk):
            m = jnp.max(gv, axis=0, keepdims=True)
            first = jnp.min(jnp.where(gv == m, blk_id, n_blk), axis=0, keepdims=True)
            bias = jnp.where((blk_id == first) & (m > neg), 0.0, bias)
            gv = jnp.where(blk_id == first, neg, gv)
        sel_ref[g] = bias

    key_id = lax.broadcasted_iota(jnp.int32, (blk, blk), 0)
    qry_id = lax.broadcasted_iota(jnp.int32, (blk, blk), 1)
    for g in range(group):
        own = jnp.dot(k_ref[g, qi], qt_ref[g], preferred_element_type=F32)
        s1_ref[g, halves[0], :] = jnp.where(key_id <= qry_id, own, neg).astype(BF16)
    stage_pair(s0_ref, 0)
    ms, alphas = stage_softmax(s1_ref, 0, None, None)
    stage_pv(0, qi, alphas)

    def trip(t, ms, cur, nxt):
        stage_pair(nxt, 2 * t + 2)
        for h in range(2):
            ms, alphas = stage_softmax(cur, h, ms, 2 * t + h)
            stage_pv(h, 2 * t + h, alphas)
        return ms

    def body(t, ms):
        return lax.cond(t % 2 == 0,
                        lambda m: trip(t, m, s0_ref, s1_ref),
                        lambda m: trip(t, m, s1_ref, s0_ref), ms)

    n_full = qi // 2
    ms = lax.fori_loop(0, n_full, body, ms)

    for parity, buf in enumerate((s0_ref, s1_ref)):
        @pl.when((qi % 2 == 1) & (n_full % 2 == parity))
        def _():
            _, alphas = stage_softmax(buf, 0, ms, qi - 1)
            stage_pv(0, qi - 1, alphas)
    for g in range(group):
        acc = acc_ref[g]
        o_ref[:, g * dh:(g + 1) * dh] = (acc[:dh] / acc[dh:dh + 1]).astype(BF16).T


def _moba_attention(qt, k, vt, k_mean, batch, seq):
    heads, n_tiles, dh, blk = qt.shape
    dv_rows = vt.shape[2]
    n_blk = seq // blk
    topk = min(MOBA_TOPK, n_blk)
    width = heads * dh
    group = MOBA_HEAD_GROUP
    km = k_mean.reshape(batch, n_blk, width)
    per_seq = lambda b, h, i: (h, b, 0, 0)
    est = group * seq * (dh + dv_rows) * 2 + 2 * group * blk * blk * (2 + 2) + 6 * group * blk * dh * 4
    return pl.pallas_call(
        functools.partial(_moba_kernel, topk=topk, group=group, dh=dh),
        grid=(batch, heads // group, n_blk),
        in_specs=[
            pl.BlockSpec((group, None, dh, blk), lambda b, h, i: (h, b * n_blk + i, 0, 0)),
            pl.BlockSpec((group, n_blk, blk, dh), per_seq, pipeline_mode=pl.Buffered(1)),
            pl.BlockSpec((group, n_blk, dv_rows, blk), per_seq, pipeline_mode=pl.Buffered(1)),
            pl.BlockSpec((None, n_blk, group * dh), lambda b, h, i: (b, 0, h)),
        ],
        out_specs=pl.BlockSpec((blk, group * dh), lambda b, h, i: (b * n_blk + i, h)),
        out_shape=jax.ShapeDtypeStruct((n_tiles * blk, width), BF16),
        scratch_shapes=[pltpu.VMEM((group, n_blk, blk), F32), pltpu.VMEM((group, dv_rows, blk), F32),
                        pltpu.VMEM((group, 2 * blk, blk), BF16), pltpu.VMEM((group, 2 * blk, blk), BF16),
                        pltpu.VMEM((group, 2 * blk, blk), BF16)],
        compiler_params=pltpu.CompilerParams(
            dimension_semantics=("arbitrary", "arbitrary", "arbitrary"), vmem_limit_bytes=_vmem_limit(est)),
        name="moba_attention",
    )(qt, k, vt, km)


def kernel(x, ret_norm, ret_w_in, ret_w_out, kv_norm, w_kv, attn_norm, w_q, w_o, ffn_norm, router_group_w, router_group_b, router_expert_w, router_expert_b, expert_w_gate, expert_w_up, expert_w_down, final_norm):
    B, S, D = x.shape
    T = B * S
    assert S % MOBA_BLOCK == 0 and S % RET_KERNEL_CHUNK == 0 and T % MOE_TILE == 0
    assert ret_norm.shape[0] == 1 and attn_norm.shape[0] == 1 and ffn_norm.shape[0] == 2
    h = x.reshape(T, D)
    final_g = final_norm[None, :]

    def proj_router(y, w, resid, layer, name):
        return _proj_router(y, w.astype(BF16), resid, ffn_norm[layer][None, :], router_group_w[layer],
                            router_group_b[layer], router_expert_w[layer], router_expert_b[layer], name)

    def moe(h_aug, route, counts, layer, last):
        return _moe_layer(h_aug, route, counts, ffn_norm[layer][None, :], final_g,
                          expert_w_gate, expert_w_up, expert_w_down, layer, final_norm=last)

    q, kt, v, gate = _ret_in_proj(h, ret_norm[0][None, :], ret_w_in[0].astype(BF16), S)
    y = _ret_core(q, kt, v, gate, B, S)
    h = moe(*proj_router(y, ret_w_out[0], h, 0, "ret_out_proj_router"), 0, False)

    k2, vt2, qt2, k_mean = _kvq_proj(h, kv_norm[None, :], attn_norm[0][None, :],
                                     w_kv.astype(BF16), w_q[0].astype(BF16), S)
    o = _moba_attention(qt2, k2, vt2, k_mean, B, S)
    h = moe(*proj_router(o, w_o[0], h, 1, "attn_out_proj_router"), 1, True)
    return h.reshape(B, S, D)
```

```python
import functools

import jax
import jax.numpy as jnp
from jax import lax
from jax.experimental import pallas as pl
from jax.experimental.pallas import tpu as pltpu

F32 = jnp.float32
BF16 = jnp.bfloat16

NORM_EPS = 1e-6
RET_HEADS = 4
RET_ROT_BASE = 10000.0
ATT_HEADS = 8
ROPE_FRACTION = 4
ROPE_THETA = 500000.0
MOBA_BLOCK = 256
MOBA_TOPK = 3
N_GROUPS = 4
EXPERTS_PER_GROUP = 4
PAIR_LO = (0, 0, 1, 1, 0, 2)
PAIR_HI = (1, 2, 2, 3, 3, 3)
N_PAIRS = len(PAIR_LO)
N_CLASSES = N_GROUPS * N_PAIRS

LANES = 128
SUBLANES = 8
RET_KERNEL_CHUNK = 256
RET_CHUNKS_PER_TILE = 2
RET_CORE_CHUNKS_PER_TILE = 4
MOE_TILE = 256
ROUTE_ROWS = 32
MOBA_HEAD_GROUP = 8
KVQ_BLOCKS_PER_TILE = 4
MOBA_KEY_CHUNK = 32
VT_ONES_ROWS = 16
LOG2_E = 1.4426950408889634
V7X_VMEM_BYTES = 64 * 1024 * 1024
V7X_VMEM_RESERVED_BYTES = 8 * 1024 * 1024
DEFAULT_SCOPED_VMEM_BYTES = 32 * 1024 * 1024


def _vmem_limit(estimate_bytes):
    wanted = max(DEFAULT_SCOPED_VMEM_BYTES, estimate_bytes * 5 // 4)
    return int(min(V7X_VMEM_BYTES - V7X_VMEM_RESERVED_BYTES, wanted))


def _rms_scale(x):
    return lax.rsqrt(jnp.mean(x * x, axis=-1, keepdims=True) + NORM_EPS)


def _silu(a):
    return a * jax.nn.sigmoid(a)


def _ret_in_kernel(x_ref, g_ref, w_ref, cb_ref, sb_ref, cr_ref, sr_ref, qdec_ref, kdec_ref,
                   q_ref, kt_ref, v_ref, gate_ref, *, heads, dk, dv):
    x = x_ref[...]
    xn = (x * _rms_scale(x) * g_ref[...]).astype(BF16)
    cos, sin = _tile_cos_sin(cb_ref, sb_ref, cr_ref, sr_ref)
    half = dk // 2

    def rotated(col0, h, dec):
        p = jnp.dot(xn, w_ref[:, col0 + h * dk:col0 + (h + 1) * dk], preferred_element_type=F32)
        x1 = p[:, :half]
        x2 = p[:, half:]
        cos_h = cos * dec
        sin_h = sin * dec
        return x1 * cos_h - x2 * sin_h, x2 * cos_h + x1 * sin_h

    cc = kt_ref.shape[3]
    for h in range(heads):
        lo, hi = rotated(0, h, qdec_ref[h])
        q_ref[:, h * dk:h * dk + half] = lo.astype(BF16)
        q_ref[:, h * dk + half:(h + 1) * dk] = hi.astype(BF16)
        lo, hi = rotated(heads * dk, h, kdec_ref[h])
        for c in range(kt_ref.shape[1]):
            rows = slice(c * cc, (c + 1) * cc)
            kt_ref[h, c, :half] = lo[rows].astype(BF16).T
            kt_ref[h, c, half:] = hi[rows].astype(BF16).T
    v0 = 2 * heads * dk
    g0 = v0 + heads * dv
    for h in range(heads):
        v_ref[:, h * dv:(h + 1) * dv] = jnp.dot(
            xn, w_ref[:, v0 + h * dv:v0 + (h + 1) * dv], preferred_element_type=F32).astype(BF16)
        gate_ref[:, h * dv:(h + 1) * dv] = jnp.dot(
            xn, w_ref[:, g0 + h * dv:g0 + (h + 1) * dv], preferred_element_type=F32).astype(BF16)


def _ret_in_proj(h, norm_g, w_bf16, seq):
    T, D = h.shape
    heads = RET_HEADS
    dk = D // heads
    dv = 2 * dk
    n_cols = w_bf16.shape[1]
    cc = RET_KERNEL_CHUNK
    n_sub = RET_CHUNKS_PER_TILE
    tm = n_sub * cc
    assert seq % tm == 0
    n_pos_tiles = seq // tm
    half = dk // 2
    inv = 1.0 / (RET_ROT_BASE ** (jnp.arange(half, dtype=F32) / half))
    cos_b, sin_b, cos_r, sin_r = _angle_tables(inv, seq, tm)
    log_gamma = _ret_log_gamma(heads)
    idx = jnp.tile(jnp.arange(cc, dtype=F32), n_sub)
    q_dec = jnp.broadcast_to(jnp.exp(log_gamma[:, None] * idx)[:, :, None], (heads, tm, half))
    k_dec = jnp.broadcast_to((jnp.exp(-log_gamma[:, None] * idx) * dk ** -0.5)[:, :, None], (heads, tm, half))
    pos = lambda i: (i % n_pos_tiles, 0, 0)
    est = D * n_cols * 2 + 2 * tm * D * 4 + 2 * tm * n_cols * 2 + (4 + 4 * heads) * tm * LANES * 4
    return pl.pallas_call(
        functools.partial(_ret_in_kernel, heads=heads, dk=dk, dv=dv),
        grid=(T // tm,),
        in_specs=[
            pl.BlockSpec((tm, D), lambda i: (i, 0)),
            pl.BlockSpec((1, D), lambda i: (0, 0)),
            pl.BlockSpec((D, n_cols), lambda i: (0, 0), pipeline_mode=pl.Buffered(1)),
            pl.BlockSpec((None, 1, half), pos),
            pl.BlockSpec((None, 1, half), pos),
            pl.BlockSpec((tm, half), lambda i: (0, 0)),
            pl.BlockSpec((tm, half), lambda i: (0, 0)),
            pl.BlockSpec((heads, tm, half), lambda i: (0, 0, 0)),
            pl.BlockSpec((heads, tm, half), lambda i: (0, 0, 0)),
        ],
        out_specs=[
            pl.BlockSpec((tm, heads * dk), lambda i: (i, 0)),
            pl.BlockSpec((heads, n_sub, dk, cc), lambda i: (0, i, 0, 0)),
            pl.BlockSpec((tm, heads * dv), lambda i: (i, 0)),
            pl.BlockSpec((tm, heads * dv), lambda i: (i, 0)),
        ],
        out_shape=[
            jax.ShapeDtypeStruct((T, heads * dk), BF16),
            jax.ShapeDtypeStruct((heads, T // cc, dk, cc), BF16),
            jax.ShapeDtypeStruct((T, heads * dv), BF16),
            jax.ShapeDtypeStruct((T, heads * dv), BF16),
        ],
        compiler_params=pltpu.CompilerParams(
            dimension_semantics=("arbitrary",), vmem_limit_bytes=_vmem_limit(est)),
        name="ret_in_proj",
    )(h, norm_g, w_bf16, cos_b, sin_b, cos_r, sin_r, q_dec, k_dec)


def _ret_log_gamma(heads):
    return jnp.log1p(-jnp.power(2.0, -5.0 - jnp.arange(heads, dtype=F32)))


def _ret_core_kernel(q_ref, kt_ref, v_ref, g_ref, cd_ref, y_ref, state_ref, *, heads):
    @pl.when(pl.program_id(1) == 0)
    def _():
        state_ref[...] = jnp.zeros_like(state_ref)

    n_sub, cc = kt_ref.shape[1], kt_ref.shape[3]
    dk = q_ref.shape[1] // heads
    dv = v_ref.shape[1] // heads
    causal = lax.broadcasted_iota(jnp.int32, (cc, cc), 0) >= lax.broadcasted_iota(jnp.int32, (cc, cc), 1)
    for c in range(n_sub):
        rows = slice(c * cc, (c + 1) * cc)
        for h in range(heads):
            q = q_ref[rows, h * dk:(h + 1) * dk]
            kt = kt_ref[h, c]
            v = v_ref[rows, h * dv:(h + 1) * dv]
            s = jnp.where(causal, jnp.dot(q, kt, preferred_element_type=F32), 0.0)
            u = state_ref[h]
            o = (jnp.dot(s.astype(BF16), v, preferred_element_type=F32)
                 + jnp.dot(q, u.astype(BF16), preferred_element_type=F32))
            state_ref[h] = cd_ref[h] * (u + jnp.dot(kt, v, preferred_element_type=F32))
            o = o * _rms_scale(o)
            gate = g_ref[rows, h * dv:(h + 1) * dv].astype(F32)
            y_ref[rows, h * dv:(h + 1) * dv] = (_silu(gate) * o).astype(BF16)


def _ret_core(q, kt, v, gate, batch, seq):
    T = q.shape[0]
    heads = RET_HEADS
    dk = q.shape[1] // heads
    dv = v.shape[1] // heads
    cc = RET_KERNEL_CHUNK
    n_sub = RET_CORE_CHUNKS_PER_TILE
    tm = n_sub * cc
    assert seq % tm == 0
    nc = seq // tm
    chunk_decay = jnp.broadcast_to(jnp.exp(_ret_log_gamma(heads) * cc)[:, None, None], (heads, 1, dv))
    row = lambda b, c: (b * nc + c, 0)
    return pl.pallas_call(
        functools.partial(_ret_core_kernel, heads=heads),
        grid=(batch, nc),
        in_specs=[
            pl.BlockSpec((tm, heads * dk), row),
            pl.BlockSpec((heads, n_sub, dk, cc), lambda b, c: (0, b * nc + c, 0, 0)),
            pl.BlockSpec((tm, heads * dv), row),
            pl.BlockSpec((tm, heads * dv), row),
            pl.BlockSpec((heads, 1, dv), lambda b, c: (0, 0, 0)),
        ],
        out_specs=pl.BlockSpec((tm, heads * dv), row),
        out_shape=jax.ShapeDtypeStruct((T, heads * dv), BF16),
        scratch_shapes=[pltpu.VMEM((heads, dk, dv), F32)],
        compiler_params=pltpu.CompilerParams(dimension_semantics=("arbitrary", "arbitrary")),
        name="ret_core",
    )(q, kt, v, gate, chunk_decay)


def _proj_router_kernel(y_ref, w_ref, r_ref, g_ref, wr_ref, br_ref, haug_ref, route_ref, counts_ref, run_ref,
                        *, groups, experts, d_model):
    @pl.when(pl.program_id(0) == 0)
    def _():
        run_ref[...] = jnp.zeros_like(run_ref)

    h = r_ref[...] + jnp.dot(y_ref[...], w_ref[...], preferred_element_type=F32)
    xn = h * _rms_scale(h) * g_ref[...]
    xh = xn.astype(BF16)
    xl = (xn - xh.astype(F32)).astype(BF16)
    hi_both = jnp.dot(xh, wr_ref[...], preferred_element_type=F32)
    lo_hi = jnp.dot(xl, wr_ref[:, :LANES], preferred_element_type=F32)
    logits = hi_both[:, :LANES] + (hi_both[:, LANES:] + lo_hi) + br_ref[...]
    lt = logits.T[:ROUTE_ROWS]
    tm = lt.shape[1]
    row = lax.broadcasted_iota(jnp.int32, lt.shape, 0)
    neg = -jnp.inf

    def argmax_first(vals):
        m = jnp.max(vals, axis=0, keepdims=True)
        first = jnp.min(jnp.where(vals == m, row, ROUTE_ROWS), axis=0, keepdims=True)
        return m, first

    gl = jnp.where(row < groups, lt, neg)
    gmax, gidx = argmax_first(gl)
    gsum = jnp.sum(jnp.where(row < groups, jnp.exp(gl - gmax), 0.0), axis=0, keepdims=True)
    g_val = 1.0 / gsum
    e0 = groups + experts * gidx
    el = jnp.where((row >= e0) & (row < e0 + experts), lt, neg)
    m1, i1 = argmax_first(el)
    m2, i2 = argmax_first(jnp.where(row == i1, neg, el))
    t = jnp.exp(m2 - m1)
    w1 = 1.0 / (1.0 + t)
    w2 = t / (1.0 + t)
    first_is_lo = i1 < i2
    lo = jnp.minimum(i1, i2) - e0
    hi = jnp.maximum(i1, i2) - e0
    w_lo = g_val * jnp.where(first_is_lo, w1, w2)
    w_hi = g_val * jnp.where(first_is_lo, w2, w1)
    lex = (lo * (7 - lo)) // 2 + (hi - lo - 1)
    pair = jnp.where(lex == 2, 4, jnp.where(lex == 3, 2, jnp.where(lex == 4, 3, lex)))
    cls = gidx * N_PAIRS + pair

    onehot = row == cls
    oh = jnp.where(onehot, 1.0, 0.0)
    before = lax.broadcasted_iota(jnp.int32, (tm, tm), 0) < lax.broadcasted_iota(jnp.int32, (tm, tm), 1)
    prefix = jnp.dot(oh.astype(BF16), jnp.where(before, 1.0, 0.0).astype(BF16), preferred_element_type=F32)
    rank = jnp.sum(jnp.where(onehot, prefix + run_ref[:, 0:1], 0.0), axis=0, keepdims=True)
    run_new = run_ref[...] + jnp.sum(oh, axis=1, keepdims=True)
    run_ref[...] = run_new
    counts_ref[...] = run_new

    r8 = lax.broadcasted_iota(jnp.int32, route_ref.shape, 0)
    route_ref[...] = jnp.where(r8 == 0, cls, jnp.where(r8 == 1, rank.astype(jnp.int32), 0))
    rl = lax.broadcasted_iota(jnp.int32, (LANES, tm), 0)
    meta_t = jnp.where(rl == 0, w_lo, jnp.where(rl == 1, w_hi, 0.0))
    haug_ref[:, :d_model] = h
    haug_ref[:, d_model:] = meta_t.T


def _proj_router(y, w_bf16, resid, norm_g, w_rg, b_rg, w_re, b_re, name, tm=1024):
    T, K = y.shape
    D = w_bf16.shape[1]
    G, E = N_GROUPS, EXPERTS_PER_GROUP
    n_used = G + G * E
    wr = jnp.concatenate([w_rg, jnp.transpose(w_re, (1, 0, 2)).reshape(D, G * E)], axis=1)
    wr = jnp.pad(wr, ((0, 0), (0, LANES - n_used)))
    wr_hi = wr.astype(BF16)
    wr = jnp.concatenate([wr_hi, (wr - wr_hi.astype(F32)).astype(BF16)], axis=1)
    br = jnp.pad(jnp.concatenate([b_rg, b_re.reshape(G * E)]), (0, LANES - n_used))[None, :]
    est = 2 * K * D * 2 + 2 * tm * K * 2 + 6 * tm * D * 4 + 2 * D * LANES * 4 + 4 * tm * tm * 4
    const = lambda i: (0, 0)
    return pl.pallas_call(
        functools.partial(_proj_router_kernel, groups=G, experts=E, d_model=D),
        grid=(T // tm,),
        in_specs=[
            pl.BlockSpec((tm, K), lambda i: (i, 0)),
            pl.BlockSpec((K, D), const),
            pl.BlockSpec((tm, D), lambda i: (i, 0)),
            pl.BlockSpec((1, D), const),
            pl.BlockSpec((D, 2 * LANES), const),
            pl.BlockSpec((1, LANES), const),
        ],
        out_specs=[
            pl.BlockSpec((tm, D + LANES), lambda i: (i, 0)),
            pl.BlockSpec((8, tm), lambda i: (0, i)),
            pl.BlockSpec((ROUTE_ROWS, LANES), const),
        ],
        out_shape=[
            jax.ShapeDtypeStruct((T, D + LANES), F32),
            jax.ShapeDtypeStruct((8, T), jnp.int32),
            jax.ShapeDtypeStruct((ROUTE_ROWS, LANES), F32),
        ],
        scratch_shapes=[pltpu.VMEM((ROUTE_ROWS, LANES), F32)],
        compiler_params=pltpu.CompilerParams(
            dimension_semantics=("arbitrary",), vmem_limit_bytes=_vmem_limit(est)),
        name=name,
    )(y, w_bf16, resid, norm_g, wr, br)


def _tile_schedule(counts_blk, tm, n_tiles):
    counts = counts_blk[:N_CLASSES, 0].astype(jnp.int32)
    tiles_per = (counts + tm - 1) // tm
    tile_end = jnp.cumsum(tiles_per)
    tile_begin = tile_end - tiles_per
    cstart = jnp.cumsum(counts) - counts
    t = jnp.arange(n_tiles, dtype=jnp.int32)
    live = t < tile_end[-1]
    tq = jnp.where(live, t, tile_end[-1] - 1)
    tcls = jnp.sum((tile_end[None, :] <= tq[:, None]).astype(jnp.int32), axis=1)
    onehot = tcls[:, None] == jnp.arange(N_CLASSES, dtype=jnp.int32)[None, :]
    pick = lambda table: jnp.sum(jnp.where(onehot, jnp.asarray(table, jnp.int32)[None, :], 0), axis=1)
    k = tq - pick(tile_begin)
    nvalid = jnp.where(live, jnp.clip(pick(counts) - k * tm, 0, tm), 0).astype(jnp.int32)
    base = jnp.where(live, pick(cstart) + k * tm, 0).astype(jnp.int32)
    tile_g = pick([c // N_PAIRS for c in range(N_CLASSES)])
    tile_lo = pick([PAIR_LO[c % N_PAIRS] for c in range(N_CLASSES)])
    tile_hi = pick([PAIR_HI[c % N_PAIRS] for c in range(N_CLASSES)])
    cstart_pad = jnp.pad(cstart, (0, ROUTE_ROWS - N_CLASSES)).astype(jnp.int32)
    return tile_g, tile_lo, tile_hi, nvalid, base, cstart_pad


def _moe_kernel(tg_ref, tlo_ref, thi_ref, tnv_ref, tbase_ref, cs_ref,
                h_hbm, route_ref, gn_ref, fn_ref, wg_lo, wu_lo, wd_lo, wg_hi, wu_hi, wd_hi,
                out_hbm, rt_ref, pos_smem, pos_vmem, x0, x1, x2, o0, o1, o2, w_in_bf, w_out_bf,
                gsem, ssem, psem, *, tm, d_model, n_tok, final_norm):
    i = pl.program_id(0)
    xbufs = (x0, x1, x2)
    obufs = (o0, o1, o2)
    depth = len(xbufs)

    def issue_gather(t, sl):
        base = tbase_ref[t]
        for r in range(tm):
            tok = rt_ref[base + r]
            pltpu.make_async_copy(h_hbm.at[pl.ds(tok, 1)], xbufs[sl].at[pl.ds(r, 1)], gsem.at[sl]).start()

    def scatter_row(sl, r, tok):
        return pltpu.make_async_copy(obufs[sl].at[pl.ds(r, 1)], out_hbm.at[pl.ds(tok, 1)], ssem.at[sl])

    def issue_scatter(sl, base, n_valid):
        for r in range(tm):
            @pl.when(r < n_valid)
            def _():
                scatter_row(sl, r, rt_ref[base + r]).start()

    def wait_gather(sl):
        pltpu.make_async_copy(h_hbm.at[pl.ds(0, tm)], xbufs[sl], gsem.at[sl]).wait()

    def wait_scatter(sl, n_valid):
        aligned = pl.multiple_of((n_valid // SUBLANES) * SUBLANES, SUBLANES)

        @pl.when(aligned > 0)
        def _():
            rows = pl.ds(0, aligned)
            pltpu.make_async_copy(obufs[sl].at[rows], out_hbm.at[rows], ssem.at[sl]).wait()

        def single(r, c):
            scatter_row(sl, r, 0).wait()
            return c
        lax.fori_loop(aligned, n_valid, single, 0)

    def live(t):
        return tnv_ref[jnp.maximum(t, 0)] > 0

    @pl.when(i == 0)
    def _():
        cls2 = route_ref[0]
        pos = route_ref[1]
        for c in range(N_CLASSES):
            pos = pos + jnp.where(cls2 == c, cs_ref[c], 0)
        pos_vmem[...] = pos
        to_smem = pltpu.make_async_copy(pos_vmem, pos_smem, psem)
        to_smem.start()
        to_smem.wait()

        def place(row, c):
            for col in range(LANES):
                rt_ref[pos_smem[row, col]] = row * LANES + col
            return c
        lax.fori_loop(0, n_tok // LANES, place, 0)
        for r in range(tm):
            rt_ref[n_tok + r] = 0
        issue_gather(0, 0)
        issue_gather(1, 1)

    prev = jnp.maximum(i - 1, 0)
    group_changed = (i == 0) | (tg_ref[i] != tg_ref[prev])
    for which, t_ref, (wg, wu, wd) in ((0, tlo_ref, (wg_lo, wu_lo, wd_lo)), (1, thi_ref, (wg_hi, wu_hi, wd_hi))):
        @pl.when(group_changed | (t_ref[i] != t_ref[prev]))
        def _():
            w_in_bf[2 * which] = wg[...].astype(BF16)
            w_in_bf[2 * which + 1] = wu[...].astype(BF16)
            w_out_bf[which] = wd[...].astype(BF16)

    def tile(cur):
        far = (cur + depth - 1) % depth
        @pl.when((i <= 1) | live(i - 2))
        def _():
            wait_gather(cur)

        @pl.when((i >= depth) & live(i - depth))
        def _():
            wait_scatter(cur, tnv_ref[jnp.maximum(i - depth, 0)])

        prev_base = tbase_ref[prev]
        prev_valid = jnp.where(i == 0, 0, tnv_ref[prev])

        @pl.when(live(i))
        def _():
            issue_scatter(far, prev_base, prev_valid)
            issue_gather(i + 2, far)
            xa = xbufs[cur][...]
            x = xa[:, :d_model]
            xn = (x * _rms_scale(x) * gn_ref[...]).astype(BF16)

            def expert(which, w):
                a = jnp.dot(xn, w_in_bf[2 * which], preferred_element_type=F32)
                b = jnp.dot(xn, w_in_bf[2 * which + 1], preferred_element_type=F32)
                return jnp.dot((_silu(a) * b * w).astype(BF16), w_out_bf[which], preferred_element_type=F32)

            y = expert(0, xa[:, d_model:d_model + 1]) + expert(1, xa[:, d_model + 1:d_model + 2])
            obufs[cur][...] = x + y
            if final_norm:
                for r0 in range(0, tm, 4 * SUBLANES):
                    rows = slice(r0, r0 + 4 * SUBLANES)
                    o = obufs[cur][rows, :]
                    obufs[cur][rows, :] = o * _rms_scale(o) * fn_ref[...]

        @pl.when(jnp.logical_not(live(i)) & (i >= 1) & live(i - 1))
        def _():
            issue_scatter(far, prev_base, prev_valid)

    for residue in range(depth):
        @pl.when(i % depth == residue)
        def _():
            tile(residue)


def _moe_layer(h_aug, route, counts_blk, ffn_norm_g, final_norm_g, wg, wu, wd, layer, final_norm):
    T, DA = h_aug.shape
    D = DA - LANES
    F = wg.shape[-1]
    tm = MOE_TILE
    assert T % (LANES * SUBLANES) == 0 and T >= 2 * tm
    n_tiles = T // tm + N_CLASSES + 3
    tile_g, tile_lo, tile_hi, nvalid, base, cstart = _tile_schedule(counts_blk, tm, n_tiles)
    route2d = route[:2].reshape(2, T // LANES, LANES)

    def w_spec(shape, which):
        def index(i, tg, tlo, thi, *_):
            return (layer, tg[i], (tlo if which == 0 else thi)[i], 0, 0)
        return pl.BlockSpec((None, None, None) + shape, index)

    est = 2 * 6 * D * F * 4 + 6 * D * F * 2 + 3 * tm * DA * 4 + 3 * tm * D * 4 + 8 * tm * F * 4
    return pl.pallas_call(
        functools.partial(_moe_kernel, tm=tm, d_model=D, n_tok=T, final_norm=final_norm),
        grid_spec=pltpu.PrefetchScalarGridSpec(
            num_scalar_prefetch=6,
            grid=(n_tiles,),
            in_specs=[
                pl.BlockSpec(memory_space=pl.ANY),
                pl.BlockSpec((2, T // LANES, LANES), lambda i, *_: (0, 0, 0)),
                pl.BlockSpec((1, D), lambda i, *_: (0, 0)),
                pl.BlockSpec((1, D), lambda i, *_: (0, 0)),
                w_spec((D, F), 0), w_spec((D, F), 0), w_spec((F, D), 0),
                w_spec((D, F), 1), w_spec((D, F), 1), w_spec((F, D), 1),
            ],
            out_specs=pl.BlockSpec(memory_space=pl.ANY),
            scratch_shapes=[
                pltpu.SMEM((T + tm,), jnp.int32),
                pltpu.SMEM((T // LANES, LANES), jnp.int32),
                pltpu.VMEM((T // LANES, LANES), jnp.int32),
                pltpu.VMEM((tm, DA), F32), pltpu.VMEM((tm, DA), F32), pltpu.VMEM((tm, DA), F32),
                pltpu.VMEM((tm, D), F32), pltpu.VMEM((tm, D), F32), pltpu.VMEM((tm, D), F32),
                pltpu.VMEM((4, D, F), BF16),
                pltpu.VMEM((2, F, D), BF16),
                pltpu.SemaphoreType.DMA((3,)),
                pltpu.SemaphoreType.DMA((3,)),
                pltpu.SemaphoreType.DMA(()),
            ],
        ),
        out_shape=jax.ShapeDtypeStruct((T, D), F32),
        compiler_params=pltpu.CompilerParams(
            dimension_semantics=("arbitrary",), vmem_limit_bytes=_vmem_limit(est)),
        name="moe_experts",
    )(tile_g, tile_lo, tile_hi, nvalid, base, cstart,
      h_aug, route2d, ffn_norm_g, final_norm_g, wg, wu, wd, wg, wu, wd)


def _partial_rotary(xh, cosf, sinp, sinm, half):
    lanes = xh.shape[-1]
    return xh * cosf + pltpu.roll(xh, half, 1) * sinp + pltpu.roll(xh, lanes - half, 1) * sinm


def _tile_cos_sin(cb_ref, sb_ref, cr_ref, sr_ref):
    cb, sb, cr, sr = cb_ref[...], sb_ref[...], cr_ref[...], sr_ref[...]
    return cb * cr - sb * sr, sb * cr + cb * sr


def _kvq_kernel(h_ref, gkv_ref, gq_ref, wkv_ref, wq_ref, cb_ref, sb_ref, cr_ref, sr_ref,
                k_ref, vt_ref, qt_ref, km_ref, *, heads, dh, half):
    x = h_ref[...]
    xr = x * _rms_scale(x)
    xkv = (xr * gkv_ref[...]).astype(BF16)
    xq = (xr * gq_ref[...]).astype(BF16)
    cosf, sin_all = _tile_cos_sin(cb_ref, sb_ref, cr_ref, sr_ref)
    lane = lax.broadcasted_iota(jnp.int32, sin_all.shape, 1)
    sinp = jnp.where(lane >= half, sin_all, 0.0)
    sinm = jnp.where(lane < half, -sin_all, 0.0)
    width = heads * dh
    q_scale = dh ** -0.5 * LOG2_E
    pair_w = 2 * dh
    blk = k_ref.shape[2]
    n_sub = x.shape[0] // blk
    for c0 in range(0, width, pair_w):
        kk = jnp.dot(xkv, wkv_ref[:, c0:c0 + pair_w], preferred_element_type=F32)
        vv = jnp.dot(xkv, wkv_ref[:, width + c0:width + c0 + pair_w], preferred_element_type=F32)
        qq = jnp.dot(xq, wq_ref[:, c0:c0 + pair_w], preferred_element_type=F32)
        for u in range(2):
            hh = c0 // dh + u
            cols = slice(hh * dh, (hh + 1) * dh)
            kh = _partial_rotary(kk[:, u * dh:(u + 1) * dh], cosf, sinp, sinm, half)
            qh = _partial_rotary(qq[:, u * dh:(u + 1) * dh], cosf, sinp, sinm, half) * q_scale
            vh = vv[:, u * dh:(u + 1) * dh]
            for b in range(n_sub):
                rows = slice(b * blk, (b + 1) * blk)
                k_ref[hh, b] = kh[rows].astype(BF16)
                km_ref[b, :, cols] = jnp.mean(kh[rows], axis=0, keepdims=True)
                vt_ref[hh, b, :dh] = vh[rows].astype(BF16).T
                vt_ref[hh, b, dh:] = jnp.ones((VT_ONES_ROWS, blk), BF16)
                qt_ref[hh, b] = qh[rows].astype(BF16).T


def _angle_tables(inv_freq, seq, tm):
    base = jnp.arange(0, seq, tm, dtype=F32)[:, None] * inv_freq[None, :]
    offs = jnp.arange(tm, dtype=F32)[:, None] * inv_freq[None, :]
    return jnp.cos(base)[:, None, :], jnp.sin(base)[:, None, :], jnp.cos(offs), jnp.sin(offs)


def _rope_tables(seq, dh, tm):
    rot = dh // ROPE_FRACTION
    half = rot // 2
    inv = 1.0 / (ROPE_THETA ** (jnp.arange(half, dtype=F32) / half))
    inv_lanes = jnp.concatenate([inv, inv, jnp.zeros((dh - rot,), F32)])
    return _angle_tables(inv_lanes, seq, tm) + (half,)


def _kvq_proj(h, kv_norm_g, q_norm_g, wkv_bf16, wq_bf16, seq):
    T, D = h.shape
    heads = ATT_HEADS
    dh = D // heads
    blk = MOBA_BLOCK
    n_sub = KVQ_BLOCKS_PER_TILE
    tm = n_sub * blk
    assert seq % tm == 0
    n_pos_tiles = seq // tm
    cos_b, sin_b, cos_r, sin_r, half = _rope_tables(seq, dh, tm)
    width = heads * dh
    est = 2 * (D * 2 * width + D * width) * 2 + 2 * tm * D * 4 + 6 * tm * width * 2 + 8 * tm * dh * 4
    pos = lambda i: (i % n_pos_tiles, 0, 0)
    const = lambda i: (0, 0)
    per_block = lambda i: (0, i, 0, 0)
    return pl.pallas_call(
        functools.partial(_kvq_kernel, heads=heads, dh=dh, half=half),
        grid=(T // tm,),
        in_specs=[
            pl.BlockSpec((tm, D), lambda i: (i, 0)),
            pl.BlockSpec((1, D), const),
            pl.BlockSpec((1, D), const),
            pl.BlockSpec((D, 2 * width), const),
            pl.BlockSpec((D, width), const),
            pl.BlockSpec((None, 1, dh), pos),
            pl.BlockSpec((None, 1, dh), pos),
            pl.BlockSpec((tm, dh), const),
            pl.BlockSpec((tm, dh), const),
        ],
        out_specs=[
            pl.BlockSpec((heads, n_sub, blk, dh), per_block),
            pl.BlockSpec((heads, n_sub, dh + VT_ONES_ROWS, blk), per_block),
            pl.BlockSpec((heads, n_sub, dh, blk), per_block),
            pl.BlockSpec((n_sub, 1, width), lambda i: (i, 0, 0)),
        ],
        out_shape=[
            jax.ShapeDtypeStruct((heads, T // blk, blk, dh), BF16),
            jax.ShapeDtypeStruct((heads, T // blk, dh + VT_ONES_ROWS, blk), BF16),
            jax.ShapeDtypeStruct((heads, T // blk, dh, blk), BF16),
            jax.ShapeDtypeStruct((T // blk, 1, width), F32),
        ],
        compiler_params=pltpu.CompilerParams(
            dimension_semantics=("arbitrary",), vmem_limit_bytes=_vmem_limit(est)),
        name="kvq_proj",
    )(h, kv_norm_g, q_norm_g, wkv_bf16, wq_bf16, cos_b, sin_b, cos_r, sin_r)


def _moba_kernel(qt_ref, k_ref, vt_ref, km_ref, o_ref, sel_ref, acc_ref, s0_ref, s1_ref, p_ref,
                 *, topk, group, dh):
    qi = pl.program_id(2)
    n_blk = km_ref.shape[0]
    blk = s0_ref.shape[1] // 2
    neg = -jnp.inf
    halves = (slice(0, blk), slice(blk, 2 * blk))

    def chunks(h):
        return [slice(c, c + MOBA_KEY_CHUNK) for c in range(h * blk, (h + 1) * blk, MOBA_KEY_CHUNK)]

    def stage_pair(s_buf, j):
        jp = jnp.minimum(j, n_blk - 2)
        for g in range(group):
            keys = k_ref[g, pl.ds(jp, 2)].reshape(2 * blk, dh)
            s_buf[g] = jnp.dot(keys, qt_ref[g], preferred_element_type=F32).astype(BF16)

    def stage_softmax(s_buf, h, ms, j):
        rows = chunks(h)
        new_m, alphas = [], []
        for g in range(group):
            part = s_buf[g, rows[0], :]
            for c in rows[1:]:
                part = jnp.maximum(part, s_buf[g, c, :])
            m_blk = jnp.max(part, axis=0, keepdims=True).astype(F32)
            if j is None:
                m_new = shift = m_blk
            else:
                bias = sel_ref[g, pl.ds(j, 1), :]
                m_new = jnp.maximum(ms[g], m_blk + bias)
                shift = m_new - bias
            shift = shift.astype(BF16)
            for c in rows:
                p_ref[g, c, :] = jnp.exp2(s_buf[g, c, :] - shift)
            new_m.append(m_new)
            alphas.append(None if j is None else jnp.exp2(ms[g] - m_new))
        return tuple(new_m), alphas

    def stage_pv(h, j, alphas):
        for g in range(group):
            pv = jnp.dot(vt_ref[g, j], p_ref[g, halves[h], :], preferred_element_type=F32)
            acc_ref[g] = pv if alphas[g] is None else alphas[g] * acc_ref[g] + pv

    for g in range(group):
        km = km_ref[:, g * dh:(g + 1) * dh]
        km_hi = km.astype(BF16)
        km_lo = (km - km_hi.astype(F32)).astype(BF16)
        gate = (jnp.dot(km_hi, qt_ref[g], preferred_element_type=F32)
                + jnp.dot(km_lo, qt_ref[g], preferred_element_type=F32))
        blk_id = lax.broadcasted_iota(jnp.int32, gate.shape, 0)
        gv = jnp.where(blk_id < qi, gate, neg)
        bias = jnp.full(gate.shape, neg, F32)
        for _ in range(topk):
            m = jnp.max(gv, axis=0, keepdims=True)
            first = jnp.min(jnp.where(gv == m, blk_id, n_blk), axis=0, keepdims=True)
            bias = jnp.where((blk_id == first) & (m > neg), 0.0, bias)
            gv = jnp.where(blk_id == first, neg, gv)
        sel_ref[g] = bias

    key_id = lax.broadcasted_iota(jnp.int32, (blk, blk), 0)
    qry_id = lax.broadcasted_iota(jnp.int32, (blk, blk), 1)
    for g in range(group):
        own = jnp.dot(k_ref[g, qi], qt_ref[g], preferred_element_type=F32)
        s1_ref[g, halves[0], :] = jnp.where(key_id <= qry_id, own, neg).astype(BF16)
    stage_pair(s0_ref, 0)
    ms, alphas = stage_softmax(s1_ref, 0, None, None)
    stage_pv(0, qi, alphas)

    def trip(t, ms, cur, nxt):
        stage_pair(nxt, 2 * t + 2)
        for h in range(2):
            ms, alphas = stage_softmax(cur, h, ms, 2 * t + h)
            stage_pv(h, 2 * t + h, alphas)
        return ms

    def body(t, ms):
        return lax.cond(t % 2 == 0,
                        lambda m: trip(t, m, s0_ref, s1_ref),
                        lambda m: trip(t, m, s1_ref, s0_ref), ms)

    n_full = qi // 2
    ms = lax.fori_loop(0, n_full, body, ms)

    for parity, buf in enumerate((s0_ref, s1_ref)):
        @pl.when((qi % 2 == 1) & (n_full % 2 == parity))
        def _():
            _, alphas = stage_softmax(buf, 0, ms, qi - 1)
            stage_pv(0, qi - 1, alphas)
    for g in range(group):
        acc = acc_ref[g]
        o_ref[:, g * dh:(g + 1) * dh] = (acc[:dh] / acc[dh:dh + 1]).astype(BF16).T


def _moba_attention(qt, k, vt, k_mean, batch, seq):
    heads, n_tiles, dh, blk = qt.shape
    dv_rows = vt.shape[2]
    n_blk = seq // blk
    topk = min(MOBA_TOPK, n_blk)
    width = heads * dh
    group = MOBA_HEAD_GROUP
    km = k_mean.reshape(batch, n_blk, width)
    per_seq = lambda b, h, i: (h, b, 0, 0)
    est = group * seq * (dh + dv_rows) * 2 + 2 * group * blk * blk * (2 + 2) + 6 * group * blk * dh * 4
    return pl.pallas_call(
        functools.partial(_moba_kernel, topk=topk, group=group, dh=dh),
        grid=(batch, heads // group, n_blk),
        in_specs=[
            pl.BlockSpec((group, None, dh, blk), lambda b, h, i: (h, b * n_blk + i, 0, 0)),
            pl.BlockSpec((group, n_blk, blk, dh), per_seq, pipeline_mode=pl.Buffered(1)),
            pl.BlockSpec((group, n_blk, dv_rows, blk), per_seq, pipeline_mode=pl.Buffered(1)),
            pl.BlockSpec((None, n_blk, group * dh), lambda b, h, i: (b, 0, h)),
        ],
        out_specs=pl.BlockSpec((blk, group * dh), lambda b, h, i: (b * n_blk + i, h)),
        out_shape=jax.ShapeDtypeStruct((n_tiles * blk, width), BF16),
        scratch_shapes=[pltpu.VMEM((group, n_blk, blk), F32), pltpu.VMEM((group, dv_rows, blk), F32),
                        pltpu.VMEM((group, 2 * blk, blk), BF16), pltpu.VMEM((group, 2 * blk, blk), BF16),
                        pltpu.VMEM((group, 2 * blk, blk), BF16)],
        compiler_params=pltpu.CompilerParams(
            dimension_semantics=("arbitrary", "arbitrary", "arbitrary"), vmem_limit_bytes=_vmem_limit(est)),
        name="moba_attention",
    )(qt, k, vt, km)


def kernel(x, ret_norm, ret_w_in, ret_w_out, kv_norm, w_kv, attn_norm, w_q, w_o, ffn_norm, router_group_w, router_group_b, router_expert_w, router_expert_b, expert_w_gate, expert_w_up, expert_w_down, final_norm):
    B, S, D = x.shape
    T = B * S
    assert S % MOBA_BLOCK == 0 and S % RET_KERNEL_CHUNK == 0 and T % MOE_TILE == 0
    assert ret_norm.shape[0] == 1 and attn_norm.shape[0] == 1 and ffn_norm.shape[0] == 2
    h = x.reshape(T, D)
    final_g = final_norm[None, :]

    def proj_router(y, w, resid, layer, name):
        return _proj_router(y, w.astype(BF16), resid, ffn_norm[layer][None, :], router_group_w[layer],
                            router_group_b[layer], router_expert_w[layer], router_expert_b[layer], name)

    def moe(h_aug, route, counts, layer, last):
        return _moe_layer(h_aug, route, counts, ffn_norm[layer][None, :], final_g,
                          expert_w_gate, expert_w_up, expert_w_down, layer, final_norm=last)

    q, kt, v, gate = _ret_in_proj(h, ret_norm[0][None, :], ret_w_in[0].astype(BF16), S)
    y = _ret_core(q, kt, v, gate, B, S)
    h = moe(*proj_router(y, ret_w_out[0], h, 0, "ret_out_proj_router"), 0, False)

    k2, vt2, qt2, k_mean = _kvq_proj(h, kv_norm[None, :], attn_norm[0][None, :],
                                     w_kv.astype(BF16), w_q[0].astype(BF16), S)
    o = _moba_attention(qt2, k2, vt2, k_mean, B, S)
    h = moe(*proj_router(o, w_o[0], h, 1, "attn_out_proj_router"), 1, True)
    return h.reshape(B, S, D)
```

```python
import functools

import jax
import jax.numpy as jnp
from jax import lax
from jax.experimental import pallas as pl
from jax.experimental.pallas import tpu as pltpu

F32 = jnp.float32
BF16 = jnp.bfloat16

NORM_EPS = 1e-6
RET_HEADS = 4
RET_ROT_BASE = 10000.0
ATT_HEADS = 8
ROPE_FRACTION = 4
ROPE_THETA = 500000.0
MOBA_BLOCK = 256
MOBA_TOPK = 3
N_GROUPS = 4
EXPERTS_PER_GROUP = 4
PAIR_LO = (0, 0, 1, 1, 0, 2)
PAIR_HI = (1, 2, 2, 3, 3, 3)
N_PAIRS = len(PAIR_LO)
N_CLASSES = N_GROUPS * N_PAIRS

LANES = 128
SUBLANES = 8
RET_KERNEL_CHUNK = 256
RET_CHUNKS_PER_TILE = 2
RET_CORE_CHUNKS_PER_TILE = 4
MOE_TILE = 256
ROUTE_ROWS = 32
MOBA_HEAD_GROUP = 8
KVQ_BLOCKS_PER_TILE = 4
MOBA_KEY_CHUNK = 32
VT_ONES_ROWS = 16
LOG2_E = 1.4426950408889634
V7X_VMEM_BYTES = 64 * 1024 * 1024
V7X_VMEM_RESERVED_BYTES = 8 * 1024 * 1024
DEFAULT_SCOPED_VMEM_BYTES = 32 * 1024 * 1024


def _vmem_limit(estimate_bytes):
    wanted = max(DEFAULT_SCOPED_VMEM_BYTES, estimate_bytes * 5 // 4)
    return int(min(V7X_VMEM_BYTES - V7X_VMEM_RESERVED_BYTES, wanted))


def _rms_scale(x):
    return lax.rsqrt(jnp.mean(x * x, axis=-1, keepdims=True) + NORM_EPS)


def _silu(a):
    return a * jax.nn.sigmoid(a)


def _ret_in_kernel(x_ref, g_ref, w_ref, cb_ref, sb_ref, cr_ref, sr_ref, qdec_ref, kdec_ref,
                   q_ref, kt_ref, v_ref, gate_ref, *, heads, dk, dv):
    x = x_ref[...]
    xn = (x * _rms_scale(x) * g_ref[...]).astype(BF16)
    cos, sin = _tile_cos_sin(cb_ref, sb_ref, cr_ref, sr_ref)
    half = dk // 2

    def rotated(col0, h, dec):
        p = jnp.dot(xn, w_ref[:, col0 + h * dk:col0 + (h + 1) * dk], preferred_element_type=F32)
        x1 = p[:, :half]
        x2 = p[:, half:]
        cos_h = cos * dec
        sin_h = sin * dec
        return x1 * cos_h - x2 * sin_h, x2 * cos_h + x1 * sin_h

    cc = kt_ref.shape[3]
    for h in range(heads):
        lo, hi = rotated(0, h, qdec_ref[h])
        q_ref[:, h * dk:h * dk + half] = lo.astype(BF16)
        q_ref[:, h * dk + half:(h + 1) * dk] = hi.astype(BF16)
        lo, hi = rotated(heads * dk, h, kdec_ref[h])
        for c in range(kt_ref.shape[1]):
            rows = slice(c * cc, (c + 1) * cc)
            kt_ref[h, c, :half] = lo[rows].astype(BF16).T
            kt_ref[h, c, half:] = hi[rows].astype(BF16).T
    v0 = 2 * heads * dk
    g0 = v0 + heads * dv
    for h in range(heads):
        v_ref[:, h * dv:(h + 1) * dv] = jnp.dot(
            xn, w_ref[:, v0 + h * dv:v0 + (h + 1) * dv], preferred_element_type=F32).astype(BF16)
        gate_ref[:, h * dv:(h + 1) * dv] = jnp.dot(
            xn, w_ref[:, g0 + h * dv:g0 + (h + 1) * dv], preferred_element_type=F32).astype(BF16)


def _ret_in_proj(h, norm_g, w_bf16, seq):
    T, D = h.shape
    heads = RET_HEADS
    dk = D // heads
    dv = 2 * dk
    n_cols = w_bf16.shape[1]
    cc = RET_KERNEL_CHUNK
    n_sub = RET_CHUNKS_PER_TILE
    tm = n_sub * cc
    assert seq % tm == 0
    n_pos_tiles = seq // tm
    half = dk // 2
    inv = 1.0 / (RET_ROT_BASE ** (jnp.arange(half, dtype=F32) / half))
    cos_b, sin_b, cos_r, sin_r = _angle_tables(inv, seq, tm)
    log_gamma = _ret_log_gamma(heads)
    idx = jnp.tile(jnp.arange(cc, dtype=F32), n_sub)
    q_dec = jnp.broadcast_to(jnp.exp(log_gamma[:, None] * idx)[:, :, None], (heads, tm, half))
    k_dec = jnp.broadcast_to((jnp.exp(-log_gamma[:, None] * idx) * dk ** -0.5)[:, :, None], (heads, tm, half))
    pos = lambda i: (i % n_pos_tiles, 0, 0)
    est = D * n_cols * 2 + 2 * tm * D * 4 + 2 * tm * n_cols * 2 + (4 + 4 * heads) * tm * LANES * 4
    return pl.pallas_call(
        functools.partial(_ret_in_kernel, heads=heads, dk=dk, dv=dv),
        grid=(T // tm,),
        in_specs=[
            pl.BlockSpec((tm, D), lambda i: (i, 0)),
            pl.BlockSpec((1, D), lambda i: (0, 0)),
            pl.BlockSpec((D, n_cols), lambda i: (0, 0), pipeline_mode=pl.Buffered(1)),
            pl.BlockSpec((None, 1, half), pos),
            pl.BlockSpec((None, 1, half), pos),
            pl.BlockSpec((tm, half), lambda i: (0, 0)),
            pl.BlockSpec((tm, half), lambda i: (0, 0)),
            pl.BlockSpec((heads, tm, half), lambda i: (0, 0, 0)),
            pl.BlockSpec((heads, tm, half), lambda i: (0, 0, 0)),
        ],
        out_specs=[
            pl.BlockSpec((tm, heads * dk), lambda i: (i, 0)),
            pl.BlockSpec((heads, n_sub, dk, cc), lambda i: (0, i, 0, 0)),
            pl.BlockSpec((tm, heads * dv), lambda i: (i, 0)),
            pl.BlockSpec((tm, heads * dv), lambda i: (i, 0)),
        ],
        out_shape=[
            jax.ShapeDtypeStruct((T, heads * dk), BF16),
            jax.ShapeDtypeStruct((heads, T // cc, dk, cc), BF16),
            jax.ShapeDtypeStruct((T, heads * dv), BF16),
            jax.ShapeDtypeStruct((T, heads * dv), BF16),
        ],
        compiler_params=pltpu.CompilerParams(
            dimension_semantics=("arbitrary",), vmem_limit_bytes=_vmem_limit(est)),
        name="ret_in_proj",
    )(h, norm_g, w_bf16, cos_b, sin_b, cos_r, sin_r, q_dec, k_dec)


def _ret_log_gamma(heads):
    return jnp.log1p(-jnp.power(2.0, -5.0 - jnp.arange(heads, dtype=F32)))


def _ret_core_kernel(q_ref, kt_ref, v_ref, g_ref, cd_ref, y_ref, state_ref, *, heads):
    @pl.when(pl.program_id(1) == 0)
    def _():
        state_ref[...] = jnp.zeros_like(state_ref)

    n_sub, cc = kt_ref.shape[1], kt_ref.shape[3]
    dk = q_ref.shape[1] // heads
    dv = v_ref.shape[1] // heads
    causal = lax.broadcasted_iota(jnp.int32, (cc, cc), 0) >= lax.broadcasted_iota(jnp.int32, (cc, cc), 1)
    for c in range(n_sub):
        rows = slice(c * cc, (c + 1) * cc)
        for h in range(heads):
            q = q_ref[rows, h * dk:(h + 1) * dk]
            kt = kt_ref[h, c]
            v = v_ref[rows, h * dv:(h + 1) * dv]
            s = jnp.where(causal, jnp.dot(q, kt, preferred_element_type=F32), 0.0)
            u = state_ref[h]
            o = (jnp.dot(s.astype(BF16), v, preferred_element_type=F32)
                 + jnp.dot(q, u.astype(BF16), preferred_element_type=F32))
            state_ref[h] = cd_ref[h] * (u + jnp.dot(kt, v, preferred_element_type=F32))
            o = o * _rms_scale(o)
            gate = g_ref[rows, h * dv:(h + 1) * dv].astype(F32)
            y_ref[rows, h * dv:(h + 1) * dv] = (_silu(gate) * o).astype(BF16)


def _ret_core(q, kt, v, gate, batch, seq):
    T = q.shape[0]
    heads = RET_HEADS
    dk = q.shape[1] // heads
    dv = v.shape[1] // heads
    cc = RET_KERNEL_CHUNK
    n_sub = RET_CORE_CHUNKS_PER_TILE
    tm = n_sub * cc
    assert seq % tm == 0
    nc = seq // tm
    chunk_decay = jnp.broadcast_to(jnp.exp(_ret_log_gamma(heads) * cc)[:, None, None], (heads, 1, dv))
    row = lambda b, c: (b * nc + c, 0)
    return pl.pallas_call(
        functools.partial(_ret_core_kernel, heads=heads),
        grid=(batch, nc),
        in_specs=[
            pl.BlockSpec((tm, heads * dk), row),
            pl.BlockSpec((heads, n_sub, dk, cc), lambda b, c: (0, b * nc + c, 0, 0)),
            pl.BlockSpec((tm, heads * dv), row),
            pl.BlockSpec((tm, heads * dv), row),
            pl.BlockSpec((heads, 1, dv), lambda b, c: (0, 0, 0)),
        ],
        out_specs=pl.BlockSpec((tm, heads * dv), row),
        out_shape=jax.ShapeDtypeStruct((T, heads * dv), BF16),
        scratch_shapes=[pltpu.VMEM((heads, dk, dv), F32)],
        compiler_params=pltpu.CompilerParams(dimension_semantics=("arbitrary", "arbitrary")),
        name="ret_core",
    )(q, kt, v, gate, chunk_decay)


def _proj_router_kernel(y_ref, w_ref, r_ref, g_ref, wr_ref, br_ref, haug_ref, route_ref, counts_ref, run_ref,
                        *, groups, experts, d_model):
    @pl.when(pl.program_id(0) == 0)
    def _():
        run_ref[...] = jnp.zeros_like(run_ref)

    h = r_ref[...] + jnp.dot(y_ref[...], w_ref[...], preferred_element_type=F32)
    xn = h * _rms_scale(h) * g_ref[...]
    xh = xn.astype(BF16)
    xl = (xn - xh.astype(F32)).astype(BF16)
    hi_both = jnp.dot(xh, wr_ref[...], preferred_element_type=F32)
    lo_hi = jnp.dot(xl, wr_ref[:, :LANES], preferred_element_type=F32)
    logits = hi_both[:, :LANES] + (hi_both[:, LANES:] + lo_hi) + br_ref[...]
    lt = logits.T[:ROUTE_ROWS]
    tm = lt.shape[1]
    row = lax.broadcasted_iota(jnp.int32, lt.shape, 0)
    neg = -jnp.inf

    def argmax_first(vals):
        m = jnp.max(vals, axis=0, keepdims=True)
        first = jnp.min(jnp.where(vals == m, row, ROUTE_ROWS), axis=0, keepdims=True)
        return m, first

    gl = jnp.where(row < groups, lt, neg)
    gmax, gidx = argmax_first(gl)
    gsum = jnp.sum(jnp.where(row < groups, jnp.exp(gl - gmax), 0.0), axis=0, keepdims=True)
    g_val = 1.0 / gsum
    e0 = groups + experts * gidx
    el = jnp.where((row >= e0) & (row < e0 + experts), lt, neg)
    m1, i1 = argmax_first(el)
    m2, i2 = argmax_first(jnp.where(row == i1, neg, el))
    t = jnp.exp(m2 - m1)
    w1 = 1.0 / (1.0 + t)
    w2 = t / (1.0 + t)
    first_is_lo = i1 < i2
    lo = jnp.minimum(i1, i2) - e0
    hi = jnp.maximum(i1, i2) - e0
    w_lo = g_val * jnp.where(first_is_lo, w1, w2)
    w_hi = g_val * jnp.where(first_is_lo, w2, w1)
    lex = (lo * (7 - lo)) // 2 + (hi - lo - 1)
    pair = jnp.where(lex == 2, 4, jnp.where(lex == 3, 2, jnp.where(lex == 4, 3, lex)))
    cls = gidx * N_PAIRS + pair

    onehot = row == cls
    oh = jnp.where(onehot, 1.0, 0.0)
    before = lax.broadcasted_iota(jnp.int32, (tm, tm), 0) < lax.broadcasted_iota(jnp.int32, (tm, tm), 1)
    prefix = jnp.dot(oh.astype(BF16), jnp.where(before, 1.0, 0.0).astype(BF16), preferred_element_type=F32)
    rank = jnp.sum(jnp.where(onehot, prefix + run_ref[:, 0:1], 0.0), axis=0, keepdims=True)
    run_new = run_ref[...] + jnp.sum(oh, axis=1, keepdims=True)
    run_ref[...] = run_new
    counts_ref[...] = run_new

    r8 = lax.broadcasted_iota(jnp.int32, route_ref.shape, 0)
    route_ref[...] = jnp.where(r8 == 0, cls, jnp.where(r8 == 1, rank.astype(jnp.int32), 0))
    rl = lax.broadcasted_iota(jnp.int32, (LANES, tm), 0)
    meta_t = jnp.where(rl == 0, w_lo, jnp.where(rl == 1, w_hi, 0.0))
    haug_ref[:, :d_model] = h
    haug_ref[:, d_model:] = meta_t.T


def _proj_router(y, w_bf16, resid, norm_g, w_rg, b_rg, w_re, b_re, name, tm=1024):
    T, K = y.shape
    D = w_bf16.shape[1]
    G, E = N_GROUPS, EXPERTS_PER_GROUP
    n_used = G + G * E
    wr = jnp.concatenate([w_rg, jnp.transpose(w_re, (1, 0, 2)).reshape(D, G * E)], axis=1)
    wr = jnp.pad(wr, ((0, 0), (0, LANES - n_used)))
    wr_hi = wr.astype(BF16)
    wr = jnp.concatenate([wr_hi, (wr - wr_hi.astype(F32)).astype(BF16)], axis=1)
    br = jnp.pad(jnp.concatenate([b_rg, b_re.reshape(G * E)]), (0, LANES - n_used))[None, :]
    est = 2 * K * D * 2 + 2 * tm * K * 2 + 6 * tm * D * 4 + 2 * D * LANES * 4 + 4 * tm * tm * 4
    const = lambda i: (0, 0)
    return pl.pallas_call(
        functools.partial(_proj_router_kernel, groups=G, experts=E, d_model=D),
        grid=(T // tm,),
        in_specs=[
            pl.BlockSpec((tm, K), lambda i: (i, 0)),
            pl.BlockSpec((K, D), const),
            pl.BlockSpec((tm, D), lambda i: (i, 0)),
            pl.BlockSpec((1, D), const),
            pl.BlockSpec((D, 2 * LANES), const),
            pl.BlockSpec((1, LANES), const),
        ],
        out_specs=[
            pl.BlockSpec((tm, D + LANES), lambda i: (i, 0)),
            pl.BlockSpec((8, tm), lambda i: (0, i)),
            pl.BlockSpec((ROUTE_ROWS, LANES), const),
        ],
        out_shape=[
            jax.ShapeDtypeStruct((T, D + LANES), F32),
            jax.ShapeDtypeStruct((8, T), jnp.int32),
            jax.ShapeDtypeStruct((ROUTE_ROWS, LANES), F32),
        ],
        scratch_shapes=[pltpu.VMEM((ROUTE_ROWS, LANES), F32)],
        compiler_params=pltpu.CompilerParams(
            dimension_semantics=("arbitrary",), vmem_limit_bytes=_vmem_limit(est)),
        name=name,
    )(y, w_bf16, resid, norm_g, wr, br)


def _tile_schedule(counts_blk, tm, n_tiles):
    counts = counts_blk[:N_CLASSES, 0].astype(jnp.int32)
    tiles_per = (counts + tm - 1) // tm
    tile_end = jnp.cumsum(tiles_per)
    tile_begin = tile_end - tiles_per
    cstart = jnp.cumsum(counts) - counts
    t = jnp.arange(n_tiles, dtype=jnp.int32)
    live = t < tile_end[-1]
    tq = jnp.where(live, t, tile_end[-1] - 1)
    tcls = jnp.sum((tile_end[None, :] <= tq[:, None]).astype(jnp.int32), axis=1)
    onehot = tcls[:, None] == jnp.arange(N_CLASSES, dtype=jnp.int32)[None, :]
    pick = lambda table: jnp.sum(jnp.where(onehot, jnp.asarray(table, jnp.int32)[None, :], 0), axis=1)
    k = tq - pick(tile_begin)
    nvalid = jnp.where(live, jnp.clip(pick(counts) - k * tm, 0, tm), 0).astype(jnp.int32)
    base = jnp.where(live, pick(cstart) + k * tm, 0).astype(jnp.int32)
    tile_g = pick([c // N_PAIRS for c in range(N_CLASSES)])
    tile_lo = pick([PAIR_LO[c % N_PAIRS] for c in range(N_CLASSES)])
    tile_hi = pick([PAIR_HI[c % N_PAIRS] for c in range(N_CLASSES)])
    cstart_pad = jnp.pad(cstart, (0, ROUTE_ROWS - N_CLASSES)).astype(jnp.int32)
    return tile_g, tile_lo, tile_hi, nvalid, base, cstart_pad


def _moe_kernel(tg_ref, tlo_ref, thi_ref, tnv_ref, tbase_ref, cs_ref,
                h_hbm, route_ref, gn_ref, fn_ref, wg_lo, wu_lo, wd_lo, wg_hi, wu_hi, wd_hi,
                out_hbm, rt_ref, pos_smem, pos_vmem, x0, x1, x2, o0, o1, o2, w_in_bf, w_out_bf,
                gsem, ssem, psem, *, tm, d_model, n_tok, final_norm):
    i = pl.program_id(0)
    xbufs = (x0, x1, x2)
    obufs = (o0, o1, o2)
    depth = len(xbufs)

    def issue_gather(t, sl):
        base = tbase_ref[t]
        for r in range(tm):
            tok = rt_ref[base + r]
            pltpu.make_async_copy(h_hbm.at[pl.ds(tok, 1)], xbufs[sl].at[pl.ds(r, 1)], gsem.at[sl]).start()

    def scatter_row(sl, r, tok):
        return pltpu.make_async_copy(obufs[sl].at[pl.ds(r, 1)], out_hbm.at[pl.ds(tok, 1)], ssem.at[sl])

    def issue_scatter(sl, base, n_valid):
        for r in range(tm):
            @pl.when(r < n_valid)
            def _():
                scatter_row(sl, r, rt_ref[base + r]).start()

    def wait_gather(sl):
        pltpu.make_async_copy(h_hbm.at[pl.ds(0, tm)], xbufs[sl], gsem.at[sl]).wait()

    def wait_scatter(sl, n_valid):
        aligned = pl.multiple_of((n_valid // SUBLANES) * SUBLANES, SUBLANES)

        @pl.when(aligned > 0)
        def _():
            rows = pl.ds(0, aligned)
            pltpu.make_async_copy(obufs[sl].at[rows], out_hbm.at[rows], ssem.at[sl]).wait()

        def single(r, c):
            scatter_row(sl, r, 0).wait()
            return c
        lax.fori_loop(aligned, n_valid, single, 0)

    def live(t):
        return tnv_ref[jnp.maximum(t, 0)] > 0

    @pl.when(i == 0)
    def _():
        cls2 = route_ref[0]
        pos = route_ref[1]
        for c in range(N_CLASSES):
            pos = pos + jnp.where(cls2 == c, cs_ref[c], 0)
        pos_vmem[...] = pos
        to_smem = pltpu.make_async_copy(pos_vmem, pos_smem, psem)
        to_smem.start()
        to_smem.wait()

        def place(row, c):
            for col in range(LANES):
                rt_ref[pos_smem[row, col]] = row * LANES + col
            return c
        lax.fori_loop(0, n_tok // LANES, place, 0)
        for r in range(tm):
            rt_ref[n_tok + r] = 0
        issue_gather(0, 0)
        issue_gather(1, 1)

    prev = jnp.maximum(i - 1, 0)
    group_changed = (i == 0) | (tg_ref[i] != tg_ref[prev])
    for which, t_ref, (wg, wu, wd) in ((0, tlo_ref, (wg_lo, wu_lo, wd_lo)), (1, thi_ref, (wg_hi, wu_hi, wd_hi))):
        @pl.when(group_changed | (t_ref[i] != t_ref[prev]))
        def _():
            w_in_bf[2 * which] = wg[...].astype(BF16)
            w_in_bf[2 * which + 1] = wu[...].astype(BF16)
            w_out_bf[which] = wd[...].astype(BF16)

    def tile(cur):
        far = (cur + depth - 1) % depth
        @pl.when((i <= 1) | live(i - 2))
        def _():
            wait_gather(cur)

        @pl.when((i >= depth) & live(i - depth))
        def _():
            wait_scatter(cur, tnv_ref[jnp.maximum(i - depth, 0)])

        prev_base = tbase_ref[prev]
        prev_valid = jnp.where(i == 0, 0, tnv_ref[prev])

        @pl.when(live(i))
        def _():
            issue_scatter(far, prev_base, prev_valid)
            issue_gather(i + 2, far)
            xa = xbufs[cur][...]
            x = xa[:, :d_model]
            xn = (x * _rms_scale(x) * gn_ref[...]).astype(BF16)

            def expert(which, w):
                a = jnp.dot(xn, w_in_bf[2 * which], preferred_element_type=F32)
                b = jnp.dot(xn, w_in_bf[2 * which + 1], preferred_element_type=F32)
                return jnp.dot((_silu(a) * b * w).astype(BF16), w_out_bf[which], preferred_element_type=F32)

            y = expert(0, xa[:, d_model:d_model + 1]) + expert(1, xa[:, d_model + 1:d_model + 2])
            obufs[cur][...] = x + y
            if final_norm:
                for r0 in range(0, tm, 4 * SUBLANES):
                    rows = slice(r0, r0 + 4 * SUBLANES)
                    o = obufs[cur][rows, :]
                    obufs[cur][rows, :] = o * _rms_scale(o) * fn_ref[...]

        @pl.when(jnp.logical_not(live(i)) & (i >= 1) & live(i - 1))
        def _():
            issue_scatter(far, prev_base, prev_valid)

    for residue in range(depth):
        @pl.when(i % depth == residue)
        def _():
            tile(residue)


def _moe_layer(h_aug, route, counts_blk, ffn_norm_g, final_norm_g, wg, wu, wd, layer, final_norm):
    T, DA = h_aug.shape
    D = DA - LANES
    F = wg.shape[-1]
    tm = MOE_TILE
    assert T % (LANES * SUBLANES) == 0 and T >= 2 * tm
    n_tiles = T // tm + N_CLASSES + 3
    tile_g, tile_lo, tile_hi, nvalid, base, cstart = _tile_schedule(counts_blk, tm, n_tiles)
    route2d = route[:2].reshape(2, T // LANES, LANES)

    def w_spec(shape, which):
        def index(i, tg, tlo, thi, *_):
            return (layer, tg[i], (tlo if which == 0 else thi)[i], 0, 0)
        return pl.BlockSpec((None, None, None) + shape, index)

    est = 2 * 6 * D * F * 4 + 6 * D * F * 2 + 3 * tm * DA * 4 + 3 * tm * D * 4 + 8 * tm * F * 4
    return pl.pallas_call(
        functools.partial(_moe_kernel, tm=tm, d_model=D, n_tok=T, final_norm=final_norm),
        grid_spec=pltpu.PrefetchScalarGridSpec(
            num_scalar_prefetch=6,
            grid=(n_tiles,),
            in_specs=[
                pl.BlockSpec(memory_space=pl.ANY),
                pl.BlockSpec((2, T // LANES, LANES), lambda i, *_: (0, 0, 0)),
                pl.BlockSpec((1, D), lambda i, *_: (0, 0)),
                pl.BlockSpec((1, D), lambda i, *_: (0, 0)),
                w_spec((D, F), 0), w_spec((D, F), 0), w_spec((F, D), 0),
                w_spec((D, F), 1), w_spec((D, F), 1), w_spec((F, D), 1),
            ],
            out_specs=pl.BlockSpec(memory_space=pl.ANY),
            scratch_shapes=[
                pltpu.SMEM((T + tm,), jnp.int32),
                pltpu.SMEM((T // LANES, LANES), jnp.int32),
                pltpu.VMEM((T // LANES, LANES), jnp.int32),
                pltpu.VMEM((tm, DA), F32), pltpu.VMEM((tm, DA), F32), pltpu.VMEM((tm, DA), F32),
                pltpu.VMEM((tm, D), F32), pltpu.VMEM((tm, D), F32), pltpu.VMEM((tm, D), F32),
                pltpu.VMEM((4, D, F), BF16),
                pltpu.VMEM((2, F, D), BF16),
                pltpu.SemaphoreType.DMA((3,)),
                pltpu.SemaphoreType.DMA((3,)),
                pltpu.SemaphoreType.DMA(()),
            ],
        ),
        out_shape=jax.ShapeDtypeStruct((T, D), F32),
        compiler_params=pltpu.CompilerParams(
            dimension_semantics=("arbitrary",), vmem_limit_bytes=_vmem_limit(est)),
        name="moe_experts",
    )(tile_g, tile_lo, tile_hi, nvalid, base, cstart,
      h_aug, route2d, ffn_norm_g, final_norm_g, wg, wu, wd, wg, wu, wd)


def _partial_rotary(xh, cosf, sinp, sinm, half):
    lanes = xh.shape[-1]
    return xh * cosf + pltpu.roll(xh, half, 1) * sinp + pltpu.roll(xh, lanes - half, 1) * sinm


def _tile_cos_sin(cb_ref, sb_ref, cr_ref, sr_ref):
    cb, sb, cr, sr = cb_ref[...], sb_ref[...], cr_ref[...], sr_ref[...]
    return cb * cr - sb * sr, sb * cr + cb * sr


def _kvq_kernel(h_ref, gkv_ref, gq_ref, wkv_ref, wq_ref, cb_ref, sb_ref, cr_ref, sr_ref,
                k_ref, vt_ref, qt_ref, km_ref, *, heads, dh, half):
    x = h_ref[...]
    xr = x * _rms_scale(x)
    xkv = (xr * gkv_ref[...]).astype(BF16)
    xq = (xr * gq_ref[...]).astype(BF16)
    cosf, sin_all = _tile_cos_sin(cb_ref, sb_ref, cr_ref, sr_ref)
    lane = lax.broadcasted_iota(jnp.int32, sin_all.shape, 1)
    sinp = jnp.where(lane >= half, sin_all, 0.0)
    sinm = jnp.where(lane < half, -sin_all, 0.0)
    width = heads * dh
    q_scale = dh ** -0.5 * LOG2_E
    pair_w = 2 * dh
    blk = k_ref.shape[2]
    n_sub = x.shape[0] // blk
    for c0 in range(0, width, pair_w):
        kk = jnp.dot(xkv, wkv_ref[:, c0:c0 + pair_w], preferred_element_type=F32)
        vv = jnp.dot(xkv, wkv_ref[:, width + c0:width + c0 + pair_w], preferred_element_type=F32)
        qq = jnp.dot(xq, wq_ref[:, c0:c0 + pair_w], preferred_element_type=F32)
        for u in range(2):
            hh = c0 // dh + u
            cols = slice(hh * dh, (hh + 1) * dh)
            kh = _partial_rotary(kk[:, u * dh:(u + 1) * dh], cosf, sinp, sinm, half)
            qh = _partial_rotary(qq[:, u * dh:(u + 1) * dh], cosf, sinp, sinm, half) * q_scale
            vh = vv[:, u * dh:(u + 1) * dh]
            for b in range(n_sub):
                rows = slice(b * blk, (b + 1) * blk)
                k_ref[hh, b] = kh[rows].astype(BF16)
                km_ref[b, :, cols] = jnp.mean(kh[rows], axis=0, keepdims=True)
                vt_ref[hh, b, :dh] = vh[rows].astype(BF16).T
                vt_ref[hh, b, dh:] = jnp.ones((VT_ONES_ROWS, blk), BF16)
                qt_ref[hh, b] = qh[rows].astype(BF16).T


def _angle_tables(inv_freq, seq, tm):
    base = jnp.arange(0, seq, tm, dtype=F32)[:, None] * inv_freq[None, :]
    offs = jnp.arange(tm, dtype=F32)[:, None] * inv_freq[None, :]
    return jnp.cos(base)[:, None, :], jnp.sin(base)[:, None, :], jnp.cos(offs), jnp.sin(offs)


def _rope_tables(seq, dh, tm):
    rot = dh // ROPE_FRACTION
    half = rot // 2
    inv = 1.0 / (ROPE_THETA ** (jnp.arange(half, dtype=F32) / half))
    inv_lanes = jnp.concatenate([inv, inv, jnp.zeros((dh - rot,), F32)])
    return _angle_tables(inv_lanes, seq, tm) + (half,)


def _kvq_proj(h, kv_norm_g, q_norm_g, wkv_bf16, wq_bf16, seq):
    T, D = h.shape
    heads = ATT_HEADS
    dh = D // heads
    blk = MOBA_BLOCK
    n_sub = KVQ_BLOCKS_PER_TILE
    tm = n_sub * blk
    assert seq % tm == 0
    n_pos_tiles = seq // tm
    cos_b, sin_b, cos_r, sin_r, half = _rope_tables(seq, dh, tm)
    width = heads * dh
    est = 2 * (D * 2 * width + D * width) * 2 + 2 * tm * D * 4 + 6 * tm * width * 2 + 8 * tm * dh * 4
    pos = lambda i: (i % n_pos_tiles, 0, 0)
    const = lambda i: (0, 0)
    per_block = lambda i: (0, i, 0, 0)
    return pl.pallas_call(
        functools.partial(_kvq_kernel, heads=heads, dh=dh, half=half),
        grid=(T // tm,),
        in_specs=[
            pl.BlockSpec((tm, D), lambda i: (i, 0)),
            pl.BlockSpec((1, D), const),
            pl.BlockSpec((1, D), const),
            pl.BlockSpec((D, 2 * width), const),
            pl.BlockSpec((D, width), const),
            pl.BlockSpec((None, 1, dh), pos),
            pl.BlockSpec((None, 1, dh), pos),
            pl.BlockSpec((tm, dh), const),
            pl.BlockSpec((tm, dh), const),
        ],
        out_specs=[
            pl.BlockSpec((heads, n_sub, blk, dh), per_block),
            pl.BlockSpec((heads, n_sub, dh + VT_ONES_ROWS, blk), per_block),
            pl.BlockSpec((heads, n_sub, dh, blk), per_block),
            pl.BlockSpec((n_sub, 1, width), lambda i: (i, 0, 0)),
        ],
        out_shape=[
            jax.ShapeDtypeStruct((heads, T // blk, blk, dh), BF16),
            jax.ShapeDtypeStruct((heads, T // blk, dh + VT_ONES_ROWS, blk), BF16),
            jax.ShapeDtypeStruct((heads, T // blk, dh, blk), BF16),
            jax.ShapeDtypeStruct((T // blk, 1, width), F32),
        ],
        compiler_params=pltpu.CompilerParams(
            dimension_semantics=("arbitrary",), vmem_limit_bytes=_vmem_limit(est)),
        name="kvq_proj",
    )(h, kv_norm_g, q_norm_g, wkv_bf16, wq_bf16, cos_b, sin_b, cos_r, sin_r)


def _moba_select_kernel(qt_ref, km_ref, sel_ref, *, topk, heads, dh, n_blk):
    qi = pl.program_id(0) % n_blk
    neg = -jnp.inf
    for g in range(heads):
        km = km_ref[:, g * dh:(g + 1) * dh]
        km_hi = km.astype(BF16)
        km_lo = (km - km_hi.astype(F32)).astype(BF16)
        gate = (jnp.dot(km_hi, qt_ref[g], preferred_element_type=F32)
                + jnp.dot(km_lo, qt_ref[g], preferred_element_type=F32))
        blk_id = lax.broadcasted_iota(jnp.int32, gate.shape, 0)
        gv = jnp.where(blk_id < qi, gate, neg)
        bias = jnp.full(gate.shape, neg, F32)
        for _ in range(topk):
            m = jnp.max(gv, axis=0, keepdims=True)
            first = jnp.min(jnp.where(gv == m, blk_id, n_blk), axis=0, keepdims=True)
            bias = jnp.where((blk_id == first) & (m > neg), 0.0, bias)
            gv = jnp.where(blk_id == first, neg, gv)
        sel_ref[g] = bias


def _moba_select(qt, km, n_blk, topk):
    heads, n_tiles, dh, blk = qt.shape
    return pl.pallas_call(
        functools.partial(_moba_select_kernel, topk=topk, heads=heads, dh=dh, n_blk=n_blk),
        grid=(n_tiles,),
        in_specs=[
            pl.BlockSpec((heads, None, dh, blk), lambda i: (0, i, 0, 0)),
            pl.BlockSpec((None, n_blk, heads * dh), lambda i: (i // n_blk, 0, 0)),
        ],
        out_specs=pl.BlockSpec((heads, None, n_blk, blk), lambda i: (0, i, 0, 0)),
        out_shape=jax.ShapeDtypeStruct((heads, n_tiles, n_blk, blk), F32),
        compiler_params=pltpu.CompilerParams(dimension_semantics=("arbitrary",)),
        name="moba_select",
    )(qt, km)


def _moba_kernel(qt_ref, k_ref, vt_ref, sel_ref, o_ref, acc_ref, s0_ref, s1_ref, p_ref,
                 *, topk, group, dh):
    qi = pl.program_id(2)
    n_blk = sel_ref.shape[1]
    blk = s0_ref.shape[1] // 2
    neg = -jnp.inf
    halves = (slice(0, blk), slice(blk, 2 * blk))

    def chunks(h):
        return [slice(c, c + MOBA_KEY_CHUNK) for c in range(h * blk, (h + 1) * blk, MOBA_KEY_CHUNK)]

    def stage_pair(s_buf, j):
        jp = jnp.minimum(j, n_blk - 2)
        for g in range(group):
            keys = k_ref[g, pl.ds(jp, 2)].reshape(2 * blk, dh)
            s_buf[g] = jnp.dot(keys, qt_ref[g], preferred_element_type=F32).astype(BF16)

    def stage_softmax(s_buf, h, ms, j):
        rows = chunks(h)
        new_m, alphas = [], []
        for g in range(group):
            part = s_buf[g, rows[0], :]
            for c in rows[1:]:
                part = jnp.maximum(part, s_buf[g, c, :])
            m_blk = jnp.max(part, axis=0, keepdims=True).astype(F32)
            if j is None:
                m_new = shift = m_blk
            else:
                bias = sel_ref[g, pl.ds(j, 1), :]
                m_new = jnp.maximum(ms[g], m_blk + bias)
                shift = m_new - bias
            shift = shift.astype(BF16)
            for c in rows:
                p_ref[g, c, :] = jnp.exp2(s_buf[g, c, :] - shift)
            new_m.append(m_new)
            alphas.append(None if j is None else jnp.exp2(ms[g] - m_new))
        return tuple(new_m), alphas

    def stage_pv(h, j, alphas):
        for g in range(group):
            pv = jnp.dot(vt_ref[g, j], p_ref[g, halves[h], :], preferred_element_type=F32)
            acc_ref[g] = pv if alphas[g] is None else alphas[g] * acc_ref[g] + pv

    key_id = lax.broadcasted_iota(jnp.int32, (blk, blk), 0)
    qry_id = lax.broadcasted_iota(jnp.int32, (blk, blk), 1)
    for g in range(group):
        own = jnp.dot(k_ref[g, qi], qt_ref[g], preferred_element_type=F32)
        s1_ref[g, halves[0], :] = jnp.where(key_id <= qry_id, own, neg).astype(BF16)
    stage_pair(s0_ref, 0)
    ms, alphas = stage_softmax(s1_ref, 0, None, None)
    stage_pv(0, qi, alphas)

    def trip(t, ms, cur, nxt):
        stage_pair(nxt, 2 * t + 2)
        for h in range(2):
            ms, alphas = stage_softmax(cur, h, ms, 2 * t + h)
            stage_pv(h, 2 * t + h, alphas)
        return ms

    def body(t, ms):
        return lax.cond(t % 2 == 0,
                        lambda m: trip(t, m, s0_ref, s1_ref),
                        lambda m: trip(t, m, s1_ref, s0_ref), ms)

    n_full = qi // 2
    ms = lax.fori_loop(0, n_full, body, ms)

    for parity, buf in enumerate((s0_ref, s1_ref)):
        @pl.when((qi % 2 == 1) & (n_full % 2 == parity))
        def _():
            _, alphas = stage_softmax(buf, 0, ms, qi - 1)
            stage_pv(0, qi - 1, alphas)
    for g in range(group):
        acc = acc_ref[g]
        o_ref[:, g * dh:(g + 1) * dh] = (acc[:dh] / acc[dh:dh + 1]).astype(BF16).T


def _moba_attention(qt, k, vt, k_mean, batch, seq):
    heads, n_tiles, dh, blk = qt.shape
    dv_rows = vt.shape[2]
    n_blk = seq // blk
    topk = min(MOBA_TOPK, n_blk)
    width = heads * dh
    group = MOBA_HEAD_GROUP
    km = k_mean.reshape(batch, n_blk, width)
    per_seq = lambda b, h, i: (h, b, 0, 0)
    est = group * seq * (dh + dv_rows) * 2 + 2 * group * blk * blk * (2 + 2) + 6 * group * blk * dh * 4
    return pl.pallas_call(
        functools.partial(_moba_kernel, topk=topk, group=group, dh=dh),
        grid=(batch, heads // group, n_blk),
        in_specs=[
            pl.BlockSpec((group, None, dh, blk), lambda b, h, i: (h, b * n_blk + i, 0, 0)),
            pl.BlockSpec((group, n_blk, blk, dh), per_seq, pipeline_mode=pl.Buffered(1)),
            pl.BlockSpec((group, n_blk, dv_rows, blk), per_seq, pipeline_mode=pl.Buffered(1)),
            pl.BlockSpec((group, None, n_blk, blk), lambda b, h, i: (h, b * n_blk + i, 0, 0)),
        ],
        out_specs=pl.BlockSpec((blk, group * dh), lambda b, h, i: (b * n_blk + i, h)),
        out_shape=jax.ShapeDtypeStruct((n_tiles * blk, width), BF16),
        scratch_shapes=[pltpu.VMEM((group, dv_rows, blk), F32),
                        pltpu.VMEM((group, 2 * blk, blk), BF16), pltpu.VMEM((group, 2 * blk, blk), BF16),
                        pltpu.VMEM((group, 2 * blk, blk), BF16)],
        compiler_params=pltpu.CompilerParams(
            dimension_semantics=("arbitrary", "arbitrary", "arbitrary"), vmem_limit_bytes=_vmem_limit(est)),
        name="moba_attention",
    )(qt, k, vt, _moba_select(qt, km, n_blk, topk))


def kernel(x, ret_norm, ret_w_in, ret_w_out, kv_norm, w_kv, attn_norm, w_q, w_o, ffn_norm, router_group_w, router_group_b, router_expert_w, router_expert_b, expert_w_gate, expert_w_up, expert_w_down, final_norm):
    B, S, D = x.shape
    T = B * S
    assert S % MOBA_BLOCK == 0 and S % RET_KERNEL_CHUNK == 0 and T % MOE_TILE == 0
    assert ret_norm.shape[0] == 1 and attn_norm.shape[0] == 1 and ffn_norm.shape[0] == 2
    h = x.reshape(T, D)
    final_g = final_norm[None, :]

    def proj_router(y, w, resid, layer, name):
        return _proj_router(y, w.astype(BF16), resid, ffn_norm[layer][None, :], router_group_w[layer],
                            router_group_b[layer], router_expert_w[layer], router_expert_b[layer], name)

    def moe(h_aug, route, counts, layer, last):
        return _moe_layer(h_aug, route, counts, ffn_norm[layer][None, :], final_g,
                          expert_w_gate, expert_w_up, expert_w_down, layer, final_norm=last)

    q, kt, v, gate = _ret_in_proj(h, ret_norm[0][None, :], ret_w_in[0].astype(BF16), S)
    y = _ret_core(q, kt, v, gate, B, S)
    h = moe(*proj_router(y, ret_w_out[0], h, 0, "ret_out_proj_router"), 0, False)

    k2, vt2, qt2, k_mean = _kvq_proj(h, kv_norm[None, :], attn_norm[0][None, :],
                                     w_kv.astype(BF16), w_q[0].astype(BF16), S)
    o = _moba_attention(qt2, k2, vt2, k_mean, B, S)
    h = moe(*proj_router(o, w_o[0], h, 1, "attn_out_proj_router"), 1, True)
    return h.reshape(B, S, D)
```
